```python
import jax, jax.numpy as jnp
from jax import lax
import numpy as np

D_MODEL = 2048
BATCH = 4
SEQ = 4096
DEPTH = 4

HEAD_DIM = 64
SWA_HEADS = 8
SWA_KV_HEADS = 2
WINDOW = 128
FOX_HEADS = 8
FOX_QBLOCK = 128
GLA_HEADS = 4
GLA_DK = 64
GLA_DV = 128
GLA_GATE_RANK = 16
GLA_TAU = 16.0
RET_HEADS = 4
RET_DK = 64
RET_DV = 128
RET_THETA_BASE = 10000.0
CHUNK = 64
BRANCH_WIDTH = SWA_HEADS * HEAD_DIM
N_BRANCH = 4
D_FF = 5632
N_MOD = 9
EPS = 1e-6

IN_SIZES = (
    SWA_HEADS * HEAD_DIM, SWA_KV_HEADS * HEAD_DIM, SWA_KV_HEADS * HEAD_DIM,
    FOX_HEADS * HEAD_DIM, FOX_HEADS * HEAD_DIM, FOX_HEADS * HEAD_DIM, FOX_HEADS,
    GLA_HEADS * GLA_DK, GLA_HEADS * GLA_DK, GLA_HEADS * GLA_DV, GLA_GATE_RANK, GLA_HEADS * GLA_DV,
    RET_HEADS * RET_DK, RET_HEADS * RET_DK, RET_HEADS * RET_DV, RET_HEADS * RET_DV,
)
IN_COLS = sum(IN_SIZES)

kernel_name = 'hybrid_macaron_swa_fox_gla_retnet_adaln'


def _split_cols(p):
    out = []
    start = 0
    for size in IN_SIZES:
        out.append(p[..., start:start + size])
        start += size
    return out


def rmsnorm(x, g):
    xf = x.astype(jnp.float32)
    y = xf * lax.rsqrt(jnp.mean(xf * xf, axis=-1, keepdims=True) + EPS)
    return (y * g.astype(jnp.float32)).astype(x.dtype)


def modulate(x, shift, scale):
    return x * (1.0 + scale) + shift


def swiglu(x, w_gate, w_up, w_down):
    return (jax.nn.silu(x @ w_gate) * (x @ w_up)) @ w_down


def head_group_norm(o, w, bias):
    b, s = o.shape[:2]
    of = o.astype(jnp.float32)
    mu = jnp.mean(of, axis=-1, keepdims=True)
    var = jnp.mean(jnp.square(of - mu), axis=-1, keepdims=True)
    y = ((of - mu) * lax.rsqrt(var + EPS)).reshape(b, s, -1)
    return (y * w.astype(jnp.float32) + bias.astype(jnp.float32)).astype(o.dtype)


def rotary(x, pos):
    d = x.shape[-1]
    half = d // 2
    inv = RET_THETA_BASE ** (-jnp.arange(half, dtype=jnp.float32) / half)
    ang = pos.astype(jnp.float32)[:, None] * inv[None, :]
    cos = jnp.cos(ang)[None, :, None, :]
    sin = jnp.sin(ang)[None, :, None, :]
    x1 = x[..., :half].astype(jnp.float32)
    x2 = x[..., half:].astype(jnp.float32)
    return jnp.concatenate([x1 * cos - x2 * sin, x2 * cos + x1 * sin], axis=-1).astype(x.dtype)


def sliding_window_attention(q, k, v, sinks):
    b, s, h, d = q.shape
    kvh = k.shape[2]
    g = h // kvh
    nb = s // WINDOW
    qb = q.reshape(b, nb, WINDOW, kvh, g, d)

    def with_prev(t):
        t = t.reshape(b, nb, WINDOW, kvh, d)
        prev = jnp.concatenate([jnp.zeros_like(t[:, :1]), t[:, :-1]], axis=1)
        return jnp.concatenate([prev, t], axis=2)

    kb, vb = with_prev(k), with_prev(v)
    logits = jnp.einsum('bnqkgd,bnskd->bnkgqs', qb, kb).astype(jnp.float32) * (d ** -0.5)
    qpos = jnp.arange(WINDOW)[:, None] + WINDOW
    kpos = jnp.arange(2 * WINDOW)[None, :]
    rel = qpos - kpos
    band = (rel >= 0) & (rel < WINDOW)
    has_prev = (jnp.arange(nb)[:, None, None] > 0) | (kpos[None] >= WINDOW)
    mask = band[None] & has_prev
    logits = jnp.where(mask[None, :, None, None], logits, -jnp.inf)
    sink = sinks.astype(jnp.float32).reshape(1, 1, kvh, g, 1, 1)
    m = jnp.maximum(jnp.max(logits, axis=-1, keepdims=True), sink)
    p = jnp.exp(logits - m)
    denom = jnp.sum(p, axis=-1, keepdims=True) + jnp.exp(sink - m)
    w = (p / denom).astype(v.dtype)
    o = jnp.einsum('bnkgqs,bnskd->bnqkgd', w, vb)
    return o.reshape(b, s, h * d)


def forgetting_attention(q, k, v, f_logit):
    b, s, h, d = q.shape
    nb = s // FOX_QBLOCK
    logf_cum = jnp.cumsum(jax.nn.log_sigmoid(f_logit.astype(jnp.float32)), axis=1).transpose(0, 2, 1)
    q_blocks = q.reshape(b, nb, FOX_QBLOCK, h, d).transpose(1, 0, 2, 3, 4)
    f_blocks = logf_cum.reshape(b, h, nb, FOX_QBLOCK).transpose(2, 0, 1, 3)
    kpos = jnp.arange(s)
    scale = d ** -0.5

    def block(args):
        qi, fi, i = args
        logits = (jnp.einsum('bqhd,bshd->bhqs', qi, k).astype(jnp.float32) * scale
                  + fi[..., :, None] - logf_cum[:, :, None, :])
        qpos = i * FOX_QBLOCK + jnp.arange(FOX_QBLOCK)
        logits = jnp.where(kpos[None, :] <= qpos[:, None], logits, -jnp.inf)
        p = jax.nn.softmax(logits, axis=-1).astype(v.dtype)
        return jnp.einsum('bhqs,bshd->bqhd', p, v)

    o = lax.map(block, (q_blocks, f_blocks, jnp.arange(nb)))
    return o.transpose(1, 0, 2, 3, 4).reshape(b, s, h * d)


def chunked_gated_linear_attention(q, k, v, log_decay):
    b, s, h, dk = q.shape
    dv = v.shape[-1]
    nc = s // CHUNK
    f32 = jnp.float32
    qc = q.astype(f32).reshape(b, nc, CHUNK, h, dk) * (dk ** -0.5)
    kc = k.astype(f32).reshape(b, nc, CHUNK, h, dk)
    vc = v.astype(f32).reshape(b, nc, CHUNK, h, dv)
    ld = jnp.broadcast_to(log_decay.astype(f32), (b, s, h, dk)).reshape(b, nc, CHUNK, h, dk)
    cum = jnp.cumsum(ld, axis=2)
    last = cum[:, :, -1:]
    q_in = qc * jnp.exp(cum)
    k_in = kc * jnp.exp(-cum)
    k_state = kc * jnp.exp(last - cum)
    causal = jnp.tril(jnp.ones((CHUNK, CHUNK), dtype=bool))
    scores = jnp.where(causal, jnp.einsum('bnihk,bnjhk->bnhij', q_in, k_in), 0.0)
    o_intra = jnp.einsum('bnhij,bnjhv->bnihv', scores, vc)
    updates = jnp.einsum('bnjhk,bnjhv->bnhkv', k_state, vc)
    chunk_decay = jnp.exp(last[:, :, 0])

    def step(state, inp):
        dec, upd = inp
        return dec[..., None] * state + upd, state

    init = jnp.zeros((b, h, dk, dv), f32)
    _, prev_states = lax.scan(step, init, (chunk_decay.transpose(1, 0, 2, 3), updates.transpose(1, 0, 2, 3, 4)))
    prev_states = prev_states.transpose(1, 0, 2, 3, 4)
    o_inter = jnp.einsum('bnihk,bnhkv->bnihv', q_in, prev_states)
    return (o_intra + o_inter).reshape(b, s, h, dv).astype(v.dtype)


def hybrid_token_mixer(n, w_in, fox_b_forget, attn_sinks, gla_w_gate, gla_b_gate, gla_norm_g,
                       ret_gn_w, ret_gn_b, w_branch, w_merge, b_merge, w_out):
    b, s, _ = n.shape
    (a_q, a_k, a_v, f_q, f_k, f_v, f_f, g_q, g_k, g_v, g_lr, g_r,
     r_q, r_k, r_v, r_g) = _split_cols(n @ w_in)

    def heads(t, nh):
        return t.reshape(b, s, nh, -1)

    o_a = sliding_window_attention(heads(a_q, SWA_HEADS), heads(a_k, SWA_KV_HEADS),
                                   heads(a_v, SWA_KV_HEADS), attn_sinks)
    o_b = forgetting_attention(heads(f_q, FOX_HEADS), heads(f_k, FOX_HEADS), heads(f_v, FOX_HEADS),
                               f_f + fox_b_forget)
    log_alpha = jax.nn.log_sigmoid((g_lr @ gla_w_gate + gla_b_gate).astype(jnp.float32)) / GLA_TAU
    o_c = chunked_gated_linear_attention(heads(g_q, GLA_HEADS), heads(g_k, GLA_HEADS),
                                         heads(g_v, GLA_HEADS), heads(log_alpha, GLA_HEADS))
    o_c = rmsnorm(o_c, gla_norm_g).reshape(b, s, -1) * jax.nn.silu(g_r)
    pos = jnp.arange(s)
    log_gamma = jnp.log1p(-jnp.exp2(-5.0 - jnp.arange(RET_HEADS, dtype=jnp.float32)))
    o_d = chunked_gated_linear_attention(rotary(heads(r_q, RET_HEADS), pos), rotary(heads(r_k, RET_HEADS), pos),
                                         heads(r_v, RET_HEADS), log_gamma[None, None, :, None])
    o_d = head_group_norm(o_d, ret_gn_w, ret_gn_b) * jax.nn.silu(r_g)
    merged = jnp.zeros((b, s, D_MODEL), n.dtype)
    for i, o in enumerate((o_a, o_b, o_c, o_d)):
        gate = jax.nn.sigmoid((n @ w_merge[i] + b_merge[i]).astype(jnp.float32)).astype(n.dtype)
        merged = merged + gate * (o @ w_branch[i])
    return merged @ w_out


def setup_inputs(seed: int = 0) -> dict:
    key = jax.random.key(seed)
    ks = jax.random.split(key, 24)
    L, D, F = DEPTH, D_MODEL, D_FF
    nrm = jax.random.normal
    f32 = jnp.float32
    return {
        'x': nrm(ks[0], (BATCH, SEQ, D), f32),
        'c': nrm(ks[1], (BATCH, D), f32),
        'w_ada': nrm(ks[2], (L, D, N_MOD * D), f32) * (0.5 * D ** -0.5),
        'b_ada': nrm(ks[3], (L, N_MOD * D), f32) * 0.02,
        'norm_g': 1.0 + 0.02 * nrm(ks[4], (L, 3, D), f32),
        'ffn_w_gate': nrm(ks[5], (L, 2, D, F), f32) * D ** -0.5,
        'ffn_w_up': nrm(ks[6], (L, 2, D, F), f32) * D ** -0.5,
        'ffn_w_down': nrm(ks[7], (L, 2, F, D), f32) * F ** -0.5,
        'w_in': nrm(ks[8], (L, D, IN_COLS), f32) * D ** -0.5,
        'fox_b_forget': 2.0 + 0.5 * nrm(ks[9], (L, FOX_HEADS), f32),
        'attn_sinks': 0.5 * nrm(ks[10], (L, SWA_HEADS), f32),
        'gla_w_gate': nrm(ks[11], (L, GLA_GATE_RANK, GLA_HEADS * GLA_DK), f32) * GLA_GATE_RANK ** -0.5,
        'gla_b_gate': 0.02 * nrm(ks[12], (L, GLA_HEADS * GLA_DK), f32),
        'gla_norm_g': 1.0 + 0.02 * nrm(ks[13], (L, GLA_DV), f32),
        'ret_gn_w': 1.0 + 0.02 * nrm(ks[14], (L, RET_HEADS * RET_DV), f32),
        'ret_gn_b': 0.02 * nrm(ks[15], (L, RET_HEADS * RET_DV), f32),
        'w_branch': nrm(ks[16], (L, N_BRANCH, BRANCH_WIDTH, D), f32) * BRANCH_WIDTH ** -0.5,
        'w_merge': nrm(ks[17], (L, N_BRANCH, D, D), f32) * D ** -0.5,
        'b_merge': 0.02 * nrm(ks[18], (L, N_BRANCH, D), f32),
        'w_out': nrm(ks[19], (L, D, D), f32) * D ** -0.5,
        'final_norm_g': 1.0 + 0.02 * nrm(ks[20], (D,), f32),
    }


def reference(x, c, w_ada, b_ada, norm_g, ffn_w_gate, ffn_w_up, ffn_w_down, w_in, fox_b_forget,
              attn_sinks, gla_w_gate, gla_b_gate, gla_norm_g, ret_gn_w, ret_gn_b, w_branch,
              w_merge, b_merge, w_out, final_norm_g):
    cond = jax.nn.silu(c)
    h = x
    for l in range(DEPTH):
        mod = cond @ w_ada[l] + b_ada[l]
        sh1, sc1, g1, sh2, sc2, g2, sh3, sc3, g3 = [m[:, None, :] for m in jnp.split(mod, N_MOD, axis=-1)]
        u = modulate(rmsnorm(h, norm_g[l, 0]), sh1, sc1)
        h = h + 0.5 * g1 * swiglu(u, ffn_w_gate[l, 0], ffn_w_up[l, 0], ffn_w_down[l, 0])
        u = modulate(rmsnorm(h, norm_g[l, 1]), sh2, sc2)
        h = h + g2 * hybrid_token_mixer(u, w_in[l], fox_b_forget[l], attn_sinks[l], gla_w_gate[l],
                                        gla_b_gate[l], gla_norm_g[l], ret_gn_w[l], ret_gn_b[l],
                                        w_branch[l], w_merge[l], b_merge[l], w_out[l])
        u = modulate(rmsnorm(h, norm_g[l, 2]), sh3, sc3)
        h = h + 0.5 * g3 * swiglu(u, ffn_w_gate[l, 1], ffn_w_up[l, 1], ffn_w_down[l, 1])
    return rmsnorm(h, final_norm_g)
```

```python
import functools

import numpy as np
import jax
import jax.numpy as jnp
from jax import lax
from jax.experimental import pallas as pl
from jax.experimental.pallas import tpu as pltpu

HEAD_DIM = 64
SWA_HEADS = 8
SWA_KV_HEADS = 2
WINDOW = 128
FOX_HEADS = 8
GLA_HEADS = 4
GLA_DK = 64
GLA_DV = 128
GLA_GATE_RANK = 16
GLA_TAU = 16.0
RET_HEADS = 4
RET_DK = 64
RET_DV = 128
RET_THETA_BASE = 10000.0
CHUNK = 64
BRANCH_WIDTH = 512
N_BRANCH = 4
N_MOD = 9
EPS = 1e-6

BF16 = jnp.bfloat16
F32 = jnp.float32
HIGHEST = lax.Precision.HIGHEST

VMEM_LIMIT_BYTES = 56 * 1024 * 1024

_REF_COLS = {}
_off = 0
for _name, _size in (
        ("a_q", 512), ("a_k", 128), ("a_v", 128),
        ("f_q", 512), ("f_k", 512), ("f_v", 512), ("f_f", 8),
        ("g_q", 256), ("g_k", 256), ("g_v", 512), ("g_lr", 16), ("g_r", 512),
        ("r_q", 256), ("r_k", 256), ("r_v", 512), ("r_g", 512)):
    _REF_COLS[_name] = (_off, _size)
    _off += _size
IN_COLS = _off
_NEW_ORDER = ("a_q", "f_q", "f_k", "f_v", "g_v", "g_r", "r_v", "r_g",
              "g_q", "g_k", "r_q", "r_k", "a_k", "a_v", "f_f", "g_lr")
COL = {}
_off = 0
_perm = []
for _name in _NEW_ORDER:
    _s, _w = _REF_COLS[_name]
    COL[_name] = _off
    _perm.extend(range(_s, _s + _w))
    _off += _w
PROJ_COLS = 5632
_COL_PERM = np.asarray(_perm, dtype=np.int32)
SMALL_COL = COL["f_f"]


def _cparams(*sem):
    return pltpu.CompilerParams(dimension_semantics=sem, vmem_limit_bytes=VMEM_LIMIT_BYTES)


def _tile(n, pref):
    t = (min(pref, n) // 128) * 128
    while t >= 128:
        if n % t == 0:
            return t
        t -= 128
    return n


def _adanorm(x, g, shift, scale):
    ms = jnp.mean(x * x, axis=-1, keepdims=True)
    y = x * lax.rsqrt(ms + EPS) * g
    return y * (1.0 + scale) + shift


def _log_sigmoid(x):
    return jnp.minimum(x, 0.0) - jnp.log1p(jnp.exp(-jnp.abs(x)))


def _silu(x):
    return x * jax.nn.sigmoid(x)


def _ada_kernel(c_ref, w_ref, b_ref, o_ref):
    c = c_ref[...]
    cond = _silu(c)
    o_ref[...] = jnp.dot(cond, w_ref[...], precision=HIGHEST, preferred_element_type=F32) + b_ref[...]


def ada_modulation(c, w_ada, b_ada):
    L, D, N = w_ada.shape
    B = c.shape[0]
    BP = 8
    cp = jnp.zeros((BP, D), F32).at[:B].set(c)
    tn = _tile(N, 1024)
    out = pl.pallas_call(
        _ada_kernel,
        grid=(L, N // tn),
        in_specs=[
            pl.BlockSpec((BP, D), lambda l, j: (0, 0)),
            pl.BlockSpec((None, D, tn), lambda l, j: (l, 0, j)),
            pl.BlockSpec((None, 1, tn), lambda l, j: (l, 0, j)),
        ],
        out_specs=pl.BlockSpec((None, BP, tn), lambda l, j: (l, 0, j)),
        out_shape=jax.ShapeDtypeStruct((L, BP, N), F32),
        compiler_params=_cparams("parallel", "parallel"),
        name="ada_modulation",
    )(cp, w_ada, b_ada.reshape(L, 1, N))
    return out[:, :B].reshape(L, B, N_MOD, D)


def _ffn_kernel(h_ref, mod_ref, g_ref, wg_ref, wu_ref, wd_ref, o_ref, u_scr, acc_scr, *, mrow, res_scale):
    f = pl.program_id(1)

    @pl.when(f == 0)
    def _():
        u = _adanorm(h_ref[...], g_ref[...], mod_ref[mrow:mrow + 1, :], mod_ref[mrow + 1:mrow + 2, :])
        u_scr[...] = u.astype(BF16)
        acc_scr[...] = jnp.zeros_like(acc_scr)

    u = u_scr[...]
    a = jnp.dot(u, wg_ref[...], preferred_element_type=F32)
    b = jnp.dot(u, wu_ref[...], preferred_element_type=F32)
    mid = (_silu(a) * b).astype(BF16)
    acc_scr[...] += jnp.dot(mid, wd_ref[...], preferred_element_type=F32)

    @pl.when(f == pl.num_programs(1) - 1)
    def _():
        o_ref[...] = h_ref[...] + (res_scale * mod_ref[mrow + 2:mrow + 3, :]) * acc_scr[...]


def ffn_block(h, mod, g, wg, wu, wd, *, seq, mrow, res_scale):
    T, D = h.shape
    F = wg.shape[1]
    tm = min(512, seq)
    tf = 512 if F % 512 == 0 else F
    kern = functools.partial(_ffn_kernel, mrow=mrow, res_scale=res_scale)
    return pl.pallas_call(
        kern,
        grid=(T // tm, F // tf),
        in_specs=[
            pl.BlockSpec((tm, D), lambda i, f: (i, 0)),
            pl.BlockSpec((None, N_MOD, D), lambda i, f: ((i * tm) // seq, 0, 0)),
            pl.BlockSpec((1, D), lambda i, f: (0, 0)),
            pl.BlockSpec((D, tf), lambda i, f: (0, f)),
            pl.BlockSpec((D, tf), lambda i, f: (0, f)),
            pl.BlockSpec((tf, D), lambda i, f: (f, 0)),
        ],
        out_specs=pl.BlockSpec((tm, D), lambda i, f: (i, 0)),
        out_shape=jax.ShapeDtypeStruct((T, D), F32),
        scratch_shapes=[pltpu.VMEM((tm, D), BF16), pltpu.VMEM((tm, D), F32)],
        compiler_params=_cparams("parallel", "arbitrary"),
        name="ffn_block",
    )(h, mod, g.reshape(1, D), wg, wu, wd)


def _norm_proj_kernel(h_ref, mod_ref, g_ref, w_ref, p_ref, u_ref, *, mrow):
    @pl.when(pl.program_id(1) == 0)
    def _():
        u = _adanorm(h_ref[...], g_ref[...], mod_ref[mrow:mrow + 1, :], mod_ref[mrow + 1:mrow + 2, :])
        u_ref[...] = u.astype(BF16)

    p_ref[...] = jnp.dot(u_ref[...], w_ref[...], preferred_element_type=F32)


def norm_proj(h, mod, g, w, *, seq, mrow):
    T, D = h.shape
    N = w.shape[1]
    tm = min(1024, seq)
    tn = 512
    kern = functools.partial(_norm_proj_kernel, mrow=mrow)
    return pl.pallas_call(
        kern,
        grid=(T // tm, N // tn),
        in_specs=[
            pl.BlockSpec((tm, D), lambda i, j: (i, 0)),
            pl.BlockSpec((None, N_MOD, D), lambda i, j: ((i * tm) // seq, 0, 0)),
            pl.BlockSpec((1, D), lambda i, j: (0, 0)),
            pl.BlockSpec((D, tn), lambda i, j: (0, j)),
        ],
        out_specs=[
            pl.BlockSpec((tm, tn), lambda i, j: (i, j)),
            pl.BlockSpec((tm, D), lambda i, j: (i, 0)),
        ],
        out_shape=[jax.ShapeDtypeStruct((T, N), F32), jax.ShapeDtypeStruct((T, D), BF16)],
        compiler_params=_cparams("parallel", "arbitrary"),
        name="norm_proj",
    )(h, mod, g.reshape(1, D), w)


def _swa_kernel(sink_ref, q_ref, kv_ref, kvp_ref, o_ref):
    n = pl.program_id(1)
    W = WINDOW
    q = q_ref[...]
    kv = kv_ref[...]
    kvp = kvp_ref[...]
    kk = jnp.concatenate([kvp[:, :W], kv[:, :W]], axis=0).astype(BF16)
    vv = jnp.concatenate([kvp[:, W:], kv[:, W:]], axis=0).astype(BF16)
    qpos = lax.broadcasted_iota(jnp.int32, (W, 2 * W), 0) + W
    kpos = lax.broadcasted_iota(jnp.int32, (W, 2 * W), 1)
    rel = qpos - kpos
    mask = (rel >= 0) & (rel < W) & ((kpos >= W) | (n > 0))
    group = SWA_HEADS // SWA_KV_HEADS
    for hd in range(SWA_HEADS):
        kvh = hd // group
        k = kk[:, kvh * HEAD_DIM:(kvh + 1) * HEAD_DIM]
        v = vv[:, kvh * HEAD_DIM:(kvh + 1) * HEAD_DIM]
        qh = (q[:, hd * HEAD_DIM:(hd + 1) * HEAD_DIM] * (HEAD_DIM ** -0.5)).astype(BF16)
        logits = lax.dot_general(qh, k, (((1,), (1,)), ((), ())), preferred_element_type=F32)
        logits = jnp.where(mask, logits, -jnp.inf)
        sink = sink_ref[hd]
        m = jnp.maximum(jnp.max(logits, axis=-1, keepdims=True), sink)
        p = jnp.exp(logits - m)
        denom = jnp.sum(p, axis=-1, keepdims=True) + jnp.exp(sink - m)
        w = (p / denom).astype(BF16)
        o = jnp.dot(w, v, preferred_element_type=F32)
        o_ref[:, hd * HEAD_DIM:(hd + 1) * HEAD_DIM] = o.astype(o_ref.dtype)


def swa_attention(proj, sinks, *, batch, seq):
    T = proj.shape[0]
    nb = seq // WINDOW
    qblk = COL["a_q"] // 512
    kvblk = COL["a_k"] // 256
    return pl.pallas_call(
        _swa_kernel,
        grid=(batch, nb),
        in_specs=[
            pl.BlockSpec(memory_space=pltpu.SMEM),
            pl.BlockSpec((WINDOW, 512), lambda b, n: (b * nb + n, qblk)),
            pl.BlockSpec((WINDOW, 256), lambda b, n: (b * nb + n, kvblk)),
            pl.BlockSpec((WINDOW, 256), lambda b, n: (b * nb + jnp.maximum(n - 1, 0), kvblk)),
        ],
        out_specs=pl.BlockSpec((WINDOW, 512), lambda b, n: (b * nb + n, 0)),
        out_shape=jax.ShapeDtypeStruct((T, BRANCH_WIDTH), BF16),
        compiler_params=_cparams("parallel", "arbitrary"),
        name="swa_attention",
    )(sinks, proj, proj, proj)


def _fox_cum_kernel(x_ref, b_ref, o_ref, *, seq):
    R = 128
    ri = lax.broadcasted_iota(jnp.int32, (R, R), 0)
    ci = lax.broadcasted_iota(jnp.int32, (R, R), 1)
    tri = (ci <= ri).astype(F32)

    def body(i, carry):
        r0 = pl.multiple_of(i * R, R)
        x = x_ref[pl.ds(r0, R), :][:, :FOX_HEADS] + b_ref[...]
        ls = _log_sigmoid(x)
        cum = jnp.dot(tri, ls, precision=HIGHEST, preferred_element_type=F32) + carry
        o_ref[pl.ds(r0, R), :] = cum
        return cum[R - 1:R, :]

    lax.fori_loop(0, seq // R, body, jnp.zeros((1, FOX_HEADS), F32))


def fox_forget_cumsum(proj, fox_b, *, batch, seq):
    T = proj.shape[0]
    blk = SMALL_COL // 128
    return pl.pallas_call(
        functools.partial(_fox_cum_kernel, seq=seq),
        grid=(batch,),
        in_specs=[
            pl.BlockSpec((seq, 128), lambda b: (b, blk)),
            pl.BlockSpec((1, FOX_HEADS), lambda b: (0, 0)),
        ],
        out_specs=pl.BlockSpec((seq, FOX_HEADS), lambda b: (b, 0)),
        out_shape=jax.ShapeDtypeStruct((T, FOX_HEADS), F32),
        compiler_params=_cparams("parallel"),
        name="fox_forget_cumsum",
    )(proj, fox_b.reshape(1, FOX_HEADS))


def _fox_kernel(q_ref, k_ref, v_ref, fc_ref, fr_ref, o_ref, *, tq, hp_heads):
    qi = pl.program_id(2)
    tk = tq
    d = HEAD_DIM
    row = lax.broadcasted_iota(jnp.int32, (tq, tk), 0)
    col = lax.broadcasted_iota(jnp.int32, (tq, tk), 1)
    causal = col <= row
    for hh in range(hp_heads):
        qh = (q_ref[:, hh * d:(hh + 1) * d] * (d ** -0.5)).astype(BF16)
        fq = fc_ref[:, hh:hh + 1]

        def scores(kb):
            k0 = pl.multiple_of(kb * tk, tk)
            k = k_ref[pl.ds(k0, tk), hh * d:(hh + 1) * d].astype(BF16)
            v = v_ref[pl.ds(k0, tk), hh * d:(hh + 1) * d].astype(BF16)
            fk = fr_ref[hh:hh + 1, pl.ds(k0, tk)]
            s = lax.dot_general(qh, k, (((1,), (1,)), ((), ())), preferred_element_type=F32)
            return s + (fq - fk), v

        def update(carry, s, v):
            m, l, acc = carry
            m_new = jnp.maximum(m, jnp.max(s, axis=-1, keepdims=True))
            alpha = jnp.exp(m - m_new)
            p = jnp.exp(s - m_new)
            l_new = alpha * l + jnp.sum(p, axis=-1, keepdims=True)
            acc_new = alpha * acc + jnp.dot(p.astype(BF16), v, preferred_element_type=F32)
            return m_new, l_new, acc_new

        def body(kb, carry):
            s, v = scores(kb)
            return update(carry, s, v)

        init = (jnp.full((tq, 1), -jnp.inf, F32), jnp.zeros((tq, 1), F32), jnp.zeros((tq, d), F32))
        carry = lax.fori_loop(0, qi, body, init)
        s, v = scores(qi)
        s = jnp.where(causal, s, -jnp.inf)
        m, l, acc = update(carry, s, v)
        o_ref[:, hh * d:(hh + 1) * d] = (acc / l).astype(o_ref.dtype)


def fox_attention(proj, fc, fr, *, batch, seq):
    T = proj.shape[0]
    tq = min(256, seq)
    nq = seq // tq
    hp_heads = 2
    n_hp = FOX_HEADS // hp_heads
    qblk, kblk, vblk = COL["f_q"] // 128, COL["f_k"] // 128, COL["f_v"] // 128
    kern = functools.partial(_fox_kernel, tq=tq, hp_heads=hp_heads)
    return pl.pallas_call(
        kern,
        grid=(batch, n_hp, nq),
        in_specs=[
            pl.BlockSpec((tq, 128), lambda b, hp, qi: (b * nq + qi, qblk + hp)),
            pl.BlockSpec((seq, 128), lambda b, hp, qi: (b, kblk + hp)),
            pl.BlockSpec((seq, 128), lambda b, hp, qi: (b, vblk + hp)),
            pl.BlockSpec((None, None, tq, hp_heads), lambda b, hp, qi: (b, hp, qi, 0)),
            pl.BlockSpec((None, None, hp_heads, seq), lambda b, hp, qi: (b, hp, 0, 0)),
        ],
        out_specs=pl.BlockSpec((tq, 128), lambda b, hp, qi: (b * nq + qi, hp)),
        out_shape=jax.ShapeDtypeStruct((T, BRANCH_WIDTH), BF16),
        compiler_params=_cparams("parallel", "parallel", "arbitrary"),
        name="fox_attention",
    )(proj, proj, proj, fc, fr)


def _rotate_half(x, neg_first_half):
    n = x.shape[-1]
    half = HEAD_DIM // 2
    fwd = pltpu.roll(x, half, 1)
    bwd = pltpu.roll(x, n - half, 1)
    return jnp.where(neg_first_half, -bwd, fwd)


def _linear_attn_kernel(*refs, mode, cb):
    if mode == "gla":
        (q_ref, k_ref, v_ref, gate_ref, small_ref, wg_ref, bg_ref, ng_ref, o_ref, st_ref) = refs
    else:
        (q_ref, k_ref, v_ref, gate_ref, lg_ref, cos_ref, sin_ref, gw_ref, gb_ref, o_ref, st_ref) = refs
    H, DK, DV, C = GLA_HEADS, GLA_DK, GLA_DV, CHUNK

    @pl.when(pl.program_id(1) == 0)
    def _():
        st_ref[...] = jnp.zeros_like(st_ref)

    ri = lax.broadcasted_iota(jnp.int32, (C, C), 0)
    ci = lax.broadcasted_iota(jnp.int32, (C, C), 1)
    causal = ci <= ri
    if mode == "gla":
        tri = causal.astype(F32)
    else:
        lane = lax.broadcasted_iota(jnp.int32, (C, H * DK), 1)
        first_half = (lane % HEAD_DIM) < (HEAD_DIM // 2)
        steps = (lax.broadcasted_iota(jnp.int32, (C, H * DK), 0) + 1).astype(F32)

    for c in range(cb // C):
        rows = slice(c * C, (c + 1) * C)
        q = q_ref[rows, :]
        k = k_ref[rows, :]
        if mode == "gla":
            glr = small_ref[rows, :][:, FOX_HEADS:FOX_HEADS + GLA_GATE_RANK]
            z = jnp.dot(glr, wg_ref[...], precision=HIGHEST, preferred_element_type=F32) + bg_ref[...]
            ld = _log_sigmoid(z) / GLA_TAU
            cum = jnp.dot(tri, ld, precision=HIGHEST, preferred_element_type=F32)
        else:
            cos = cos_ref[rows, :]
            sin = sin_ref[rows, :]
            q = q * cos + _rotate_half(q, first_half) * sin
            k = k * cos + _rotate_half(k, first_half) * sin
            cum = steps * lg_ref[...]
        last = cum[C - 1:C, :]
        q_in = (q * (DK ** -0.5) * jnp.exp(cum)).astype(BF16)
        k_in = (k * jnp.exp(-cum)).astype(BF16)
        k_st = (k * jnp.exp(last - cum)).astype(BF16)
        dec = jnp.exp(last)
        for hd in range(H):
            ks = slice(hd * DK, (hd + 1) * DK)
            vs = slice(hd * DV, (hd + 1) * DV)
            v = v_ref[rows, vs].astype(BF16)
            sc = lax.dot_general(q_in[:, ks], k_in[:, ks], (((1,), (1,)), ((), ())), preferred_element_type=F32)
            sc = jnp.where(causal, sc, 0.0).astype(BF16)
            st = st_ref[hd]
            o = jnp.dot(sc, v, preferred_element_type=F32)
            o = o + lax.dot_general(q_in[:, ks], st.astype(BF16), (((1,), (1,)), ((), ())),
                                    preferred_element_type=F32)
            upd = lax.dot_general(v, k_st[:, ks], (((0,), (0,)), ((), ())), preferred_element_type=F32)
            st_ref[hd] = st * dec[:, ks] + upd
            gate = _silu(gate_ref[rows, vs])
            if mode == "gla":
                y = o * lax.rsqrt(jnp.mean(o * o, axis=-1, keepdims=True) + EPS) * ng_ref[...]
            else:
                mu = jnp.mean(o, axis=-1, keepdims=True)
                var = jnp.mean(jnp.square(o - mu), axis=-1, keepdims=True)
                y = (o - mu) * lax.rsqrt(var + EPS) * gw_ref[:, vs] + gb_ref[:, vs]
            o_ref[rows, vs] = (y * gate).astype(o_ref.dtype)


def linear_attention(proj, params, *, mode, batch, seq):
    T = proj.shape[0]
    cb = min(512, seq)
    nc = seq // cb
    pre = "g" if mode == "gla" else "r"
    qblk, kblk = COL[pre + "_q"] // 256, COL[pre + "_k"] // 256
    vblk = COL[pre + "_v"] // 512
    gblk = COL["g_r" if mode == "gla" else "r_g"] // 512
    row = lambda b, c: b * nc + c
    in_specs = [
        pl.BlockSpec((cb, 256), lambda b, c: (row(b, c), qblk)),
        pl.BlockSpec((cb, 256), lambda b, c: (row(b, c), kblk)),
        pl.BlockSpec((cb, 512), lambda b, c: (row(b, c), vblk)),
        pl.BlockSpec((cb, 512), lambda b, c: (row(b, c), gblk)),
    ]
    args = [proj, proj, proj, proj]
    if mode == "gla":
        wg, bg, ng = params
        in_specs += [
            pl.BlockSpec((cb, 128), lambda b, c: (row(b, c), SMALL_COL // 128)),
            pl.BlockSpec(wg.shape, lambda b, c: (0, 0)),
            pl.BlockSpec((1, bg.shape[-1]), lambda b, c: (0, 0)),
            pl.BlockSpec((1, ng.shape[-1]), lambda b, c: (0, 0)),
        ]
        args += [proj, wg, bg.reshape(1, -1), ng.reshape(1, -1)]
    else:
        lg, cos, sin, gw, gb = params
        in_specs += [
            pl.BlockSpec((1, lg.shape[-1]), lambda b, c: (0, 0)),
            pl.BlockSpec((cb, 256), lambda b, c: (c, 0)),
            pl.BlockSpec((cb, 256), lambda b, c: (c, 0)),
            pl.BlockSpec((1, gw.shape[-1]), lambda b, c: (0, 0)),
            pl.BlockSpec((1, gb.shape[-1]), lambda b, c: (0, 0)),
        ]
        args += [lg.reshape(1, -1), cos, sin, gw.reshape(1, -1), gb.reshape(1, -1)]
    return pl.pallas_call(
        functools.partial(_linear_attn_kernel, mode=mode, cb=cb),
        grid=(batch, nc),
        in_specs=in_specs,
        out_specs=pl.BlockSpec((cb, 512), lambda b, c: (row(b, c), 0)),
        out_shape=jax.ShapeDtypeStruct((T, BRANCH_WIDTH), BF16),
        scratch_shapes=[pltpu.VMEM((GLA_HEADS, GLA_DV, GLA_DK), F32)],
        compiler_params=_cparams("parallel", "arbitrary"),
        name="linear_attention_" + mode,
    )(*args)


def _retention_tables(seq):
    half = RET_DK // 2
    inv = RET_THETA_BASE ** (-jnp.arange(half, dtype=F32) / half)
    ang = jnp.arange(seq).astype(F32)[:, None] * inv[None, :]
    cos = jnp.tile(jnp.cos(ang), (1, 2 * RET_HEADS))
    sin = jnp.tile(jnp.sin(ang), (1, 2 * RET_HEADS))
    log_gamma = jnp.log1p(-jnp.exp2(-5.0 - jnp.arange(RET_HEADS, dtype=F32)))
    return jnp.repeat(log_gamma, RET_DK), cos, sin


def _merge_kernel(u_ref, oa_ref, ob_ref, oc_ref, od_ref, wm_ref, wb_ref, bm_ref, m_ref):
    u = u_ref[...]
    acc = None
    for i, o_ref in enumerate((oa_ref, ob_ref, oc_ref, od_ref)):
        z = jnp.dot(u, wm_ref[i], preferred_element_type=F32) + bm_ref[i]
        pr = jnp.dot(o_ref[...], wb_ref[i], preferred_element_type=F32)
        t = jax.nn.sigmoid(z) * pr
        acc = t if acc is None else acc + t
    m_ref[...] = acc.astype(m_ref.dtype)


def merge_branches(u, branches, wm, wb, bm):
    T, D = u.shape
    tm, tn = min(512, T), min(512, D)
    ospec = pl.BlockSpec((tm, BRANCH_WIDTH), lambda i, j: (i, 0))
    return pl.pallas_call(
        _merge_kernel,
        grid=(T // tm, D // tn),
        in_specs=[
            pl.BlockSpec((tm, D), lambda i, j: (i, 0)),
            ospec, ospec, ospec, ospec,
            pl.BlockSpec((N_BRANCH, D, tn), lambda i, j: (0, 0, j)),
            pl.BlockSpec((N_BRANCH, BRANCH_WIDTH, tn), lambda i, j: (0, 0, j)),
            pl.BlockSpec((N_BRANCH, 1, tn), lambda i, j: (0, 0, j)),
        ],
        out_specs=pl.BlockSpec((tm, tn), lambda i, j: (i, j)),
        out_shape=jax.ShapeDtypeStruct((T, D), BF16),
        compiler_params=_cparams("parallel", "arbitrary"),
        name="merge_branches",
    )(u, *branches, wm, wb, bm.reshape(N_BRANCH, 1, D))


def _out_proj_kernel(h_ref, m_ref, w_ref, mod_ref, o_ref, *, mrow, tn):
    j = pl.program_id(1)
    c0 = pl.multiple_of(j * tn, tn)
    gate = mod_ref[mrow:mrow + 1, pl.ds(c0, tn)]
    o_ref[...] = h_ref[...] + gate * jnp.dot(m_ref[...], w_ref[...], preferred_element_type=F32)


def out_proj(h, merged, w, mod, *, seq, mrow):
    T, D = h.shape
    tm, tn = min(1024, seq), min(512, D)
    return pl.pallas_call(
        functools.partial(_out_proj_kernel, mrow=mrow, tn=tn),
        grid=(T // tm, D // tn),
        in_specs=[
            pl.BlockSpec((tm, tn), lambda i, j: (i, j)),
            pl.BlockSpec((tm, D), lambda i, j: (i, 0)),
            pl.BlockSpec((D, tn), lambda i, j: (0, j)),
            pl.BlockSpec((None, N_MOD, D), lambda i, j: ((i * tm) // seq, 0, 0)),
        ],
        out_specs=pl.BlockSpec((tm, tn), lambda i, j: (i, j)),
        out_shape=jax.ShapeDtypeStruct((T, D), F32),
        compiler_params=_cparams("parallel", "arbitrary"),
        name="out_proj",
    )(h, merged, w, mod)


def _final_norm_kernel(h_ref, g_ref, o_ref):
    x = h_ref[...]
    ms = jnp.mean(x * x, axis=-1, keepdims=True)
    o_ref[...] = x * lax.rsqrt(ms + EPS) * g_ref[...]


def final_norm(h, g):
    T, D = h.shape
    tm = min(1024, T)
    return pl.pallas_call(
        _final_norm_kernel,
        grid=(T // tm,),
        in_specs=[pl.BlockSpec((tm, D), lambda i: (i, 0)), pl.BlockSpec((1, D), lambda i: (0, 0))],
        out_specs=pl.BlockSpec((tm, D), lambda i: (i, 0)),
        out_shape=jax.ShapeDtypeStruct((T, D), F32),
        compiler_params=_cparams("parallel"),
        name="final_norm",
    )(h, g.reshape(1, D))


def _permuted_w_in(w_in_l):
    D = w_in_l.shape[0]
    w = jnp.take(w_in_l, _COL_PERM, axis=1).astype(BF16)
    return jnp.concatenate([w, jnp.zeros((D, PROJ_COLS - IN_COLS), BF16)], axis=1)


def kernel(x, c, w_ada, b_ada, norm_g, ffn_w_gate, ffn_w_up, ffn_w_down, w_in, fox_b_forget, attn_sinks,
           gla_w_gate, gla_b_gate, gla_norm_g, ret_gn_w, ret_gn_b, w_branch, w_merge, b_merge, w_out,
           final_norm_g):
    B, S, D = x.shape
    L = w_ada.shape[0]
    T = B * S
    mod_all = ada_modulation(c, w_ada, b_ada)
    log_gamma, cos, sin = _retention_tables(S)
    h = x.reshape(T, D)
    for l in range(L):
        mod = mod_all[l]
        h = ffn_block(h, mod, norm_g[l, 0], ffn_w_gate[l, 0].astype(BF16), ffn_w_up[l, 0].astype(BF16),
                      ffn_w_down[l, 0].astype(BF16), seq=S, mrow=0, res_scale=0.5)
        proj, u = norm_proj(h, mod, norm_g[l, 1], _permuted_w_in(w_in[l]), seq=S, mrow=3)
        o_a = swa_attention(proj, attn_sinks[l], batch=B, seq=S)
        fc = fox_forget_cumsum(proj, fox_b_forget[l], batch=B, seq=S)
        fc4 = fc.reshape(B, S, FOX_HEADS // 2, 2)
        o_b = fox_attention(proj, fc4.transpose(0, 2, 1, 3), fc4.transpose(0, 2, 3, 1), batch=B, seq=S)
        o_c = linear_attention(proj, (gla_w_gate[l], gla_b_gate[l], gla_norm_g[l]), mode="gla", batch=B, seq=S)
        o_d = linear_attention(proj, (log_gamma, cos, sin, ret_gn_w[l], ret_gn_b[l]), mode="ret", batch=B, seq=S)
        merged = merge_branches(u, (o_a, o_b, o_c, o_d), w_merge[l].astype(BF16), w_branch[l].astype(BF16),
                                b_merge[l])
        h = out_proj(h, merged, w_out[l].astype(BF16), mod, seq=S, mrow=5)
        h = ffn_block(h, mod, norm_g[l, 2], ffn_w_gate[l, 1].astype(BF16), ffn_w_up[l, 1].astype(BF16),
                      ffn_w_down[l, 1].astype(BF16), seq=S, mrow=6, res_scale=0.5)
    return final_norm(h, final_norm_g).reshape(B, S, D)
```

```python
import functools

import numpy as np
import jax
import jax.numpy as jnp
from jax import lax
from jax.experimental import pallas as pl
from jax.experimental.pallas import tpu as pltpu

HEAD_DIM = 64
SWA_HEADS = 8
SWA_KV_HEADS = 2
WINDOW = 128
FOX_HEADS = 8
GLA_HEADS = 4
GLA_DK = 64
GLA_DV = 128
GLA_GATE_RANK = 16
GLA_TAU = 16.0
RET_HEADS = 4
RET_DK = 64
RET_DV = 128
RET_THETA_BASE = 10000.0
CHUNK = 64
BRANCH_WIDTH = 512
N_BRANCH = 4
N_MOD = 9
EPS = 1e-6

BF16 = jnp.bfloat16
F32 = jnp.float32
HIGHEST = lax.Precision.HIGHEST

VMEM_LIMIT_BYTES = 56 * 1024 * 1024

_REF_COLS = {}
_off = 0
for _name, _size in (
        ("a_q", 512), ("a_k", 128), ("a_v", 128),
        ("f_q", 512), ("f_k", 512), ("f_v", 512), ("f_f", 8),
        ("g_q", 256), ("g_k", 256), ("g_v", 512), ("g_lr", 16), ("g_r", 512),
        ("r_q", 256), ("r_k", 256), ("r_v", 512), ("r_g", 512)):
    _REF_COLS[_name] = (_off, _size)
    _off += _size
IN_COLS = _off


def _layout(order):
    col, perm, off = {}, [], 0
    for name in order:
        start, width = _REF_COLS[name]
        col[name] = off
        perm.extend(range(start, start + width))
        off += width
    return col, np.asarray(perm, dtype=np.int32)


COLB, _PERM_B = _layout(("a_q", "f_q", "f_k", "f_v", "g_v", "r_v", "g_q", "g_k", "r_q", "r_k", "a_k", "a_v"))
COLF, _PERM_F = _layout(("g_r", "r_g", "f_f", "g_lr"))
PROJ_B_COLS = 4608
PROJ_F_COLS = 1152
SMALL_COL = COLF["f_f"]


def _cparams(*sem):
    return pltpu.CompilerParams(dimension_semantics=sem, vmem_limit_bytes=VMEM_LIMIT_BYTES)


def _tile(n, pref):
    t = (min(pref, n) // 128) * 128
    while t >= 128:
        if n % t == 0:
            return t
        t -= 128
    return n


def _adanorm(x, g, shift, scale):
    ms = jnp.mean(x * x, axis=-1, keepdims=True)
    y = x * lax.rsqrt(ms + EPS) * g
    return y * (1.0 + scale) + shift


def _log_sigmoid(x):
    return jnp.minimum(x, 0.0) - jnp.log1p(jnp.exp(-jnp.abs(x)))


def _silu(x):
    return x * jax.nn.sigmoid(x)


def _ada_kernel(c_ref, w_ref, b_ref, o_ref):
    c = c_ref[...]
    cond = _silu(c)
    o_ref[...] = jnp.dot(cond, w_ref[...], precision=HIGHEST, preferred_element_type=F32) + b_ref[...]


def ada_modulation(c, w_ada, b_ada):
    L, D, N = w_ada.shape
    B = c.shape[0]
    BP = 8
    cp = jnp.zeros((BP, D), F32).at[:B].set(c)
    tn = _tile(N, 1024)
    out = pl.pallas_call(
        _ada_kernel,
        grid=(L, N // tn),
        in_specs=[
            pl.BlockSpec((BP, D), lambda l, j: (0, 0)),
            pl.BlockSpec((None, D, tn), lambda l, j: (l, 0, j)),
            pl.BlockSpec((None, 1, tn), lambda l, j: (l, 0, j)),
        ],
        out_specs=pl.BlockSpec((None, BP, tn), lambda l, j: (l, 0, j)),
        out_shape=jax.ShapeDtypeStruct((L, BP, N), F32),
        compiler_params=_cparams("parallel", "parallel"),
        name="ada_modulation",
    )(cp, w_ada, b_ada.reshape(L, 1, N))
    return out[:, :B].reshape(L, B, N_MOD, D)


def _ffn_kernel(h_ref, mod_ref, g_ref, wg_ref, wu_ref, wd_ref, o_ref, u_scr, acc_scr, *, mrow, res_scale):
    f = pl.program_id(1)

    @pl.when(f == 0)
    def _():
        u = _adanorm(h_ref[...], g_ref[...], mod_ref[mrow:mrow + 1, :], mod_ref[mrow + 1:mrow + 2, :])
        u_scr[...] = u.astype(BF16)
        acc_scr[...] = jnp.zeros_like(acc_scr)

    u = u_scr[...]
    a = jnp.dot(u, wg_ref[...], preferred_element_type=F32)
    b = jnp.dot(u, wu_ref[...], preferred_element_type=F32)
    mid = (_silu(a) * b).astype(BF16)
    acc_scr[...] += jnp.dot(mid, wd_ref[...], preferred_element_type=F32)

    @pl.when(f == pl.num_programs(1) - 1)
    def _():
        o_ref[...] = h_ref[...] + (res_scale * mod_ref[mrow + 2:mrow + 3, :]) * acc_scr[...]


def ffn_block(h, mod, g, wg, wu, wd, *, seq, mrow, res_scale):
    T, D = h.shape
    F = wg.shape[1]
    tm = min(512, seq)
    tf = 512 if F % 512 == 0 else F
    kern = functools.partial(_ffn_kernel, mrow=mrow, res_scale=res_scale)
    return pl.pallas_call(
        kern,
        grid=(T // tm, F // tf),
        in_specs=[
            pl.BlockSpec((tm, D), lambda i, f: (i, 0)),
            pl.BlockSpec((None, N_MOD, D), lambda i, f: ((i * tm) // seq, 0, 0)),
            pl.BlockSpec((1, D), lambda i, f: (0, 0)),
            pl.BlockSpec((D, tf), lambda i, f: (0, f)),
            pl.BlockSpec((D, tf), lambda i, f: (0, f)),
            pl.BlockSpec((tf, D), lambda i, f: (f, 0)),
        ],
        out_specs=pl.BlockSpec((tm, D), lambda i, f: (i, 0)),
        out_shape=jax.ShapeDtypeStruct((T, D), F32),
        scratch_shapes=[pltpu.VMEM((tm, D), BF16), pltpu.VMEM((tm, D), F32)],
        compiler_params=_cparams("parallel", "arbitrary"),
        name="ffn_block",
    )(h, mod, g.reshape(1, D), wg, wu, wd)


def _norm_proj_kernel(h_ref, mod_ref, g_ref, w_ref, p_ref, u_ref, *, mrow):
    @pl.when(pl.program_id(1) == 0)
    def _():
        u = _adanorm(h_ref[...], g_ref[...], mod_ref[mrow:mrow + 1, :], mod_ref[mrow + 1:mrow + 2, :])
        u_ref[...] = u.astype(BF16)

    p_ref[...] = jnp.dot(u_ref[...], w_ref[...], preferred_element_type=F32).astype(p_ref.dtype)


def _matmul_kernel(a_ref, w_ref, o_ref):
    o_ref[...] = jnp.dot(a_ref[...], w_ref[...], preferred_element_type=F32).astype(o_ref.dtype)


def matmul(a, w, out_dtype):
    T, D = a.shape
    N = w.shape[1]
    tm = min(1024, T)
    return pl.pallas_call(
        _matmul_kernel,
        grid=(T // tm,),
        in_specs=[pl.BlockSpec((tm, D), lambda i: (i, 0)), pl.BlockSpec((D, N), lambda i: (0, 0))],
        out_specs=pl.BlockSpec((tm, N), lambda i: (i, 0)),
        out_shape=jax.ShapeDtypeStruct((T, N), out_dtype),
        compiler_params=_cparams("parallel"),
        name="matmul",
    )(a, w)


def norm_proj(h, mod, g, w, *, seq, mrow):
    T, D = h.shape
    N = w.shape[1]
    tm = min(1024, seq)
    tn = 512
    kern = functools.partial(_norm_proj_kernel, mrow=mrow)
    return pl.pallas_call(
        kern,
        grid=(T // tm, N // tn),
        in_specs=[
            pl.BlockSpec((tm, D), lambda i, j: (i, 0)),
            pl.BlockSpec((None, N_MOD, D), lambda i, j: ((i * tm) // seq, 0, 0)),
            pl.BlockSpec((1, D), lambda i, j: (0, 0)),
            pl.BlockSpec((D, tn), lambda i, j: (0, j)),
        ],
        out_specs=[
            pl.BlockSpec((tm, tn), lambda i, j: (i, j)),
            pl.BlockSpec((tm, D), lambda i, j: (i, 0)),
        ],
        out_shape=[jax.ShapeDtypeStruct((T, N), BF16), jax.ShapeDtypeStruct((T, D), BF16)],
        compiler_params=_cparams("parallel", "arbitrary"),
        name="norm_proj",
    )(h, mod, g.reshape(1, D), w)


def _swa_kernel(sink_ref, q_ref, kv_ref, kvp_ref, o_ref):
    n = pl.program_id(1)
    W = WINDOW
    q = q_ref[...]
    kv = kv_ref[...]
    kvp = kvp_ref[...]
    kk = jnp.concatenate([kvp[:, :W], kv[:, :W]], axis=0)
    vv = jnp.concatenate([kvp[:, W:], kv[:, W:]], axis=0)
    qpos = lax.broadcasted_iota(jnp.int32, (W, 2 * W), 0) + W
    kpos = lax.broadcasted_iota(jnp.int32, (W, 2 * W), 1)
    rel = qpos - kpos
    mask = (rel >= 0) & (rel < W) & ((kpos >= W) | (n > 0))
    group = SWA_HEADS // SWA_KV_HEADS
    for hd in range(SWA_HEADS):
        kvh = hd // group
        k = kk[:, kvh * HEAD_DIM:(kvh + 1) * HEAD_DIM]
        v = vv[:, kvh * HEAD_DIM:(kvh + 1) * HEAD_DIM]
        qh = q[:, hd * HEAD_DIM:(hd + 1) * HEAD_DIM] * (HEAD_DIM ** -0.5)
        logits = lax.dot_general(qh, k, (((1,), (1,)), ((), ())), preferred_element_type=F32)
        logits = jnp.where(mask, logits, -jnp.inf)
        sink = sink_ref[hd]
        m = jnp.maximum(jnp.max(logits, axis=-1, keepdims=True), sink)
        p = jnp.exp(logits - m)
        denom = jnp.sum(p, axis=-1, keepdims=True) + jnp.exp(sink - m)
        w = (p / denom).astype(BF16)
        o = jnp.dot(w, v, preferred_element_type=F32)
        o_ref[:, hd * HEAD_DIM:(hd + 1) * HEAD_DIM] = o.astype(o_ref.dtype)


def swa_attention(proj, sinks, *, batch, seq):
    T = proj.shape[0]
    nb = seq // WINDOW
    qblk = COLB["a_q"] // 512
    kvblk = COLB["a_k"] // 256
    return pl.pallas_call(
        _swa_kernel,
        grid=(batch, nb),
        in_specs=[
            pl.BlockSpec(memory_space=pltpu.SMEM),
            pl.BlockSpec((WINDOW, 512), lambda b, n: (b * nb + n, qblk)),
            pl.BlockSpec((WINDOW, 256), lambda b, n: (b * nb + n, kvblk)),
            pl.BlockSpec((WINDOW, 256), lambda b, n: (b * nb + jnp.maximum(n - 1, 0), kvblk)),
        ],
        out_specs=pl.BlockSpec((WINDOW, 512), lambda b, n: (b * nb + n, 0)),
        out_shape=jax.ShapeDtypeStruct((T, BRANCH_WIDTH), BF16),
        compiler_params=_cparams("parallel", "arbitrary"),
        name="swa_attention",
    )(sinks, proj, proj, proj)


def _fox_cum_kernel(x_ref, b_ref, o_ref, *, seq):
    R = 128
    ri = lax.broadcasted_iota(jnp.int32, (R, R), 0)
    ci = lax.broadcasted_iota(jnp.int32, (R, R), 1)
    tri = (ci <= ri).astype(F32)

    def body(i, carry):
        r0 = pl.multiple_of(i * R, R)
        x = x_ref[pl.ds(r0, R), :][:, :FOX_HEADS] + b_ref[...]
        ls = _log_sigmoid(x)
        cum = jnp.dot(tri, ls, precision=HIGHEST, preferred_element_type=F32) + carry
        o_ref[pl.ds(r0, R), :] = cum
        return cum[R - 1:R, :]

    lax.fori_loop(0, seq // R, body, jnp.zeros((1, FOX_HEADS), F32))


def fox_forget_cumsum(proj, fox_b, *, batch, seq):
    T = proj.shape[0]
    blk = SMALL_COL // 128
    return pl.pallas_call(
        functools.partial(_fox_cum_kernel, seq=seq),
        grid=(batch,),
        in_specs=[
            pl.BlockSpec((seq, 128), lambda b: (b, blk)),
            pl.BlockSpec((1, FOX_HEADS), lambda b: (0, 0)),
        ],
        out_specs=pl.BlockSpec((seq, FOX_HEADS), lambda b: (b, 0)),
        out_shape=jax.ShapeDtypeStruct((T, FOX_HEADS), F32),
        compiler_params=_cparams("parallel"),
        name="fox_forget_cumsum",
    )(proj, fox_b.reshape(1, FOX_HEADS))


def _fox_kernel(q_ref, k_ref, v_ref, fc_ref, fr_ref, o_ref, m_scr, acc_scr, s_scr, *, tq, hp_heads):
    qi = pl.program_id(2)
    tk = tq
    d = HEAD_DIM
    lane = lax.broadcasted_iota(jnp.int32, (tq, hp_heads * d), 1)
    q = q_ref[...] * (d ** -0.5)
    qs = [jnp.where((lane >= hh * d) & (lane < (hh + 1) * d), q, jnp.zeros_like(q)) for hh in range(hp_heads)]
    w = hp_heads * d
    m_scr[...] = jnp.full(m_scr.shape, -jnp.inf, F32)
    acc_scr[...] = jnp.zeros(acc_scr.shape, F32)
    fqs = [jnp.broadcast_to(fc_ref[:, hh:hh + 1], (tq, w)) for hh in range(hp_heads)]
    ones = jnp.ones((tk, w), BF16)

    def scores(kb, slot):
        k0 = pl.multiple_of(kb * tk, tk)
        k = k_ref[pl.ds(k0, tk), :]
        for hh in range(hp_heads):
            s_scr[slot, hh] = lax.dot_general(qs[hh], k, (((1,), (1,)), ((), ())), preferred_element_type=F32)

    def softmax_pv(kb, slot, masked):
        k0 = pl.multiple_of(kb * tk, tk)
        v = jnp.concatenate([v_ref[pl.ds(k0, tk), :], ones], axis=1)
        for hh in range(hp_heads):
            fk = fr_ref[hh:hh + 1, pl.ds(k0, tk)]
            t = s_scr[slot, hh] - fk
            if masked:
                row = lax.broadcasted_iota(jnp.int32, (tq, tk), 0)
                col = lax.broadcasted_iota(jnp.int32, (tq, tk), 1)
                t = jnp.where(col <= row, t, -jnp.inf)
            m_old = m_scr[hh]
            m_new = jnp.maximum(m_old, fqs[hh] + jnp.max(t, axis=-1, keepdims=True))
            alpha = jnp.exp(m_old - m_new)
            c = fqs[hh] - m_new
            p = jnp.exp(t + jnp.concatenate([c] * (tk // w), axis=1))
            acc_scr[hh] = (jnp.concatenate([alpha, alpha], axis=1) * acc_scr[hh]
                           + jnp.dot(p.astype(BF16), v, preferred_element_type=F32))
            m_scr[hh] = m_new

    scores(0, 0)

    def body(kb, carry):
        slot = lax.rem(kb, 2)
        softmax_pv(kb, slot, False)
        scores(kb + 1, 1 - slot)
        return carry

    lax.fori_loop(0, qi, body, 0)
    softmax_pv(qi, lax.rem(qi, 2), True)
    out = None
    for hh in range(hp_heads):
        o = acc_scr[hh, :, :w] / acc_scr[hh, :, w:]
        out = o if out is None else jnp.where(lane >= hh * d, o, out)
    o_ref[...] = out.astype(o_ref.dtype)


def fox_attention(proj, fc, fr, *, batch, seq):
    T = proj.shape[0]
    tq = min(512, seq)
    nq = seq // tq
    hp_heads = 2
    n_hp = FOX_HEADS // hp_heads
    qblk, kblk, vblk = COLB["f_q"] // 128, COLB["f_k"] // 128, COLB["f_v"] // 128
    kern = functools.partial(_fox_kernel, tq=tq, hp_heads=hp_heads)
    return pl.pallas_call(
        kern,
        grid=(batch, n_hp, nq),
        scratch_shapes=[pltpu.VMEM((hp_heads, tq, hp_heads * HEAD_DIM), F32),
                        pltpu.VMEM((hp_heads, tq, 2 * hp_heads * HEAD_DIM), F32),
                        pltpu.VMEM((2, hp_heads, tq, tq), F32)],
        in_specs=[
            pl.BlockSpec((tq, 128), lambda b, hp, qi: (b * nq + qi, qblk + hp)),
            pl.BlockSpec((seq, 128), lambda b, hp, qi: (b, kblk + hp)),
            pl.BlockSpec((seq, 128), lambda b, hp, qi: (b, vblk + hp)),
            pl.BlockSpec((None, None, tq, hp_heads), lambda b, hp, qi: (b, hp, qi, 0)),
            pl.BlockSpec((None, None, hp_heads, seq), lambda b, hp, qi: (b, hp, 0, 0)),
        ],
        out_specs=pl.BlockSpec((tq, 128), lambda b, hp, qi: (b * nq + qi, hp)),
        out_shape=jax.ShapeDtypeStruct((T, BRANCH_WIDTH), BF16),
        compiler_params=_cparams("parallel", "parallel", "arbitrary"),
        name="fox_attention",
    )(proj, proj, proj, fc, fr)


def _rotate_half(x, neg_first_half):
    n = x.shape[-1]
    half = HEAD_DIM // 2
    fwd = pltpu.roll(x, half, 1)
    bwd = pltpu.roll(x, n - half, 1)
    return jnp.where(neg_first_half, -bwd, fwd)


def _linear_attn_kernel(*refs, mode, cb):
    if mode == "gla":
        (q_ref, k_ref, v_ref, gate_ref, small_ref, wg_ref, bg_ref, ng_ref, o_ref, st_ref) = refs
    else:
        (q_ref, k_ref, v_ref, gate_ref, lg_ref, cos_ref, sin_ref, gw_ref, gb_ref, o_ref, st_ref) = refs
    H, DK, DV, C = GLA_HEADS, GLA_DK, GLA_DV, CHUNK

    @pl.when(pl.program_id(1) == 0)
    def _():
        st_ref[...] = jnp.zeros_like(st_ref)

    ri = lax.broadcasted_iota(jnp.int32, (C, C), 0)
    ci = lax.broadcasted_iota(jnp.int32, (C, C), 1)
    causal = ci <= ri
    if mode == "gla":
        tri = causal.astype(F32)
    else:
        lane = lax.broadcasted_iota(jnp.int32, (C, H * DK), 1)
        first_half = (lane % HEAD_DIM) < (HEAD_DIM // 2)
        steps = (lax.broadcasted_iota(jnp.int32, (C, H * DK), 0) + 1).astype(F32)

    for c in range(cb // C):
        rows = slice(c * C, (c + 1) * C)
        q = q_ref[rows, :].astype(F32)
        k = k_ref[rows, :].astype(F32)
        if mode == "gla":
            glr = small_ref[rows, :][:, FOX_HEADS:FOX_HEADS + GLA_GATE_RANK]
            z = jnp.dot(glr, wg_ref[...], precision=HIGHEST, preferred_element_type=F32) + bg_ref[...]
            ld = _log_sigmoid(z) / GLA_TAU
            cum = jnp.dot(tri, ld, precision=HIGHEST, preferred_element_type=F32)
        else:
            cos = cos_ref[rows, :]
            sin = sin_ref[rows, :]
            q = q * cos + _rotate_half(q, first_half) * sin
            k = k * cos + _rotate_half(k, first_half) * sin
            cum = steps * lg_ref[...]
        last = cum[C - 1:C, :]
        q_in = (q * (DK ** -0.5) * jnp.exp(cum)).astype(BF16)
        k_in = (k * jnp.exp(-cum)).astype(BF16)
        k_st = (k * jnp.exp(last - cum)).astype(BF16)
        dec = jnp.exp(last)
        for hd in range(H):
            ks = slice(hd * DK, (hd + 1) * DK)
            vs = slice(hd * DV, (hd + 1) * DV)
            v = v_ref[rows, vs]
            sc = lax.dot_general(q_in[:, ks], k_in[:, ks], (((1,), (1,)), ((), ())), preferred_element_type=F32)
            sc = jnp.where(causal, sc, 0.0).astype(BF16)
            st = st_ref[hd]
            o = jnp.dot(sc, v, preferred_element_type=F32)
            o = o + lax.dot_general(q_in[:, ks], st.astype(BF16), (((1,), (1,)), ((), ())),
                                    preferred_element_type=F32)
            upd = lax.dot_general(v, k_st[:, ks], (((0,), (0,)), ((), ())), preferred_element_type=F32)
            st_ref[hd] = st * dec[:, ks] + upd
            gate = _silu(gate_ref[rows, vs])
            if mode == "gla":
                y = o * lax.rsqrt(jnp.mean(o * o, axis=-1, keepdims=True) + EPS) * ng_ref[...]
            else:
                mu = jnp.mean(o, axis=-1, keepdims=True)
                var = jnp.mean(jnp.square(o - mu), axis=-1, keepdims=True)
                y = (o - mu) * lax.rsqrt(var + EPS) * gw_ref[:, vs] + gb_ref[:, vs]
            o_ref[rows, vs] = (y * gate).astype(o_ref.dtype)


def linear_attention(proj, projf, params, *, mode, batch, seq):
    T = proj.shape[0]
    cb = min(512, seq)
    nc = seq // cb
    pre = "g" if mode == "gla" else "r"
    qblk, kblk = COLB[pre + "_q"] // 256, COLB[pre + "_k"] // 256
    vblk = COLB[pre + "_v"] // 512
    gblk = COLF["g_r" if mode == "gla" else "r_g"] // 512
    row = lambda b, c: b * nc + c
    in_specs = [
        pl.BlockSpec((cb, 256), lambda b, c: (row(b, c), qblk)),
        pl.BlockSpec((cb, 256), lambda b, c: (row(b, c), kblk)),
        pl.BlockSpec((cb, 512), lambda b, c: (row(b, c), vblk)),
        pl.BlockSpec((cb, 512), lambda b, c: (row(b, c), gblk)),
    ]
    args = [proj, proj, proj, projf]
    if mode == "gla":
        wg, bg, ng = params
        in_specs += [
            pl.BlockSpec((cb, 128), lambda b, c: (row(b, c), SMALL_COL // 128)),
            pl.BlockSpec(wg.shape, lambda b, c: (0, 0)),
            pl.BlockSpec((1, bg.shape[-1]), lambda b, c: (0, 0)),
            pl.BlockSpec((1, ng.shape[-1]), lambda b, c: (0, 0)),
        ]
        args += [projf, wg, bg.reshape(1, -1), ng.reshape(1, -1)]
    else:
        lg, cos, sin, gw, gb = params
        in_specs += [
            pl.BlockSpec((1, lg.shape[-1]), lambda b, c: (0, 0)),
            pl.BlockSpec((cb, 256), lambda b, c: (c, 0)),
            pl.BlockSpec((cb, 256), lambda b, c: (c, 0)),
            pl.BlockSpec((1, gw.shape[-1]), lambda b, c: (0, 0)),
            pl.BlockSpec((1, gb.shape[-1]), lambda b, c: (0, 0)),
        ]
        args += [lg.reshape(1, -1), cos, sin, gw.reshape(1, -1), gb.reshape(1, -1)]
    return pl.pallas_call(
        functools.partial(_linear_attn_kernel, mode=mode, cb=cb),
        grid=(batch, nc),
        in_specs=in_specs,
        out_specs=pl.BlockSpec((cb, 512), lambda b, c: (row(b, c), 0)),
        out_shape=jax.ShapeDtypeStruct((T, BRANCH_WIDTH), BF16),
        scratch_shapes=[pltpu.VMEM((GLA_HEADS, GLA_DV, GLA_DK), F32)],
        compiler_params=_cparams("parallel", "arbitrary"),
        name="linear_attention_" + mode,
    )(*args)


def _retention_tables(seq):
    half = RET_DK // 2
    inv = RET_THETA_BASE ** (-jnp.arange(half, dtype=F32) / half)
    ang = jnp.arange(seq).astype(F32)[:, None] * inv[None, :]
    cos = jnp.tile(jnp.cos(ang), (1, 2 * RET_HEADS))
    sin = jnp.tile(jnp.sin(ang), (1, 2 * RET_HEADS))
    log_gamma = jnp.log1p(-jnp.exp2(-5.0 - jnp.arange(RET_HEADS, dtype=F32)))
    return jnp.repeat(log_gamma, RET_DK), cos, sin


def _merge_kernel(u_ref, oa_ref, ob_ref, oc_ref, od_ref, wm_ref, wb_ref, bm_ref, m_ref):
    u = u_ref[...]
    acc = None
    for i, o_ref in enumerate((oa_ref, ob_ref, oc_ref, od_ref)):
        z = jnp.dot(u, wm_ref[i], preferred_element_type=F32) + bm_ref[i]
        pr = jnp.dot(o_ref[...], wb_ref[i], preferred_element_type=F32)
        t = jax.nn.sigmoid(z) * pr
        acc = t if acc is None else acc + t
    m_ref[...] = acc.astype(m_ref.dtype)


def merge_branches(u, branches, wm, wb, bm):
    T, D = u.shape
    tm, tn = min(512, T), min(512, D)
    ospec = pl.BlockSpec((tm, BRANCH_WIDTH), lambda i, j: (i, 0))
    return pl.pallas_call(
        _merge_kernel,
        grid=(T // tm, D // tn),
        in_specs=[
            pl.BlockSpec((tm, D), lambda i, j: (i, 0)),
            ospec, ospec, ospec, ospec,
            pl.BlockSpec((N_BRANCH, D, tn), lambda i, j: (0, 0, j)),
            pl.BlockSpec((N_BRANCH, BRANCH_WIDTH, tn), lambda i, j: (0, 0, j)),
            pl.BlockSpec((N_BRANCH, 1, tn), lambda i, j: (0, 0, j)),
        ],
        out_specs=pl.BlockSpec((tm, tn), lambda i, j: (i, j)),
        out_shape=jax.ShapeDtypeStruct((T, D), BF16),
        compiler_params=_cparams("parallel", "arbitrary"),
        name="merge_branches",
    )(u, *branches, wm, wb, bm.reshape(N_BRANCH, 1, D))


def _out_proj_kernel(h_ref, m_ref, w_ref, mod_ref, o_ref, *, mrow, tn):
    j = pl.program_id(1)
    c0 = pl.multiple_of(j * tn, tn)
    gate = mod_ref[mrow:mrow + 1, pl.ds(c0, tn)]
    o_ref[...] = h_ref[...] + gate * jnp.dot(m_ref[...], w_ref[...], preferred_element_type=F32)


def out_proj(h, merged, w, mod, *, seq, mrow):
    T, D = h.shape
    tm, tn = min(1024, seq), min(512, D)
    return pl.pallas_call(
        functools.partial(_out_proj_kernel, mrow=mrow, tn=tn),
        grid=(T // tm, D // tn),
        in_specs=[
            pl.BlockSpec((tm, tn), lambda i, j: (i, j)),
            pl.BlockSpec((tm, D), lambda i, j: (i, 0)),
            pl.BlockSpec((D, tn), lambda i, j: (0, j)),
            pl.BlockSpec((None, N_MOD, D), lambda i, j: ((i * tm) // seq, 0, 0)),
        ],
        out_specs=pl.BlockSpec((tm, tn), lambda i, j: (i, j)),
        out_shape=jax.ShapeDtypeStruct((T, D), F32),
        compiler_params=_cparams("parallel", "arbitrary"),
        name="out_proj",
    )(h, merged, w, mod)


def _final_norm_kernel(h_ref, g_ref, o_ref):
    x = h_ref[...]
    ms = jnp.mean(x * x, axis=-1, keepdims=True)
    o_ref[...] = x * lax.rsqrt(ms + EPS) * g_ref[...]


def final_norm(h, g):
    T, D = h.shape
    tm = min(1024, T)
    return pl.pallas_call(
        _final_norm_kernel,
        grid=(T // tm,),
        in_specs=[pl.BlockSpec((tm, D), lambda i: (i, 0)), pl.BlockSpec((1, D), lambda i: (0, 0))],
        out_specs=pl.BlockSpec((tm, D), lambda i: (i, 0)),
        out_shape=jax.ShapeDtypeStruct((T, D), F32),
        compiler_params=_cparams("parallel"),
        name="final_norm",
    )(h, g.reshape(1, D))


def _permuted_w_in(w_in_l, perm, cols):
    D = w_in_l.shape[0]
    w = jnp.take(w_in_l, perm, axis=1).astype(BF16)
    return jnp.concatenate([w, jnp.zeros((D, cols - perm.shape[0]), BF16)], axis=1)


def kernel(x, c, w_ada, b_ada, norm_g, ffn_w_gate, ffn_w_up, ffn_w_down, w_in, fox_b_forget, attn_sinks,
           gla_w_gate, gla_b_gate, gla_norm_g, ret_gn_w, ret_gn_b, w_branch, w_merge, b_merge, w_out,
           final_norm_g):
    B, S, D = x.shape
    L = w_ada.shape[0]
    T = B * S
    mod_all = ada_modulation(c, w_ada, b_ada)
    log_gamma, cos, sin = _retention_tables(S)
    h = x.reshape(T, D)
    for l in range(L):
        mod = mod_all[l]
        h = ffn_block(h, mod, norm_g[l, 0], ffn_w_gate[l, 0].astype(BF16), ffn_w_up[l, 0].astype(BF16),
                      ffn_w_down[l, 0].astype(BF16), seq=S, mrow=0, res_scale=0.5)
        proj, u = norm_proj(h, mod, norm_g[l, 1], _permuted_w_in(w_in[l], _PERM_B, PROJ_B_COLS), seq=S, mrow=3)
        projf = matmul(u, _permuted_w_in(w_in[l], _PERM_F, PROJ_F_COLS), F32)
        o_a = swa_attention(proj, attn_sinks[l], batch=B, seq=S)
        fc = fox_forget_cumsum(projf, fox_b_forget[l], batch=B, seq=S)
        fc4 = fc.reshape(B, S, FOX_HEADS // 2, 2)
        o_b = fox_attention(proj, fc4.transpose(0, 2, 1, 3), fc4.transpose(0, 2, 3, 1), batch=B, seq=S)
        o_c = linear_attention(proj, projf, (gla_w_gate[l], gla_b_gate[l], gla_norm_g[l]), mode="gla",
                               batch=B, seq=S)
        o_d = linear_attention(proj, projf, (log_gamma, cos, sin, ret_gn_w[l], ret_gn_b[l]), mode="ret",
                               batch=B, seq=S)
        merged = merge_branches(u, (o_a, o_b, o_c, o_d), w_merge[l].astype(BF16), w_branch[l].astype(BF16),
                                b_merge[l])
        h = out_proj(h, merged, w_out[l].astype(BF16), mod, seq=S, mrow=5)
        h = ffn_block(h, mod, norm_g[l, 2], ffn_w_gate[l, 1].astype(BF16), ffn_w_up[l, 1].astype(BF16),
                      ffn_w_down[l, 1].astype(BF16), seq=S, mrow=6, res_scale=0.5)
    return final_norm(h, final_norm_g).reshape(B, S, D)
```

```python
import functools

import numpy as np
import jax
import jax.numpy as jnp
from jax import lax
from jax.experimental import pallas as pl
from jax.experimental.pallas import tpu as pltpu

HEAD_DIM = 64
SWA_HEADS = 8
SWA_KV_HEADS = 2
WINDOW = 128
FOX_HEADS = 8
GLA_HEADS = 4
GLA_DK = 64
GLA_DV = 128
GLA_GATE_RANK = 16
GLA_TAU = 16.0
RET_HEADS = 4
RET_DK = 64
RET_DV = 128
RET_THETA_BASE = 10000.0
CHUNK = 64
BRANCH_WIDTH = 512
N_BRANCH = 4
N_MOD = 9
EPS = 1e-6

BF16 = jnp.bfloat16
F32 = jnp.float32
HIGHEST = lax.Precision.HIGHEST

VMEM_LIMIT_BYTES = 60 * 1024 * 1024

_REF_COLS = {}
_off = 0
for _name, _size in (
        ("a_q", 512), ("a_k", 128), ("a_v", 128),
        ("f_q", 512), ("f_k", 512), ("f_v", 512), ("f_f", 8),
        ("g_q", 256), ("g_k", 256), ("g_v", 512), ("g_lr", 16), ("g_r", 512),
        ("r_q", 256), ("r_k", 256), ("r_v", 512), ("r_g", 512)):
    _REF_COLS[_name] = (_off, _size)
    _off += _size
IN_COLS = _off


def _layout(order):
    col, off = {}, 0
    for name in order:
        col[name] = off
        off += _REF_COLS[name][1]
    return col


_ORDER_B = ("a_q", "f_q", "f_k", "f_v", "g_v", "r_v", "g_q", "g_k", "r_q", "r_k", "a_k", "a_v")
_ORDER_F = ("g_r", "r_g", "f_f", "g_lr")
COLB = _layout(_ORDER_B)
COLF = _layout(_ORDER_F)
PROJ_B_COLS = 4608
PROJ_F_COLS = 1152
SMALL_COL = COLF["f_f"]


def _cparams(*sem):
    return pltpu.CompilerParams(dimension_semantics=sem, vmem_limit_bytes=VMEM_LIMIT_BYTES)


def _tile(n, pref):
    t = (min(pref, n) // 128) * 128
    while t >= 128:
        if n % t == 0:
            return t
        t -= 128
    return n


def _adanorm(x, g, shift, scale):
    ms = jnp.mean(x * x, axis=-1, keepdims=True)
    y = x * lax.rsqrt(ms + EPS) * g
    return y * (1.0 + scale) + shift


def _log_sigmoid(x):
    return jnp.minimum(x, 0.0) - jnp.log1p(jnp.exp(-jnp.abs(x)))


def _silu(x):
    return x * jax.nn.sigmoid(x)


def _ada_kernel(c_ref, w_ref, b_ref, o_ref):
    c = c_ref[...]
    cond = _silu(c)
    o_ref[...] = jnp.dot(cond, w_ref[...], precision=HIGHEST, preferred_element_type=F32) + b_ref[...]


def ada_modulation(c, w_ada, b_ada):
    L, D, N = w_ada.shape
    B = c.shape[0]
    BP = 8
    cp = jnp.zeros((BP, D), F32).at[:B].set(c)
    tn = _tile(N, 1024)
    out = pl.pallas_call(
        _ada_kernel,
        grid=(L, N // tn),
        in_specs=[
            pl.BlockSpec((BP, D), lambda l, j: (0, 0)),
            pl.BlockSpec((None, D, tn), lambda l, j: (l, 0, j)),
            pl.BlockSpec((None, 1, tn), lambda l, j: (l, 0, j)),
        ],
        out_specs=pl.BlockSpec((None, BP, tn), lambda l, j: (l, 0, j)),
        out_shape=jax.ShapeDtypeStruct((L, BP, N), F32),
        compiler_params=_cparams("parallel", "parallel"),
        name="ada_modulation",
    )(cp, w_ada, b_ada.reshape(L, 1, N))
    return out[:, :B].reshape(L, B, N_MOD, D)


def _ffn_kernel(h_ref, mod_ref, g_ref, wg_ref, wu_ref, wd_ref, o_ref, u_scr, *, mrow, res_scale):
    f = pl.program_id(1)

    @pl.when(f == 0)
    def _():
        u = _adanorm(h_ref[...], g_ref[...], mod_ref[mrow:mrow + 1, :], mod_ref[mrow + 1:mrow + 2, :])
        u_scr[...] = u.astype(BF16)
        o_ref[...] = jnp.zeros_like(o_ref)

    u = u_scr[...]
    a = jnp.dot(u, wg_ref[...].astype(BF16), preferred_element_type=F32)
    b = jnp.dot(u, wu_ref[...].astype(BF16), preferred_element_type=F32)
    mid = (_silu(a) * b).astype(BF16)
    o_ref[...] += jnp.dot(mid, wd_ref[...].astype(BF16), preferred_element_type=F32)

    @pl.when(f == pl.num_programs(1) - 1)
    def _():
        o_ref[...] = h_ref[...] + (res_scale * mod_ref[mrow + 2:mrow + 3, :]) * o_ref[...]


def ffn_block(h, mod, g, wg, wu, wd, *, layer, which, seq, mrow, res_scale):
    T, D = h.shape
    F = wg.shape[-1]
    tm = min(1024, seq)
    tf = _tile(F, 256)
    kern = functools.partial(_ffn_kernel, mrow=mrow, res_scale=res_scale)
    return pl.pallas_call(
        kern,
        grid=(T // tm, F // tf),
        in_specs=[
            pl.BlockSpec((tm, D), lambda i, f: (i, 0)),
            pl.BlockSpec((None, N_MOD, D), lambda i, f: ((i * tm) // seq, 0, 0)),
            pl.BlockSpec((1, D), lambda i, f: (0, 0)),
            pl.BlockSpec((None, None, D, tf), lambda i, f: (layer, which, 0, f)),
            pl.BlockSpec((None, None, D, tf), lambda i, f: (layer, which, 0, f)),
            pl.BlockSpec((None, None, tf, D), lambda i, f: (layer, which, f, 0)),
        ],
        out_specs=pl.BlockSpec((tm, D), lambda i, f: (i, 0)),
        out_shape=jax.ShapeDtypeStruct((T, D), F32),
        scratch_shapes=[pltpu.VMEM((tm, D), BF16)],
        compiler_params=_cparams("parallel", "arbitrary"),
        name="ffn_block",
    )(h, mod, g.reshape(1, D), wg, wu, wd)


def _norm_proj_kernel(h_ref, mod_ref, g_ref, w_ref, p_ref, u_ref, *, mrow):
    @pl.when(pl.program_id(1) == 0)
    def _():
        u = _adanorm(h_ref[...], g_ref[...], mod_ref[mrow:mrow + 1, :], mod_ref[mrow + 1:mrow + 2, :])
        u_ref[...] = u.astype(BF16)

    p_ref[...] = jnp.dot(u_ref[...], w_ref[...], preferred_element_type=F32).astype(p_ref.dtype)


def _matmul_kernel(a_ref, w_ref, o_ref):
    o_ref[...] = jnp.dot(a_ref[...], w_ref[...], preferred_element_type=F32).astype(o_ref.dtype)


def matmul(a, w, out_dtype):
    T, D = a.shape
    N = w.shape[1]
    tm = min(1024, T)
    return pl.pallas_call(
        _matmul_kernel,
        grid=(T // tm,),
        in_specs=[pl.BlockSpec((tm, D), lambda i: (i, 0)), pl.BlockSpec((D, N), lambda i: (0, 0))],
        out_specs=pl.BlockSpec((tm, N), lambda i: (i, 0)),
        out_shape=jax.ShapeDtypeStruct((T, N), out_dtype),
        compiler_params=_cparams("parallel"),
        name="matmul",
    )(a, w)


def norm_proj(h, mod, g, w, *, seq, mrow):
    T, D = h.shape
    N = w.shape[1]
    tm = min(1024, seq)
    tn = 512
    kern = functools.partial(_norm_proj_kernel, mrow=mrow)
    return pl.pallas_call(
        kern,
        grid=(T // tm, N // tn),
        in_specs=[
            pl.BlockSpec((tm, D), lambda i, j: (i, 0)),
            pl.BlockSpec((None, N_MOD, D), lambda i, j: ((i * tm) // seq, 0, 0)),
            pl.BlockSpec((1, D), lambda i, j: (0, 0)),
            pl.BlockSpec((D, tn), lambda i, j: (0, j)),
        ],
        out_specs=[
            pl.BlockSpec((tm, tn), lambda i, j: (i, j)),
            pl.BlockSpec((tm, D), lambda i, j: (i, 0)),
        ],
        out_shape=[jax.ShapeDtypeStruct((T, N), BF16), jax.ShapeDtypeStruct((T, D), BF16)],
        compiler_params=_cparams("parallel", "arbitrary"),
        name="norm_proj",
    )(h, mod, g.reshape(1, D), w)


def _swa_kernel(sink_ref, q_ref, kv_ref, kvp_ref, o_ref):
    n = pl.program_id(1)
    W = WINDOW
    q = q_ref[...]
    kv = kv_ref[...]
    kvp = kvp_ref[...]
    kk = jnp.concatenate([kvp[:, :W], kv[:, :W]], axis=0)
    vv = jnp.concatenate([kvp[:, W:], kv[:, W:]], axis=0)
    qpos = lax.broadcasted_iota(jnp.int32, (W, 2 * W), 0) + W
    kpos = lax.broadcasted_iota(jnp.int32, (W, 2 * W), 1)
    rel = qpos - kpos
    mask = (rel >= 0) & (rel < W) & ((kpos >= W) | (n > 0))
    group = SWA_HEADS // SWA_KV_HEADS
    for hd in range(SWA_HEADS):
        kvh = hd // group
        k = kk[:, kvh * HEAD_DIM:(kvh + 1) * HEAD_DIM]
        v = vv[:, kvh * HEAD_DIM:(kvh + 1) * HEAD_DIM]
        qh = q[:, hd * HEAD_DIM:(hd + 1) * HEAD_DIM] * (HEAD_DIM ** -0.5)
        logits = lax.dot_general(qh, k, (((1,), (1,)), ((), ())), preferred_element_type=F32)
        logits = jnp.where(mask, logits, -jnp.inf)
        sink = sink_ref[hd]
        m = jnp.maximum(jnp.max(logits, axis=-1, keepdims=True), sink)
        p = jnp.exp(logits - m)
        denom = jnp.sum(p, axis=-1, keepdims=True) + jnp.exp(sink - m)
        w = (p / denom).astype(BF16)
        o = jnp.dot(w, v, preferred_element_type=F32)
        o_ref[:, hd * HEAD_DIM:(hd + 1) * HEAD_DIM] = o.astype(o_ref.dtype)


def swa_attention(proj, sinks, *, batch, seq):
    T = proj.shape[0]
    nb = seq // WINDOW
    qblk = COLB["a_q"] // 512
    kvblk = COLB["a_k"] // 256
    return pl.pallas_call(
        _swa_kernel,
        grid=(batch, nb),
        in_specs=[
            pl.BlockSpec(memory_space=pltpu.SMEM),
            pl.BlockSpec((WINDOW, 512), lambda b, n: (b * nb + n, qblk)),
            pl.BlockSpec((WINDOW, 256), lambda b, n: (b * nb + n, kvblk)),
            pl.BlockSpec((WINDOW, 256), lambda b, n: (b * nb + jnp.maximum(n - 1, 0), kvblk)),
        ],
        out_specs=pl.BlockSpec((WINDOW, 512), lambda b, n: (b * nb + n, 0)),
        out_shape=jax.ShapeDtypeStruct((T, BRANCH_WIDTH), BF16),
        compiler_params=_cparams("parallel", "arbitrary"),
        name="swa_attention",
    )(sinks, proj, proj, proj)


def _fox_cum_kernel(x_ref, b_ref, o_ref, *, seq):
    R = 128
    ri = lax.broadcasted_iota(jnp.int32, (R, R), 0)
    ci = lax.broadcasted_iota(jnp.int32, (R, R), 1)
    tri = (ci <= ri).astype(F32)

    def body(i, carry):
        r0 = pl.multiple_of(i * R, R)
        x = x_ref[pl.ds(r0, R), :][:, :FOX_HEADS] + b_ref[...]
        ls = _log_sigmoid(x)
        cum = jnp.dot(tri, ls, precision=HIGHEST, preferred_element_type=F32) + carry
        o_ref[pl.ds(r0, R), :] = cum
        return cum[R - 1:R, :]

    lax.fori_loop(0, seq // R, body, jnp.zeros((1, FOX_HEADS), F32))


def fox_forget_cumsum(proj, fox_b, *, batch, seq):
    T = proj.shape[0]
    blk = SMALL_COL // 128
    return pl.pallas_call(
        functools.partial(_fox_cum_kernel, seq=seq),
        grid=(batch,),
        in_specs=[
            pl.BlockSpec((seq, 128), lambda b: (b, blk)),
            pl.BlockSpec((1, FOX_HEADS), lambda b: (0, 0)),
        ],
        out_specs=pl.BlockSpec((seq, FOX_HEADS), lambda b: (b, 0)),
        out_shape=jax.ShapeDtypeStruct((T, FOX_HEADS), F32),
        compiler_params=_cparams("parallel"),
        name="fox_forget_cumsum",
    )(proj, fox_b.reshape(1, FOX_HEADS))


def _fox_kernel(q_ref, k_ref, v_ref, fc_ref, fr_ref, o_ref, m_scr, acc_scr, s_scr, *, tq, hp_heads):
    qi = pl.program_id(2)
    tk = tq
    d = HEAD_DIM
    lane = lax.broadcasted_iota(jnp.int32, (tq, hp_heads * d), 1)
    q = q_ref[...] * (d ** -0.5)
    qs = [jnp.where((lane >= hh * d) & (lane < (hh + 1) * d), q, jnp.zeros_like(q)) for hh in range(hp_heads)]
    w = hp_heads * d
    m_scr[...] = jnp.full(m_scr.shape, -jnp.inf, F32)
    acc_scr[...] = jnp.zeros(acc_scr.shape, F32)
    fqs = [jnp.broadcast_to(fc_ref[:, hh:hh + 1], (tq, w)) for hh in range(hp_heads)]
    ones = jnp.ones((tk, w), BF16)

    def scores(kb, slot):
        k0 = pl.multiple_of(kb * tk, tk)
        k = k_ref[pl.ds(k0, tk), :]
        for hh in range(hp_heads):
            s_scr[slot, hh] = lax.dot_general(qs[hh], k, (((1,), (1,)), ((), ())), preferred_element_type=F32)

    def softmax_pv(kb, slot, masked):
        k0 = pl.multiple_of(kb * tk, tk)
        v = jnp.concatenate([v_ref[pl.ds(k0, tk), :], ones], axis=1)
        for hh in range(hp_heads):
            fk = fr_ref[hh:hh + 1, pl.ds(k0, tk)]
            t = s_scr[slot, hh] - fk
            if masked:
                row = lax.broadcasted_iota(jnp.int32, (tq, tk), 0)
                col = lax.broadcasted_iota(jnp.int32, (tq, tk), 1)
                t = jnp.where(col <= row, t, -jnp.inf)
            m_old = m_scr[hh]
            m_new = jnp.maximum(m_old, fqs[hh] + jnp.max(t, axis=-1, keepdims=True))
            alpha = jnp.exp(m_old - m_new)
            c = fqs[hh] - m_new
            p = jnp.exp(t + jnp.concatenate([c] * (tk // w), axis=1))
            acc_scr[hh] = (jnp.concatenate([alpha, alpha], axis=1) * acc_scr[hh]
                           + jnp.dot(p.astype(BF16), v, preferred_element_type=F32))
            m_scr[hh] = m_new

    scores(0, 0)

    def body(kb, carry):
        slot = lax.rem(kb, 2)
        softmax_pv(kb, slot, False)
        scores(kb + 1, 1 - slot)
        return carry

    lax.fori_loop(0, qi, body, 0)
    softmax_pv(qi, lax.rem(qi, 2), True)
    out = None
    for hh in range(hp_heads):
        o = acc_scr[hh, :, :w] / acc_scr[hh, :, w:]
        out = o if out is None else jnp.where(lane >= hh * d, o, out)
    o_ref[...] = out.astype(o_ref.dtype)


def fox_attention(proj, fc, fr, *, batch, seq):
    T = proj.shape[0]
    tq = min(512, seq)
    nq = seq // tq
    hp_heads = 2
    n_hp = FOX_HEADS // hp_heads
    qblk, kblk, vblk = COLB["f_q"] // 128, COLB["f_k"] // 128, COLB["f_v"] // 128
    kern = functools.partial(_fox_kernel, tq=tq, hp_heads=hp_heads)
    return pl.pallas_call(
        kern,
        grid=(batch, n_hp, nq),
        scratch_shapes=[pltpu.VMEM((hp_heads, tq, hp_heads * HEAD_DIM), F32),
                        pltpu.VMEM((hp_heads, tq, 2 * hp_heads * HEAD_DIM), F32),
                        pltpu.VMEM((2, hp_heads, tq, tq), F32)],
        in_specs=[
            pl.BlockSpec((tq, 128), lambda b, hp, qi: (b * nq + qi, qblk + hp)),
            pl.BlockSpec((seq, 128), lambda b, hp, qi: (b, kblk + hp)),
            pl.BlockSpec((seq, 128), lambda b, hp, qi: (b, vblk + hp)),
            pl.BlockSpec((None, None, tq, hp_heads), lambda b, hp, qi: (b, hp, qi, 0)),
            pl.BlockSpec((None, None, hp_heads, seq), lambda b, hp, qi: (b, hp, 0, 0)),
        ],
        out_specs=pl.BlockSpec((tq, 128), lambda b, hp, qi: (b * nq + qi, hp)),
        out_shape=jax.ShapeDtypeStruct((T, BRANCH_WIDTH), BF16),
        compiler_params=_cparams("parallel", "parallel", "arbitrary"),
        name="fox_attention",
    )(proj, proj, proj, fc, fr)


def _rotate_half(x, neg_first_half):
    n = x.shape[-1]
    half = HEAD_DIM // 2
    fwd = pltpu.roll(x, half, 1)
    bwd = pltpu.roll(x, n - half, 1)
    return jnp.where(neg_first_half, -bwd, fwd)


def _linear_attn_kernel(*refs, mode, cb):
    if mode == "gla":
        (q_ref, k_ref, v_ref, gate_ref, small_ref, wg_ref, bg_ref, ng_ref, o_ref, st_ref) = refs
    else:
        (q_ref, k_ref, v_ref, gate_ref, lg_ref, cos_ref, sin_ref, gw_ref, gb_ref, o_ref, st_ref) = refs
    H, DK, DV, C = GLA_HEADS, GLA_DK, GLA_DV, CHUNK

    @pl.when(pl.program_id(1) == 0)
    def _():
        st_ref[...] = jnp.zeros_like(st_ref)

    ri = lax.broadcasted_iota(jnp.int32, (C, C), 0)
    ci = lax.broadcasted_iota(jnp.int32, (C, C), 1)
    causal = ci <= ri
    if mode == "gla":
        tri = causal.astype(F32)
    else:
        lane = lax.broadcasted_iota(jnp.int32, (C, H * DK), 1)
        first_half = (lane % HEAD_DIM) < (HEAD_DIM // 2)
        steps = (lax.broadcasted_iota(jnp.int32, (C, H * DK), 0) + 1).astype(F32)

    for c in range(cb // C):
        rows = slice(c * C, (c + 1) * C)
        q = q_ref[rows, :].astype(F32)
        k = k_ref[rows, :].astype(F32)
        if mode == "gla":
            glr = small_ref[rows, :][:, FOX_HEADS:FOX_HEADS + GLA_GATE_RANK]
            z = jnp.dot(glr, wg_ref[...], precision=HIGHEST, preferred_element_type=F32) + bg_ref[...]
            ld = _log_sigmoid(z) / GLA_TAU
            cum = jnp.dot(tri, ld, precision=HIGHEST, preferred_element_type=F32)
        else:
            cos = cos_ref[rows, :]
            sin = sin_ref[rows, :]
            q = q * cos + _rotate_half(q, first_half) * sin
            k = k * cos + _rotate_half(k, first_half) * sin
            cum = steps * lg_ref[...]
        last = cum[C - 1:C, :]
        q_in = (q * (DK ** -0.5) * jnp.exp(cum)).astype(BF16)
        k_in = (k * jnp.exp(-cum)).astype(BF16)
        k_st = (k * jnp.exp(last - cum)).astype(BF16)
        dec = jnp.exp(last)
        for hd in range(H):
            ks = slice(hd * DK, (hd + 1) * DK)
            vs = slice(hd * DV, (hd + 1) * DV)
            v = v_ref[rows, vs]
            sc = lax.dot_general(q_in[:, ks], k_in[:, ks], (((1,), (1,)), ((), ())), preferred_element_type=F32)
            sc = jnp.where(causal, sc, 0.0).astype(BF16)
            st = st_ref[hd]
            o = jnp.dot(sc, v, preferred_element_type=F32)
            o = o + lax.dot_general(q_in[:, ks], st.astype(BF16), (((1,), (1,)), ((), ())),
                                    preferred_element_type=F32)
            upd = lax.dot_general(v, k_st[:, ks], (((0,), (0,)), ((), ())), preferred_element_type=F32)
            st_ref[hd] = st * dec[:, ks] + upd
            gate = _silu(gate_ref[rows, vs])
            if mode == "gla":
                y = o * lax.rsqrt(jnp.mean(o * o, axis=-1, keepdims=True) + EPS) * ng_ref[...]
            else:
                mu = jnp.mean(o, axis=-1, keepdims=True)
                var = jnp.mean(jnp.square(o - mu), axis=-1, keepdims=True)
                y = (o - mu) * lax.rsqrt(var + EPS) * gw_ref[:, vs] + gb_ref[:, vs]
            o_ref[rows, vs] = (y * gate).astype(o_ref.dtype)


def linear_attention(proj, projf, params, *, mode, batch, seq):
    T = proj.shape[0]
    cb = min(512, seq)
    nc = seq // cb
    pre = "g" if mode == "gla" else "r"
    qblk, kblk = COLB[pre + "_q"] // 256, COLB[pre + "_k"] // 256
    vblk = COLB[pre + "_v"] // 512
    gblk = COLF["g_r" if mode == "gla" else "r_g"] // 512
    row = lambda b, c: b * nc + c
    in_specs = [
        pl.BlockSpec((cb, 256), lambda b, c: (row(b, c), qblk)),
        pl.BlockSpec((cb, 256), lambda b, c: (row(b, c), kblk)),
        pl.BlockSpec((cb, 512), lambda b, c: (row(b, c), vblk)),
        pl.BlockSpec((cb, 512), lambda b, c: (row(b, c), gblk)),
    ]
    args = [proj, proj, proj, projf]
    if mode == "gla":
        wg, bg, ng = params
        in_specs += [
            pl.BlockSpec((cb, 128), lambda b, c: (row(b, c), SMALL_COL // 128)),
            pl.BlockSpec(wg.shape, lambda b, c: (0, 0)),
            pl.BlockSpec((1, bg.shape[-1]), lambda b, c: (0, 0)),
            pl.BlockSpec((1, ng.shape[-1]), lambda b, c: (0, 0)),
        ]
        args += [projf, wg, bg.reshape(1, -1), ng.reshape(1, -1)]
    else:
        lg, cos, sin, gw, gb = params
        in_specs += [
            pl.BlockSpec((1, lg.shape[-1]), lambda b, c: (0, 0)),
            pl.BlockSpec((cb, 256), lambda b, c: (c, 0)),
            pl.BlockSpec((cb, 256), lambda b, c: (c, 0)),
            pl.BlockSpec((1, gw.shape[-1]), lambda b, c: (0, 0)),
            pl.BlockSpec((1, gb.shape[-1]), lambda b, c: (0, 0)),
        ]
        args += [lg.reshape(1, -1), cos, sin, gw.reshape(1, -1), gb.reshape(1, -1)]
    return pl.pallas_call(
        functools.partial(_linear_attn_kernel, mode=mode, cb=cb),
        grid=(batch, nc),
        in_specs=in_specs,
        out_specs=pl.BlockSpec((cb, 512), lambda b, c: (row(b, c), 0)),
        out_shape=jax.ShapeDtypeStruct((T, BRANCH_WIDTH), BF16),
        scratch_shapes=[pltpu.VMEM((GLA_HEADS, GLA_DV, GLA_DK), F32)],
        compiler_params=_cparams("parallel", "arbitrary"),
        name="linear_attention_" + mode,
    )(*args)


def _retention_tables(seq):
    half = RET_DK // 2
    inv = RET_THETA_BASE ** (-jnp.arange(half, dtype=F32) / half)
    ang = jnp.arange(seq).astype(F32)[:, None] * inv[None, :]
    cos = jnp.tile(jnp.cos(ang), (1, 2 * RET_HEADS))
    sin = jnp.tile(jnp.sin(ang), (1, 2 * RET_HEADS))
    log_gamma = jnp.log1p(-jnp.exp2(-5.0 - jnp.arange(RET_HEADS, dtype=F32)))
    return jnp.repeat(log_gamma, RET_DK), cos, sin


def _merge_kernel(u_ref, oa_ref, ob_ref, oc_ref, od_ref, wm_ref, wb_ref, bm_ref, m_ref):
    u = u_ref[...]
    acc = None
    for i, o_ref in enumerate((oa_ref, ob_ref, oc_ref, od_ref)):
        z = jnp.dot(u, wm_ref[i].astype(BF16), preferred_element_type=F32) + bm_ref[i]
        pr = jnp.dot(o_ref[...], wb_ref[i].astype(BF16), preferred_element_type=F32)
        t = jax.nn.sigmoid(z) * pr
        acc = t if acc is None else acc + t
    m_ref[...] = acc.astype(m_ref.dtype)


def merge_branches(u, branches, wm, wb, bm, *, layer):
    T, D = u.shape
    tm, tn = min(1024, T), min(256, D)
    ospec = pl.BlockSpec((tm, BRANCH_WIDTH), lambda i, j: (i, 0))
    return pl.pallas_call(
        _merge_kernel,
        grid=(T // tm, D // tn),
        in_specs=[
            pl.BlockSpec((tm, D), lambda i, j: (i, 0)),
            ospec, ospec, ospec, ospec,
            pl.BlockSpec((None, N_BRANCH, D, tn), lambda i, j: (layer, 0, 0, j)),
            pl.BlockSpec((None, N_BRANCH, BRANCH_WIDTH, tn), lambda i, j: (layer, 0, 0, j)),
            pl.BlockSpec((None, N_BRANCH, 1, tn), lambda i, j: (layer, 0, 0, j)),
        ],
        out_specs=pl.BlockSpec((tm, tn), lambda i, j: (i, j)),
        out_shape=jax.ShapeDtypeStruct((T, D), BF16),
        compiler_params=_cparams("parallel", "arbitrary"),
        name="merge_branches",
    )(u, *branches, wm, wb, bm.reshape(bm.shape[0], N_BRANCH, 1, D))


def _out_proj_kernel(h_ref, m_ref, w_ref, mod_ref, o_ref, *, mrow, tn):
    j = pl.program_id(1)
    c0 = pl.multiple_of(j * tn, tn)
    gate = mod_ref[mrow:mrow + 1, pl.ds(c0, tn)]
    o_ref[...] = h_ref[...] + gate * jnp.dot(m_ref[...], w_ref[...].astype(BF16), preferred_element_type=F32)


def out_proj(h, merged, w, mod, *, layer, seq, mrow):
    T, D = h.shape
    tm, tn = min(1024, seq), min(512, D)
    return pl.pallas_call(
        functools.partial(_out_proj_kernel, mrow=mrow, tn=tn),
        grid=(T // tm, D // tn),
        in_specs=[
            pl.BlockSpec((tm, tn), lambda i, j: (i, j)),
            pl.BlockSpec((tm, D), lambda i, j: (i, 0)),
            pl.BlockSpec((None, D, tn), lambda i, j: (layer, 0, j)),
            pl.BlockSpec((None, N_MOD, D), lambda i, j: ((i * tm) // seq, 0, 0)),
        ],
        out_specs=pl.BlockSpec((tm, tn), lambda i, j: (i, j)),
        out_shape=jax.ShapeDtypeStruct((T, D), F32),
        compiler_params=_cparams("parallel", "arbitrary"),
        name="out_proj",
    )(h, merged, w, mod)


def _final_norm_kernel(h_ref, g_ref, o_ref):
    x = h_ref[...]
    ms = jnp.mean(x * x, axis=-1, keepdims=True)
    o_ref[...] = x * lax.rsqrt(ms + EPS) * g_ref[...]


def final_norm(h, g):
    T, D = h.shape
    tm = min(1024, T)
    return pl.pallas_call(
        _final_norm_kernel,
        grid=(T // tm,),
        in_specs=[pl.BlockSpec((tm, D), lambda i: (i, 0)), pl.BlockSpec((1, D), lambda i: (0, 0))],
        out_specs=pl.BlockSpec((tm, D), lambda i: (i, 0)),
        out_shape=jax.ShapeDtypeStruct((T, D), F32),
        compiler_params=_cparams("parallel"),
        name="final_norm",
    )(h, g.reshape(1, D))


def _permuted_w_in(w_in_l, order, cols):
    D = w_in_l.shape[0]
    parts = [w_in_l[:, _REF_COLS[n][0]:_REF_COLS[n][0] + _REF_COLS[n][1]].astype(BF16) for n in order]
    used = sum(_REF_COLS[n][1] for n in order)
    return jnp.concatenate(parts + [jnp.zeros((D, cols - used), BF16)], axis=1)


def kernel(x, c, w_ada, b_ada, norm_g, ffn_w_gate, ffn_w_up, ffn_w_down, w_in, fox_b_forget, attn_sinks,
           gla_w_gate, gla_b_gate, gla_norm_g, ret_gn_w, ret_gn_b, w_branch, w_merge, b_merge, w_out,
           final_norm_g):
    B, S, D = x.shape
    L = w_ada.shape[0]
    T = B * S
    mod_all = ada_modulation(c, w_ada, b_ada)
    log_gamma, cos, sin = _retention_tables(S)
    h = x.reshape(T, D)
    for l in range(L):
        mod = mod_all[l]
        h = ffn_block(h, mod, norm_g[l, 0], ffn_w_gate, ffn_w_up, ffn_w_down, layer=l, which=0, seq=S,
                      mrow=0, res_scale=0.5)
        proj, u = norm_proj(h, mod, norm_g[l, 1], _permuted_w_in(w_in[l], _ORDER_B, PROJ_B_COLS), seq=S, mrow=3)
        projf = matmul(u, _permuted_w_in(w_in[l], _ORDER_F, PROJ_F_COLS), F32)
        o_a = swa_attention(proj, attn_sinks[l], batch=B, seq=S)
        fc = fox_forget_cumsum(projf, fox_b_forget[l], batch=B, seq=S)
        fc4 = fc.reshape(B, S, FOX_HEADS // 2, 2)
        o_b = fox_attention(proj, fc4.transpose(0, 2, 1, 3), fc4.transpose(0, 2, 3, 1), batch=B, seq=S)
        o_c = linear_attention(proj, projf, (gla_w_gate[l], gla_b_gate[l], gla_norm_g[l]), mode="gla",
                               batch=B, seq=S)
        o_d = linear_attention(proj, projf, (log_gamma, cos, sin, ret_gn_w[l], ret_gn_b[l]), mode="ret",
                               batch=B, seq=S)
        merged = merge_branches(u, (o_a, o_b, o_c, o_d), w_merge, w_branch, b_merge, layer=l)
        h = out_proj(h, merged, w_out, mod, layer=l, seq=S, mrow=5)
        h = ffn_block(h, mod, norm_g[l, 2], ffn_w_gate, ffn_w_up, ffn_w_down, layer=l, which=1, seq=S,
                      mrow=6, res_scale=0.5)
    return final_norm(h, final_norm_g).reshape(B, S, D)
```

```python
import functools

import numpy as np
import jax
import jax.numpy as jnp
from jax import lax
from jax.experimental import pallas as pl
from jax.experimental.pallas import tpu as pltpu

HEAD_DIM = 64
SWA_HEADS = 8
SWA_KV_HEADS = 2
WINDOW = 128
FOX_HEADS = 8
GLA_HEADS = 4
GLA_DK = 64
GLA_DV = 128
GLA_GATE_RANK = 16
GLA_TAU = 16.0
RET_HEADS = 4
RET_DK = 64
RET_DV = 128
RET_THETA_BASE = 10000.0
CHUNK = 64
BRANCH_WIDTH = 512
N_BRANCH = 4
N_MOD = 9
EPS = 1e-6

BF16 = jnp.bfloat16
F32 = jnp.float32
HIGHEST = lax.Precision.HIGHEST

VMEM_LIMIT_BYTES = 60 * 1024 * 1024

_REF_COLS = {}
_off = 0
for _name, _size in (
        ("a_q", 512), ("a_k", 128), ("a_v", 128),
        ("f_q", 512), ("f_k", 512), ("f_v", 512), ("f_f", 8),
        ("g_q", 256), ("g_k", 256), ("g_v", 512), ("g_lr", 16), ("g_r", 512),
        ("r_q", 256), ("r_k", 256), ("r_v", 512), ("r_g", 512)):
    _REF_COLS[_name] = (_off, _size)
    _off += _size
IN_COLS = _off


def _layout(order):
    col, off = {}, 0
    for name in order:
        col[name] = off
        off += _REF_COLS[name][1]
    return col


_ORDER_B = ("a_q", "f_q", "f_k", "f_v", "g_v", "r_v", "g_q", "g_k", "r_q", "r_k", "a_k", "a_v")
_ORDER_F = ("g_r", "r_g", "f_f", "g_lr")
COLB = _layout(_ORDER_B)
COLF = _layout(_ORDER_F)
PROJ_B_COLS = 4608
PROJ_F_COLS = 1152
SMALL_COL = COLF["f_f"]


def _cparams(*sem):
    return pltpu.CompilerParams(dimension_semantics=sem, vmem_limit_bytes=VMEM_LIMIT_BYTES)


def _tile(n, pref):
    t = (min(pref, n) // 128) * 128
    while t >= 128:
        if n % t == 0:
            return t
        t -= 128
    return n


NORM_ROWS = 64


def _adanorm_into(h_ref, g_ref, mod_ref, mrow, u_ref, zero_ref=None):
    gs = g_ref[...] * (1.0 + mod_ref[mrow + 1:mrow + 2, :])
    shift = mod_ref[mrow:mrow + 1, :]
    rows = min(NORM_ROWS, h_ref.shape[0])

    def body(r, carry):
        r0 = pl.multiple_of(r * rows, rows)
        x = h_ref[pl.ds(r0, rows), :]
        ms = jnp.mean(x * x, axis=-1, keepdims=True)
        u_ref[pl.ds(r0, rows), :] = (x * lax.rsqrt(ms + EPS) * gs + shift).astype(u_ref.dtype)
        if zero_ref is not None:
            zero_ref[pl.ds(r0, rows), :] = jnp.zeros((rows, zero_ref.shape[1]), zero_ref.dtype)
        return carry

    lax.fori_loop(0, h_ref.shape[0] // rows, body, 0)


def _log_sigmoid(x):
    return jnp.minimum(x, 0.0) - jnp.log1p(jnp.exp(-jnp.abs(x)))


def _silu(x):
    return x * jax.nn.sigmoid(x)


def _ada_kernel(c_ref, w_ref, b_ref, o_ref):
    c = c_ref[...]
    cond = _silu(c)
    o_ref[...] = jnp.dot(cond, w_ref[...], precision=HIGHEST, preferred_element_type=F32) + b_ref[...]


def ada_modulation(c, w_ada, b_ada):
    L, D, N = w_ada.shape
    B = c.shape[0]
    BP = 8
    cp = jnp.zeros((BP, D), F32).at[:B].set(c)
    tn = _tile(N, 1024)
    out = pl.pallas_call(
        _ada_kernel,
        grid=(L, N // tn),
        in_specs=[
            pl.BlockSpec((BP, D), lambda l, j: (0, 0)),
            pl.BlockSpec((None, D, tn), lambda l, j: (l, 0, j)),
            pl.BlockSpec((None, 1, tn), lambda l, j: (l, 0, j)),
        ],
        out_specs=pl.BlockSpec((None, BP, tn), lambda l, j: (l, 0, j)),
        out_shape=jax.ShapeDtypeStruct((L, BP, N), F32),
        compiler_params=_cparams("parallel", "parallel"),
        name="ada_modulation",
    )(cp, w_ada, b_ada.reshape(L, 1, N))
    return out[:, :B].reshape(L, B, N_MOD, D)


def _ffn_kernel(h_ref, mod_ref, g_ref, wg_ref, wu_ref, wd_ref, o_ref, u_scr, *, mrow, res_scale):
    f = pl.program_id(1)

    @pl.when(f == 0)
    def _():
        _adanorm_into(h_ref, g_ref, mod_ref, mrow, u_scr, zero_ref=o_ref)

    u = u_scr[...]
    a = jnp.dot(u, wg_ref[...].astype(BF16), preferred_element_type=F32)
    b = jnp.dot(u, wu_ref[...].astype(BF16), preferred_element_type=F32)
    mid = (_silu(a) * b).astype(BF16)
    o_ref[...] += jnp.dot(mid, wd_ref[...].astype(BF16), preferred_element_type=F32)

    @pl.when(f == pl.num_programs(1) - 1)
    def _():
        o_ref[...] = h_ref[...] + (res_scale * mod_ref[mrow + 2:mrow + 3, :]) * o_ref[...]


def ffn_block(h, mod, g, wg, wu, wd, *, layer, which, seq, mrow, res_scale):
    T, D = h.shape
    F = wg.shape[-1]
    tm = min(1024, seq)
    tf = _tile(F, 256)
    kern = functools.partial(_ffn_kernel, mrow=mrow, res_scale=res_scale)
    return pl.pallas_call(
        kern,
        grid=(T // tm, F // tf),
        in_specs=[
            pl.BlockSpec((tm, D), lambda i, f: (i, 0)),
            pl.BlockSpec((None, N_MOD, D), lambda i, f: ((i * tm) // seq, 0, 0)),
            pl.BlockSpec((1, D), lambda i, f: (0, 0)),
            pl.BlockSpec((None, None, D, tf), lambda i, f: (layer, which, 0, f)),
            pl.BlockSpec((None, None, D, tf), lambda i, f: (layer, which, 0, f)),
            pl.BlockSpec((None, None, tf, D), lambda i, f: (layer, which, f, 0)),
        ],
        out_specs=pl.BlockSpec((tm, D), lambda i, f: (i, 0)),
        out_shape=jax.ShapeDtypeStruct((T, D), F32),
        scratch_shapes=[pltpu.VMEM((tm, D), BF16)],
        compiler_params=_cparams("parallel", "arbitrary"),
        name="ffn_block",
    )(h, mod, g.reshape(1, D), wg, wu, wd)


PROJ_COL_CHUNK = 512


def _norm_proj_kernel(h_ref, mod_ref, g_ref, wb_ref, wf_ref, pb_ref, pf_ref, u_ref, *, mrow):
    _adanorm_into(h_ref, g_ref, mod_ref, mrow, u_ref)
    u = u_ref[...]
    for c0 in range(0, pb_ref.shape[1], PROJ_COL_CHUNK):
        cols = slice(c0, c0 + PROJ_COL_CHUNK)
        pb_ref[:, cols] = jnp.dot(u, wb_ref[:, cols], preferred_element_type=F32).astype(pb_ref.dtype)
    pf_ref[...] = jnp.dot(u, wf_ref[...], preferred_element_type=F32)


def norm_proj(h, mod, g, wb, wf, *, seq, mrow):
    T, D = h.shape
    NB, NF = wb.shape[1], wf.shape[1]
    tm = min(512, seq)
    kern = functools.partial(_norm_proj_kernel, mrow=mrow)
    resident = pl.Buffered(1)
    return pl.pallas_call(
        kern,
        grid=(T // tm,),
        in_specs=[
            pl.BlockSpec((tm, D), lambda i: (i, 0)),
            pl.BlockSpec((None, N_MOD, D), lambda i: ((i * tm) // seq, 0, 0)),
            pl.BlockSpec((1, D), lambda i: (0, 0)),
            pl.BlockSpec((D, NB), lambda i: (0, 0), pipeline_mode=resident),
            pl.BlockSpec((D, NF), lambda i: (0, 0), pipeline_mode=resident),
        ],
        out_specs=[
            pl.BlockSpec((tm, NB), lambda i: (i, 0)),
            pl.BlockSpec((tm, NF), lambda i: (i, 0)),
            pl.BlockSpec((tm, D), lambda i: (i, 0)),
        ],
        out_shape=[jax.ShapeDtypeStruct((T, NB), BF16), jax.ShapeDtypeStruct((T, NF), F32),
                   jax.ShapeDtypeStruct((T, D), BF16)],
        compiler_params=_cparams("parallel"),
        name="norm_proj",
    )(h, mod, g.reshape(1, D), wb, wf)


def _swa_kernel(sink_ref, q_ref, kv_ref, kvp_ref, o_ref):
    n = pl.program_id(1)
    W = WINDOW
    q = q_ref[...]
    kv = kv_ref[...]
    kvp = kvp_ref[...]
    kk = jnp.concatenate([kvp[:, :W], kv[:, :W]], axis=0)
    vv = jnp.concatenate([kvp[:, W:], kv[:, W:]], axis=0)
    qpos = lax.broadcasted_iota(jnp.int32, (W, 2 * W), 0) + W
    kpos = lax.broadcasted_iota(jnp.int32, (W, 2 * W), 1)
    rel = qpos - kpos
    mask = (rel >= 0) & (rel < W) & ((kpos >= W) | (n > 0))
    group = SWA_HEADS // SWA_KV_HEADS
    for hd in range(SWA_HEADS):
        kvh = hd // group
        k = kk[:, kvh * HEAD_DIM:(kvh + 1) * HEAD_DIM]
        v = vv[:, kvh * HEAD_DIM:(kvh + 1) * HEAD_DIM]
        qh = q[:, hd * HEAD_DIM:(hd + 1) * HEAD_DIM] * (HEAD_DIM ** -0.5)
        logits = lax.dot_general(qh, k, (((1,), (1,)), ((), ())), preferred_element_type=F32)
        logits = jnp.where(mask, logits, -jnp.inf)
        sink = sink_ref[hd]
        m = jnp.maximum(jnp.max(logits, axis=-1, keepdims=True), sink)
        p = jnp.exp(logits - m)
        denom = jnp.sum(p, axis=-1, keepdims=True) + jnp.exp(sink - m)
        w = (p / denom).astype(BF16)
        o = jnp.dot(w, v, preferred_element_type=F32)
        o_ref[:, hd * HEAD_DIM:(hd + 1) * HEAD_DIM] = o.astype(o_ref.dtype)


def swa_attention(proj, sinks, *, batch, seq):
    T = proj.shape[0]
    nb = seq // WINDOW
    qblk = COLB["a_q"] // 512
    kvblk = COLB["a_k"] // 256
    return pl.pallas_call(
        _swa_kernel,
        grid=(batch, nb),
        in_specs=[
            pl.BlockSpec(memory_space=pltpu.SMEM),
            pl.BlockSpec((WINDOW, 512), lambda b, n: (b * nb + n, qblk)),
            pl.BlockSpec((WINDOW, 256), lambda b, n: (b * nb + n, kvblk)),
            pl.BlockSpec((WINDOW, 256), lambda b, n: (b * nb + jnp.maximum(n - 1, 0), kvblk)),
        ],
        out_specs=pl.BlockSpec((WINDOW, 512), lambda b, n: (b * nb + n, 0)),
        out_shape=jax.ShapeDtypeStruct((T, BRANCH_WIDTH), BF16),
        compiler_params=_cparams("parallel", "arbitrary"),
        name="swa_attention",
    )(sinks, proj, proj, proj)


def _fox_cum_kernel(x_ref, b_ref, o_ref, *, seq):
    R = 128
    ri = lax.broadcasted_iota(jnp.int32, (R, R), 0)
    ci = lax.broadcasted_iota(jnp.int32, (R, R), 1)
    tri = (ci <= ri).astype(F32)

    def body(i, carry):
        r0 = pl.multiple_of(i * R, R)
        x = x_ref[pl.ds(r0, R), :][:, :FOX_HEADS] + b_ref[...]
        ls = _log_sigmoid(x)
        cum = jnp.dot(tri, ls, precision=HIGHEST, preferred_element_type=F32) + carry
        o_ref[pl.ds(r0, R), :] = cum
        return cum[R - 1:R, :]

    lax.fori_loop(0, seq // R, body, jnp.zeros((1, FOX_HEADS), F32))


def fox_forget_cumsum(proj, fox_b, *, batch, seq):
    T = proj.shape[0]
    blk = SMALL_COL // 128
    return pl.pallas_call(
        functools.partial(_fox_cum_kernel, seq=seq),
        grid=(batch,),
        in_specs=[
            pl.BlockSpec((seq, 128), lambda b: (b, blk)),
            pl.BlockSpec((1, FOX_HEADS), lambda b: (0, 0)),
        ],
        out_specs=pl.BlockSpec((seq, FOX_HEADS), lambda b: (b, 0)),
        out_shape=jax.ShapeDtypeStruct((T, FOX_HEADS), F32),
        compiler_params=_cparams("parallel"),
        name="fox_forget_cumsum",
    )(proj, fox_b.reshape(1, FOX_HEADS))


def _fox_kernel(q_ref, k_ref, v_ref, fc_ref, fr_ref, o_ref, m_scr, acc_scr, s_scr, *, tq, hp_heads):
    qi = pl.program_id(2)
    tk = tq
    d = HEAD_DIM
    lane = lax.broadcasted_iota(jnp.int32, (tq, hp_heads * d), 1)
    q = q_ref[...] * (d ** -0.5)
    qs = [jnp.where((lane >= hh * d) & (lane < (hh + 1) * d), q, jnp.zeros_like(q)) for hh in range(hp_heads)]
    w = hp_heads * d
    m_scr[...] = jnp.full(m_scr.shape, -jnp.inf, F32)
    acc_scr[...] = jnp.zeros(acc_scr.shape, F32)
    fqs = [jnp.broadcast_to(fc_ref[:, hh:hh + 1], (tq, w)) for hh in range(hp_heads)]
    ones = jnp.ones((tk, w), BF16)

    def scores(kb, slot):
        k0 = pl.multiple_of(kb * tk, tk)
        k = k_ref[pl.ds(k0, tk), :]
        for hh in range(hp_heads):
            s_scr[slot, hh] = lax.dot_general(qs[hh], k, (((1,), (1,)), ((), ())), preferred_element_type=F32)

    def softmax_pv(kb, slot, masked):
        k0 = pl.multiple_of(kb * tk, tk)
        v = jnp.concatenate([v_ref[pl.ds(k0, tk), :], ones], axis=1)
        for hh in range(hp_heads):
            fk = fr_ref[hh:hh + 1, pl.ds(k0, tk)]
            t = s_scr[slot, hh] - fk
            if masked:
                row = lax.broadcasted_iota(jnp.int32, (tq, tk), 0)
                col = lax.broadcasted_iota(jnp.int32, (tq, tk), 1)
                t = jnp.where(col <= row, t, -jnp.inf)
            m_old = m_scr[hh]
            m_new = jnp.maximum(m_old, fqs[hh] + jnp.max(t, axis=-1, keepdims=True))
            alpha = jnp.exp(m_old - m_new)
            c = fqs[hh] - m_new
            p = jnp.exp(t + jnp.concatenate([c] * (tk // w), axis=1))
            acc_scr[hh] = (jnp.concatenate([alpha, alpha], axis=1) * acc_scr[hh]
                           + jnp.dot(p.astype(BF16), v, preferred_element_type=F32))
            m_scr[hh] = m_new

    scores(0, 0)

    def body(kb, carry):
        slot = lax.rem(kb, 2)
        softmax_pv(kb, slot, False)
        scores(kb + 1, 1 - slot)
        return carry

    lax.fori_loop(0, qi, body, 0)
    softmax_pv(qi, lax.rem(qi, 2), True)
    out = None
    for hh in range(hp_heads):
        o = acc_scr[hh, :, :w] / acc_scr[hh, :, w:]
        out = o if out is None else jnp.where(lane >= hh * d, o, out)
    o_ref[...] = out.astype(o_ref.dtype)


def fox_attention(proj, fc, fr, *, batch, seq):
    T = proj.shape[0]
    tq = min(512, seq)
    nq = seq // tq
    hp_heads = 2
    n_hp = FOX_HEADS // hp_heads
    qblk, kblk, vblk = COLB["f_q"] // 128, COLB["f_k"] // 128, COLB["f_v"] // 128
    kern = functools.partial(_fox_kernel, tq=tq, hp_heads=hp_heads)
    return pl.pallas_call(
        kern,
        grid=(batch, n_hp, nq),
        scratch_shapes=[pltpu.VMEM((hp_heads, tq, hp_heads * HEAD_DIM), F32),
                        pltpu.VMEM((hp_heads, tq, 2 * hp_heads * HEAD_DIM), F32),
                        pltpu.VMEM((2, hp_heads, tq, tq), F32)],
        in_specs=[
            pl.BlockSpec((tq, 128), lambda b, hp, qi: (b * nq + qi, qblk + hp)),
            pl.BlockSpec((seq, 128), lambda b, hp, qi: (b, kblk + hp)),
            pl.BlockSpec((seq, 128), lambda b, hp, qi: (b, vblk + hp)),
            pl.BlockSpec((None, None, tq, hp_heads), lambda b, hp, qi: (b, hp, qi, 0)),
            pl.BlockSpec((None, None, hp_heads, seq), lambda b, hp, qi: (b, hp, 0, 0)),
        ],
        out_specs=pl.BlockSpec((tq, 128), lambda b, hp, qi: (b * nq + qi, hp)),
        out_shape=jax.ShapeDtypeStruct((T, BRANCH_WIDTH), BF16),
        compiler_params=_cparams("parallel", "parallel", "arbitrary"),
        name="fox_attention",
    )(proj, proj, proj, fc, fr)


def _rotate_half(x, neg_first_half):
    n = x.shape[-1]
    half = HEAD_DIM // 2
    fwd = pltpu.roll(x, half, 1)
    bwd = pltpu.roll(x, n - half, 1)
    return jnp.where(neg_first_half, -bwd, fwd)


def _linear_attn_kernel(*refs, mode, cb):
    if mode == "gla":
        (q_ref, k_ref, v_ref, gate_ref, small_ref, wg_ref, bg_ref, ng_ref, o_ref, st_ref) = refs
    else:
        (q_ref, k_ref, v_ref, gate_ref, lg_ref, cos_ref, sin_ref, gw_ref, gb_ref, o_ref, st_ref) = refs
    H, DK, DV, C = GLA_HEADS, GLA_DK, GLA_DV, CHUNK

    @pl.when(pl.program_id(1) == 0)
    def _():
        st_ref[...] = jnp.zeros_like(st_ref)

    ri = lax.broadcasted_iota(jnp.int32, (C, C), 0)
    ci = lax.broadcasted_iota(jnp.int32, (C, C), 1)
    causal = ci <= ri
    if mode == "gla":
        tri = causal.astype(F32)
    else:
        lane = lax.broadcasted_iota(jnp.int32, (C, H * DK), 1)
        first_half = (lane % HEAD_DIM) < (HEAD_DIM // 2)
        steps = (lax.broadcasted_iota(jnp.int32, (C, H * DK), 0) + 1).astype(F32)

    for c in range(cb // C):
        rows = slice(c * C, (c + 1) * C)
        q = q_ref[rows, :].astype(F32)
        k = k_ref[rows, :].astype(F32)
        if mode == "gla":
            glr = small_ref[rows, :][:, FOX_HEADS:FOX_HEADS + GLA_GATE_RANK]
            z = jnp.dot(glr, wg_ref[...], precision=HIGHEST, preferred_element_type=F32) + bg_ref[...]
            ld = _log_sigmoid(z) / GLA_TAU
            cum = jnp.dot(tri, ld, precision=HIGHEST, preferred_element_type=F32)
        else:
            cos = cos_ref[rows, :]
            sin = sin_ref[rows, :]
            q = q * cos + _rotate_half(q, first_half) * sin
            k = k * cos + _rotate_half(k, first_half) * sin
            cum = steps * lg_ref[...]
        last = cum[C - 1:C, :]
        q_in = (q * (DK ** -0.5) * jnp.exp(cum)).astype(BF16)
        k_in = (k * jnp.exp(-cum)).astype(BF16)
        k_st = (k * jnp.exp(last - cum)).astype(BF16)
        dec = jnp.exp(last)
        for hd in range(H):
            ks = slice(hd * DK, (hd + 1) * DK)
            vs = slice(hd * DV, (hd + 1) * DV)
            v = v_ref[rows, vs]
            sc = lax.dot_general(q_in[:, ks], k_in[:, ks], (((1,), (1,)), ((), ())), preferred_element_type=F32)
            sc = jnp.where(causal, sc, 0.0).astype(BF16)
            st = st_ref[hd]
            o = jnp.dot(sc, v, preferred_element_type=F32)
            o = o + lax.dot_general(q_in[:, ks], st.astype(BF16), (((1,), (1,)), ((), ())),
                                    preferred_element_type=F32)
            upd = lax.dot_general(v, k_st[:, ks], (((0,), (0,)), ((), ())), preferred_element_type=F32)
            st_ref[hd] = st * dec[:, ks] + upd
            gate = _silu(gate_ref[rows, vs])
            if mode == "gla":
                y = o * lax.rsqrt(jnp.mean(o * o, axis=-1, keepdims=True) + EPS) * ng_ref[...]
            else:
                mu = jnp.mean(o, axis=-1, keepdims=True)
                var = jnp.mean(jnp.square(o - mu), axis=-1, keepdims=True)
                y = (o - mu) * lax.rsqrt(var + EPS) * gw_ref[:, vs] + gb_ref[:, vs]
            o_ref[rows, vs] = (y * gate).astype(o_ref.dtype)


def linear_attention(proj, projf, params, *, mode, batch, seq):
    T = proj.shape[0]
    cb = min(512, seq)
    nc = seq // cb
    pre = "g" if mode == "gla" else "r"
    qblk, kblk = COLB[pre + "_q"] // 256, COLB[pre + "_k"] // 256
    vblk = COLB[pre + "_v"] // 512
    gblk = COLF["g_r" if mode == "gla" else "r_g"] // 512
    row = lambda b, c: b * nc + c
    in_specs = [
        pl.BlockSpec((cb, 256), lambda b, c: (row(b, c), qblk)),
        pl.BlockSpec((cb, 256), lambda b, c: (row(b, c), kblk)),
        pl.BlockSpec((cb, 512), lambda b, c: (row(b, c), vblk)),
        pl.BlockSpec((cb, 512), lambda b, c: (row(b, c), gblk)),
    ]
    args = [proj, proj, proj, projf]
    if mode == "gla":
        wg, bg, ng = params
        in_specs += [
            pl.BlockSpec((cb, 128), lambda b, c: (row(b, c), SMALL_COL // 128)),
            pl.BlockSpec(wg.shape, lambda b, c: (0, 0)),
            pl.BlockSpec((1, bg.shape[-1]), lambda b, c: (0, 0)),
            pl.BlockSpec((1, ng.shape[-1]), lambda b, c: (0, 0)),
        ]
        args += [projf, wg, bg.reshape(1, -1), ng.reshape(1, -1)]
    else:
        lg, cos, sin, gw, gb = params
        in_specs += [
            pl.BlockSpec((1, lg.shape[-1]), lambda b, c: (0, 0)),
            pl.BlockSpec((cb, 256), lambda b, c: (c, 0)),
            pl.BlockSpec((cb, 256), lambda b, c: (c, 0)),
            pl.BlockSpec((1, gw.shape[-1]), lambda b, c: (0, 0)),
            pl.BlockSpec((1, gb.shape[-1]), lambda b, c: (0, 0)),
        ]
        args += [lg.reshape(1, -1), cos, sin, gw.reshape(1, -1), gb.reshape(1, -1)]
    return pl.pallas_call(
        functools.partial(_linear_attn_kernel, mode=mode, cb=cb),
        grid=(batch, nc),
        in_specs=in_specs,
        out_specs=pl.BlockSpec((cb, 512), lambda b, c: (row(b, c), 0)),
        out_shape=jax.ShapeDtypeStruct((T, BRANCH_WIDTH), BF16),
        scratch_shapes=[pltpu.VMEM((GLA_HEADS, GLA_DV, GLA_DK), F32)],
        compiler_params=_cparams("parallel", "arbitrary"),
        name="linear_attention_" + mode,
    )(*args)


def _retention_tables(seq):
    half = RET_DK // 2
    inv = RET_THETA_BASE ** (-jnp.arange(half, dtype=F32) / half)
    ang = jnp.arange(seq).astype(F32)[:, None] * inv[None, :]
    cos = jnp.tile(jnp.cos(ang), (1, 2 * RET_HEADS))
    sin = jnp.tile(jnp.sin(ang), (1, 2 * RET_HEADS))
    log_gamma = jnp.log1p(-jnp.exp2(-5.0 - jnp.arange(RET_HEADS, dtype=F32)))
    return jnp.repeat(log_gamma, RET_DK), cos, sin


def _merge_kernel(u_ref, oa_ref, ob_ref, oc_ref, od_ref, wm_ref, wb_ref, bm_ref, m_ref):
    u = u_ref[...]
    acc = None
    for i, o_ref in enumerate((oa_ref, ob_ref, oc_ref, od_ref)):
        z = jnp.dot(u, wm_ref[i].astype(BF16), preferred_element_type=F32) + bm_ref[i]
        pr = jnp.dot(o_ref[...], wb_ref[i].astype(BF16), preferred_element_type=F32)
        t = jax.nn.sigmoid(z) * pr
        acc = t if acc is None else acc + t
    m_ref[...] = acc.astype(m_ref.dtype)


def merge_branches(u, branches, wm, wb, bm, *, layer):
    T, D = u.shape
    tm, tn = min(1024, T), min(256, D)
    ospec = pl.BlockSpec((tm, BRANCH_WIDTH), lambda i, j: (i, 0))
    return pl.pallas_call(
        _merge_kernel,
        grid=(T // tm, D // tn),
        in_specs=[
            pl.BlockSpec((tm, D), lambda i, j: (i, 0)),
            ospec, ospec, ospec, ospec,
            pl.BlockSpec((None, N_BRANCH, D, tn), lambda i, j: (layer, 0, 0, j)),
            pl.BlockSpec((None, N_BRANCH, BRANCH_WIDTH, tn), lambda i, j: (layer, 0, 0, j)),
            pl.BlockSpec((None, N_BRANCH, 1, tn), lambda i, j: (layer, 0, 0, j)),
        ],
        out_specs=pl.BlockSpec((tm, tn), lambda i, j: (i, j)),
        out_shape=jax.ShapeDtypeStruct((T, D), BF16),
        compiler_params=_cparams("parallel", "arbitrary"),
        name="merge_branches",
    )(u, *branches, wm, wb, bm.reshape(bm.shape[0], N_BRANCH, 1, D))


def _out_proj_kernel(h_ref, m_ref, w_ref, mod_ref, o_ref, *, mrow, tn):
    j = pl.program_id(1)
    c0 = pl.multiple_of(j * tn, tn)
    gate = mod_ref[mrow:mrow + 1, pl.ds(c0, tn)]
    o_ref[...] = h_ref[...] + gate * jnp.dot(m_ref[...], w_ref[...].astype(BF16), preferred_element_type=F32)


def out_proj(h, merged, w, mod, *, layer, seq, mrow):
    T, D = h.shape
    tm, tn = min(1024, seq), min(512, D)
    return pl.pallas_call(
        functools.partial(_out_proj_kernel, mrow=mrow, tn=tn),
        grid=(T // tm, D // tn),
        in_specs=[
            pl.BlockSpec((tm, tn), lambda i, j: (i, j)),
            pl.BlockSpec((tm, D), lambda i, j: (i, 0)),
            pl.BlockSpec((None, D, tn), lambda i, j: (layer, 0, j)),
            pl.BlockSpec((None, N_MOD, D), lambda i, j: ((i * tm) // seq, 0, 0)),
        ],
        out_specs=pl.BlockSpec((tm, tn), lambda i, j: (i, j)),
        out_shape=jax.ShapeDtypeStruct((T, D), F32),
        compiler_params=_cparams("parallel", "arbitrary"),
        name="out_proj",
    )(h, merged, w, mod)


def _final_norm_kernel(h_ref, g_ref, o_ref):
    x = h_ref[...]
    ms = jnp.mean(x * x, axis=-1, keepdims=True)
    o_ref[...] = x * lax.rsqrt(ms + EPS) * g_ref[...]


def final_norm(h, g):
    T, D = h.shape
    tm = min(1024, T)
    return pl.pallas_call(
        _final_norm_kernel,
        grid=(T // tm,),
        in_specs=[pl.BlockSpec((tm, D), lambda i: (i, 0)), pl.BlockSpec((1, D), lambda i: (0, 0))],
        out_specs=pl.BlockSpec((tm, D), lambda i: (i, 0)),
        out_shape=jax.ShapeDtypeStruct((T, D), F32),
        compiler_params=_cparams("parallel"),
        name="final_norm",
    )(h, g.reshape(1, D))


def _permuted_w_in(w_in_l, order, cols):
    D = w_in_l.shape[0]
    parts = [w_in_l[:, _REF_COLS[n][0]:_REF_COLS[n][0] + _REF_COLS[n][1]].astype(BF16) for n in order]
    used = sum(_REF_COLS[n][1] for n in order)
    return jnp.concatenate(parts + [jnp.zeros((D, cols - used), BF16)], axis=1)


def kernel(x, c, w_ada, b_ada, norm_g, ffn_w_gate, ffn_w_up, ffn_w_down, w_in, fox_b_forget, attn_sinks,
           gla_w_gate, gla_b_gate, gla_norm_g, ret_gn_w, ret_gn_b, w_branch, w_merge, b_merge, w_out,
           final_norm_g):
    B, S, D = x.shape
    L = w_ada.shape[0]
    T = B * S
    mod_all = ada_modulation(c, w_ada, b_ada)
    log_gamma, cos, sin = _retention_tables(S)
    h = x.reshape(T, D)
    for l in range(L):
        mod = mod_all[l]
        h = ffn_block(h, mod, norm_g[l, 0], ffn_w_gate, ffn_w_up, ffn_w_down, layer=l, which=0, seq=S,
                      mrow=0, res_scale=0.5)
        proj, projf, u = norm_proj(h, mod, norm_g[l, 1], _permuted_w_in(w_in[l], _ORDER_B, PROJ_B_COLS),
                                   _permuted_w_in(w_in[l], _ORDER_F, PROJ_F_COLS), seq=S, mrow=3)
        o_a = swa_attention(proj, attn_sinks[l], batch=B, seq=S)
        fc = fox_forget_cumsum(projf, fox_b_forget[l], batch=B, seq=S)
        fc4 = fc.reshape(B, S, FOX_HEADS // 2, 2)
        o_b = fox_attention(proj, fc4.transpose(0, 2, 1, 3), fc4.transpose(0, 2, 3, 1), batch=B, seq=S)
        o_c = linear_attention(proj, projf, (gla_w_gate[l], gla_b_gate[l], gla_norm_g[l]), mode="gla",
                               batch=B, seq=S)
        o_d = linear_attention(proj, projf, (log_gamma, cos, sin, ret_gn_w[l], ret_gn_b[l]), mode="ret",
                               batch=B, seq=S)
        merged = merge_branches(u, (o_a, o_b, o_c, o_d), w_merge, w_branch, b_merge, layer=l)
        h = out_proj(h, merged, w_out, mod, layer=l, seq=S, mrow=5)
        h = ffn_block(h, mod, norm_g[l, 2], ffn_w_gate, ffn_w_up, ffn_w_down, layer=l, which=1, seq=S,
                      mrow=6, res_scale=0.5)
    return final_norm(h, final_norm_g).reshape(B, S, D)
```

```python
import functools

import numpy as np
import jax
import jax.numpy as jnp
from jax import lax
from jax.experimental import pallas as pl
from jax.experimental.pallas import tpu as pltpu

HEAD_DIM = 64
SWA_HEADS = 8
SWA_KV_HEADS = 2
WINDOW = 128
FOX_HEADS = 8
GLA_HEADS = 4
GLA_DK = 64
GLA_DV = 128
GLA_GATE_RANK = 16
GLA_TAU = 16.0
RET_HEADS = 4
RET_DK = 64
RET_DV = 128
RET_THETA_BASE = 10000.0
CHUNK = 64
BRANCH_WIDTH = 512
N_BRANCH = 4
N_MOD = 9
EPS = 1e-6

BF16 = jnp.bfloat16
F32 = jnp.float32
HIGHEST = lax.Precision.HIGHEST

VMEM_LIMIT_BYTES = 60 * 1024 * 1024

_REF_COLS = {}
_off = 0
for _name, _size in (
        ("a_q", 512), ("a_k", 128), ("a_v", 128),
        ("f_q", 512), ("f_k", 512), ("f_v", 512), ("f_f", 8),
        ("g_q", 256), ("g_k", 256), ("g_v", 512), ("g_lr", 16), ("g_r", 512),
        ("r_q", 256), ("r_k", 256), ("r_v", 512), ("r_g", 512)):
    _REF_COLS[_name] = (_off, _size)
    _off += _size
IN_COLS = _off
for _name in ("a_k", "a_v"):
    for _i in range(SWA_KV_HEADS):
        _REF_COLS[_name + str(_i)] = (_REF_COLS[_name][0] + _i * HEAD_DIM, HEAD_DIM)


def _layout(order):
    col, off = {}, 0
    for name in order:
        col.setdefault(name, off)
        off += _REF_COLS[name][1]
    return col


_ORDER_B = ("a_q", "f_q", "f_k", "f_v", "g_v", "r_v", "g_q", "g_k", "r_q", "r_k",
            "a_k0", "a_k0", "a_k1", "a_k1", "a_v0", "a_v0", "a_v1", "a_v1")
_ORDER_F = ("g_r", "r_g", "f_f", "g_lr")
COLB = _layout(_ORDER_B)
COLF = _layout(_ORDER_F)
PROJ_B_COLS = 4608
PROJ_F_COLS = 1152
SMALL_COL = COLF["f_f"]


def _cparams(*sem):
    return pltpu.CompilerParams(dimension_semantics=sem, vmem_limit_bytes=VMEM_LIMIT_BYTES)


def _tile(n, pref):
    t = (min(pref, n) // 128) * 128
    while t >= 128:
        if n % t == 0:
            return t
        t -= 128
    return n


NORM_ROWS = 64


def _adanorm_into(h_ref, g_ref, mod_ref, mrow, u_ref, zero_ref=None):
    gs = g_ref[...] * (1.0 + mod_ref[mrow + 1:mrow + 2, :])
    shift = mod_ref[mrow:mrow + 1, :]
    rows = min(NORM_ROWS, h_ref.shape[0])

    def body(r, carry):
        r0 = pl.multiple_of(r * rows, rows)
        x = h_ref[pl.ds(r0, rows), :]
        ms = jnp.mean(x * x, axis=-1, keepdims=True)
        u_ref[pl.ds(r0, rows), :] = (x * lax.rsqrt(ms + EPS) * gs + shift).astype(u_ref.dtype)
        if zero_ref is not None:
            zero_ref[pl.ds(r0, rows), :] = jnp.zeros((rows, zero_ref.shape[1]), zero_ref.dtype)
        return carry

    lax.fori_loop(0, h_ref.shape[0] // rows, body, 0)


def _log_sigmoid(x):
    return jnp.minimum(x, 0.0) - jnp.log1p(jnp.exp(-jnp.abs(x)))


def _silu(x):
    return x * jax.nn.sigmoid(x)


def _ada_kernel(c_ref, w_ref, b_ref, o_ref):
    c = c_ref[...]
    cond = _silu(c)
    o_ref[...] = jnp.dot(cond, w_ref[...], precision=HIGHEST, preferred_element_type=F32) + b_ref[...]


def ada_modulation(c, w_ada, b_ada):
    L, D, N = w_ada.shape
    B = c.shape[0]
    BP = 8
    cp = jnp.zeros((BP, D), F32).at[:B].set(c)
    tn = _tile(N, 1024)
    out = pl.pallas_call(
        _ada_kernel,
        grid=(L, N // tn),
        in_specs=[
            pl.BlockSpec((BP, D), lambda l, j: (0, 0)),
            pl.BlockSpec((None, D, tn), lambda l, j: (l, 0, j)),
            pl.BlockSpec((None, 1, tn), lambda l, j: (l, 0, j)),
        ],
        out_specs=pl.BlockSpec((None, BP, tn), lambda l, j: (l, 0, j)),
        out_shape=jax.ShapeDtypeStruct((L, BP, N), F32),
        compiler_params=_cparams("parallel", "parallel"),
        name="ada_modulation",
    )(cp, w_ada, b_ada.reshape(L, 1, N))
    return out[:, :B].reshape(L, B, N_MOD, D)


def _ffn_kernel(h_ref, mod_ref, g_ref, wg_ref, wu_ref, wd_ref, o_ref, u_scr, *, mrow, res_scale):
    f = pl.program_id(1)

    @pl.when(f == 0)
    def _():
        _adanorm_into(h_ref, g_ref, mod_ref, mrow, u_scr, zero_ref=o_ref)

    u = u_scr[...]
    a = jnp.dot(u, wg_ref[...].astype(BF16), preferred_element_type=F32)
    b = jnp.dot(u, wu_ref[...].astype(BF16), preferred_element_type=F32)
    mid = (_silu(a) * b).astype(BF16)
    o_ref[...] += jnp.dot(mid, wd_ref[...].astype(BF16), preferred_element_type=F32)

    @pl.when(f == pl.num_programs(1) - 1)
    def _():
        o_ref[...] = h_ref[...] + (res_scale * mod_ref[mrow + 2:mrow + 3, :]) * o_ref[...]


def ffn_block(h, mod, g, wg, wu, wd, *, layer, which, seq, mrow, res_scale):
    T, D = h.shape
    F = wg.shape[-1]
    tm = min(1024, seq)
    tf = _tile(F, 256)
    kern = functools.partial(_ffn_kernel, mrow=mrow, res_scale=res_scale)
    return pl.pallas_call(
        kern,
        grid=(T // tm, F // tf),
        in_specs=[
            pl.BlockSpec((tm, D), lambda i, f: (i, 0)),
            pl.BlockSpec((None, N_MOD, D), lambda i, f: ((i * tm) // seq, 0, 0)),
            pl.BlockSpec((1, D), lambda i, f: (0, 0)),
            pl.BlockSpec((None, None, D, tf), lambda i, f: (layer, which, 0, f)),
            pl.BlockSpec((None, None, D, tf), lambda i, f: (layer, which, 0, f)),
            pl.BlockSpec((None, None, tf, D), lambda i, f: (layer, which, f, 0)),
        ],
        out_specs=pl.BlockSpec((tm, D), lambda i, f: (i, 0)),
        out_shape=jax.ShapeDtypeStruct((T, D), F32),
        scratch_shapes=[pltpu.VMEM((tm, D), BF16)],
        compiler_params=_cparams("parallel", "arbitrary"),
        name="ffn_block",
    )(h, mod, g.reshape(1, D), wg, wu, wd)


PROJ_COL_CHUNK = 512


def _norm_proj_kernel(h_ref, mod_ref, g_ref, wb_ref, wf_ref, pb_ref, pf_ref, u_ref, *, mrow):
    _adanorm_into(h_ref, g_ref, mod_ref, mrow, u_ref)
    u = u_ref[...]
    for c0 in range(0, pb_ref.shape[1], PROJ_COL_CHUNK):
        cols = slice(c0, c0 + PROJ_COL_CHUNK)
        pb_ref[:, cols] = jnp.dot(u, wb_ref[:, cols], preferred_element_type=F32).astype(pb_ref.dtype)
    pf_ref[...] = jnp.dot(u, wf_ref[...], preferred_element_type=F32)


def norm_proj(h, mod, g, wb, wf, *, seq, mrow):
    T, D = h.shape
    NB, NF = wb.shape[1], wf.shape[1]
    tm = min(512, seq)
    kern = functools.partial(_norm_proj_kernel, mrow=mrow)
    resident = pl.Buffered(1)
    return pl.pallas_call(
        kern,
        grid=(T // tm,),
        in_specs=[
            pl.BlockSpec((tm, D), lambda i: (i, 0)),
            pl.BlockSpec((None, N_MOD, D), lambda i: ((i * tm) // seq, 0, 0)),
            pl.BlockSpec((1, D), lambda i: (0, 0)),
            pl.BlockSpec((D, NB), lambda i: (0, 0), pipeline_mode=resident),
            pl.BlockSpec((D, NF), lambda i: (0, 0), pipeline_mode=resident),
        ],
        out_specs=[
            pl.BlockSpec((tm, NB), lambda i: (i, 0)),
            pl.BlockSpec((tm, NF), lambda i: (i, 0)),
            pl.BlockSpec((tm, D), lambda i: (i, 0)),
        ],
        out_shape=[jax.ShapeDtypeStruct((T, NB), BF16), jax.ShapeDtypeStruct((T, NF), F32),
                   jax.ShapeDtypeStruct((T, D), BF16)],
        compiler_params=_cparams("parallel"),
        name="norm_proj",
    )(h, mod, g.reshape(1, D), wb, wf)


def _swa_kernel(sink_ref, q_ref, kv_ref, kvp_ref, o_ref):
    n = pl.program_id(1)
    W = WINDOW
    d = HEAD_DIM
    q = q_ref[...] * (d ** -0.5)
    kv = jnp.concatenate([kvp_ref[...], kv_ref[...]], axis=0)
    rowi = lax.broadcasted_iota(jnp.int32, (2 * W, 2 * d), 0)
    lanei = lax.broadcasted_iota(jnp.int32, (2 * W, 2 * d), 1)
    own = (rowi >= W) == (lanei >= d)
    first_half = lax.broadcasted_iota(jnp.int32, (W, 2 * d), 1) < d
    qpos =lax.broadcasted_iota(jnp.int32, (2 * W, 2 * W), 0) % W + W
    kpos = lax.broadcasted_iota(jnp.int32, (2 * W, 2 * W), 1)
    rel = qpos - kpos
    mask = (rel >= 0) & (rel < W) & ((kpos >= W) | (n > 0))
    ones = jnp.ones((2 * W, 2 * d), BF16)
    group = SWA_HEADS // SWA_KV_HEADS
    for g in range(SWA_HEADS // 2):
        kvh = (2 * g) // group
        k2 = kv[:, kvh * 2 * d:(kvh + 1) * 2 * d]
        v2 = kv[:, (SWA_KV_HEADS + kvh) * 2 * d:(SWA_KV_HEADS + kvh + 1) * 2 * d]
        v_ext = jnp.concatenate([v2, ones], axis=1)
        qp = q[:, g * 2 * d:(g + 1) * 2 * d]
        qstack = jnp.where(own, jnp.concatenate([qp, qp], axis=0), jnp.zeros((2 * W, 2 * d), BF16))
        logits = lax.dot_general(qstack, k2, (((1,), (1,)), ((), ())), preferred_element_type=F32)
        logits = jnp.where(mask, logits, -jnp.inf)
        sink = jnp.where(rowi >= W, sink_ref[2 * g + 1], sink_ref[2 * g])
        m = jnp.maximum(jnp.max(logits, axis=-1, keepdims=True), sink)
        p = jnp.exp(logits - jnp.concatenate([m, m], axis=1)).astype(BF16)
        r = jnp.dot(p, v_ext, preferred_element_type=F32)
        o2 = r[:, :2 * d] / (r[:, 2 * d:] + jnp.exp(sink - m))
        out = jnp.where(first_half, o2[:W], o2[W:])
        o_ref[:, g * 2 * d:(g + 1) * 2 * d] = out.astype(o_ref.dtype)


def swa_attention(proj, sinks, *, batch, seq):
    T = proj.shape[0]
    nb = seq // WINDOW
    qblk = COLB["a_q"] // 512
    kvblk = COLB["a_k0"] // 512
    return pl.pallas_call(
        _swa_kernel,
        grid=(batch, nb),
        in_specs=[
            pl.BlockSpec(memory_space=pltpu.SMEM),
            pl.BlockSpec((WINDOW, 512), lambda b, n: (b * nb + n, qblk)),
            pl.BlockSpec((WINDOW, 512), lambda b, n: (b * nb + n, kvblk)),
            pl.BlockSpec((WINDOW, 512), lambda b, n: (b * nb + jnp.maximum(n - 1, 0), kvblk)),
        ],
        out_specs=pl.BlockSpec((WINDOW, 512), lambda b, n: (b * nb + n, 0)),
        out_shape=jax.ShapeDtypeStruct((T, BRANCH_WIDTH), BF16),
        compiler_params=_cparams("parallel", "arbitrary"),
        name="swa_attention",
    )(sinks, proj, proj, proj)


def _fox_cum_kernel(x_ref, b_ref, o_ref, *, seq):
    R = 128
    ri = lax.broadcasted_iota(jnp.int32, (R, R), 0)
    ci = lax.broadcasted_iota(jnp.int32, (R, R), 1)
    tri = (ci <= ri).astype(F32)

    def body(i, carry):
        r0 = pl.multiple_of(i * R, R)
        x = x_ref[pl.ds(r0, R), :][:, :FOX_HEADS] + b_ref[...]
        ls = _log_sigmoid(x)
        cum = jnp.dot(tri, ls, precision=HIGHEST, preferred_element_type=F32) + carry
        o_ref[pl.ds(r0, R), :] = cum
        return cum[R - 1:R, :]

    lax.fori_loop(0, seq // R, body, jnp.zeros((1, FOX_HEADS), F32))


def fox_forget_cumsum(proj, fox_b, *, batch, seq):
    T = proj.shape[0]
    blk = SMALL_COL // 128
    return pl.pallas_call(
        functools.partial(_fox_cum_kernel, seq=seq),
        grid=(batch,),
        in_specs=[
            pl.BlockSpec((seq, 128), lambda b: (b, blk)),
            pl.BlockSpec((1, FOX_HEADS), lambda b: (0, 0)),
        ],
        out_specs=pl.BlockSpec((seq, FOX_HEADS), lambda b: (b, 0)),
        out_shape=jax.ShapeDtypeStruct((T, FOX_HEADS), F32),
        compiler_params=_cparams("parallel"),
        name="fox_forget_cumsum",
    )(proj, fox_b.reshape(1, FOX_HEADS))


def _fox_kernel(q_ref, k_ref, v_ref, fc_ref, fr_ref, o_ref, m_scr, acc_scr, s_scr, *, tq, hp_heads):
    qi = pl.program_id(2)
    tk = tq
    d = HEAD_DIM
    lane = lax.broadcasted_iota(jnp.int32, (tq, hp_heads * d), 1)
    q = q_ref[...] * (d ** -0.5)
    qs = [jnp.where((lane >= hh * d) & (lane < (hh + 1) * d), q, jnp.zeros_like(q)) for hh in range(hp_heads)]
    w = hp_heads * d
    m_scr[...] = jnp.full(m_scr.shape, -jnp.inf, F32)
    acc_scr[...] = jnp.zeros(acc_scr.shape, F32)
    fqs = [jnp.broadcast_to(fc_ref[:, hh:hh + 1], (tq, w)) for hh in range(hp_heads)]
    ones = jnp.ones((tk, w), BF16)

    def scores(kb, slot):
        k0 = pl.multiple_of(kb * tk, tk)
        k = k_ref[pl.ds(k0, tk), :]
        for hh in range(hp_heads):
            s_scr[slot, hh] = lax.dot_general(qs[hh], k, (((1,), (1,)), ((), ())), preferred_element_type=F32)

    def softmax_pv(kb, slot, masked):
        k0 = pl.multiple_of(kb * tk, tk)
        v = jnp.concatenate([v_ref[pl.ds(k0, tk), :], ones], axis=1)
        for hh in range(hp_heads):
            fk = fr_ref[hh:hh + 1, pl.ds(k0, tk)]
            t = s_scr[slot, hh] - fk
            if masked:
                row = lax.broadcasted_iota(jnp.int32, (tq, tk), 0)
                col = lax.broadcasted_iota(jnp.int32, (tq, tk), 1)
                t = jnp.where(col <= row, t, -jnp.inf)
            m_old = m_scr[hh]
            m_new = jnp.maximum(m_old, fqs[hh] + jnp.max(t, axis=-1, keepdims=True))
            alpha = jnp.exp(m_old - m_new)
            c = fqs[hh] - m_new
            p = jnp.exp(t + jnp.concatenate([c] * (tk // w), axis=1))
            acc_scr[hh] = (jnp.concatenate([alpha, alpha], axis=1) * acc_scr[hh]
                           + jnp.dot(p.astype(BF16), v, preferred_element_type=F32))
            m_scr[hh] = m_new

    scores(0, 0)

    def body(kb, carry):
        slot = lax.rem(kb, 2)
        softmax_pv(kb, slot, False)
        scores(kb + 1, 1 - slot)
        return carry

    lax.fori_loop(0, qi, body, 0)
    softmax_pv(qi, lax.rem(qi, 2), True)
    out = None
    for hh in range(hp_heads):
        o = acc_scr[hh, :, :w] / acc_scr[hh, :, w:]
        out = o if out is None else jnp.where(lane >= hh * d, o, out)
    o_ref[...] = out.astype(o_ref.dtype)


def fox_attention(proj, fc, fr, *, batch, seq):
    T = proj.shape[0]
    tq = min(512, seq)
    nq = seq // tq
    hp_heads = 2
    n_hp = FOX_HEADS // hp_heads
    qblk, kblk, vblk = COLB["f_q"] // 128, COLB["f_k"] // 128, COLB["f_v"] // 128
    kern = functools.partial(_fox_kernel, tq=tq, hp_heads=hp_heads)
    return pl.pallas_call(
        kern,
        grid=(batch, n_hp, nq),
        scratch_shapes=[pltpu.VMEM((hp_heads, tq, hp_heads * HEAD_DIM), F32),
                        pltpu.VMEM((hp_heads, tq, 2 * hp_heads * HEAD_DIM), F32),
                        pltpu.VMEM((2, hp_heads, tq, tq), F32)],
        in_specs=[
            pl.BlockSpec((tq, 128), lambda b, hp, qi: (b * nq + qi, qblk + hp)),
            pl.BlockSpec((seq, 128), lambda b, hp, qi: (b, kblk + hp)),
            pl.BlockSpec((seq, 128), lambda b, hp, qi: (b, vblk + hp)),
            pl.BlockSpec((None, None, tq, hp_heads), lambda b, hp, qi: (b, hp, qi, 0)),
            pl.BlockSpec((None, None, hp_heads, seq), lambda b, hp, qi: (b, hp, 0, 0)),
        ],
        out_specs=pl.BlockSpec((tq, 128), lambda b, hp, qi: (b * nq + qi, hp)),
        out_shape=jax.ShapeDtypeStruct((T, BRANCH_WIDTH), BF16),
        compiler_params=_cparams("parallel", "parallel", "arbitrary"),
        name="fox_attention",
    )(proj, proj, proj, fc, fr)


def _split2(x):
    hi = x.astype(BF16)
    return hi, (x - hi.astype(F32)).astype(BF16)


def _split3(x):
    hi = x.astype(BF16)
    r = x - hi.astype(F32)
    mid = r.astype(BF16)
    return hi, mid, (r - mid.astype(F32)).astype(BF16)


def _rotate_half(x, neg_first_half):
    n = x.shape[-1]
    half = HEAD_DIM // 2
    fwd = pltpu.roll(x, half, 1)
    bwd = pltpu.roll(x, n - half, 1)
    return jnp.where(neg_first_half, -bwd, fwd)


def _linear_attn_kernel(*refs, mode, cb):
    if mode == "gla":
        (q_ref, k_ref, v_ref, gate_ref, small_ref, wg_ref, bg_ref, ng_ref, o_ref, st_ref) = refs
    else:
        (q_ref, k_ref, v_ref, gate_ref, lg_ref, cos_ref, sin_ref, gw_ref, gb_ref, o_ref, st_ref) = refs
    H, DK, DV, C = GLA_HEADS, GLA_DK, GLA_DV, CHUNK
    nch = cb // C
    W2 = 2 * DK
    HW = H * DK
    dn_lanes = (((1,), (1,)), ((), ()))
    dn_rows = (((0,), (0,)), ((), ()))

    @pl.when(pl.program_id(1) == 0)
    def _():
        st_ref[...] = jnp.zeros_like(st_ref)

    q = q_ref[...].astype(F32)
    k = k_ref[...].astype(F32)
    if mode == "gla":
        glr = small_ref[...][:, FOX_HEADS:FOX_HEADS + GLA_GATE_RANK]
        a_hi, a_lo = _split2(glr)
        w_hi, w_lo = _split2(wg_ref[...])
        z = (jnp.dot(a_hi, w_hi, preferred_element_type=F32) + jnp.dot(a_hi, w_lo, preferred_element_type=F32)
             + jnp.dot(a_lo, w_hi, preferred_element_type=F32)) + bg_ref[...]
        ld = _log_sigmoid(z) / GLA_TAU
        ld_w = jnp.concatenate([ld[c * C:(c + 1) * C] for c in range(nch)], axis=1)
        ri = lax.broadcasted_iota(jnp.int32, (C, 3 * C), 0)
        ci = lax.broadcasted_iota(jnp.int32, (C, 3 * C), 1)
        tri3 = jnp.where((ci % C) <= ri, 1.0, 0.0).astype(BF16)
        cum_w = jnp.dot(tri3, jnp.concatenate(_split3(ld_w), axis=0), preferred_element_type=F32)
        cum = jnp.concatenate([cum_w[:, c * HW:(c + 1) * HW] for c in range(nch)], axis=0)
        lasts = [cum_w[C - 1:C, c * HW:(c + 1) * HW] for c in range(nch)]
        last_b = jnp.concatenate([jnp.broadcast_to(l, (C, HW)) for l in lasts], axis=0)
    else:
        lane = lax.broadcasted_iota(jnp.int32, (cb, HW), 1)
        first_half = (lane % HEAD_DIM) < (HEAD_DIM // 2)
        cos = cos_ref[...]
        sin = sin_ref[...]
        q = q * cos + _rotate_half(q, first_half) * sin
        k = k * cos + _rotate_half(k, first_half) * sin
        steps = (lax.broadcasted_iota(jnp.int32, (cb, HW), 0) % C + 1).astype(F32)
        cum = steps * lg_ref[...]
        last_b = float(C) * lg_ref[...]
        lasts = [last_b] * nch
    q_in = (q * (DK ** -0.5) * jnp.exp(cum)).astype(BF16)
    k_in = (k * jnp.exp(-cum)).astype(BF16)
    k_st = (k * jnp.exp(last_b - cum)).astype(BF16)
    decs = [jnp.exp(l) for l in lasts]

    rowi = lax.broadcasted_iota(jnp.int32, (2 * C, W2), 0)
    lanei = lax.broadcasted_iota(jnp.int32, (2 * C, W2), 1)
    own = (rowi >= C) == (lanei >= DK)
    bd_causal = own & ((lanei % C) <= (rowi % C))
    zero_bf = jnp.zeros((2 * C, W2), BF16)

    def pair_ops(c, p):
        rows = slice(c * C, (c + 1) * C)
        ls = slice(p * W2, (p + 1) * W2)
        qp, kp, ksp = q_in[rows, ls], k_in[rows, ls], k_st[rows, ls]
        qstack = jnp.where(own, jnp.concatenate([qp, qp], axis=0), zero_bf)
        ksstack = jnp.where(own, jnp.concatenate([ksp, ksp], axis=0), zero_bf)
        k2 = jnp.concatenate([kp, kp], axis=0)
        vstack = jnp.concatenate([v_ref[rows, (2 * p) * DV:(2 * p + 1) * DV],
                                  v_ref[rows, (2 * p + 1) * DV:(2 * p + 2) * DV]], axis=0)
        sw = lax.dot_general(qstack, k2, dn_lanes, preferred_element_type=F32)
        sc = jnp.where(bd_causal, sw, 0.0).astype(BF16)
        o_intra = jnp.dot(sc, vstack, preferred_element_type=F32)
        upd = lax.dot_general(vstack, ksstack, dn_rows, preferred_element_type=F32)
        return qstack, o_intra, upd

    pre = [[pair_ops(c, p) for p in range(H // 2)] for c in range(nch)]
    state = [st_ref[p] for p in range(H // 2)]
    for c in range(nch):
        rows = slice(c * C, (c + 1) * C)
        for p in range(H // 2):
            qstack, o_intra, upd = pre[c][p]
            o = o_intra + lax.dot_general(qstack, state[p].astype(BF16), dn_lanes, preferred_element_type=F32)
            state[p] = state[p] * decs[c][:, p * W2:(p + 1) * W2] + upd
            for hh in range(2):
                hd = 2 * p + hh
                vs = slice(hd * DV, (hd + 1) * DV)
                oh = o[hh * C:(hh + 1) * C]
                gate = _silu(gate_ref[rows, vs])
                if mode == "gla":
                    y = oh * lax.rsqrt(jnp.mean(oh * oh, axis=-1, keepdims=True) + EPS) * ng_ref[...]
                else:
                    mu = jnp.mean(oh, axis=-1, keepdims=True)
                    var = jnp.mean(jnp.square(oh - mu), axis=-1, keepdims=True)
                    y = (oh - mu) * lax.rsqrt(var + EPS) * gw_ref[:, vs] + gb_ref[:, vs]
                o_ref[rows, vs] = (y * gate).astype(o_ref.dtype)
    for p in range(H // 2):
        st_ref[p] = state[p]


def linear_attention(proj, projf, params, *, mode, batch, seq):
    T = proj.shape[0]
    cb = min(512, seq)
    nc = seq // cb
    pre = "g" if mode == "gla" else "r"
    qblk, kblk = COLB[pre + "_q"] // 256, COLB[pre + "_k"] // 256
    vblk = COLB[pre + "_v"] // 512
    gblk = COLF["g_r" if mode == "gla" else "r_g"] // 512
    row = lambda b, c: b * nc + c
    in_specs = [
        pl.BlockSpec((cb, 256), lambda b, c: (row(b, c), qblk)),
        pl.BlockSpec((cb, 256), lambda b, c: (row(b, c), kblk)),
        pl.BlockSpec((cb, 512), lambda b, c: (row(b, c), vblk)),
        pl.BlockSpec((cb, 512), lambda b, c: (row(b, c), gblk)),
    ]
    args = [proj, proj, proj, projf]
    if mode == "gla":
        wg, bg, ng = params
        in_specs += [
            pl.BlockSpec((cb, 128), lambda b, c: (row(b, c), SMALL_COL // 128)),
            pl.BlockSpec(wg.shape, lambda b, c: (0, 0)),
            pl.BlockSpec((1, bg.shape[-1]), lambda b, c: (0, 0)),
            pl.BlockSpec((1, ng.shape[-1]), lambda b, c: (0, 0)),
        ]
        args += [projf, wg, bg.reshape(1, -1), ng.reshape(1, -1)]
    else:
        lg, cos, sin, gw, gb = params
        in_specs += [
            pl.BlockSpec((1, lg.shape[-1]), lambda b, c: (0, 0)),
            pl.BlockSpec((cb, 256), lambda b, c: (c, 0)),
            pl.BlockSpec((cb, 256), lambda b, c: (c, 0)),
            pl.BlockSpec((1, gw.shape[-1]), lambda b, c: (0, 0)),
            pl.BlockSpec((1, gb.shape[-1]), lambda b, c: (0, 0)),
        ]
        args += [lg.reshape(1, -1), cos, sin, gw.reshape(1, -1), gb.reshape(1, -1)]
    return pl.pallas_call(
        functools.partial(_linear_attn_kernel, mode=mode, cb=cb),
        grid=(batch, nc),
        in_specs=in_specs,
        out_specs=pl.BlockSpec((cb, 512), lambda b, c: (row(b, c), 0)),
        out_shape=jax.ShapeDtypeStruct((T, BRANCH_WIDTH), BF16),
        scratch_shapes=[pltpu.VMEM((GLA_HEADS // 2, GLA_DV, 2 * GLA_DK), F32)],
        compiler_params=_cparams("parallel", "arbitrary"),
        name="linear_attention_" + mode,
    )(*args)


def _retention_tables(seq):
    half = RET_DK // 2
    inv = RET_THETA_BASE ** (-jnp.arange(half, dtype=F32) / half)
    ang = jnp.arange(seq).astype(F32)[:, None] * inv[None, :]
    cos = jnp.tile(jnp.cos(ang), (1, 2 * RET_HEADS))
    sin = jnp.tile(jnp.sin(ang), (1, 2 * RET_HEADS))
    log_gamma = jnp.log1p(-jnp.exp2(-5.0 - jnp.arange(RET_HEADS, dtype=F32)))
    return jnp.repeat(log_gamma, RET_DK), cos, sin


def _merge_kernel(u_ref, oa_ref, ob_ref, oc_ref, od_ref, wm_ref, wb_ref, bm_ref, m_ref):
    u = u_ref[...]
    acc = None
    for i, o_ref in enumerate((oa_ref, ob_ref, oc_ref, od_ref)):
        z = jnp.dot(u, wm_ref[i].astype(BF16), preferred_element_type=F32) + bm_ref[i]
        pr = jnp.dot(o_ref[...], wb_ref[i].astype(BF16), preferred_element_type=F32)
        t = jax.nn.sigmoid(z) * pr
        acc = t if acc is None else acc + t
    m_ref[...] = acc.astype(m_ref.dtype)


def merge_branches(u, branches, wm, wb, bm, *, layer):
    T, D = u.shape
    tm, tn = min(1024, T), min(256, D)
    ospec = pl.BlockSpec((tm, BRANCH_WIDTH), lambda i, j: (i, 0))
    return pl.pallas_call(
        _merge_kernel,
        grid=(T // tm, D // tn),
        in_specs=[
            pl.BlockSpec((tm, D), lambda i, j: (i, 0)),
            ospec, ospec, ospec, ospec,
            pl.BlockSpec((None, N_BRANCH, D, tn), lambda i, j: (layer, 0, 0, j)),
            pl.BlockSpec((None, N_BRANCH, BRANCH_WIDTH, tn), lambda i, j: (layer, 0, 0, j)),
            pl.BlockSpec((None, N_BRANCH, 1, tn), lambda i, j: (layer, 0, 0, j)),
        ],
        out_specs=pl.BlockSpec((tm, tn), lambda i, j: (i, j)),
        out_shape=jax.ShapeDtypeStruct((T, D), BF16),
        compiler_params=_cparams("parallel", "arbitrary"),
        name="merge_branches",
    )(u, *branches, wm, wb, bm.reshape(bm.shape[0], N_BRANCH, 1, D))


def _out_proj_kernel(h_ref, m_ref, w_ref, mod_ref, o_ref, *, mrow, tn):
    j = pl.program_id(1)
    c0 = pl.multiple_of(j * tn, tn)
    gate = mod_ref[mrow:mrow + 1, pl.ds(c0, tn)]
    o_ref[...] = h_ref[...] + gate * jnp.dot(m_ref[...], w_ref[...].astype(BF16), preferred_element_type=F32)


def out_proj(h, merged, w, mod, *, layer, seq, mrow):
    T, D = h.shape
    tm, tn = min(1024, seq), min(512, D)
    return pl.pallas_call(
        functools.partial(_out_proj_kernel, mrow=mrow, tn=tn),
        grid=(T // tm, D // tn),
        in_specs=[
            pl.BlockSpec((tm, tn), lambda i, j: (i, j)),
            pl.BlockSpec((tm, D), lambda i, j: (i, 0)),
            pl.BlockSpec((None, D, tn), lambda i, j: (layer, 0, j)),
            pl.BlockSpec((None, N_MOD, D), lambda i, j: ((i * tm) // seq, 0, 0)),
        ],
        out_specs=pl.BlockSpec((tm, tn), lambda i, j: (i, j)),
        out_shape=jax.ShapeDtypeStruct((T, D), F32),
        compiler_params=_cparams("parallel", "arbitrary"),
        name="out_proj",
    )(h, merged, w, mod)


def _final_norm_kernel(h_ref, g_ref, o_ref):
    x = h_ref[...]
    ms = jnp.mean(x * x, axis=-1, keepdims=True)
    o_ref[...] = x * lax.rsqrt(ms + EPS) * g_ref[...]


def final_norm(h, g):
    T, D = h.shape
    tm = min(1024, T)
    return pl.pallas_call(
        _final_norm_kernel,
        grid=(T // tm,),
        in_specs=[pl.BlockSpec((tm, D), lambda i: (i, 0)), pl.BlockSpec((1, D), lambda i: (0, 0))],
        out_specs=pl.BlockSpec((tm, D), lambda i: (i, 0)),
        out_shape=jax.ShapeDtypeStruct((T, D), F32),
        compiler_params=_cparams("parallel"),
        name="final_norm",
    )(h, g.reshape(1, D))


def _permuted_w_in(w_in_l, order, cols):
    D = w_in_l.shape[0]
    parts = [w_in_l[:, _REF_COLS[n][0]:_REF_COLS[n][0] + _REF_COLS[n][1]].astype(BF16) for n in order]
    used = sum(_REF_COLS[n][1] for n in order)
    return jnp.concatenate(parts + [jnp.zeros((D, cols - used), BF16)], axis=1)


def kernel(x, c, w_ada, b_ada, norm_g, ffn_w_gate, ffn_w_up, ffn_w_down, w_in, fox_b_forget, attn_sinks,
           gla_w_gate, gla_b_gate, gla_norm_g, ret_gn_w, ret_gn_b, w_branch, w_merge, b_merge, w_out,
           final_norm_g):
    B, S, D = x.shape
    L = w_ada.shape[0]
    T = B * S
    mod_all = ada_modulation(c, w_ada, b_ada)
    log_gamma, cos, sin = _retention_tables(S)
    h = x.reshape(T, D)
    for l in range(L):
        mod = mod_all[l]
        h = ffn_block(h, mod, norm_g[l, 0], ffn_w_gate, ffn_w_up, ffn_w_down, layer=l, which=0, seq=S,
                      mrow=0, res_scale=0.5)
        proj, projf, u = norm_proj(h, mod, norm_g[l, 1], _permuted_w_in(w_in[l], _ORDER_B, PROJ_B_COLS),
                                   _permuted_w_in(w_in[l], _ORDER_F, PROJ_F_COLS), seq=S, mrow=3)
        o_a = swa_attention(proj, attn_sinks[l], batch=B, seq=S)
        fc = fox_forget_cumsum(projf, fox_b_forget[l], batch=B, seq=S)
        fc4 = fc.reshape(B, S, FOX_HEADS // 2, 2)
        o_b = fox_attention(proj, fc4.transpose(0, 2, 1, 3), fc4.transpose(0, 2, 3, 1), batch=B, seq=S)
        o_c = linear_attention(proj, projf, (gla_w_gate[l], gla_b_gate[l], gla_norm_g[l]), mode="gla",
                               batch=B, seq=S)
        o_d = linear_attention(proj, projf, (log_gamma, cos, sin, ret_gn_w[l], ret_gn_b[l]), mode="ret",
                               batch=B, seq=S)
        merged = merge_branches(u, (o_a, o_b, o_c, o_d), w_merge, w_branch, b_merge, layer=l)
        h = out_proj(h, merged, w_out, mod, layer=l, seq=S, mrow=5)
        h = ffn_block(h, mod, norm_g[l, 2], ffn_w_gate, ffn_w_up, ffn_w_down, layer=l, which=1, seq=S,
                      mrow=6, res_scale=0.5)
    return final_norm(h, final_norm_g).reshape(B, S, D)
```

```python
import functools

import numpy as np
import jax
import jax.numpy as jnp
from jax import lax
from jax.experimental import pallas as pl
from jax.experimental.pallas import tpu as pltpu

HEAD_DIM = 64
SWA_HEADS = 8
SWA_KV_HEADS = 2
WINDOW = 128
FOX_HEADS = 8
GLA_HEADS = 4
GLA_DK = 64
GLA_DV = 128
GLA_GATE_RANK = 16
GLA_TAU = 16.0
RET_HEADS = 4
RET_DK = 64
RET_DV = 128
RET_THETA_BASE = 10000.0
CHUNK = 64
BRANCH_WIDTH = 512
N_BRANCH = 4
N_MOD = 9
EPS = 1e-6

BF16 = jnp.bfloat16
F32 = jnp.float32
HIGHEST = lax.Precision.HIGHEST

VMEM_LIMIT_BYTES = 60 * 1024 * 1024

_REF_COLS = {}
_off = 0
for _name, _size in (
        ("a_q", 512), ("a_k", 128), ("a_v", 128),
        ("f_q", 512), ("f_k", 512), ("f_v", 512), ("f_f", 8),
        ("g_q", 256), ("g_k", 256), ("g_v", 512), ("g_lr", 16), ("g_r", 512),
        ("r_q", 256), ("r_k", 256), ("r_v", 512), ("r_g", 512)):
    _REF_COLS[_name] = (_off, _size)
    _off += _size
IN_COLS = _off
for _name in ("a_k", "a_v"):
    for _i in range(SWA_KV_HEADS):
        _REF_COLS[_name + str(_i)] = (_REF_COLS[_name][0] + _i * HEAD_DIM, HEAD_DIM)


def _layout(order):
    col, off = {}, 0
    for name in order:
        col.setdefault(name, off)
        off += _REF_COLS[name][1]
    return col


_ORDER_B = ("a_q", "f_q", "f_k", "f_v", "g_v", "r_v", "g_q", "g_k", "r_q", "r_k",
            "a_k0", "a_k0", "a_k1", "a_k1", "a_v0", "a_v0", "a_v1", "a_v1")
_ORDER_F = ("g_r", "r_g", "f_f", "g_lr")
COLB = _layout(_ORDER_B)
COLF = _layout(_ORDER_F)
PROJ_B_COLS = 4608
PROJ_F_COLS = 1152
SMALL_COL = COLF["f_f"]


def _cparams(*sem):
    return pltpu.CompilerParams(dimension_semantics=sem, vmem_limit_bytes=VMEM_LIMIT_BYTES)


def _tile(n, pref):
    t = (min(pref, n) // 128) * 128
    while t >= 128:
        if n % t == 0:
            return t
        t -= 128
    return n


NORM_ROWS = 64


def _adanorm_into(h_ref, g_ref, mod_ref, mrow, u_ref, zero_ref=None):
    gs = g_ref[...] * (1.0 + mod_ref[mrow + 1:mrow + 2, :])
    shift = mod_ref[mrow:mrow + 1, :]
    rows = min(NORM_ROWS, h_ref.shape[0])

    def body(r, carry):
        r0 = pl.multiple_of(r * rows, rows)
        x = h_ref[pl.ds(r0, rows), :]
        ms = jnp.mean(x * x, axis=-1, keepdims=True)
        u_ref[pl.ds(r0, rows), :] = (x * lax.rsqrt(ms + EPS) * gs + shift).astype(u_ref.dtype)
        if zero_ref is not None:
            zero_ref[pl.ds(r0, rows), :] = jnp.zeros((rows, zero_ref.shape[1]), zero_ref.dtype)
        return carry

    lax.fori_loop(0, h_ref.shape[0] // rows, body, 0)


def _log_sigmoid(x):
    return jnp.minimum(x, 0.0) - jnp.log1p(jnp.exp(-jnp.abs(x)))


def _silu(x):
    return x * jax.nn.sigmoid(x)


ADA_ROWS = 16


def _ada_kernel(c_ref, w_ref, b_ref, o_ref):
    cond = _silu(c_ref[...])
    r = jnp.dot(jnp.concatenate(_split3(cond), axis=0), w_ref[...].astype(BF16), preferred_element_type=F32)
    bp = ADA_ROWS
    o_ref[...] = (r[:bp] + r[bp:2 * bp]) + r[2 * bp:] + b_ref[...]


def ada_modulation(c, w_ada, b_ada):
    L, D, N = w_ada.shape
    B = c.shape[0]
    BP = ADA_ROWS
    cp = jnp.zeros((BP, D), F32).at[:B].set(c)
    tn = _tile(N, 1024)
    out = pl.pallas_call(
        _ada_kernel,
        grid=(L, N // tn),
        in_specs=[
            pl.BlockSpec((BP, D), lambda l, j: (0, 0)),
            pl.BlockSpec((None, D, tn), lambda l, j: (l, 0, j)),
            pl.BlockSpec((None, 1, tn), lambda l, j: (l, 0, j)),
        ],
        out_specs=pl.BlockSpec((None, BP, tn), lambda l, j: (l, 0, j)),
        out_shape=jax.ShapeDtypeStruct((L, BP, N), F32),
        compiler_params=_cparams("parallel", "parallel"),
        name="ada_modulation",
    )(cp, w_ada, b_ada.reshape(L, 1, N))
    return out[:, :B].reshape(L, B, N_MOD, D)


def _ffn_kernel(h_ref, mod_ref, g_ref, wg_ref, wu_ref, wd_ref, o_ref, u_scr, *, mrow, res_scale):
    f = pl.program_id(1)

    @pl.when(f == 0)
    def _():
        _adanorm_into(h_ref, g_ref, mod_ref, mrow, u_scr, zero_ref=o_ref)

    u = u_scr[...]
    a = jnp.dot(u, wg_ref[...].astype(BF16), preferred_element_type=F32)
    b = jnp.dot(u, wu_ref[...].astype(BF16), preferred_element_type=F32)
    mid = (_silu(a) * b).astype(BF16)
    o_ref[...] += jnp.dot(mid, wd_ref[...].astype(BF16), preferred_element_type=F32)

    @pl.when(f == pl.num_programs(1) - 1)
    def _():
        o_ref[...] = h_ref[...] + (res_scale * mod_ref[mrow + 2:mrow + 3, :]) * o_ref[...]


def ffn_block(h, mod, g, wg, wu, wd, *, layer, which, seq, mrow, res_scale):
    T, D = h.shape
    F = wg.shape[-1]
    tm = min(1024, seq)
    tf = _tile(F, 256)
    kern = functools.partial(_ffn_kernel, mrow=mrow, res_scale=res_scale)
    return pl.pallas_call(
        kern,
        grid=(T // tm, F // tf),
        in_specs=[
            pl.BlockSpec((tm, D), lambda i, f: (i, 0)),
            pl.BlockSpec((None, N_MOD, D), lambda i, f: ((i * tm) // seq, 0, 0)),
            pl.BlockSpec((1, D), lambda i, f: (0, 0)),
            pl.BlockSpec((None, None, D, tf), lambda i, f: (layer, which, 0, f)),
            pl.BlockSpec((None, None, D, tf), lambda i, f: (layer, which, 0, f)),
            pl.BlockSpec((None, None, tf, D), lambda i, f: (layer, which, f, 0)),
        ],
        out_specs=pl.BlockSpec((tm, D), lambda i, f: (i, 0)),
        out_shape=jax.ShapeDtypeStruct((T, D), F32),
        scratch_shapes=[pltpu.VMEM((tm, D), BF16)],
        compiler_params=_cparams("parallel", "arbitrary"),
        name="ffn_block",
    )(h, mod, g.reshape(1, D), wg, wu, wd)


PROJ_COL_CHUNK = 512


def _norm_proj_kernel(h_ref, mod_ref, g_ref, wb_ref, wf_ref, pb_ref, pf_ref, u_ref, *, mrow):
    _adanorm_into(h_ref, g_ref, mod_ref, mrow, u_ref)
    u = u_ref[...]
    for c0 in range(0, pb_ref.shape[1], PROJ_COL_CHUNK):
        cols = slice(c0, c0 + PROJ_COL_CHUNK)
        pb_ref[:, cols] = jnp.dot(u, wb_ref[:, cols], preferred_element_type=F32).astype(pb_ref.dtype)
    pf_ref[...] = jnp.dot(u, wf_ref[...], preferred_element_type=F32)


def norm_proj(h, mod, g, wb, wf, *, seq, mrow):
    T, D = h.shape
    NB, NF = wb.shape[1], wf.shape[1]
    tm = min(512, seq)
    kern = functools.partial(_norm_proj_kernel, mrow=mrow)
    resident = pl.Buffered(1)
    return pl.pallas_call(
        kern,
        grid=(T // tm,),
        in_specs=[
            pl.BlockSpec((tm, D), lambda i: (i, 0)),
            pl.BlockSpec((None, N_MOD, D), lambda i: ((i * tm) // seq, 0, 0)),
            pl.BlockSpec((1, D), lambda i: (0, 0)),
            pl.BlockSpec((D, NB), lambda i: (0, 0), pipeline_mode=resident),
            pl.BlockSpec((D, NF), lambda i: (0, 0), pipeline_mode=resident),
        ],
        out_specs=[
            pl.BlockSpec((tm, NB), lambda i: (i, 0)),
            pl.BlockSpec((tm, NF), lambda i: (i, 0)),
            pl.BlockSpec((tm, D), lambda i: (i, 0)),
        ],
        out_shape=[jax.ShapeDtypeStruct((T, NB), BF16), jax.ShapeDtypeStruct((T, NF), F32),
                   jax.ShapeDtypeStruct((T, D), BF16)],
        compiler_params=_cparams("parallel"),
        name="norm_proj",
    )(h, mod, g.reshape(1, D), wb, wf)


def _swa_kernel(sink_ref, q_ref, kv_ref, kvp_ref, o_ref):
    n = pl.program_id(1)
    W = WINDOW
    d = HEAD_DIM
    q = q_ref[...] * (d ** -0.5)
    kv = jnp.concatenate([kvp_ref[...], kv_ref[...]], axis=0)
    rowi = lax.broadcasted_iota(jnp.int32, (2 * W, 2 * d), 0)
    lanei = lax.broadcasted_iota(jnp.int32, (2 * W, 2 * d), 1)
    own = (rowi >= W) == (lanei >= d)
    first_half = lax.broadcasted_iota(jnp.int32, (W, 2 * d), 1) < d
    qpos =lax.broadcasted_iota(jnp.int32, (2 * W, 2 * W), 0) % W + W
    kpos = lax.broadcasted_iota(jnp.int32, (2 * W, 2 * W), 1)
    rel = qpos - kpos
    mask = (rel >= 0) & (rel < W) & ((kpos >= W) | (n > 0))
    ones = jnp.ones((2 * W, 2 * d), BF16)
    group = SWA_HEADS // SWA_KV_HEADS
    for g in range(SWA_HEADS // 2):
        kvh = (2 * g) // group
        k2 = kv[:, kvh * 2 * d:(kvh + 1) * 2 * d]
        v2 = kv[:, (SWA_KV_HEADS + kvh) * 2 * d:(SWA_KV_HEADS + kvh + 1) * 2 * d]
        v_ext = jnp.concatenate([v2, ones], axis=1)
        qp = q[:, g * 2 * d:(g + 1) * 2 * d]
        qstack = jnp.where(own, jnp.concatenate([qp, qp], axis=0), jnp.zeros((2 * W, 2 * d), BF16))
        logits = lax.dot_general(qstack, k2, (((1,), (1,)), ((), ())), preferred_element_type=F32)
        logits = jnp.where(mask, logits, -jnp.inf)
        sink = jnp.where(rowi >= W, sink_ref[2 * g + 1], sink_ref[2 * g])
        m = jnp.maximum(jnp.max(logits, axis=-1, keepdims=True), sink)
        p = jnp.exp(logits - jnp.concatenate([m, m], axis=1)).astype(BF16)
        r = jnp.dot(p, v_ext, preferred_element_type=F32)
        o2 = r[:, :2 * d] / (r[:, 2 * d:] + jnp.exp(sink - m))
        out = jnp.where(first_half, o2[:W], o2[W:])
        o_ref[:, g * 2 * d:(g + 1) * 2 * d] = out.astype(o_ref.dtype)


def swa_attention(proj, sinks, *, batch, seq):
    T = proj.shape[0]
    nb = seq // WINDOW
    qblk = COLB["a_q"] // 512
    kvblk = COLB["a_k0"] // 512
    return pl.pallas_call(
        _swa_kernel,
        grid=(batch, nb),
        in_specs=[
            pl.BlockSpec(memory_space=pltpu.SMEM),
            pl.BlockSpec((WINDOW, 512), lambda b, n: (b * nb + n, qblk)),
            pl.BlockSpec((WINDOW, 512), lambda b, n: (b * nb + n, kvblk)),
            pl.BlockSpec((WINDOW, 512), lambda b, n: (b * nb + jnp.maximum(n - 1, 0), kvblk)),
        ],
        out_specs=pl.BlockSpec((WINDOW, 512), lambda b, n: (b * nb + n, 0)),
        out_shape=jax.ShapeDtypeStruct((T, BRANCH_WIDTH), BF16),
        compiler_params=_cparams("parallel", "arbitrary"),
        name="swa_attention",
    )(sinks, proj, proj, proj)


def _fox_cum_kernel(x_ref, b_ref, o_ref, *, seq):
    R = 128
    ri = lax.broadcasted_iota(jnp.int32, (R, R), 0)
    ci = lax.broadcasted_iota(jnp.int32, (R, R), 1)
    tri = (ci <= ri).astype(F32)

    def body(i, carry):
        r0 = pl.multiple_of(i * R, R)
        x = x_ref[pl.ds(r0, R), :][:, :FOX_HEADS] + b_ref[...]
        ls = _log_sigmoid(x)
        cum = jnp.dot(tri, ls, precision=HIGHEST, preferred_element_type=F32) + carry
        o_ref[pl.ds(r0, R), :] = cum
        return cum[R - 1:R, :]

    lax.fori_loop(0, seq // R, body, jnp.zeros((1, FOX_HEADS), F32))


def fox_forget_cumsum(proj, fox_b, *, batch, seq):
    T = proj.shape[0]
    blk = SMALL_COL // 128
    return pl.pallas_call(
        functools.partial(_fox_cum_kernel, seq=seq),
        grid=(batch,),
        in_specs=[
            pl.BlockSpec((seq, 128), lambda b: (b, blk)),
            pl.BlockSpec((1, FOX_HEADS), lambda b: (0, 0)),
        ],
        out_specs=pl.BlockSpec((seq, FOX_HEADS), lambda b: (b, 0)),
        out_shape=jax.ShapeDtypeStruct((T, FOX_HEADS), F32),
        compiler_params=_cparams("parallel"),
        name="fox_forget_cumsum",
    )(proj, fox_b.reshape(1, FOX_HEADS))


FOX_KEY_ROWS = 512


def _fox_kernel(q_ref, k_ref, v_ref, fc_ref, fcs_ref, o_ref, m_scr, acc_scr, s_scr, kx_scr, *, tq, hp_heads):
    qi = pl.program_id(2)
    tk = tq
    d = HEAD_DIM
    w = hp_heads * d
    seq = k_ref.shape[0]

    @pl.when(qi == 0)
    def _():
        kr = min(FOX_KEY_ROWS, seq)
        lane_k = lax.broadcasted_iota(jnp.int32, (kr, w), 1)
        for r0 in range(0, seq, kr):
            rows = slice(r0, r0 + kr)
            ext = jnp.zeros((kr, w), F32)
            for hh in range(hp_heads):
                pieces = _split3(jnp.broadcast_to(-fcs_ref[rows, hh:hh + 1], (kr, w)))
                for j, piece in enumerate(pieces):
                    ext = jnp.where(lane_k == 3 * hh + j, piece.astype(F32), ext)
            kx_scr[rows, :w] = k_ref[rows, :]
            kx_scr[rows, w:] = ext.astype(BF16)

    lane = lax.broadcasted_iota(jnp.int32, (tq, w), 1)
    q = q_ref[...] * (d ** -0.5)
    zero = jnp.zeros_like(q)
    qs = [jnp.concatenate([jnp.where((lane >= hh * d) & (lane < (hh + 1) * d), q, zero),
                           jnp.where((lane >= 3 * hh) & (lane < 3 * hh + 3), 1.0, 0.0).astype(BF16)], axis=1)
          for hh in range(hp_heads)]
    m_scr[...] = jnp.full(m_scr.shape, -jnp.inf, F32)
    acc_scr[...] = jnp.zeros(acc_scr.shape, F32)
    fqs = [jnp.broadcast_to(fc_ref[:, hh:hh + 1], (tq, w)) for hh in range(hp_heads)]
    ones = jnp.ones((tk, w), BF16)

    def scores(kb, slot):
        k0 = pl.multiple_of(kb * tk, tk)
        kx = kx_scr[pl.ds(k0, tk), :]
        for hh in range(hp_heads):
            s_scr[slot, hh] = lax.dot_general(qs[hh], kx, (((1,), (1,)), ((), ())), preferred_element_type=F32)

    def softmax_pv(kb, slot, masked):
        k0 = pl.multiple_of(kb * tk, tk)
        v = jnp.concatenate([v_ref[pl.ds(k0, tk), :], ones], axis=1)
        for hh in range(hp_heads):
            t = s_scr[slot, hh]
            if masked:
                row = lax.broadcasted_iota(jnp.int32, (tq, tk), 0)
                col = lax.broadcasted_iota(jnp.int32, (tq, tk), 1)
                t = jnp.where(col <= row, t, -jnp.inf)
            m_old = m_scr[hh]
            m_new = jnp.maximum(m_old, fqs[hh] + jnp.max(t, axis=-1, keepdims=True))
            alpha = jnp.exp(m_old - m_new)
            c = fqs[hh] - m_new
            p = jnp.exp(t + jnp.concatenate([c] * (tk // w), axis=1))
            acc_scr[hh] = (jnp.concatenate([alpha, alpha], axis=1) * acc_scr[hh]
                           + jnp.dot(p.astype(BF16), v, preferred_element_type=F32))
            m_scr[hh] = m_new

    scores(0, 0)

    def body(kb, carry):
        slot = lax.rem(kb, 2)
        softmax_pv(kb, slot, False)
        scores(kb + 1, 1 - slot)
        return carry

    lax.fori_loop(0, qi, body, 0)
    softmax_pv(qi, lax.rem(qi, 2), True)
    out = None
    for hh in range(hp_heads):
        o = acc_scr[hh, :, :w] / acc_scr[hh, :, w:]
        out = o if out is None else jnp.where(lane >= hh * d, o, out)
    o_ref[...] = out.astype(o_ref.dtype)


def fox_attention(proj, fc, *, batch, seq):
    T = proj.shape[0]
    tq = min(512, seq)
    nq = seq // tq
    hp_heads = 2
    n_hp = FOX_HEADS // hp_heads
    qblk, kblk, vblk = COLB["f_q"] // 128, COLB["f_k"] // 128, COLB["f_v"] // 128
    kern = functools.partial(_fox_kernel, tq=tq, hp_heads=hp_heads)
    return pl.pallas_call(
        kern,
        grid=(batch, n_hp, nq),
        scratch_shapes=[pltpu.VMEM((hp_heads, tq, hp_heads * HEAD_DIM), F32),
                        pltpu.VMEM((hp_heads, tq, 2 * hp_heads * HEAD_DIM), F32),
                        pltpu.VMEM((2, hp_heads, tq, tq), F32),
                        pltpu.VMEM((seq, 2 * hp_heads * HEAD_DIM), BF16)],
        in_specs=[
            pl.BlockSpec((tq, 128), lambda b, hp, qi: (b * nq + qi, qblk + hp)),
            pl.BlockSpec((seq, 128), lambda b, hp, qi: (b, kblk + hp)),
            pl.BlockSpec((seq, 128), lambda b, hp, qi: (b, vblk + hp)),
            pl.BlockSpec((None, None, tq, hp_heads), lambda b, hp, qi: (b, hp, qi, 0)),
            pl.BlockSpec((None, None, seq, hp_heads), lambda b, hp, qi: (b, hp, 0, 0)),
        ],
        out_specs=pl.BlockSpec((tq, 128), lambda b, hp, qi: (b * nq + qi, hp)),
        out_shape=jax.ShapeDtypeStruct((T, BRANCH_WIDTH), BF16),
        compiler_params=_cparams("parallel", "parallel", "arbitrary"),
        name="fox_attention",
    )(proj, proj, proj, fc, fc)


def _split2(x):
    hi = x.astype(BF16)
    return hi, (x - hi.astype(F32)).astype(BF16)


def _split3(x):
    hi = x.astype(BF16)
    r = x - hi.astype(F32)
    mid = r.astype(BF16)
    return hi, mid, (r - mid.astype(F32)).astype(BF16)


def _rotate_half(x, neg_first_half):
    n = x.shape[-1]
    half = HEAD_DIM // 2
    fwd = pltpu.roll(x, half, 1)
    bwd = pltpu.roll(x, n - half, 1)
    return jnp.where(neg_first_half, -bwd, fwd)


def _linear_attn_kernel(*refs, mode, cb):
    if mode == "gla":
        (q_ref, k_ref, v_ref, gate_ref, small_ref, wg_ref, bg_ref, ng_ref, o_ref, st_ref) = refs
    else:
        (q_ref, k_ref, v_ref, gate_ref, lg_ref, cos_ref, sin_ref, gw_ref, gb_ref, o_ref, st_ref) = refs
    H, DK, DV, C = GLA_HEADS, GLA_DK, GLA_DV, CHUNK
    nch = cb // C
    W2 = 2 * DK
    HW = H * DK
    dn_lanes = (((1,), (1,)), ((), ()))
    dn_rows = (((0,), (0,)), ((), ()))

    @pl.when(pl.program_id(1) == 0)
    def _():
        st_ref[...] = jnp.zeros_like(st_ref)

    q = q_ref[...].astype(F32)
    k = k_ref[...].astype(F32)
    if mode == "gla":
        glr = small_ref[...][:, FOX_HEADS:FOX_HEADS + GLA_GATE_RANK]
        a_hi, a_lo = _split2(glr)
        w_hi, w_lo = _split2(wg_ref[...])
        z = (jnp.dot(a_hi, w_hi, preferred_element_type=F32) + jnp.dot(a_hi, w_lo, preferred_element_type=F32)
             + jnp.dot(a_lo, w_hi, preferred_element_type=F32)) + bg_ref[...]
        ld = _log_sigmoid(z) / GLA_TAU
        ld_w = jnp.concatenate([ld[c * C:(c + 1) * C] for c in range(nch)], axis=1)
        ri = lax.broadcasted_iota(jnp.int32, (C, 3 * C), 0)
        ci = lax.broadcasted_iota(jnp.int32, (C, 3 * C), 1)
        tri3 = jnp.where((ci % C) <= ri, 1.0, 0.0).astype(BF16)
        cum_w = jnp.dot(tri3, jnp.concatenate(_split3(ld_w), axis=0), preferred_element_type=F32)
        cum = jnp.concatenate([cum_w[:, c * HW:(c + 1) * HW] for c in range(nch)], axis=0)
        lasts = [cum_w[C - 1:C, c * HW:(c + 1) * HW] for c in range(nch)]
        last_b = jnp.concatenate([jnp.broadcast_to(l, (C, HW)) for l in lasts], axis=0)
    else:
        lane = lax.broadcasted_iota(jnp.int32, (cb, HW), 1)
        first_half = (lane % HEAD_DIM) < (HEAD_DIM // 2)
        cos = cos_ref[...]
        sin = sin_ref[...]
        q = q * cos + _rotate_half(q, first_half) * sin
        k = k * cos + _rotate_half(k, first_half) * sin
        steps = (lax.broadcasted_iota(jnp.int32, (cb, HW), 0) % C + 1).astype(F32)
        cum = steps * lg_ref[...]
        last_b = float(C) * lg_ref[...]
        lasts = [last_b] * nch
    q_in = (q * (DK ** -0.5) * jnp.exp(cum)).astype(BF16)
    k_in = (k * jnp.exp(-cum)).astype(BF16)
    k_st = (k * jnp.exp(last_b - cum)).astype(BF16)
    decs = [jnp.exp(l) for l in lasts]

    rowi = lax.broadcasted_iota(jnp.int32, (2 * C, W2), 0)
    lanei = lax.broadcasted_iota(jnp.int32, (2 * C, W2), 1)
    own = (rowi >= C) == (lanei >= DK)
    bd_causal = own & ((lanei % C) <= (rowi % C))
    zero_bf = jnp.zeros((2 * C, W2), BF16)

    def pair_ops(c, p):
        rows = slice(c * C, (c + 1) * C)
        ls = slice(p * W2, (p + 1) * W2)
        qp, kp, ksp = q_in[rows, ls], k_in[rows, ls], k_st[rows, ls]
        qstack = jnp.where(own, jnp.concatenate([qp, qp], axis=0), zero_bf)
        ksstack = jnp.where(own, jnp.concatenate([ksp, ksp], axis=0), zero_bf)
        k2 = jnp.concatenate([kp, kp], axis=0)
        vstack = jnp.concatenate([v_ref[rows, (2 * p) * DV:(2 * p + 1) * DV],
                                  v_ref[rows, (2 * p + 1) * DV:(2 * p + 2) * DV]], axis=0)
        sw = lax.dot_general(qstack, k2, dn_lanes, preferred_element_type=F32)
        sc = jnp.where(bd_causal, sw, 0.0).astype(BF16)
        o_intra = jnp.dot(sc, vstack, preferred_element_type=F32)
        upd = lax.dot_general(vstack, ksstack, dn_rows, preferred_element_type=F32)
        return qstack, o_intra, upd

    pre = [[pair_ops(c, p) for p in range(H // 2)] for c in range(nch)]
    state = [st_ref[p] for p in range(H // 2)]
    for c in range(nch):
        rows = slice(c * C, (c + 1) * C)
        for p in range(H // 2):
            qstack, o_intra, upd = pre[c][p]
            o = o_intra + lax.dot_general(qstack, state[p].astype(BF16), dn_lanes, preferred_element_type=F32)
            state[p] = state[p] * decs[c][:, p * W2:(p + 1) * W2] + upd
            for hh in range(2):
                hd = 2 * p + hh
                vs = slice(hd * DV, (hd + 1) * DV)
                oh = o[hh * C:(hh + 1) * C]
                gate = _silu(gate_ref[rows, vs])
                if mode == "gla":
                    y = oh * lax.rsqrt(jnp.mean(oh * oh, axis=-1, keepdims=True) + EPS) * ng_ref[...]
                else:
                    mu = jnp.mean(oh, axis=-1, keepdims=True)
                    var = jnp.mean(jnp.square(oh - mu), axis=-1, keepdims=True)
                    y = (oh - mu) * lax.rsqrt(var + EPS) * gw_ref[:, vs] + gb_ref[:, vs]
                o_ref[rows, vs] = (y * gate).astype(o_ref.dtype)
    for p in range(H // 2):
        st_ref[p] = state[p]


def linear_attention(proj, projf, params, *, mode, batch, seq):
    T = proj.shape[0]
    cb = min(512, seq)
    nc = seq // cb
    pre = "g" if mode == "gla" else "r"
    qblk, kblk = COLB[pre + "_q"] // 256, COLB[pre + "_k"] // 256
    vblk = COLB[pre + "_v"] // 512
    gblk = COLF["g_r" if mode == "gla" else "r_g"] // 512
    row = lambda b, c: b * nc + c
    in_specs = [
        pl.BlockSpec((cb, 256), lambda b, c: (row(b, c), qblk)),
        pl.BlockSpec((cb, 256), lambda b, c: (row(b, c), kblk)),
        pl.BlockSpec((cb, 512), lambda b, c: (row(b, c), vblk)),
        pl.BlockSpec((cb, 512), lambda b, c: (row(b, c), gblk)),
    ]
    args = [proj, proj, proj, projf]
    if mode == "gla":
        wg, bg, ng = params
        in_specs += [
            pl.BlockSpec((cb, 128), lambda b, c: (row(b, c), SMALL_COL // 128)),
            pl.BlockSpec(wg.shape, lambda b, c: (0, 0)),
            pl.BlockSpec((1, bg.shape[-1]), lambda b, c: (0, 0)),
            pl.BlockSpec((1, ng.shape[-1]), lambda b, c: (0, 0)),
        ]
        args += [projf, wg, bg.reshape(1, -1), ng.reshape(1, -1)]
    else:
        lg, cos, sin, gw, gb = params
        in_specs += [
            pl.BlockSpec((1, lg.shape[-1]), lambda b, c: (0, 0)),
            pl.BlockSpec((cb, 256), lambda b, c: (c, 0)),
            pl.BlockSpec((cb, 256), lambda b, c: (c, 0)),
            pl.BlockSpec((1, gw.shape[-1]), lambda b, c: (0, 0)),
            pl.BlockSpec((1, gb.shape[-1]), lambda b, c: (0, 0)),
        ]
        args += [lg.reshape(1, -1), cos, sin, gw.reshape(1, -1), gb.reshape(1, -1)]
    return pl.pallas_call(
        functools.partial(_linear_attn_kernel, mode=mode, cb=cb),
        grid=(batch, nc),
        in_specs=in_specs,
        out_specs=pl.BlockSpec((cb, 512), lambda b, c: (row(b, c), 0)),
        out_shape=jax.ShapeDtypeStruct((T, BRANCH_WIDTH), BF16),
        scratch_shapes=[pltpu.VMEM((GLA_HEADS // 2, GLA_DV, 2 * GLA_DK), F32)],
        compiler_params=_cparams("parallel", "arbitrary"),
        name="linear_attention_" + mode,
    )(*args)


def _retention_tables(seq):
    half = RET_DK // 2
    inv = RET_THETA_BASE ** (-jnp.arange(half, dtype=F32) / half)
    ang = jnp.arange(seq).astype(F32)[:, None] * inv[None, :]
    cos = jnp.tile(jnp.cos(ang), (1, 2 * RET_HEADS))
    sin = jnp.tile(jnp.sin(ang), (1, 2 * RET_HEADS))
    log_gamma = jnp.log1p(-jnp.exp2(-5.0 - jnp.arange(RET_HEADS, dtype=F32)))
    return jnp.repeat(log_gamma, RET_DK), cos, sin


def _merge_kernel(u_ref, oa_ref, ob_ref, oc_ref, od_ref, wm_ref, wb_ref, bm_ref, m_ref):
    u = u_ref[...]
    acc = None
    for i, o_ref in enumerate((oa_ref, ob_ref, oc_ref, od_ref)):
        z = jnp.dot(u, wm_ref[i].astype(BF16), preferred_element_type=F32) + bm_ref[i]
        pr = jnp.dot(o_ref[...], wb_ref[i].astype(BF16), preferred_element_type=F32)
        t = jax.nn.sigmoid(z) * pr
        acc = t if acc is None else acc + t
    m_ref[...] = acc.astype(m_ref.dtype)


def merge_branches(u, branches, wm, wb, bm, *, layer):
    T, D = u.shape
    tm, tn = min(1024, T), min(256, D)
    ospec = pl.BlockSpec((tm, BRANCH_WIDTH), lambda i, j: (i, 0))
    return pl.pallas_call(
        _merge_kernel,
        grid=(T // tm, D // tn),
        in_specs=[
            pl.BlockSpec((tm, D), lambda i, j: (i, 0)),
            ospec, ospec, ospec, ospec,
            pl.BlockSpec((None, N_BRANCH, D, tn), lambda i, j: (layer, 0, 0, j)),
            pl.BlockSpec((None, N_BRANCH, BRANCH_WIDTH, tn), lambda i, j: (layer, 0, 0, j)),
            pl.BlockSpec((None, N_BRANCH, 1, tn), lambda i, j: (layer, 0, 0, j)),
        ],
        out_specs=pl.BlockSpec((tm, tn), lambda i, j: (i, j)),
        out_shape=jax.ShapeDtypeStruct((T, D), BF16),
        compiler_params=_cparams("parallel", "arbitrary"),
        name="merge_branches",
    )(u, *branches, wm, wb, bm.reshape(bm.shape[0], N_BRANCH, 1, D))


def _out_proj_kernel(h_ref, m_ref, w_ref, mod_ref, o_ref, *, mrow):
    m = m_ref[...]
    for c0 in range(0, o_ref.shape[1], PROJ_COL_CHUNK):
        cols = slice(c0, c0 + PROJ_COL_CHUNK)
        y = jnp.dot(m, w_ref[:, cols], preferred_element_type=F32)
        o_ref[:, cols] = h_ref[:, cols] + mod_ref[mrow:mrow + 1, cols] * y


def out_proj(h, merged, w, mod, *, layer, seq, mrow):
    T, D = h.shape
    tm = min(512, seq)
    return pl.pallas_call(
        functools.partial(_out_proj_kernel, mrow=mrow),
        grid=(T // tm,),
        in_specs=[
            pl.BlockSpec((tm, D), lambda i: (i, 0)),
            pl.BlockSpec((tm, D), lambda i: (i, 0)),
            pl.BlockSpec((None, D, D), lambda i: (layer, 0, 0), pipeline_mode=pl.Buffered(1)),
            pl.BlockSpec((None, N_MOD, D), lambda i: ((i * tm) // seq, 0, 0)),
        ],
        out_specs=pl.BlockSpec((tm, D), lambda i: (i, 0)),
        out_shape=jax.ShapeDtypeStruct((T, D), F32),
        compiler_params=_cparams("parallel"),
        name="out_proj",
    )(h, merged, w, mod)


def _final_norm_kernel(h_ref, g_ref, o_ref):
    x = h_ref[...]
    ms = jnp.mean(x * x, axis=-1, keepdims=True)
    o_ref[...] = x * lax.rsqrt(ms + EPS) * g_ref[...]


def final_norm(h, g):
    T, D = h.shape
    tm = min(1024, T)
    return pl.pallas_call(
        _final_norm_kernel,
        grid=(T // tm,),
        in_specs=[pl.BlockSpec((tm, D), lambda i: (i, 0)), pl.BlockSpec((1, D), lambda i: (0, 0))],
        out_specs=pl.BlockSpec((tm, D), lambda i: (i, 0)),
        out_shape=jax.ShapeDtypeStruct((T, D), F32),
        compiler_params=_cparams("parallel"),
        name="final_norm",
    )(h, g.reshape(1, D))


def _permuted_w_in(w_in_l, order, cols):
    D = w_in_l.shape[0]
    parts = [w_in_l[:, _REF_COLS[n][0]:_REF_COLS[n][0] + _REF_COLS[n][1]].astype(BF16) for n in order]
    used = sum(_REF_COLS[n][1] for n in order)
    return jnp.concatenate(parts + [jnp.zeros((D, cols - used), BF16)], axis=1)


def kernel(x, c, w_ada, b_ada, norm_g, ffn_w_gate, ffn_w_up, ffn_w_down, w_in, fox_b_forget, attn_sinks,
           gla_w_gate, gla_b_gate, gla_norm_g, ret_gn_w, ret_gn_b, w_branch, w_merge, b_merge, w_out,
           final_norm_g):
    B, S, D = x.shape
    L = w_ada.shape[0]
    T = B * S
    mod_all = ada_modulation(c, w_ada, b_ada)
    log_gamma, cos, sin = _retention_tables(S)
    w_out_bf = w_out.astype(BF16)
    h = x.reshape(T, D)
    for l in range(L):
        mod = mod_all[l]
        h = ffn_block(h, mod, norm_g[l, 0], ffn_w_gate, ffn_w_up, ffn_w_down, layer=l, which=0, seq=S,
                      mrow=0, res_scale=0.5)
        proj, projf, u = norm_proj(h, mod, norm_g[l, 1], _permuted_w_in(w_in[l], _ORDER_B, PROJ_B_COLS),
                                   _permuted_w_in(w_in[l], _ORDER_F, PROJ_F_COLS), seq=S, mrow=3)
        o_a = swa_attention(proj, attn_sinks[l], batch=B, seq=S)
        fc = fox_forget_cumsum(projf, fox_b_forget[l], batch=B, seq=S)
        fc4 = fc.reshape(B, S, FOX_HEADS // 2, 2)
        o_b = fox_attention(proj, fc4.transpose(0, 2, 1, 3), batch=B, seq=S)
        o_c = linear_attention(proj, projf, (gla_w_gate[l], gla_b_gate[l], gla_norm_g[l]), mode="gla",
                               batch=B, seq=S)
        o_d = linear_attention(proj, projf, (log_gamma, cos, sin, ret_gn_w[l], ret_gn_b[l]), mode="ret",
                               batch=B, seq=S)
        merged = merge_branches(u, (o_a, o_b, o_c, o_d), w_merge, w_branch, b_merge, layer=l)
        h = out_proj(h, merged, w_out_bf, mod, layer=l, seq=S, mrow=5)
        h = ffn_block(h, mod, norm_g[l, 2], ffn_w_gate, ffn_w_up, ffn_w_down, layer=l, which=1, seq=S,
                      mrow=6, res_scale=0.5)
    return final_norm(h, final_norm_g).reshape(B, S, D)
```

```python
import functools

import numpy as np
import jax
import jax.numpy as jnp
from jax import lax
from jax.experimental import pallas as pl
from jax.experimental.pallas import tpu as pltpu

HEAD_DIM = 64
SWA_HEADS = 8
SWA_KV_HEADS = 2
WINDOW = 128
FOX_HEADS = 8
GLA_HEADS = 4
GLA_DK = 64
GLA_DV = 128
GLA_GATE_RANK = 16
GLA_TAU = 16.0
RET_HEADS = 4
RET_DK = 64
RET_DV = 128
RET_THETA_BASE = 10000.0
CHUNK = 64
BRANCH_WIDTH = 512
N_BRANCH = 4
N_MOD = 9
EPS = 1e-6

BF16 = jnp.bfloat16
F32 = jnp.float32
HIGHEST = lax.Precision.HIGHEST

VMEM_LIMIT_BYTES = 60 * 1024 * 1024

_REF_COLS = {}
_off = 0
for _name, _size in (
        ("a_q", 512), ("a_k", 128), ("a_v", 128),
        ("f_q", 512), ("f_k", 512), ("f_v", 512), ("f_f", 8),
        ("g_q", 256), ("g_k", 256), ("g_v", 512), ("g_lr", 16), ("g_r", 512),
        ("r_q", 256), ("r_k", 256), ("r_v", 512), ("r_g", 512)):
    _REF_COLS[_name] = (_off, _size)
    _off += _size
IN_COLS = _off
for _name in ("a_k", "a_v"):
    for _i in range(SWA_KV_HEADS):
        _REF_COLS[_name + str(_i)] = (_REF_COLS[_name][0] + _i * HEAD_DIM, HEAD_DIM)


def _layout(order):
    col, off = {}, 0
    for name in order:
        col.setdefault(name, off)
        off += _REF_COLS[name][1]
    return col


_ORDER_B = ("a_q", "f_q", "f_k", "f_v", "g_v", "r_v", "g_q", "g_k", "r_q", "r_k",
            "a_k0", "a_k0", "a_k1", "a_k1", "a_v0", "a_v0", "a_v1", "a_v1")
_ORDER_F = ("g_r", "r_g", "f_f", "g_lr")
COLB = _layout(_ORDER_B)
COLF = _layout(_ORDER_F)
PROJ_B_COLS = 4608
PROJ_F_COLS = 1152
SMALL_COL = COLF["f_f"]


def _cparams(*sem):
    return pltpu.CompilerParams(dimension_semantics=sem, vmem_limit_bytes=VMEM_LIMIT_BYTES)


def _tile(n, pref):
    t = (min(pref, n) // 128) * 128
    while t >= 128:
        if n % t == 0:
            return t
        t -= 128
    return n


NORM_ROWS = 64


def _adanorm_into(h_ref, g_ref, mod_ref, mrow, u_ref, zero_ref=None):
    gs = g_ref[...] * (1.0 + mod_ref[mrow + 1:mrow + 2, :])
    shift = mod_ref[mrow:mrow + 1, :]
    rows = min(NORM_ROWS, h_ref.shape[0])

    def body(r, carry):
        r0 = pl.multiple_of(r * rows, rows)
        x = h_ref[pl.ds(r0, rows), :]
        ms = jnp.mean(x * x, axis=-1, keepdims=True)
        u_ref[pl.ds(r0, rows), :] = (x * lax.rsqrt(ms + EPS) * gs + shift).astype(u_ref.dtype)
        if zero_ref is not None:
            zero_ref[pl.ds(r0, rows), :] = jnp.zeros((rows, zero_ref.shape[1]), zero_ref.dtype)
        return carry

    lax.fori_loop(0, h_ref.shape[0] // rows, body, 0)


def _log_sigmoid(x):
    return jnp.minimum(x, 0.0) - jnp.log1p(jnp.exp(-jnp.abs(x)))


def _silu(x):
    return x * jax.nn.sigmoid(x)


ADA_ROWS = 16


def _ada_kernel(c_ref, w_ref, b_ref, o_ref):
    cond = _silu(c_ref[...])
    r = jnp.dot(jnp.concatenate(_split3(cond), axis=0), w_ref[...].astype(BF16), preferred_element_type=F32)
    bp = ADA_ROWS
    o_ref[...] = (r[:bp] + r[bp:2 * bp]) + r[2 * bp:] + b_ref[...]


def ada_modulation(c, w_ada, b_ada):
    L, D, N = w_ada.shape
    B = c.shape[0]
    BP = ADA_ROWS
    cp = jnp.zeros((BP, D), F32).at[:B].set(c)
    tn = _tile(N, 1024)
    out = pl.pallas_call(
        _ada_kernel,
        grid=(L, N // tn),
        in_specs=[
            pl.BlockSpec((BP, D), lambda l, j: (0, 0)),
            pl.BlockSpec((None, D, tn), lambda l, j: (l, 0, j)),
            pl.BlockSpec((None, 1, tn), lambda l, j: (l, 0, j)),
        ],
        out_specs=pl.BlockSpec((None, BP, tn), lambda l, j: (l, 0, j)),
        out_shape=jax.ShapeDtypeStruct((L, BP, N), F32),
        compiler_params=_cparams("parallel", "parallel"),
        name="ada_modulation",
    )(cp, w_ada, b_ada.reshape(L, 1, N))
    return out[:, :B].reshape(L, B, N_MOD, D)


def _ffn_kernel(h_ref, mod_ref, g_ref, wg_ref, wu_ref, wd_ref, *rest, mrow, res_scale, final_norm):
    if final_norm:
        fg_ref, o_ref, u_scr = rest
    else:
        o_ref, u_scr = rest
    f = pl.program_id(1)

    @pl.when(f == 0)
    def _():
        _adanorm_into(h_ref, g_ref, mod_ref, mrow, u_scr, zero_ref=o_ref)

    u = u_scr[...]
    a = jnp.dot(u, wg_ref[...].astype(BF16), preferred_element_type=F32)
    b = jnp.dot(u, wu_ref[...].astype(BF16), preferred_element_type=F32)
    mid = (_silu(a) * b).astype(BF16)
    o_ref[...] += jnp.dot(mid, wd_ref[...].astype(BF16), preferred_element_type=F32)

    @pl.when(f == pl.num_programs(1) - 1)
    def _():
        gate = res_scale * mod_ref[mrow + 2:mrow + 3, :]
        rows = min(NORM_ROWS, h_ref.shape[0])

        def body(r, carry):
            r0 = pl.multiple_of(r * rows, rows)
            y = h_ref[pl.ds(r0, rows), :] + gate * o_ref[pl.ds(r0, rows), :]
            if final_norm:
                y = y * lax.rsqrt(jnp.mean(y * y, axis=-1, keepdims=True) + EPS) * fg_ref[...]
            o_ref[pl.ds(r0, rows), :] = y
            return carry

        lax.fori_loop(0, h_ref.shape[0] // rows, body, 0)


def ffn_block(h, mod, g, wg, wu, wd, *, layer, which, seq, mrow, res_scale, final_g=None):
    T, D = h.shape
    F = wg.shape[-1]
    tm = min(1024, seq)
    tf = _tile(F, 256)
    kern = functools.partial(_ffn_kernel, mrow=mrow, res_scale=res_scale, final_norm=final_g is not None)
    in_specs = [
        pl.BlockSpec((tm, D), lambda i, f: (i, 0)),
        pl.BlockSpec((None, N_MOD, D), lambda i, f: ((i * tm) // seq, 0, 0)),
        pl.BlockSpec((1, D), lambda i, f: (0, 0)),
        pl.BlockSpec((None, None, D, tf), lambda i, f: (layer, which, 0, f)),
        pl.BlockSpec((None, None, D, tf), lambda i, f: (layer, which, 0, f)),
        pl.BlockSpec((None, None, tf, D), lambda i, f: (layer, which, f, 0)),
    ]
    args = [h, mod, g.reshape(1, D), wg, wu, wd]
    if final_g is not None:
        in_specs.append(pl.BlockSpec((1, D), lambda i, f: (0, 0)))
        args.append(final_g.reshape(1, D))
    return pl.pallas_call(
        kern,
        grid=(T // tm, F // tf),
        in_specs=in_specs,
        out_specs=pl.BlockSpec((tm, D), lambda i, f: (i, 0)),
        out_shape=jax.ShapeDtypeStruct((T, D), F32),
        scratch_shapes=[pltpu.VMEM((tm, D), BF16)],
        compiler_params=_cparams("parallel", "arbitrary"),
        name="ffn_block",
    )(*args)


PROJ_COL_CHUNK = 512


def _norm_proj_kernel(h_ref, mod_ref, g_ref, wb_ref, wf_ref, pb_ref, pf_ref, u_ref, *, mrow):
    _adanorm_into(h_ref, g_ref, mod_ref, mrow, u_ref)
    u = u_ref[...]
    for c0 in range(0, pb_ref.shape[1], PROJ_COL_CHUNK):
        cols = slice(c0, c0 + PROJ_COL_CHUNK)
        pb_ref[:, cols] = jnp.dot(u, wb_ref[:, cols], preferred_element_type=F32).astype(pb_ref.dtype)
    pf_ref[...] = jnp.dot(u, wf_ref[...], preferred_element_type=F32)


def norm_proj(h, mod, g, wb, wf, *, layer, seq, mrow):
    T, D = h.shape
    NB, NF = wb.shape[2], wf.shape[2]
    tm = min(512, seq)
    kern = functools.partial(_norm_proj_kernel, mrow=mrow)
    resident = pl.Buffered(1)
    return pl.pallas_call(
        kern,
        grid=(T // tm,),
        in_specs=[
            pl.BlockSpec((tm, D), lambda i: (i, 0)),
            pl.BlockSpec((None, N_MOD, D), lambda i: ((i * tm) // seq, 0, 0)),
            pl.BlockSpec((1, D), lambda i: (0, 0)),
            pl.BlockSpec((None, D, NB), lambda i: (layer, 0, 0), pipeline_mode=resident),
            pl.BlockSpec((None, D, NF), lambda i: (layer, 0, 0), pipeline_mode=resident),
        ],
        out_specs=[
            pl.BlockSpec((tm, NB), lambda i: (i, 0)),
            pl.BlockSpec((tm, NF), lambda i: (i, 0)),
            pl.BlockSpec((tm, D), lambda i: (i, 0)),
        ],
        out_shape=[jax.ShapeDtypeStruct((T, NB), BF16), jax.ShapeDtypeStruct((T, NF), F32),
                   jax.ShapeDtypeStruct((T, D), BF16)],
        compiler_params=_cparams("parallel"),
        name="norm_proj",
    )(h, mod, g.reshape(1, D), wb, wf)


def _swa_kernel(sink_ref, q_ref, kv_ref, kvp_ref, o_ref):
    n = pl.program_id(1)
    W = WINDOW
    d = HEAD_DIM
    q = q_ref[...] * (d ** -0.5)
    kv = jnp.concatenate([kvp_ref[...], kv_ref[...]], axis=0)
    rowi = lax.broadcasted_iota(jnp.int32, (2 * W, 2 * d), 0)
    lanei = lax.broadcasted_iota(jnp.int32, (2 * W, 2 * d), 1)
    own = (rowi >= W) == (lanei >= d)
    first_half = lax.broadcasted_iota(jnp.int32, (W, 2 * d), 1) < d
    qpos =lax.broadcasted_iota(jnp.int32, (2 * W, 2 * W), 0) % W + W
    kpos = lax.broadcasted_iota(jnp.int32, (2 * W, 2 * W), 1)
    rel = qpos - kpos
    mask = (rel >= 0) & (rel < W) & ((kpos >= W) | (n > 0))
    ones = jnp.ones((2 * W, 2 * d), BF16)
    group = SWA_HEADS // SWA_KV_HEADS
    for g in range(SWA_HEADS // 2):
        kvh = (2 * g) // group
        k2 = kv[:, kvh * 2 * d:(kvh + 1) * 2 * d]
        v2 = kv[:, (SWA_KV_HEADS + kvh) * 2 * d:(SWA_KV_HEADS + kvh + 1) * 2 * d]
        v_ext = jnp.concatenate([v2, ones], axis=1)
        qp = q[:, g * 2 * d:(g + 1) * 2 * d]
        qstack = jnp.where(own, jnp.concatenate([qp, qp], axis=0), jnp.zeros((2 * W, 2 * d), BF16))
        logits = lax.dot_general(qstack, k2, (((1,), (1,)), ((), ())), preferred_element_type=F32)
        logits = jnp.where(mask, logits, -jnp.inf)
        sink = jnp.where(rowi >= W, sink_ref[2 * g + 1], sink_ref[2 * g])
        m = jnp.maximum(jnp.max(logits, axis=-1, keepdims=True), sink)
        p = jnp.exp(logits - jnp.concatenate([m, m], axis=1)).astype(BF16)
        r = jnp.dot(p, v_ext, preferred_element_type=F32)
        o2 = r[:, :2 * d] / (r[:, 2 * d:] + jnp.exp(sink - m))
        out = jnp.where(first_half, o2[:W], o2[W:])
        o_ref[:, g * 2 * d:(g + 1) * 2 * d] = out.astype(o_ref.dtype)


def swa_attention(proj, sinks, *, batch, seq):
    T = proj.shape[0]
    nb = seq // WINDOW
    qblk = COLB["a_q"] // 512
    kvblk = COLB["a_k0"] // 512
    return pl.pallas_call(
        _swa_kernel,
        grid=(batch, nb),
        in_specs=[
            pl.BlockSpec(memory_space=pltpu.SMEM),
            pl.BlockSpec((WINDOW, 512), lambda b, n: (b * nb + n, qblk)),
            pl.BlockSpec((WINDOW, 512), lambda b, n: (b * nb + n, kvblk)),
            pl.BlockSpec((WINDOW, 512), lambda b, n: (b * nb + jnp.maximum(n - 1, 0), kvblk)),
        ],
        out_specs=pl.BlockSpec((WINDOW, 512), lambda b, n: (b * nb + n, 0)),
        out_shape=jax.ShapeDtypeStruct((T, BRANCH_WIDTH), BF16),
        compiler_params=_cparams("parallel", "arbitrary"),
        name="swa_attention",
    )(sinks, proj, proj, proj)


def _fox_cum_kernel(x_ref, b_ref, o_ref, *, seq):
    R = 128
    ri = lax.broadcasted_iota(jnp.int32, (R, R), 0)
    ci = lax.broadcasted_iota(jnp.int32, (R, R), 1)
    tri = (ci <= ri).astype(F32)

    def body(i, carry):
        r0 = pl.multiple_of(i * R, R)
        ls = _log_sigmoid(x_ref[pl.ds(r0, R), :] + b_ref[...])
        cum = jnp.dot(tri, ls, precision=HIGHEST, preferred_element_type=F32) + carry
        o_ref[pl.ds(r0, R), :] = cum
        return cum[R - 1:R, :]

    lax.fori_loop(0, seq // R, body, jnp.zeros((1, R), F32))


def fox_forget_cumsum(proj, fox_b, *, batch, seq):
    T = proj.shape[0]
    blk = SMALL_COL // 128
    bias = jnp.zeros((1, 128), F32).at[0, :FOX_HEADS].set(fox_b)
    return pl.pallas_call(
        functools.partial(_fox_cum_kernel, seq=seq),
        grid=(batch,),
        in_specs=[
            pl.BlockSpec((seq, 128), lambda b: (b, blk)),
            pl.BlockSpec((1, 128), lambda b: (0, 0)),
        ],
        out_specs=pl.BlockSpec((seq, 128), lambda b: (b, 0)),
        out_shape=jax.ShapeDtypeStruct((T, 128), F32),
        compiler_params=_cparams("parallel"),
        name="fox_forget_cumsum",
    )(proj, bias)


FOX_KEY_ROWS = 512


def _fox_kernel(q_ref, k_ref, v_ref, fc_ref, fcs_ref, o_ref, m_scr, acc_scr, s_scr, kx_scr, *, tq, hp_heads):
    hp = pl.program_id(1)
    qi = pl.program_id(2)
    tk = tq
    d = HEAD_DIM
    w = hp_heads * d
    seq = k_ref.shape[0]
    e_row = lax.broadcasted_iota(jnp.int32, (w, w), 0)
    e_lane = lax.broadcasted_iota(jnp.int32, (w, w), 1)

    @pl.when(qi == 0)
    def _():
        kr = min(FOX_KEY_ROWS, seq)
        place = [sum(jnp.where((e_row == hp * hp_heads + hh) & (e_lane == 3 * hh + j), 1.0, 0.0)
                     for hh in range(hp_heads)).astype(BF16) for j in range(3)]
        for r0 in range(0, seq, kr):
            rows = slice(r0, r0 + kr)
            ext = sum(jnp.dot(piece, place[j], preferred_element_type=F32)
                      for j, piece in enumerate(_split3(-fcs_ref[rows, :])))
            kx_scr[rows, :w] = k_ref[rows, :]
            kx_scr[rows, w:] = ext.astype(BF16)

    lane = lax.broadcasted_iota(jnp.int32, (tq, w), 1)
    q = q_ref[...] * (d ** -0.5)
    zero = jnp.zeros_like(q)
    qs = [jnp.concatenate([jnp.where((lane >= hh * d) & (lane < (hh + 1) * d), q, zero),
                           jnp.where((lane >= 3 * hh) & (lane < 3 * hh + 3), 1.0, 0.0).astype(BF16)], axis=1)
          for hh in range(hp_heads)]
    m_scr[...] = jnp.full(m_scr.shape, -jnp.inf, F32)
    acc_scr[...] = jnp.zeros(acc_scr.shape, F32)
    fq_pieces = _split3(fc_ref[...])
    fqs = []
    for hh in range(hp_heads):
        spread = jnp.where(e_row == hp * hp_heads + hh, 1.0, 0.0).astype(BF16)
        hi, mid, lo = [jnp.dot(piece, spread, preferred_element_type=F32) for piece in fq_pieces]
        fqs.append((hi + mid) + lo)
    ones = jnp.ones((tk, w), BF16)

    def scores(kb, slot):
        k0 = pl.multiple_of(kb * tk, tk)
        kx = kx_scr[pl.ds(k0, tk), :]
        for hh in range(hp_heads):
            s_scr[slot, hh] = lax.dot_general(qs[hh], kx, (((1,), (1,)), ((), ())), preferred_element_type=F32)

    def softmax_pv(kb, slot, masked):
        k0 = pl.multiple_of(kb * tk, tk)
        v = jnp.concatenate([v_ref[pl.ds(k0, tk), :], ones], axis=1)
        for hh in range(hp_heads):
            t = s_scr[slot, hh]
            if masked:
                row = lax.broadcasted_iota(jnp.int32, (tq, tk), 0)
                col = lax.broadcasted_iota(jnp.int32, (tq, tk), 1)
                t = jnp.where(col <= row, t, -jnp.inf)
            m_old = m_scr[hh]
            m_new = jnp.maximum(m_old, fqs[hh] + jnp.max(t, axis=-1, keepdims=True))
            alpha = jnp.exp(m_old - m_new)
            c = fqs[hh] - m_new
            p = jnp.exp(t + jnp.concatenate([c] * (tk // w), axis=1))
            acc_scr[hh] = (jnp.concatenate([alpha, alpha], axis=1) * acc_scr[hh]
                           + jnp.dot(p.astype(BF16), v, preferred_element_type=F32))
            m_scr[hh] = m_new

    scores(0, 0)

    def body(kb, carry):
        slot = lax.rem(kb, 2)
        softmax_pv(kb, slot, False)
        scores(kb + 1, 1 - slot)
        return carry

    lax.fori_loop(0, qi, body, 0)
    softmax_pv(qi, lax.rem(qi, 2), True)
    out = None
    for hh in range(hp_heads):
        o = acc_scr[hh, :, :w] / acc_scr[hh, :, w:]
        out = o if out is None else jnp.where(lane >= hh * d, o, out)
    o_ref[...] = out.astype(o_ref.dtype)


def fox_attention(proj, fc, *, batch, seq):
    T = proj.shape[0]
    tq = min(512, seq)
    nq = seq // tq
    hp_heads = 2
    n_hp = FOX_HEADS // hp_heads
    qblk, kblk, vblk = COLB["f_q"] // 128, COLB["f_k"] // 128, COLB["f_v"] // 128
    kern = functools.partial(_fox_kernel, tq=tq, hp_heads=hp_heads)
    return pl.pallas_call(
        kern,
        grid=(batch, n_hp, nq),
        scratch_shapes=[pltpu.VMEM((hp_heads, tq, hp_heads * HEAD_DIM), F32),
                        pltpu.VMEM((hp_heads, tq, 2 * hp_heads * HEAD_DIM), F32),
                        pltpu.VMEM((2, hp_heads, tq, tq), F32),
                        pltpu.VMEM((seq, 2 * hp_heads * HEAD_DIM), BF16)],
        in_specs=[
            pl.BlockSpec((tq, 128), lambda b, hp, qi: (b * nq + qi, qblk + hp)),
            pl.BlockSpec((seq, 128), lambda b, hp, qi: (b, kblk + hp)),
            pl.BlockSpec((seq, 128), lambda b, hp, qi: (b, vblk + hp)),
            pl.BlockSpec((tq, 128), lambda b, hp, qi: (b * nq + qi, 0)),
            pl.BlockSpec((seq, 128), lambda b, hp, qi: (b, 0)),
        ],
        out_specs=pl.BlockSpec((tq, 128), lambda b, hp, qi: (b * nq + qi, hp)),
        out_shape=jax.ShapeDtypeStruct((T, BRANCH_WIDTH), BF16),
        compiler_params=_cparams("parallel", "parallel", "arbitrary"),
        name="fox_attention",
    )(proj, proj, proj, fc, fc)


def _split2(x):
    hi = x.astype(BF16)
    return hi, (x - hi.astype(F32)).astype(BF16)


def _split3(x):
    hi = x.astype(BF16)
    r = x - hi.astype(F32)
    mid = r.astype(BF16)
    return hi, mid, (r - mid.astype(F32)).astype(BF16)


def _rotate_half(x, neg_first_half):
    n = x.shape[-1]
    half = HEAD_DIM // 2
    fwd = pltpu.roll(x, half, 1)
    bwd = pltpu.roll(x, n - half, 1)
    return jnp.where(neg_first_half, -bwd, fwd)


def _linear_attn_kernel(*refs, mode, cb):
    if mode == "gla":
        (q_ref, k_ref, v_ref, gate_ref, small_ref, wg_ref, bg_ref, ng_ref, o_ref, st_ref) = refs
    else:
        (q_ref, k_ref, v_ref, gate_ref, lg_ref, cos_ref, sin_ref, gw_ref, gb_ref, o_ref, st_ref) = refs
    H, DK, DV, C = GLA_HEADS, GLA_DK, GLA_DV, CHUNK
    nch = cb // C
    W2 = 2 * DK
    HW = H * DK
    dn_lanes = (((1,), (1,)), ((), ()))
    dn_rows = (((0,), (0,)), ((), ()))

    @pl.when(pl.program_id(1) == 0)
    def _():
        st_ref[...] = jnp.zeros_like(st_ref)

    q = q_ref[...].astype(F32)
    k = k_ref[...].astype(F32)
    if mode == "gla":
        glr = small_ref[...][:, FOX_HEADS:FOX_HEADS + GLA_GATE_RANK]
        a_hi, a_lo = _split2(glr)
        w_hi, w_lo = _split2(wg_ref[...])
        z = (jnp.dot(a_hi, w_hi, preferred_element_type=F32) + jnp.dot(a_hi, w_lo, preferred_element_type=F32)
             + jnp.dot(a_lo, w_hi, preferred_element_type=F32)) + bg_ref[...]
        ld = _log_sigmoid(z) / GLA_TAU
        ld_w = jnp.concatenate([ld[c * C:(c + 1) * C] for c in range(nch)], axis=1)
        ri = lax.broadcasted_iota(jnp.int32, (C, 3 * C), 0)
        ci = lax.broadcasted_iota(jnp.int32, (C, 3 * C), 1)
        tri3 = jnp.where((ci % C) <= ri, 1.0, 0.0).astype(BF16)
        cum_w = jnp.dot(tri3, jnp.concatenate(_split3(ld_w), axis=0), preferred_element_type=F32)
        cum = jnp.concatenate([cum_w[:, c * HW:(c + 1) * HW] for c in range(nch)], axis=0)
        lasts = [cum_w[C - 1:C, c * HW:(c + 1) * HW] for c in range(nch)]
        last_b = jnp.concatenate([jnp.broadcast_to(l, (C, HW)) for l in lasts], axis=0)
    else:
        lane = lax.broadcasted_iota(jnp.int32, (cb, HW), 1)
        first_half = (lane % HEAD_DIM) < (HEAD_DIM // 2)
        cos = cos_ref[...]
        sin = sin_ref[...]
        q = q * cos + _rotate_half(q, first_half) * sin
        k = k * cos + _rotate_half(k, first_half) * sin
        steps = (lax.broadcasted_iota(jnp.int32, (cb, HW), 0) % C + 1).astype(F32)
        cum = steps * lg_ref[...]
        last_b = float(C) * lg_ref[...]
        lasts = [last_b] * nch
    q_in = (q * (DK ** -0.5) * jnp.exp(cum)).astype(BF16)
    k_in = (k * jnp.exp(-cum)).astype(BF16)
    k_st = (k * jnp.exp(last_b - cum)).astype(BF16)
    decs = [jnp.exp(l) for l in lasts]

    rowi = lax.broadcasted_iota(jnp.int32, (2 * C, W2), 0)
    lanei = lax.broadcasted_iota(jnp.int32, (2 * C, W2), 1)
    own = (rowi >= C) == (lanei >= DK)
    bd_causal = own & ((lanei % C) <= (rowi % C))
    zero_bf = jnp.zeros((2 * C, W2), BF16)

    def pair_ops(c, p):
        rows = slice(c * C, (c + 1) * C)
        ls = slice(p * W2, (p + 1) * W2)
        qp, kp, ksp = q_in[rows, ls], k_in[rows, ls], k_st[rows, ls]
        qstack = jnp.where(own, jnp.concatenate([qp, qp], axis=0), zero_bf)
        ksstack = jnp.where(own, jnp.concatenate([ksp, ksp], axis=0), zero_bf)
        k2 = jnp.concatenate([kp, kp], axis=0)
        vstack = jnp.concatenate([v_ref[rows, (2 * p) * DV:(2 * p + 1) * DV],
                                  v_ref[rows, (2 * p + 1) * DV:(2 * p + 2) * DV]], axis=0)
        sw = lax.dot_general(qstack, k2, dn_lanes, preferred_element_type=F32)
        sc = jnp.where(bd_causal, sw, 0.0).astype(BF16)
        o_intra = jnp.dot(sc, vstack, preferred_element_type=F32)
        upd = lax.dot_general(vstack, ksstack, dn_rows, preferred_element_type=F32)
        return qstack, o_intra, upd

    pre = [[pair_ops(c, p) for p in range(H // 2)] for c in range(nch)]
    state = [st_ref[p] for p in range(H // 2)]
    for c in range(nch):
        rows = slice(c * C, (c + 1) * C)
        for p in range(H // 2):
            qstack, o_intra, upd = pre[c][p]
            o = o_intra + lax.dot_general(qstack, state[p].astype(BF16), dn_lanes, preferred_element_type=F32)
            state[p] = state[p] * decs[c][:, p * W2:(p + 1) * W2] + upd
            for hh in range(2):
                hd = 2 * p + hh
                vs = slice(hd * DV, (hd + 1) * DV)
                oh = o[hh * C:(hh + 1) * C]
                gate = _silu(gate_ref[rows, vs])
                if mode == "gla":
                    y = oh * lax.rsqrt(jnp.mean(oh * oh, axis=-1, keepdims=True) + EPS) * ng_ref[...]
                else:
                    mu = jnp.mean(oh, axis=-1, keepdims=True)
                    var = jnp.mean(jnp.square(oh - mu), axis=-1, keepdims=True)
                    y = (oh - mu) * lax.rsqrt(var + EPS) * gw_ref[:, vs] + gb_ref[:, vs]
                o_ref[rows, vs] = (y * gate).astype(o_ref.dtype)
    for p in range(H // 2):
        st_ref[p] = state[p]


def linear_attention(proj, projf, params, *, mode, batch, seq):
    T = proj.shape[0]
    cb = min(512, seq)
    nc = seq // cb
    pre = "g" if mode == "gla" else "r"
    qblk, kblk = COLB[pre + "_q"] // 256, COLB[pre + "_k"] // 256
    vblk = COLB[pre + "_v"] // 512
    gblk = COLF["g_r" if mode == "gla" else "r_g"] // 512
    row = lambda b, c: b * nc + c
    in_specs = [
        pl.BlockSpec((cb, 256), lambda b, c: (row(b, c), qblk)),
        pl.BlockSpec((cb, 256), lambda b, c: (row(b, c), kblk)),
        pl.BlockSpec((cb, 512), lambda b, c: (row(b, c), vblk)),
        pl.BlockSpec((cb, 512), lambda b, c: (row(b, c), gblk)),
    ]
    args = [proj, proj, proj, projf]
    if mode == "gla":
        wg, bg, ng = params
        in_specs += [
            pl.BlockSpec((cb, 128), lambda b, c: (row(b, c), SMALL_COL // 128)),
            pl.BlockSpec(wg.shape, lambda b, c: (0, 0)),
            pl.BlockSpec((1, bg.shape[-1]), lambda b, c: (0, 0)),
            pl.BlockSpec((1, ng.shape[-1]), lambda b, c: (0, 0)),
        ]
        args += [projf, wg, bg.reshape(1, -1), ng.reshape(1, -1)]
    else:
        lg, cos, sin, gw, gb = params
        in_specs += [
            pl.BlockSpec((1, lg.shape[-1]), lambda b, c: (0, 0)),
            pl.BlockSpec((cb, 256), lambda b, c: (c, 0)),
            pl.BlockSpec((cb, 256), lambda b, c: (c, 0)),
            pl.BlockSpec((1, gw.shape[-1]), lambda b, c: (0, 0)),
            pl.BlockSpec((1, gb.shape[-1]), lambda b, c: (0, 0)),
        ]
        args += [lg.reshape(1, -1), cos, sin, gw.reshape(1, -1), gb.reshape(1, -1)]
    return pl.pallas_call(
        functools.partial(_linear_attn_kernel, mode=mode, cb=cb),
        grid=(batch, nc),
        in_specs=in_specs,
        out_specs=pl.BlockSpec((cb, 512), lambda b, c: (row(b, c), 0)),
        out_shape=jax.ShapeDtypeStruct((T, BRANCH_WIDTH), BF16),
        scratch_shapes=[pltpu.VMEM((GLA_HEADS // 2, GLA_DV, 2 * GLA_DK), F32)],
        compiler_params=_cparams("parallel", "arbitrary"),
        name="linear_attention_" + mode,
    )(*args)


def _retention_tables(seq):
    half = RET_DK // 2
    inv = RET_THETA_BASE ** (-jnp.arange(half, dtype=F32) / half)
    ang = jnp.arange(seq).astype(F32)[:, None] * inv[None, :]
    cos = jnp.tile(jnp.cos(ang), (1, 2 * RET_HEADS))
    sin = jnp.tile(jnp.sin(ang), (1, 2 * RET_HEADS))
    log_gamma = jnp.log1p(-jnp.exp2(-5.0 - jnp.arange(RET_HEADS, dtype=F32)))
    return jnp.repeat(log_gamma, RET_DK), cos, sin


def _merge_kernel(u_ref, oa_ref, ob_ref, oc_ref, od_ref, wm_ref, wb_ref, bm_ref, m_ref):
    u = u_ref[...]
    acc = None
    for i, o_ref in enumerate((oa_ref, ob_ref, oc_ref, od_ref)):
        z = jnp.dot(u, wm_ref[i].astype(BF16), preferred_element_type=F32) + bm_ref[i]
        pr = jnp.dot(o_ref[...], wb_ref[i].astype(BF16), preferred_element_type=F32)
        t = jax.nn.sigmoid(z) * pr
        acc = t if acc is None else acc + t
    m_ref[...] = acc.astype(m_ref.dtype)


def merge_branches(u, branches, wm, wb, bm, *, layer):
    T, D = u.shape
    tm, tn = min(1024, T), min(256, D)
    ospec = pl.BlockSpec((tm, BRANCH_WIDTH), lambda i, j: (i, 0))
    return pl.pallas_call(
        _merge_kernel,
        grid=(T // tm, D // tn),
        in_specs=[
            pl.BlockSpec((tm, D), lambda i, j: (i, 0)),
            ospec, ospec, ospec, ospec,
            pl.BlockSpec((None, N_BRANCH, D, tn), lambda i, j: (layer, 0, 0, j)),
            pl.BlockSpec((None, N_BRANCH, BRANCH_WIDTH, tn), lambda i, j: (layer, 0, 0, j)),
            pl.BlockSpec((None, N_BRANCH, 1, tn), lambda i, j: (layer, 0, 0, j)),
        ],
        out_specs=pl.BlockSpec((tm, tn), lambda i, j: (i, j)),
        out_shape=jax.ShapeDtypeStruct((T, D), BF16),
        compiler_params=_cparams("parallel", "arbitrary"),
        name="merge_branches",
    )(u, *branches, wm, wb, bm.reshape(bm.shape[0], N_BRANCH, 1, D))


def _out_proj_kernel(h_ref, m_ref, w_ref, mod_ref, o_ref, *, mrow):
    m = m_ref[...]
    for c0 in range(0, o_ref.shape[1], PROJ_COL_CHUNK):
        cols = slice(c0, c0 + PROJ_COL_CHUNK)
        y = jnp.dot(m, w_ref[:, cols], preferred_element_type=F32)
        o_ref[:, cols] = h_ref[:, cols] + mod_ref[mrow:mrow + 1, cols] * y


def out_proj(h, merged, w, mod, *, layer, seq, mrow):
    T, D = h.shape
    tm = min(512, seq)
    return pl.pallas_call(
        functools.partial(_out_proj_kernel, mrow=mrow),
        grid=(T // tm,),
        in_specs=[
            pl.BlockSpec((tm, D), lambda i: (i, 0)),
            pl.BlockSpec((tm, D), lambda i: (i, 0)),
            pl.BlockSpec((None, D, D), lambda i: (layer, 0, 0), pipeline_mode=pl.Buffered(1)),
            pl.BlockSpec((None, N_MOD, D), lambda i: ((i * tm) // seq, 0, 0)),
        ],
        out_specs=pl.BlockSpec((tm, D), lambda i: (i, 0)),
        out_shape=jax.ShapeDtypeStruct((T, D), F32),
        compiler_params=_cparams("parallel"),
        name="out_proj",
    )(h, merged, w, mod)


def _permuted_w_in(w_in, order, cols):
    L, D, _ = w_in.shape
    parts = [w_in[:, :, _REF_COLS[n][0]:_REF_COLS[n][0] + _REF_COLS[n][1]].astype(BF16) for n in order]
    used = sum(_REF_COLS[n][1] for n in order)
    if cols > used:
        parts.append(jnp.zeros((L, D, cols - used), BF16))
    return jnp.concatenate(parts, axis=2)


def kernel(x, c, w_ada, b_ada, norm_g, ffn_w_gate, ffn_w_up, ffn_w_down, w_in, fox_b_forget, attn_sinks,
           gla_w_gate, gla_b_gate, gla_norm_g, ret_gn_w, ret_gn_b, w_branch, w_merge, b_merge, w_out,
           final_norm_g):
    B, S, D = x.shape
    L = w_ada.shape[0]
    T = B * S
    mod_all = ada_modulation(c, w_ada, b_ada)
    log_gamma, cos, sin = _retention_tables(S)
    w_out_bf = w_out.astype(BF16)
    w_in_b = _permuted_w_in(w_in, _ORDER_B, PROJ_B_COLS)
    w_in_f = _permuted_w_in(w_in, _ORDER_F, PROJ_F_COLS)
    h = x.reshape(T, D)
    for l in range(L):
        mod = mod_all[l]
        h = ffn_block(h, mod, norm_g[l, 0], ffn_w_gate, ffn_w_up, ffn_w_down, layer=l, which=0, seq=S,
                      mrow=0, res_scale=0.5)
        proj, projf, u = norm_proj(h, mod, norm_g[l, 1], w_in_b, w_in_f, layer=l, seq=S, mrow=3)
        o_a = swa_attention(proj, attn_sinks[l], batch=B, seq=S)
        fc = fox_forget_cumsum(projf, fox_b_forget[l], batch=B, seq=S)
        o_b = fox_attention(proj, fc, batch=B, seq=S)
        o_c = linear_attention(proj, projf, (gla_w_gate[l], gla_b_gate[l], gla_norm_g[l]), mode="gla",
                               batch=B, seq=S)
        o_d = linear_attention(proj, projf, (log_gamma, cos, sin, ret_gn_w[l], ret_gn_b[l]), mode="ret",
                               batch=B, seq=S)
        merged = merge_branches(u, (o_a, o_b, o_c, o_d), w_merge, w_branch, b_merge, layer=l)
        h = out_proj(h, merged, w_out_bf, mod, layer=l, seq=S, mrow=5)
        h = ffn_block(h, mod, norm_g[l, 2], ffn_w_gate, ffn_w_up, ffn_w_down, layer=l, which=1, seq=S,
                      mrow=6, res_scale=0.5, final_g=final_norm_g if l == L - 1 else None)
    return h.reshape(B, S, D)
```

```python
import functools

import numpy as np
import jax
import jax.numpy as jnp
from jax import lax
from jax.experimental import pallas as pl
from jax.experimental.pallas import tpu as pltpu

HEAD_DIM = 64
SWA_HEADS = 8
SWA_KV_HEADS = 2
WINDOW = 128
FOX_HEADS = 8
GLA_HEADS = 4
GLA_DK = 64
GLA_DV = 128
GLA_GATE_RANK = 16
GLA_TAU = 16.0
RET_HEADS = 4
RET_DK = 64
RET_DV = 128
RET_THETA_BASE = 10000.0
CHUNK = 64
BRANCH_WIDTH = 512
N_BRANCH = 4
N_MOD = 9
EPS = 1e-6

BF16 = jnp.bfloat16
F32 = jnp.float32
HIGHEST = lax.Precision.HIGHEST

VMEM_LIMIT_BYTES = 60 * 1024 * 1024

_REF_COLS = {}
_off = 0
for _name, _size in (
        ("a_q", 512), ("a_k", 128), ("a_v", 128),
        ("f_q", 512), ("f_k", 512), ("f_v", 512), ("f_f", 8),
        ("g_q", 256), ("g_k", 256), ("g_v", 512), ("g_lr", 16), ("g_r", 512),
        ("r_q", 256), ("r_k", 256), ("r_v", 512), ("r_g", 512)):
    _REF_COLS[_name] = (_off, _size)
    _off += _size
IN_COLS = _off
for _name in ("a_k", "a_v"):
    for _i in range(SWA_KV_HEADS):
        _REF_COLS[_name + str(_i)] = (_REF_COLS[_name][0] + _i * HEAD_DIM, HEAD_DIM)


def _layout(order):
    col, off = {}, 0
    for name in order:
        col.setdefault(name, off)
        off += _REF_COLS[name][1]
    return col


_ORDER_B = ("a_q", "f_q", "f_k", "f_v", "g_v", "r_v", "g_q", "g_k", "r_q", "r_k",
            "a_k0", "a_k0", "a_k1", "a_k1", "a_v0", "a_v0", "a_v1", "a_v1")
_ORDER_F = ("g_r", "r_g", "f_f", "g_lr")
COLB = _layout(_ORDER_B)
COLF = _layout(_ORDER_F)
PROJ_B_COLS = 4608
PROJ_F_COLS = 1152
SMALL_COL = COLF["f_f"]


def _cparams(*sem):
    return pltpu.CompilerParams(dimension_semantics=sem, vmem_limit_bytes=VMEM_LIMIT_BYTES)


def _tile(n, pref):
    t = (min(pref, n) // 128) * 128
    while t >= 128:
        if n % t == 0:
            return t
        t -= 128
    return n


NORM_ROWS = 64


def _adanorm_into(h_ref, g_ref, mod_ref, mrow, u_ref, zero_ref=None):
    gs = g_ref[...] * (1.0 + mod_ref[mrow + 1:mrow + 2, :])
    shift = mod_ref[mrow:mrow + 1, :]
    rows = min(NORM_ROWS, h_ref.shape[0])

    def body(r, carry):
        r0 = pl.multiple_of(r * rows, rows)
        x = h_ref[pl.ds(r0, rows), :]
        ms = jnp.mean(x * x, axis=-1, keepdims=True)
        u_ref[pl.ds(r0, rows), :] = (x * lax.rsqrt(ms + EPS) * gs + shift).astype(u_ref.dtype)
        if zero_ref is not None:
            zero_ref[pl.ds(r0, rows), :] = jnp.zeros((rows, zero_ref.shape[1]), zero_ref.dtype)
        return carry

    lax.fori_loop(0, h_ref.shape[0] // rows, body, 0)


def _log_sigmoid(x):
    return jnp.minimum(x, 0.0) - jnp.log1p(jnp.exp(-jnp.abs(x)))


def _silu(x):
    return x * jax.nn.sigmoid(x)


ADA_ROWS = 16


def _ada_kernel(c_ref, w_ref, b_ref, o_ref):
    cond = _silu(c_ref[...])
    r = jnp.dot(jnp.concatenate(_split3(cond), axis=0), w_ref[...].astype(BF16), preferred_element_type=F32)
    bp = ADA_ROWS
    o_ref[...] = (r[:bp] + r[bp:2 * bp]) + r[2 * bp:] + b_ref[...]


def ada_modulation(c, w_ada, b_ada):
    L, D, N = w_ada.shape
    B = c.shape[0]
    BP = ADA_ROWS
    cp = jnp.zeros((BP, D), F32).at[:B].set(c)
    tn = _tile(N, 1024)
    out = pl.pallas_call(
        _ada_kernel,
        grid=(L, N // tn),
        in_specs=[
            pl.BlockSpec((BP, D), lambda l, j: (0, 0)),
            pl.BlockSpec((None, D, tn), lambda l, j: (l, 0, j)),
            pl.BlockSpec((None, 1, tn), lambda l, j: (l, 0, j)),
        ],
        out_specs=pl.BlockSpec((None, BP, tn), lambda l, j: (l, 0, j)),
        out_shape=jax.ShapeDtypeStruct((L, BP, N), F32),
        compiler_params=_cparams("parallel", "parallel"),
        name="ada_modulation",
    )(cp, w_ada, b_ada.reshape(L, 1, N))
    return out[:, :B].reshape(L, B, N_MOD, D)


def _ffn_kernel(h_ref, mod_ref, g_ref, wg_ref, wu_ref, wd_ref, *rest, mrow, res_scale, final_norm):
    if final_norm:
        fg_ref, o_ref, u_scr = rest
    else:
        o_ref, u_scr = rest
    f = pl.program_id(1)

    @pl.when(f == 0)
    def _():
        _adanorm_into(h_ref, g_ref, mod_ref, mrow, u_scr, zero_ref=o_ref)

    u = u_scr[...]
    a = jnp.dot(u, wg_ref[...].astype(BF16), preferred_element_type=F32)
    b = jnp.dot(u, wu_ref[...].astype(BF16), preferred_element_type=F32)
    mid = (_silu(a) * b).astype(BF16)
    o_ref[...] += jnp.dot(mid, wd_ref[...].astype(BF16), preferred_element_type=F32)

    @pl.when(f == pl.num_programs(1) - 1)
    def _():
        gate = res_scale * mod_ref[mrow + 2:mrow + 3, :]
        rows = min(NORM_ROWS, h_ref.shape[0])

        def body(r, carry):
            r0 = pl.multiple_of(r * rows, rows)
            y = h_ref[pl.ds(r0, rows), :] + gate * o_ref[pl.ds(r0, rows), :]
            if final_norm:
                y = y * lax.rsqrt(jnp.mean(y * y, axis=-1, keepdims=True) + EPS) * fg_ref[...]
            o_ref[pl.ds(r0, rows), :] = y
            return carry

        lax.fori_loop(0, h_ref.shape[0] // rows, body, 0)


def ffn_block(h, mod, g, wg, wu, wd, *, layer, which, seq, mrow, res_scale, final_g=None):
    T, D = h.shape
    F = wg.shape[-1]
    tm = min(1024, seq)
    tf = _tile(F, 256)
    kern = functools.partial(_ffn_kernel, mrow=mrow, res_scale=res_scale, final_norm=final_g is not None)
    in_specs = [
        pl.BlockSpec((tm, D), lambda i, f: (i, 0)),
        pl.BlockSpec((None, N_MOD, D), lambda i, f: ((i * tm) // seq, 0, 0)),
        pl.BlockSpec((1, D), lambda i, f: (0, 0)),
        pl.BlockSpec((None, None, D, tf), lambda i, f: (layer, which, 0, f)),
        pl.BlockSpec((None, None, D, tf), lambda i, f: (layer, which, 0, f)),
        pl.BlockSpec((None, None, tf, D), lambda i, f: (layer, which, f, 0)),
    ]
    args = [h, mod, g.reshape(1, D), wg, wu, wd]
    if final_g is not None:
        in_specs.append(pl.BlockSpec((1, D), lambda i, f: (0, 0)))
        args.append(final_g.reshape(1, D))
    return pl.pallas_call(
        kern,
        grid=(T // tm, F // tf),
        in_specs=in_specs,
        out_specs=pl.BlockSpec((tm, D), lambda i, f: (i, 0)),
        out_shape=jax.ShapeDtypeStruct((T, D), F32),
        scratch_shapes=[pltpu.VMEM((tm, D), BF16)],
        compiler_params=_cparams("parallel", "arbitrary"),
        name="ffn_block",
    )(*args)


PROJ_COL_CHUNK = 512


def _norm_proj_kernel(h_ref, mod_ref, g_ref, wb_ref, wf_ref, pb_ref, pf_ref, u_ref, *, mrow):
    _adanorm_into(h_ref, g_ref, mod_ref, mrow, u_ref)
    u = u_ref[...]
    for c0 in range(0, pb_ref.shape[1], PROJ_COL_CHUNK):
        cols = slice(c0, c0 + PROJ_COL_CHUNK)
        pb_ref[:, cols] = jnp.dot(u, wb_ref[:, cols], preferred_element_type=F32).astype(pb_ref.dtype)
    pf_ref[...] = jnp.dot(u, wf_ref[...], preferred_element_type=F32)


def norm_proj(h, mod, g, wb, wf, *, seq, mrow):
    T, D = h.shape
    NB, NF = wb.shape[1], wf.shape[1]
    tm = min(512, seq)
    kern = functools.partial(_norm_proj_kernel, mrow=mrow)
    resident = pl.Buffered(1)
    return pl.pallas_call(
        kern,
        grid=(T // tm,),
        in_specs=[
            pl.BlockSpec((tm, D), lambda i: (i, 0)),
            pl.BlockSpec((None, N_MOD, D), lambda i: ((i * tm) // seq, 0, 0)),
            pl.BlockSpec((1, D), lambda i: (0, 0)),
            pl.BlockSpec((D, NB), lambda i: (0, 0), pipeline_mode=resident),
            pl.BlockSpec((D, NF), lambda i: (0, 0), pipeline_mode=resident),
        ],
        out_specs=[
            pl.BlockSpec((tm, NB), lambda i: (i, 0)),
            pl.BlockSpec((tm, NF), lambda i: (i, 0)),
            pl.BlockSpec((tm, D), lambda i: (i, 0)),
        ],
        out_shape=[jax.ShapeDtypeStruct((T, NB), BF16), jax.ShapeDtypeStruct((T, NF), F32),
                   jax.ShapeDtypeStruct((T, D), BF16)],
        compiler_params=_cparams("parallel"),
        name="norm_proj",
    )(h, mod, g.reshape(1, D), wb, wf)


def _swa_kernel(sink_ref, q_ref, kv_ref, kvp_ref, o_ref):
    n = pl.program_id(1)
    W = WINDOW
    d = HEAD_DIM
    rowi = lax.broadcasted_iota(jnp.int32, (2 * W, 2 * d), 0)
    lanei = lax.broadcasted_iota(jnp.int32, (2 * W, 2 * d), 1)
    own = (rowi >= W) == (lanei >= d)
    first_half = lax.broadcasted_iota(jnp.int32, (W, 2 * d), 1) < d
    qpos = lax.broadcasted_iota(jnp.int32, (2 * W, 2 * W), 0) % W + W
    kpos = lax.broadcasted_iota(jnp.int32, (2 * W, 2 * W), 1)
    rel = qpos - kpos
    band = (rel >= 0) & (rel < W)
    ones = jnp.ones((2 * W, 2 * d), BF16)
    group = SWA_HEADS // SWA_KV_HEADS
    for blk in range(SWA_BLOCKS_PER_STEP):
        rows = slice(blk * W, (blk + 1) * W)
        q = q_ref[rows, :] * (d ** -0.5)
        prev = kvp_ref[...] if blk == 0 else kv_ref[(blk - 1) * W:blk * W, :]
        kv = jnp.concatenate([prev, kv_ref[rows, :]], axis=0)
        mask = band & ((kpos >= W) | (n > 0)) if blk == 0 else band
        for g in range(SWA_HEADS // 2):
            kvh = (2 * g) // group
            k2 = kv[:, kvh * 2 * d:(kvh + 1) * 2 * d]
            v2 = kv[:, (SWA_KV_HEADS + kvh) * 2 * d:(SWA_KV_HEADS + kvh + 1) * 2 * d]
            v_ext = jnp.concatenate([v2, ones], axis=1)
            qp = q[:, g * 2 * d:(g + 1) * 2 * d]
            qstack = jnp.where(own, jnp.concatenate([qp, qp], axis=0), jnp.zeros((2 * W, 2 * d), BF16))
            logits = lax.dot_general(qstack, k2, (((1,), (1,)), ((), ())), preferred_element_type=F32)
            logits = jnp.where(mask, logits, -jnp.inf)
            sink = jnp.where(rowi >= W, sink_ref[2 * g + 1], sink_ref[2 * g])
            m = jnp.maximum(jnp.max(logits, axis=-1, keepdims=True), sink)
            p = jnp.exp(logits - jnp.concatenate([m, m], axis=1)).astype(BF16)
            r = jnp.dot(p, v_ext, preferred_element_type=F32)
            o2 = r[:, :2 * d] / (r[:, 2 * d:] + jnp.exp(sink - m))
            out = jnp.where(first_half, o2[:W], o2[W:])
            o_ref[rows, g * 2 * d:(g + 1) * 2 * d] = out.astype(o_ref.dtype)


SWA_BLOCKS_PER_STEP = 2


def swa_attention(proj, sinks, *, batch, seq):
    T = proj.shape[0]
    nb = seq // WINDOW
    per = SWA_BLOCKS_PER_STEP
    ns = nb // per
    qblk = COLB["a_q"] // 512
    kvblk = COLB["a_k0"] // 512
    return pl.pallas_call(
        _swa_kernel,
        grid=(batch, ns),
        in_specs=[
            pl.BlockSpec(memory_space=pltpu.SMEM),
            pl.BlockSpec((per * WINDOW, 512), lambda b, n: (b * ns + n, qblk)),
            pl.BlockSpec((per * WINDOW, 512), lambda b, n: (b * ns + n, kvblk)),
            pl.BlockSpec((WINDOW, 512), lambda b, n: (b * nb + jnp.maximum(per * n - 1, 0), kvblk)),
        ],
        out_specs=pl.BlockSpec((per * WINDOW, 512), lambda b, n: (b * ns + n, 0)),
        out_shape=jax.ShapeDtypeStruct((T, BRANCH_WIDTH), BF16),
        compiler_params=_cparams("parallel", "arbitrary"),
        name="swa_attention",
    )(sinks, proj, proj, proj)


def _fox_cum_kernel(x_ref, b_ref, o_ref, *, seq):
    R = 128
    ri = lax.broadcasted_iota(jnp.int32, (R, R), 0)
    ci = lax.broadcasted_iota(jnp.int32, (R, R), 1)
    tri = (ci <= ri).astype(F32)

    def body(i, carry):
        r0 = pl.multiple_of(i * R, R)
        ls = _log_sigmoid(x_ref[pl.ds(r0, R), :] + b_ref[...])
        cum = jnp.dot(tri, ls, precision=HIGHEST, preferred_element_type=F32) + carry
        o_ref[pl.ds(r0, R), :] = cum
        return cum[R - 1:R, :]

    lax.fori_loop(0, seq // R, body, jnp.zeros((1, R), F32))


def fox_forget_cumsum(proj, fox_b, *, batch, seq):
    T = proj.shape[0]
    blk = SMALL_COL // 128
    bias = jnp.zeros((1, 128), F32).at[0, :FOX_HEADS].set(fox_b)
    return pl.pallas_call(
        functools.partial(_fox_cum_kernel, seq=seq),
        grid=(batch,),
        in_specs=[
            pl.BlockSpec((seq, 128), lambda b: (b, blk)),
            pl.BlockSpec((1, 128), lambda b: (0, 0)),
        ],
        out_specs=pl.BlockSpec((seq, 128), lambda b: (b, 0)),
        out_shape=jax.ShapeDtypeStruct((T, 128), F32),
        compiler_params=_cparams("parallel"),
        name="fox_forget_cumsum",
    )(proj, bias)


FOX_KEY_ROWS = 512


def _fox_kernel(q_ref, k_ref, v_ref, fc_ref, fcs_ref, o_ref, m_scr, acc_scr, s_scr, kx_scr, *, tq, hp_heads):
    hp = pl.program_id(1)
    qi = pl.program_id(2)
    tk = tq
    d = HEAD_DIM
    w = hp_heads * d
    seq = k_ref.shape[0]
    e_row = lax.broadcasted_iota(jnp.int32, (w, w), 0)
    e_lane = lax.broadcasted_iota(jnp.int32, (w, w), 1)

    @pl.when(qi == 0)
    def _():
        kr = min(FOX_KEY_ROWS, seq)
        place = [sum(jnp.where((e_row == hp * hp_heads + hh) & (e_lane == 3 * hh + j), 1.0, 0.0)
                     for hh in range(hp_heads)).astype(BF16) for j in range(3)]
        for r0 in range(0, seq, kr):
            rows = slice(r0, r0 + kr)
            ext = sum(jnp.dot(piece, place[j], preferred_element_type=F32)
                      for j, piece in enumerate(_split3(-fcs_ref[rows, :])))
            kx_scr[rows, :w] = k_ref[rows, :]
            kx_scr[rows, w:] = ext.astype(BF16)

    lane = lax.broadcasted_iota(jnp.int32, (tq, w), 1)
    q = q_ref[...] * (d ** -0.5)
    zero = jnp.zeros_like(q)
    qs = [jnp.concatenate([jnp.where((lane >= hh * d) & (lane < (hh + 1) * d), q, zero),
                           jnp.where((lane >= 3 * hh) & (lane < 3 * hh + 3), 1.0, 0.0).astype(BF16)], axis=1)
          for hh in range(hp_heads)]
    m_scr[...] = jnp.full(m_scr.shape, -jnp.inf, F32)
    acc_scr[...] = jnp.zeros(acc_scr.shape, F32)
    fc = fc_ref[...]
    fqs = [jnp.broadcast_to(jnp.sum(jnp.where(lane == hp * hp_heads + hh, fc, 0.0), axis=-1, keepdims=True),
                            (tq, w)) for hh in range(hp_heads)]
    ones = jnp.ones((tk, w), BF16)

    def scores(kb, slot):
        k0 = pl.multiple_of(kb * tk, tk)
        kx = kx_scr[pl.ds(k0, tk), :]
        for hh in range(hp_heads):
            s_scr[slot, hh] = lax.dot_general(qs[hh], kx, (((1,), (1,)), ((), ())), preferred_element_type=F32)

    def softmax_pv(kb, slot, masked):
        k0 = pl.multiple_of(kb * tk, tk)
        v = jnp.concatenate([v_ref[pl.ds(k0, tk), :], ones], axis=1)
        for hh in range(hp_heads):
            t = s_scr[slot, hh]
            if masked:
                row = lax.broadcasted_iota(jnp.int32, (tq, tk), 0)
                col = lax.broadcasted_iota(jnp.int32, (tq, tk), 1)
                t = jnp.where(col <= row, t, -jnp.inf)
            m_old = m_scr[hh]
            m_new = jnp.maximum(m_old, fqs[hh] + jnp.max(t, axis=-1, keepdims=True))
            alpha = jnp.exp(m_old - m_new)
            c = fqs[hh] - m_new
            p = jnp.exp(t + jnp.concatenate([c] * (tk // w), axis=1))
            acc_scr[hh] = (jnp.concatenate([alpha, alpha], axis=1) * acc_scr[hh]
                           + jnp.dot(p.astype(BF16), v, preferred_element_type=F32))
            m_scr[hh] = m_new

    scores(0, 0)

    def body(j, carry):
        kb = 2 * j
        softmax_pv(kb, 0, False)
        scores(kb + 1, 1)
        softmax_pv(kb + 1, 1, False)
        scores(kb + 2, 0)
        return carry

    lax.fori_loop(0, qi // 2, body, 0)

    @pl.when(qi % 2 == 1)
    def _():
        softmax_pv(qi - 1, 0, False)
        scores(qi, 1)
        softmax_pv(qi, 1, True)

    @pl.when(qi % 2 == 0)
    def _():
        softmax_pv(qi, 0, True)
    out = None
    for hh in range(hp_heads):
        o = acc_scr[hh, :, :w] / acc_scr[hh, :, w:]
        out = o if out is None else jnp.where(lane >= hh * d, o, out)
    o_ref[...] = out.astype(o_ref.dtype)


def fox_attention(proj, fc, *, batch, seq):
    T = proj.shape[0]
    tq = min(512, seq)
    nq = seq // tq
    hp_heads = 2
    n_hp = FOX_HEADS // hp_heads
    qblk, kblk, vblk = COLB["f_q"] // 128, COLB["f_k"] // 128, COLB["f_v"] // 128
    kern = functools.partial(_fox_kernel, tq=tq, hp_heads=hp_heads)
    return pl.pallas_call(
        kern,
        grid=(batch, n_hp, nq),
        scratch_shapes=[pltpu.VMEM((hp_heads, tq, hp_heads * HEAD_DIM), F32),
                        pltpu.VMEM((hp_heads, tq, 2 * hp_heads * HEAD_DIM), F32),
                        pltpu.VMEM((2, hp_heads, tq, tq), F32),
                        pltpu.VMEM((seq, 2 * hp_heads * HEAD_DIM), BF16)],
        in_specs=[
            pl.BlockSpec((tq, 128), lambda b, hp, qi: (b * nq + qi, qblk + hp)),
            pl.BlockSpec((seq, 128), lambda b, hp, qi: (b, kblk + hp)),
            pl.BlockSpec((seq, 128), lambda b, hp, qi: (b, vblk + hp)),
            pl.BlockSpec((tq, 128), lambda b, hp, qi: (b * nq + qi, 0)),
            pl.BlockSpec((seq, 128), lambda b, hp, qi: (b, 0)),
        ],
        out_specs=pl.BlockSpec((tq, 128), lambda b, hp, qi: (b * nq + qi, hp)),
        out_shape=jax.ShapeDtypeStruct((T, BRANCH_WIDTH), BF16),
        compiler_params=_cparams("parallel", "parallel", "arbitrary"),
        name="fox_attention",
    )(proj, proj, proj, fc, fc)


def _split2(x):
    hi = x.astype(BF16)
    return hi, (x - hi.astype(F32)).astype(BF16)


def _split3(x):
    hi = x.astype(BF16)
    r = x - hi.astype(F32)
    mid = r.astype(BF16)
    return hi, mid, (r - mid.astype(F32)).astype(BF16)


def _rotate_half(x, neg_first_half):
    n = x.shape[-1]
    half = HEAD_DIM // 2
    fwd = pltpu.roll(x, half, 1)
    bwd = pltpu.roll(x, n - half, 1)
    return jnp.where(neg_first_half, -bwd, fwd)


def _linear_attn_kernel(*refs, mode, cb):
    if mode == "gla":
        (q_ref, k_ref, v_ref, gate_ref, small_ref, wg_ref, bg_ref, ng_ref, o_ref, st_ref) = refs
    else:
        (q_ref, k_ref, v_ref, gate_ref, lg_ref, cos_ref, sin_ref, gw_ref, gb_ref, o_ref, st_ref) = refs
    H, DK, DV, C = GLA_HEADS, GLA_DK, GLA_DV, CHUNK
    nch = cb // C
    W2 = 2 * DK
    HW = H * DK
    dn_lanes = (((1,), (1,)), ((), ()))
    dn_rows = (((0,), (0,)), ((), ()))

    @pl.when(pl.program_id(1) == 0)
    def _():
        st_ref[...] = jnp.zeros_like(st_ref)

    q = q_ref[...].astype(F32)
    k = k_ref[...].astype(F32)
    if mode == "gla":
        glr = small_ref[...][:, FOX_HEADS:FOX_HEADS + GLA_GATE_RANK]
        a_hi, a_lo = _split2(glr)
        w_hi, w_lo = _split2(wg_ref[...])
        z = (jnp.dot(a_hi, w_hi, preferred_element_type=F32) + jnp.dot(a_hi, w_lo, preferred_element_type=F32)
             + jnp.dot(a_lo, w_hi, preferred_element_type=F32)) + bg_ref[...]
        ld = _log_sigmoid(z) / GLA_TAU
        ld_w = jnp.concatenate([ld[c * C:(c + 1) * C] for c in range(nch)], axis=1)
        ri = lax.broadcasted_iota(jnp.int32, (C, 3 * C), 0)
        ci = lax.broadcasted_iota(jnp.int32, (C, 3 * C), 1)
        tri3 = jnp.where((ci % C) <= ri, 1.0, 0.0).astype(BF16)
        cum_w = jnp.dot(tri3, jnp.concatenate(_split3(ld_w), axis=0), preferred_element_type=F32)
        cum = jnp.concatenate([cum_w[:, c * HW:(c + 1) * HW] for c in range(nch)], axis=0)
        lasts = [cum_w[C - 1:C, c * HW:(c + 1) * HW] for c in range(nch)]
        last_b = jnp.concatenate([jnp.broadcast_to(l, (C, HW)) for l in lasts], axis=0)
    else:
        lane = lax.broadcasted_iota(jnp.int32, (cb, HW), 1)
        first_half = (lane % HEAD_DIM) < (HEAD_DIM // 2)
        cos = cos_ref[...]
        sin = sin_ref[...]
        q = q * cos + _rotate_half(q, first_half) * sin
        k = k * cos + _rotate_half(k, first_half) * sin
        steps = (lax.broadcasted_iota(jnp.int32, (cb, HW), 0) % C + 1).astype(F32)
        cum = steps * lg_ref[...]
        last_b = float(C) * lg_ref[...]
        lasts = [last_b] * nch
    q_in = (q * (DK ** -0.5) * jnp.exp(cum)).astype(BF16)
    k_in = (k * jnp.exp(-cum)).astype(BF16)
    k_st = (k * jnp.exp(last_b - cum)).astype(BF16)
    decs = [jnp.exp(l) for l in lasts]

    rowi = lax.broadcasted_iota(jnp.int32, (2 * C, W2), 0)
    lanei = lax.broadcasted_iota(jnp.int32, (2 * C, W2), 1)
    own = (rowi >= C) == (lanei >= DK)
    bd_causal = own & ((lanei % C) <= (rowi % C))
    zero_bf = jnp.zeros((2 * C, W2), BF16)

    def pair_ops(c, p):
        rows = slice(c * C, (c + 1) * C)
        ls = slice(p * W2, (p + 1) * W2)
        qp, kp, ksp = q_in[rows, ls], k_in[rows, ls], k_st[rows, ls]
        qstack = jnp.where(own, jnp.concatenate([qp, qp], axis=0), zero_bf)
        ksstack = jnp.where(own, jnp.concatenate([ksp, ksp], axis=0), zero_bf)
        k2 = jnp.concatenate([kp, kp], axis=0)
        vstack = jnp.concatenate([v_ref[rows, (2 * p) * DV:(2 * p + 1) * DV],
                                  v_ref[rows, (2 * p + 1) * DV:(2 * p + 2) * DV]], axis=0)
        sw = lax.dot_general(qstack, k2, dn_lanes, preferred_element_type=F32)
        sc = jnp.where(bd_causal, sw, 0.0).astype(BF16)
        o_intra = jnp.dot(sc, vstack, preferred_element_type=F32)
        upd = lax.dot_general(vstack, ksstack, dn_rows, preferred_element_type=F32)
        return qstack, o_intra, upd

    pre = [[pair_ops(c, p) for p in range(H // 2)] for c in range(nch)]
    state = [st_ref[p] for p in range(H // 2)]
    for c in range(nch):
        rows = slice(c * C, (c + 1) * C)
        for p in range(H // 2):
            qstack, o_intra, upd = pre[c][p]
            o = o_intra + lax.dot_general(qstack, state[p].astype(BF16), dn_lanes, preferred_element_type=F32)
            state[p] = state[p] * decs[c][:, p * W2:(p + 1) * W2] + upd
            for hh in range(2):
                hd = 2 * p + hh
                vs = slice(hd * DV, (hd + 1) * DV)
                oh = o[hh * C:(hh + 1) * C]
                gate = _silu(gate_ref[rows, vs])
                if mode == "gla":
                    y = oh * lax.rsqrt(jnp.mean(oh * oh, axis=-1, keepdims=True) + EPS) * ng_ref[...]
                else:
                    mu = jnp.mean(oh, axis=-1, keepdims=True)
                    var = jnp.mean(jnp.square(oh - mu), axis=-1, keepdims=True)
                    y = (oh - mu) * lax.rsqrt(var + EPS) * gw_ref[:, vs] + gb_ref[:, vs]
                o_ref[rows, vs] = (y * gate).astype(o_ref.dtype)
    for p in range(H // 2):
        st_ref[p] = state[p]


def linear_attention(proj, projf, params, *, mode, batch, seq):
    T = proj.shape[0]
    cb = min(512, seq)
    nc = seq // cb
    pre = "g" if mode == "gla" else "r"
    qblk, kblk = COLB[pre + "_q"] // 256, COLB[pre + "_k"] // 256
    vblk = COLB[pre + "_v"] // 512
    gblk = COLF["g_r" if mode == "gla" else "r_g"] // 512
    row = lambda b, c: b * nc + c
    in_specs = [
        pl.BlockSpec((cb, 256), lambda b, c: (row(b, c), qblk)),
        pl.BlockSpec((cb, 256), lambda b, c: (row(b, c), kblk)),
        pl.BlockSpec((cb, 512), lambda b, c: (row(b, c), vblk)),
        pl.BlockSpec((cb, 512), lambda b, c: (row(b, c), gblk)),
    ]
    args = [proj, proj, proj, projf]
    if mode == "gla":
        wg, bg, ng = params
        in_specs += [
            pl.BlockSpec((cb, 128), lambda b, c: (row(b, c), SMALL_COL // 128)),
            pl.BlockSpec(wg.shape, lambda b, c: (0, 0)),
            pl.BlockSpec((1, bg.shape[-1]), lambda b, c: (0, 0)),
            pl.BlockSpec((1, ng.shape[-1]), lambda b, c: (0, 0)),
        ]
        args += [projf, wg, bg.reshape(1, -1), ng.reshape(1, -1)]
    else:
        lg, cos, sin, gw, gb = params
        in_specs += [
            pl.BlockSpec((1, lg.shape[-1]), lambda b, c: (0, 0)),
            pl.BlockSpec((cb, 256), lambda b, c: (c, 0)),
            pl.BlockSpec((cb, 256), lambda b, c: (c, 0)),
            pl.BlockSpec((1, gw.shape[-1]), lambda b, c: (0, 0)),
            pl.BlockSpec((1, gb.shape[-1]), lambda b, c: (0, 0)),
        ]
        args += [lg.reshape(1, -1), cos, sin, gw.reshape(1, -1), gb.reshape(1, -1)]
    return pl.pallas_call(
        functools.partial(_linear_attn_kernel, mode=mode, cb=cb),
        grid=(batch, nc),
        in_specs=in_specs,
        out_specs=pl.BlockSpec((cb, 512), lambda b, c: (row(b, c), 0)),
        out_shape=jax.ShapeDtypeStruct((T, BRANCH_WIDTH), BF16),
        scratch_shapes=[pltpu.VMEM((GLA_HEADS // 2, GLA_DV, 2 * GLA_DK), F32)],
        compiler_params=_cparams("parallel", "arbitrary"),
        name="linear_attention_" + mode,
    )(*args)


def _retention_tables(seq):
    half = RET_DK // 2
    inv = RET_THETA_BASE ** (-jnp.arange(half, dtype=F32) / half)
    ang = jnp.arange(seq).astype(F32)[:, None] * inv[None, :]
    cos = jnp.tile(jnp.cos(ang), (1, 2 * RET_HEADS))
    sin = jnp.tile(jnp.sin(ang), (1, 2 * RET_HEADS))
    log_gamma = jnp.log1p(-jnp.exp2(-5.0 - jnp.arange(RET_HEADS, dtype=F32)))
    return jnp.repeat(log_gamma, RET_DK), cos, sin


def _merge_kernel(u_ref, oa_ref, ob_ref, oc_ref, od_ref, wm_ref, wb_ref, bm_ref, m_ref):
    u = u_ref[...]
    acc = None
    for i, o_ref in enumerate((oa_ref, ob_ref, oc_ref, od_ref)):
        z = jnp.dot(u, wm_ref[i].astype(BF16), preferred_element_type=F32) + bm_ref[i]
        pr = jnp.dot(o_ref[...], wb_ref[i].astype(BF16), preferred_element_type=F32)
        t = jax.nn.sigmoid(z) * pr
        acc = t if acc is None else acc + t
    m_ref[...] = acc.astype(m_ref.dtype)


def merge_branches(u, branches, wm, wb, bm, *, layer):
    T, D = u.shape
    tm, tn = min(1024, T), min(256, D)
    ospec = pl.BlockSpec((tm, BRANCH_WIDTH), lambda i, j: (i, 0))
    return pl.pallas_call(
        _merge_kernel,
        grid=(T // tm, D // tn),
        in_specs=[
            pl.BlockSpec((tm, D), lambda i, j: (i, 0)),
            ospec, ospec, ospec, ospec,
            pl.BlockSpec((None, N_BRANCH, D, tn), lambda i, j: (layer, 0, 0, j)),
            pl.BlockSpec((None, N_BRANCH, BRANCH_WIDTH, tn), lambda i, j: (layer, 0, 0, j)),
            pl.BlockSpec((None, N_BRANCH, 1, tn), lambda i, j: (layer, 0, 0, j)),
        ],
        out_specs=pl.BlockSpec((tm, tn), lambda i, j: (i, j)),
        out_shape=jax.ShapeDtypeStruct((T, D), BF16),
        compiler_params=_cparams("parallel", "arbitrary"),
        name="merge_branches",
    )(u, *branches, wm, wb, bm.reshape(bm.shape[0], N_BRANCH, 1, D))


def _out_proj_kernel(h_ref, m_ref, w_ref, mod_ref, o_ref, *, mrow):
    m = m_ref[...]
    for c0 in range(0, o_ref.shape[1], PROJ_COL_CHUNK):
        cols = slice(c0, c0 + PROJ_COL_CHUNK)
        y = jnp.dot(m, w_ref[:, cols], preferred_element_type=F32)
        o_ref[:, cols] = h_ref[:, cols] + mod_ref[mrow:mrow + 1, cols] * y


def out_proj(h, merged, w, mod, *, layer, seq, mrow):
    T, D = h.shape
    tm = min(512, seq)
    return pl.pallas_call(
        functools.partial(_out_proj_kernel, mrow=mrow),
        grid=(T // tm,),
        in_specs=[
            pl.BlockSpec((tm, D), lambda i: (i, 0)),
            pl.BlockSpec((tm, D), lambda i: (i, 0)),
            pl.BlockSpec((None, D, D), lambda i: (layer, 0, 0), pipeline_mode=pl.Buffered(1)),
            pl.BlockSpec((None, N_MOD, D), lambda i: ((i * tm) // seq, 0, 0)),
        ],
        out_specs=pl.BlockSpec((tm, D), lambda i: (i, 0)),
        out_shape=jax.ShapeDtypeStruct((T, D), F32),
        compiler_params=_cparams("parallel"),
        name="out_proj",
    )(h, merged, w, mod)


def _permuted_w_in(w_in_l, order, cols):
    D = w_in_l.shape[0]
    parts = [w_in_l[:, _REF_COLS[n][0]:_REF_COLS[n][0] + _REF_COLS[n][1]].astype(BF16) for n in order]
    used = sum(_REF_COLS[n][1] for n in order)
    if cols > used:
        parts.append(jnp.zeros((D, cols - used), BF16))
    return jnp.concatenate(parts, axis=1)


def kernel(x, c, w_ada, b_ada, norm_g, ffn_w_gate, ffn_w_up, ffn_w_down, w_in, fox_b_forget, attn_sinks,
           gla_w_gate, gla_b_gate, gla_norm_g, ret_gn_w, ret_gn_b, w_branch, w_merge, b_merge, w_out,
           final_norm_g):
    B, S, D = x.shape
    L = w_ada.shape[0]
    T = B * S
    mod_all = ada_modulation(c, w_ada, b_ada)
    log_gamma, cos, sin = _retention_tables(S)
    w_out_bf = w_out.astype(BF16)
    h = x.reshape(T, D)
    for l in range(L):
        mod = mod_all[l]
        h = ffn_block(h, mod, norm_g[l, 0], ffn_w_gate, ffn_w_up, ffn_w_down, layer=l, which=0, seq=S,
                      mrow=0, res_scale=0.5)
        proj, projf, u = norm_proj(h, mod, norm_g[l, 1], _permuted_w_in(w_in[l], _ORDER_B, PROJ_B_COLS),
                                   _permuted_w_in(w_in[l], _ORDER_F, PROJ_F_COLS), seq=S, mrow=3)
        o_a = swa_attention(proj, attn_sinks[l], batch=B, seq=S)
        fc = fox_forget_cumsum(projf, fox_b_forget[l], batch=B, seq=S)
        o_b = fox_attention(proj, fc, batch=B, seq=S)
        o_c = linear_attention(proj, projf, (gla_w_gate[l], gla_b_gate[l], gla_norm_g[l]), mode="gla",
                               batch=B, seq=S)
        o_d = linear_attention(proj, projf, (log_gamma, cos, sin, ret_gn_w[l], ret_gn_b[l]), mode="ret",
                               batch=B, seq=S)
        merged = merge_branches(u, (o_a, o_b, o_c, o_d), w_merge, w_branch, b_merge, layer=l)
        h = out_proj(h, merged, w_out_bf, mod, layer=l, seq=S, mrow=5)
        h = ffn_block(h, mod, norm_g[l, 2], ffn_w_gate, ffn_w_up, ffn_w_down, layer=l, which=1, seq=S,
                      mrow=6, res_scale=0.5, final_g=final_norm_g if l == L - 1 else None)
    return h.reshape(B, S, D)
```

```python
import functools

import jax
import jax.numpy as jnp
from jax import lax
from jax.experimental import pallas as pl
from jax.experimental.pallas import tpu as pltpu

HEAD_DIM = 64
SWA_HEADS = 8
SWA_KV_HEADS = 2
WINDOW = 128
FOX_HEADS = 8
GLA_HEADS = 4
GLA_DK = 64
GLA_DV = 128
GLA_GATE_RANK = 16
GLA_TAU = 16.0
RET_HEADS = 4
RET_DK = 64
RET_DV = 128
RET_THETA_BASE = 10000.0
CHUNK = 64
BRANCH_WIDTH = 512
N_BRANCH = 4
N_MOD = 9
EPS = 1e-6

BF16 = jnp.bfloat16
F32 = jnp.float32

VMEM_LIMIT_BYTES = 60 * 1024 * 1024

_REF_COLS = {}
_off = 0
for _name, _size in (
        ("a_q", 512), ("a_k", 128), ("a_v", 128),
        ("f_q", 512), ("f_k", 512), ("f_v", 512), ("f_f", 8),
        ("g_q", 256), ("g_k", 256), ("g_v", 512), ("g_lr", 16), ("g_r", 512),
        ("r_q", 256), ("r_k", 256), ("r_v", 512), ("r_g", 512)):
    _REF_COLS[_name] = (_off, _size)
    _off += _size
IN_COLS = _off
for _name in ("a_k", "a_v"):
    for _i in range(SWA_KV_HEADS):
        _REF_COLS[_name + str(_i)] = (_REF_COLS[_name][0] + _i * HEAD_DIM, HEAD_DIM)


def _layout(order):
    col, off = {}, 0
    for name in order:
        col.setdefault(name, off)
        off += _REF_COLS[name][1]
    return col


_ORDER_B = ("a_q", "f_q", "f_k", "f_v", "g_v", "r_v", "g_q", "g_k", "r_q", "r_k",
            "a_k0", "a_k0", "a_k1", "a_k1", "a_v0", "a_v0", "a_v1", "a_v1")
_ORDER_F = ("g_r", "r_g", "f_f", "g_lr")
COLB = _layout(_ORDER_B)
COLF = _layout(_ORDER_F)
PROJ_B_COLS = 4608
PROJ_F_COLS = 1152
SMALL_COL = COLF["f_f"]


def _cparams(*sem):
    return pltpu.CompilerParams(dimension_semantics=sem, vmem_limit_bytes=VMEM_LIMIT_BYTES)


def _tile(n, pref):
    t = (min(pref, n) // 128) * 128
    while t >= 128:
        if n % t == 0:
            return t
        t -= 128
    return n


NORM_ROWS = 64


def _adanorm_into(h_ref, g_ref, mod_ref, mrow, u_ref, zero_ref=None):
    gs = g_ref[...] * (1.0 + mod_ref[mrow + 1:mrow + 2, :])
    shift = mod_ref[mrow:mrow + 1, :]
    rows = min(NORM_ROWS, h_ref.shape[0])

    def body(r, carry):
        r0 = pl.multiple_of(r * rows, rows)
        x = h_ref[pl.ds(r0, rows), :]
        ms = jnp.mean(x * x, axis=-1, keepdims=True)
        u_ref[pl.ds(r0, rows), :] = (x * lax.rsqrt(ms + EPS) * gs + shift).astype(u_ref.dtype)
        if zero_ref is not None:
            zero_ref[pl.ds(r0, rows), :] = jnp.zeros((rows, zero_ref.shape[1]), zero_ref.dtype)
        return carry

    lax.fori_loop(0, h_ref.shape[0] // rows, body, 0)


def _log_sigmoid(x):
    return jnp.minimum(x, 0.0) - jnp.log1p(jnp.exp(-jnp.abs(x)))


def _silu(x):
    return x * jax.nn.sigmoid(x)


ADA_ROWS = 16


def _ada_kernel(c_ref, w_ref, b_ref, o_ref):
    cond = _silu(c_ref[...])
    r = jnp.dot(jnp.concatenate(_split3(cond), axis=0), w_ref[...].astype(BF16), preferred_element_type=F32)
    bp = ADA_ROWS
    o_ref[...] = (r[:bp] + r[bp:2 * bp]) + r[2 * bp:] + b_ref[...]


def ada_modulation(c, w_ada, b_ada):
    L, D, N = w_ada.shape
    B = c.shape[0]
    BP = ADA_ROWS
    cp = jnp.zeros((BP, D), F32).at[:B].set(c)
    tn = _tile(N, 1024)
    out = pl.pallas_call(
        _ada_kernel,
        grid=(L, N // tn),
        in_specs=[
            pl.BlockSpec((BP, D), lambda l, j: (0, 0)),
            pl.BlockSpec((None, D, tn), lambda l, j: (l, 0, j)),
            pl.BlockSpec((None, 1, tn), lambda l, j: (l, 0, j)),
        ],
        out_specs=pl.BlockSpec((None, BP, tn), lambda l, j: (l, 0, j)),
        out_shape=jax.ShapeDtypeStruct((L, BP, N), F32),
        compiler_params=_cparams("parallel", "parallel"),
        name="ada_modulation",
    )(cp, w_ada, b_ada.reshape(L, 1, N))
    return out[:, :B].reshape(L, B, N_MOD, D)


def _ffn_kernel(h_ref, mod_ref, g_ref, wg_ref, wu_ref, wd_ref, *rest, mrow, res_scale, final_norm):
    if final_norm:
        fg_ref, o_ref, u_scr = rest
    else:
        o_ref, u_scr = rest
    f = pl.program_id(1)

    @pl.when(f == 0)
    def _():
        _adanorm_into(h_ref, g_ref, mod_ref, mrow, u_scr, zero_ref=o_ref)

    u = u_scr[...]
    a = jnp.dot(u, wg_ref[...].astype(BF16), preferred_element_type=F32)
    b = jnp.dot(u, wu_ref[...].astype(BF16), preferred_element_type=F32)
    mid = (_silu(a) * b).astype(BF16)
    o_ref[...] += jnp.dot(mid, wd_ref[...].astype(BF16), preferred_element_type=F32)

    @pl.when(f == pl.num_programs(1) - 1)
    def _():
        gate = res_scale * mod_ref[mrow + 2:mrow + 3, :]
        rows = min(NORM_ROWS, h_ref.shape[0])

        def body(r, carry):
            r0 = pl.multiple_of(r * rows, rows)
            y = h_ref[pl.ds(r0, rows), :] + gate * o_ref[pl.ds(r0, rows), :]
            if final_norm:
                y = y * lax.rsqrt(jnp.mean(y * y, axis=-1, keepdims=True) + EPS) * fg_ref[...]
            o_ref[pl.ds(r0, rows), :] = y
            return carry

        lax.fori_loop(0, h_ref.shape[0] // rows, body, 0)


def ffn_block(h, mod, g, wg, wu, wd, *, layer, which, seq, mrow, res_scale, final_g=None):
    T, D = h.shape
    F = wg.shape[-1]
    tm = min(1024, seq)
    tf = _tile(F, 256)
    kern = functools.partial(_ffn_kernel, mrow=mrow, res_scale=res_scale, final_norm=final_g is not None)
    in_specs = [
        pl.BlockSpec((tm, D), lambda i, f: (i, 0)),
        pl.BlockSpec((None, N_MOD, D), lambda i, f: ((i * tm) // seq, 0, 0)),
        pl.BlockSpec((1, D), lambda i, f: (0, 0)),
        pl.BlockSpec((None, None, D, tf), lambda i, f: (layer, which, 0, f)),
        pl.BlockSpec((None, None, D, tf), lambda i, f: (layer, which, 0, f)),
        pl.BlockSpec((None, None, tf, D), lambda i, f: (layer, which, f, 0)),
    ]
    args = [h, mod, g.reshape(1, D), wg, wu, wd]
    if final_g is not None:
        in_specs.append(pl.BlockSpec((1, D), lambda i, f: (0, 0)))
        args.append(final_g.reshape(1, D))
    return pl.pallas_call(
        kern,
        grid=(T // tm, F // tf),
        in_specs=in_specs,
        out_specs=pl.BlockSpec((tm, D), lambda i, f: (i, 0)),
        out_shape=jax.ShapeDtypeStruct((T, D), F32),
        scratch_shapes=[pltpu.VMEM((tm, D), BF16)],
        compiler_params=_cparams("parallel", "arbitrary"),
        name="ffn_block",
    )(*args)


PROJ_COL_CHUNK = 512


def _norm_proj_kernel(h_ref, mod_ref, g_ref, wb_ref, wf_ref, pb_ref, pf_ref, u_ref, *, mrow):
    _adanorm_into(h_ref, g_ref, mod_ref, mrow, u_ref)
    u = u_ref[...]
    for c0 in range(0, pb_ref.shape[1], PROJ_COL_CHUNK):
        cols = slice(c0, c0 + PROJ_COL_CHUNK)
        pb_ref[:, cols] = jnp.dot(u, wb_ref[:, cols], preferred_element_type=F32).astype(pb_ref.dtype)
    pf_ref[...] = jnp.dot(u, wf_ref[...], preferred_element_type=F32)


def norm_proj(h, mod, g, wb, wf, *, layer, seq, mrow):
    T, D = h.shape
    NB, NF = wb.shape[2], wf.shape[2]
    tm = min(512, seq)
    kern = functools.partial(_norm_proj_kernel, mrow=mrow)
    resident = pl.Buffered(1)
    return pl.pallas_call(
        kern,
        grid=(T // tm,),
        in_specs=[
            pl.BlockSpec((tm, D), lambda i: (i, 0)),
            pl.BlockSpec((None, N_MOD, D), lambda i: ((i * tm) // seq, 0, 0)),
            pl.BlockSpec((1, D), lambda i: (0, 0)),
            pl.BlockSpec((None, D, NB), lambda i: (layer, 0, 0), pipeline_mode=resident),
            pl.BlockSpec((None, D, NF), lambda i: (layer, 0, 0), pipeline_mode=resident),
        ],
        out_specs=[
            pl.BlockSpec((tm, NB), lambda i: (i, 0)),
            pl.BlockSpec((tm, NF), lambda i: (i, 0)),
            pl.BlockSpec((tm, D), lambda i: (i, 0)),
        ],
        out_shape=[jax.ShapeDtypeStruct((T, NB), BF16), jax.ShapeDtypeStruct((T, NF), F32),
                   jax.ShapeDtypeStruct((T, D), BF16)],
        compiler_params=_cparams("parallel"),
        name="norm_proj",
    )(h, mod, g.reshape(1, D), wb, wf)


def _swa_kernel(sink_ref, q_ref, kv_ref, kvp_ref, o_ref):
    n = pl.program_id(1)
    W = WINDOW
    d = HEAD_DIM
    rowi = lax.broadcasted_iota(jnp.int32, (2 * W, 2 * d), 0)
    lanei = lax.broadcasted_iota(jnp.int32, (2 * W, 2 * d), 1)
    own = (rowi >= W) == (lanei >= d)
    first_half = lax.broadcasted_iota(jnp.int32, (W, 2 * d), 1) < d
    qpos = lax.broadcasted_iota(jnp.int32, (2 * W, 2 * W), 0) % W + W
    kpos = lax.broadcasted_iota(jnp.int32, (2 * W, 2 * W), 1)
    rel = qpos - kpos
    band = (rel >= 0) & (rel < W)
    ones = jnp.ones((2 * W, 2 * d), BF16)
    group = SWA_HEADS // SWA_KV_HEADS
    for blk in range(SWA_BLOCKS_PER_STEP):
        rows = slice(blk * W, (blk + 1) * W)
        q = q_ref[rows, :] * (d ** -0.5)
        prev = kvp_ref[...] if blk == 0 else kv_ref[(blk - 1) * W:blk * W, :]
        kv = jnp.concatenate([prev, kv_ref[rows, :]], axis=0)
        mask = band & ((kpos >= W) | (n > 0)) if blk == 0 else band
        for g in range(SWA_HEADS // 2):
            kvh = (2 * g) // group
            k2 = kv[:, kvh * 2 * d:(kvh + 1) * 2 * d]
            v2 = kv[:, (SWA_KV_HEADS + kvh) * 2 * d:(SWA_KV_HEADS + kvh + 1) * 2 * d]
            v_ext = jnp.concatenate([v2, ones], axis=1)
            qp = q[:, g * 2 * d:(g + 1) * 2 * d]
            qstack = jnp.where(own, jnp.concatenate([qp, qp], axis=0), jnp.zeros((2 * W, 2 * d), BF16))
            logits = lax.dot_general(qstack, k2, (((1,), (1,)), ((), ())), preferred_element_type=F32)
            logits = jnp.where(mask, logits, -jnp.inf)
            sink = jnp.where(rowi >= W, sink_ref[2 * g + 1], sink_ref[2 * g])
            m = jnp.maximum(jnp.max(logits, axis=-1, keepdims=True), sink)
            p = jnp.exp(logits - jnp.concatenate([m, m], axis=1)).astype(BF16)
            r = jnp.dot(p, v_ext, preferred_element_type=F32)
            o2 = r[:, :2 * d] / (r[:, 2 * d:] + jnp.exp(sink - m))
            out = jnp.where(first_half, o2[:W], o2[W:])
            o_ref[rows, g * 2 * d:(g + 1) * 2 * d] = out.astype(o_ref.dtype)


SWA_BLOCKS_PER_STEP = 2


def swa_attention(proj, sinks, *, batch, seq):
    T = proj.shape[0]
    nb = seq // WINDOW
    per = SWA_BLOCKS_PER_STEP
    ns = nb // per
    qblk = COLB["a_q"] // 512
    kvblk = COLB["a_k0"] // 512
    return pl.pallas_call(
        _swa_kernel,
        grid=(batch, ns),
        in_specs=[
            pl.BlockSpec(memory_space=pltpu.SMEM),
            pl.BlockSpec((per * WINDOW, 512), lambda b, n: (b * ns + n, qblk)),
            pl.BlockSpec((per * WINDOW, 512), lambda b, n: (b * ns + n, kvblk)),
            pl.BlockSpec((WINDOW, 512), lambda b, n: (b * nb + jnp.maximum(per * n - 1, 0), kvblk)),
        ],
        out_specs=pl.BlockSpec((per * WINDOW, 512), lambda b, n: (b * ns + n, 0)),
        out_shape=jax.ShapeDtypeStruct((T, BRANCH_WIDTH), BF16),
        compiler_params=_cparams("parallel", "arbitrary"),
        name="swa_attention",
    )(sinks, proj, proj, proj)


def _fox_cum_kernel(x_ref, b_ref, o_ref, *, seq):
    R = 128
    ri = lax.broadcasted_iota(jnp.int32, (R, 3 * R), 0)
    ci = lax.broadcasted_iota(jnp.int32, (R, 3 * R), 1)
    tri3 = jnp.where((ci % R) <= ri, 1.0, 0.0).astype(BF16)

    def body(i, carry):
        r0 = pl.multiple_of(i * R, R)
        ls = _log_sigmoid(x_ref[pl.ds(r0, R), :] + b_ref[...])
        cum = jnp.dot(tri3, jnp.concatenate(_split3(ls), axis=0), preferred_element_type=F32) + carry
        o_ref[pl.ds(r0, R), :] = cum
        return cum[R - 1:R, :]

    lax.fori_loop(0, seq // R, body, jnp.zeros((1, R), F32))


def fox_forget_cumsum(proj, fox_b, *, batch, seq):
    T = proj.shape[0]
    blk = SMALL_COL // 128
    bias = jnp.zeros((1, 128), F32).at[0, :FOX_HEADS].set(fox_b)
    return pl.pallas_call(
        functools.partial(_fox_cum_kernel, seq=seq),
        grid=(batch,),
        in_specs=[
            pl.BlockSpec((seq, 128), lambda b: (b, blk)),
            pl.BlockSpec((1, 128), lambda b: (0, 0)),
        ],
        out_specs=pl.BlockSpec((seq, 128), lambda b: (b, 0)),
        out_shape=jax.ShapeDtypeStruct((T, 128), F32),
        compiler_params=_cparams("parallel"),
        name="fox_forget_cumsum",
    )(proj, bias)


FOX_KEY_ROWS = 512


def _fox_kernel(q_ref, k_ref, v_ref, fc_ref, fcs_ref, o_ref, m_scr, acc_scr, s_scr, kx_scr, *, tq, hp_heads):
    hp = pl.program_id(1)
    qi = pl.program_id(2)
    tk = tq
    d = HEAD_DIM
    w = hp_heads * d
    seq = k_ref.shape[0]
    e_row = lax.broadcasted_iota(jnp.int32, (w, w), 0)
    e_lane = lax.broadcasted_iota(jnp.int32, (w, w), 1)

    @pl.when(qi == 0)
    def _():
        kr = min(FOX_KEY_ROWS, seq)
        place = [sum(jnp.where((e_row == hp * hp_heads + hh) & (e_lane == 3 * hh + j), 1.0, 0.0)
                     for hh in range(hp_heads)).astype(BF16) for j in range(3)]
        for r0 in range(0, seq, kr):
            rows = slice(r0, r0 + kr)
            ext = sum(jnp.dot(piece, place[j], preferred_element_type=F32)
                      for j, piece in enumerate(_split3(-fcs_ref[rows, :])))
            kx_scr[rows, :w] = k_ref[rows, :]
            kx_scr[rows, w:] = ext.astype(BF16)

    lane = lax.broadcasted_iota(jnp.int32, (tq, w), 1)
    q = q_ref[...] * (d ** -0.5)
    zero = jnp.zeros_like(q)
    qs = [jnp.concatenate([jnp.where((lane >= hh * d) & (lane < (hh + 1) * d), q, zero),
                           jnp.where((lane >= 3 * hh) & (lane < 3 * hh + 3), 1.0, 0.0).astype(BF16)], axis=1)
          for hh in range(hp_heads)]
    m_scr[...] = jnp.full(m_scr.shape, -jnp.inf, F32)
    acc_scr[...] = jnp.zeros(acc_scr.shape, F32)
    fc = fc_ref[...]
    fqs = [jnp.broadcast_to(jnp.sum(jnp.where(lane == hp * hp_heads + hh, fc, 0.0), axis=-1, keepdims=True),
                            (tq, w)) for hh in range(hp_heads)]
    ones = jnp.ones((tk, w), BF16)

    def scores(kb, slot):
        k0 = pl.multiple_of(kb * tk, tk)
        kx = kx_scr[pl.ds(k0, tk), :]
        for hh in range(hp_heads):
            s_scr[slot, hh] = lax.dot_general(qs[hh], kx, (((1,), (1,)), ((), ())), preferred_element_type=F32)

    def softmax_pv(kb, slot, masked):
        k0 = pl.multiple_of(kb * tk, tk)
        v = jnp.concatenate([v_ref[pl.ds(k0, tk), :], ones], axis=1)
        for hh in range(hp_heads):
            t = s_scr[slot, hh]
            if masked:
                row = lax.broadcasted_iota(jnp.int32, (tq, tk), 0)
                col = lax.broadcasted_iota(jnp.int32, (tq, tk), 1)
                t = jnp.where(col <= row, t, -jnp.inf)
            m_old = m_scr[hh]
            m_new = jnp.maximum(m_old, fqs[hh] + jnp.max(t, axis=-1, keepdims=True))
            alpha = jnp.exp(m_old - m_new)
            c = fqs[hh] - m_new
            p = jnp.exp(t + jnp.concatenate([c] * (tk // w), axis=1))
            acc_scr[hh] = (jnp.concatenate([alpha, alpha], axis=1) * acc_scr[hh]
                           + jnp.dot(p.astype(BF16), v, preferred_element_type=F32))
            m_scr[hh] = m_new

    scores(0, 0)

    def body(j, carry):
        kb = 2 * j
        softmax_pv(kb, 0, False)
        scores(kb + 1, 1)
        softmax_pv(kb + 1, 1, False)
        scores(kb + 2, 0)
        return carry

    lax.fori_loop(0, qi // 2, body, 0)

    @pl.when(qi % 2 == 1)
    def _():
        softmax_pv(qi - 1, 0, False)
        scores(qi, 1)
        softmax_pv(qi, 1, True)

    @pl.when(qi % 2 == 0)
    def _():
        softmax_pv(qi, 0, True)
    out = None
    for hh in range(hp_heads):
        o = acc_scr[hh, :, :w] / acc_scr[hh, :, w:]
        out = o if out is None else jnp.where(lane >= hh * d, o, out)
    o_ref[...] = out.astype(o_ref.dtype)


def fox_attention(proj, fc, *, batch, seq):
    T = proj.shape[0]
    tq = min(512, seq)
    nq = seq // tq
    hp_heads = 2
    n_hp = FOX_HEADS // hp_heads
    qblk, kblk, vblk = COLB["f_q"] // 128, COLB["f_k"] // 128, COLB["f_v"] // 128
    kern = functools.partial(_fox_kernel, tq=tq, hp_heads=hp_heads)
    return pl.pallas_call(
        kern,
        grid=(batch, n_hp, nq),
        scratch_shapes=[pltpu.VMEM((hp_heads, tq, hp_heads * HEAD_DIM), F32),
                        pltpu.VMEM((hp_heads, tq, 2 * hp_heads * HEAD_DIM), F32),
                        pltpu.VMEM((2, hp_heads, tq, tq), F32),
                        pltpu.VMEM((seq, 2 * hp_heads * HEAD_DIM), BF16)],
        in_specs=[
            pl.BlockSpec((tq, 128), lambda b, hp, qi: (b * nq + qi, qblk + hp)),
            pl.BlockSpec((seq, 128), lambda b, hp, qi: (b, kblk + hp)),
            pl.BlockSpec((seq, 128), lambda b, hp, qi: (b, vblk + hp)),
            pl.BlockSpec((tq, 128), lambda b, hp, qi: (b * nq + qi, 0)),
            pl.BlockSpec((seq, 128), lambda b, hp, qi: (b, 0)),
        ],
        out_specs=pl.BlockSpec((tq, 128), lambda b, hp, qi: (b * nq + qi, hp)),
        out_shape=jax.ShapeDtypeStruct((T, BRANCH_WIDTH), BF16),
        compiler_params=_cparams("parallel", "parallel", "arbitrary"),
        name="fox_attention",
    )(proj, proj, proj, fc, fc)


def _split2(x):
    hi = x.astype(BF16)
    return hi, (x - hi.astype(F32)).astype(BF16)


def _split3(x):
    hi = x.astype(BF16)
    r = x - hi.astype(F32)
    mid = r.astype(BF16)
    return hi, mid, (r - mid.astype(F32)).astype(BF16)


def _rotate_half(x, neg_first_half):
    n = x.shape[-1]
    half = HEAD_DIM // 2
    fwd = pltpu.roll(x, half, 1)
    bwd = pltpu.roll(x, n - half, 1)
    return jnp.where(neg_first_half, -bwd, fwd)


def _linear_attn_kernel(*refs, mode, cb):
    if mode == "gla":
        (q_ref, k_ref, v_ref, gate_ref, small_ref, wg_ref, bg_ref, ng_ref, o_ref, st_ref) = refs
    else:
        (q_ref, k_ref, v_ref, gate_ref, lg_ref, cos_ref, sin_ref, gw_ref, gb_ref, o_ref, st_ref) = refs
    H, DK, DV, C = GLA_HEADS, GLA_DK, GLA_DV, CHUNK
    nch = cb // C
    W2 = 2 * DK
    HW = H * DK
    dn_lanes = (((1,), (1,)), ((), ()))
    dn_rows = (((0,), (0,)), ((), ()))

    @pl.when(pl.program_id(1) == 0)
    def _():
        st_ref[...] = jnp.zeros_like(st_ref)

    q = q_ref[...].astype(F32)
    k = k_ref[...].astype(F32)
    if mode == "gla":
        glr = small_ref[...][:, FOX_HEADS:FOX_HEADS + GLA_GATE_RANK]
        a_hi, a_lo = _split2(glr)
        w_hi, w_lo = _split2(wg_ref[...])
        z = (jnp.dot(a_hi, w_hi, preferred_element_type=F32) + jnp.dot(a_hi, w_lo, preferred_element_type=F32)
             + jnp.dot(a_lo, w_hi, preferred_element_type=F32)) + bg_ref[...]
        ld = _log_sigmoid(z) / GLA_TAU
        ld_w = jnp.concatenate([ld[c * C:(c + 1) * C] for c in range(nch)], axis=1)
        ri = lax.broadcasted_iota(jnp.int32, (C, 3 * C), 0)
        ci = lax.broadcasted_iota(jnp.int32, (C, 3 * C), 1)
        tri3 = jnp.where((ci % C) <= ri, 1.0, 0.0).astype(BF16)
        cum_w = jnp.dot(tri3, jnp.concatenate(_split3(ld_w), axis=0), preferred_element_type=F32)
        cum = jnp.concatenate([cum_w[:, c * HW:(c + 1) * HW] for c in range(nch)], axis=0)
        lasts = [cum_w[C - 1:C, c * HW:(c + 1) * HW] for c in range(nch)]
        last_b = jnp.concatenate([jnp.broadcast_to(l, (C, HW)) for l in lasts], axis=0)
    else:
        lane = lax.broadcasted_iota(jnp.int32, (cb, HW), 1)
        first_half = (lane % HEAD_DIM) < (HEAD_DIM // 2)
        cos = cos_ref[...]
        sin = sin_ref[...]
        q = q * cos + _rotate_half(q, first_half) * sin
        k = k * cos + _rotate_half(k, first_half) * sin
        steps = (lax.broadcasted_iota(jnp.int32, (cb, HW), 0) % C + 1).astype(F32)
        cum = steps * lg_ref[...]
        last_b = float(C) * lg_ref[...]
        lasts = [last_b] * nch
    q_in = (q * (DK ** -0.5) * jnp.exp(cum)).astype(BF16)
    k_in = (k * jnp.exp(-cum)).astype(BF16)
    k_st = (k * jnp.exp(last_b - cum)).astype(BF16)
    decs = [jnp.exp(l) for l in lasts]

    rowi = lax.broadcasted_iota(jnp.int32, (2 * C, W2), 0)
    lanei = lax.broadcasted_iota(jnp.int32, (2 * C, W2), 1)
    own = (rowi >= C) == (lanei >= DK)
    bd_causal = own & ((lanei % C) <= (rowi % C))
    zero_bf = jnp.zeros((2 * C, W2), BF16)

    def pair_ops(c, p):
        rows = slice(c * C, (c + 1) * C)
        ls = slice(p * W2, (p + 1) * W2)
        qp, kp, ksp = q_in[rows, ls], k_in[rows, ls], k_st[rows, ls]
        qstack = jnp.where(own, jnp.concatenate([qp, qp], axis=0), zero_bf)
        ksstack = jnp.where(own, jnp.concatenate([ksp, ksp], axis=0), zero_bf)
        k2 = jnp.concatenate([kp, kp], axis=0)
        vstack = jnp.concatenate([v_ref[rows, (2 * p) * DV:(2 * p + 1) * DV],
                                  v_ref[rows, (2 * p + 1) * DV:(2 * p + 2) * DV]], axis=0)
        sw = lax.dot_general(qstack, k2, dn_lanes, preferred_element_type=F32)
        sc = jnp.where(bd_causal, sw, 0.0).astype(BF16)
        o_intra = jnp.dot(sc, vstack, preferred_element_type=F32)
        upd = lax.dot_general(vstack, ksstack, dn_rows, preferred_element_type=F32)
        return qstack, o_intra, upd

    pre = [[pair_ops(c, p) for p in range(H // 2)] for c in range(nch)]
    state = [st_ref[p] for p in range(H // 2)]
    for c in range(nch):
        rows = slice(c * C, (c + 1) * C)
        for p in range(H // 2):
            qstack, o_intra, upd = pre[c][p]
            o = o_intra + lax.dot_general(qstack, state[p].astype(BF16), dn_lanes, preferred_element_type=F32)
            state[p] = state[p] * decs[c][:, p * W2:(p + 1) * W2] + upd
            for hh in range(2):
                hd = 2 * p + hh
                vs = slice(hd * DV, (hd + 1) * DV)
                oh = o[hh * C:(hh + 1) * C]
                gate = _silu(gate_ref[rows, vs])
                if mode == "gla":
                    y = oh * lax.rsqrt(jnp.mean(oh * oh, axis=-1, keepdims=True) + EPS) * ng_ref[...]
                else:
                    mu = jnp.mean(oh, axis=-1, keepdims=True)
                    var = jnp.mean(jnp.square(oh - mu), axis=-1, keepdims=True)
                    y = (oh - mu) * lax.rsqrt(var + EPS) * gw_ref[:, vs] + gb_ref[:, vs]
                o_ref[rows, vs] = (y * gate).astype(o_ref.dtype)
    for p in range(H // 2):
        st_ref[p] = state[p]


def linear_attention(proj, projf, params, *, mode, batch, seq):
    T = proj.shape[0]
    cb = min(512, seq)
    nc = seq // cb
    pre = "g" if mode == "gla" else "r"
    qblk, kblk = COLB[pre + "_q"] // 256, COLB[pre + "_k"] // 256
    vblk = COLB[pre + "_v"] // 512
    gblk = COLF["g_r" if mode == "gla" else "r_g"] // 512
    row = lambda b, c: b * nc + c
    in_specs = [
        pl.BlockSpec((cb, 256), lambda b, c: (row(b, c), qblk)),
        pl.BlockSpec((cb, 256), lambda b, c: (row(b, c), kblk)),
        pl.BlockSpec((cb, 512), lambda b, c: (row(b, c), vblk)),
        pl.BlockSpec((cb, 512), lambda b, c: (row(b, c), gblk)),
    ]
    args = [proj, proj, proj, projf]
    if mode == "gla":
        wg, bg, ng = params
        in_specs += [
            pl.BlockSpec((cb, 128), lambda b, c: (row(b, c), SMALL_COL // 128)),
            pl.BlockSpec(wg.shape, lambda b, c: (0, 0)),
            pl.BlockSpec((1, bg.shape[-1]), lambda b, c: (0, 0)),
            pl.BlockSpec((1, ng.shape[-1]), lambda b, c: (0, 0)),
        ]
        args += [projf, wg, bg.reshape(1, -1), ng.reshape(1, -1)]
    else:
        lg, cos, sin, gw, gb = params
        in_specs += [
            pl.BlockSpec((1, lg.shape[-1]), lambda b, c: (0, 0)),
            pl.BlockSpec((cb, 256), lambda b, c: (c, 0)),
            pl.BlockSpec((cb, 256), lambda b, c: (c, 0)),
            pl.BlockSpec((1, gw.shape[-1]), lambda b, c: (0, 0)),
            pl.BlockSpec((1, gb.shape[-1]), lambda b, c: (0, 0)),
        ]
        args += [lg.reshape(1, -1), cos, sin, gw.reshape(1, -1), gb.reshape(1, -1)]
    return pl.pallas_call(
        functools.partial(_linear_attn_kernel, mode=mode, cb=cb),
        grid=(batch, nc),
        in_specs=in_specs,
        out_specs=pl.BlockSpec((cb, 512), lambda b, c: (row(b, c), 0)),
        out_shape=jax.ShapeDtypeStruct((T, BRANCH_WIDTH), BF16),
        scratch_shapes=[pltpu.VMEM((GLA_HEADS // 2, GLA_DV, 2 * GLA_DK), F32)],
        compiler_params=_cparams("parallel", "arbitrary"),
        name="linear_attention_" + mode,
    )(*args)


def _retention_tables(seq):
    half = RET_DK // 2
    inv = RET_THETA_BASE ** (-jnp.arange(half, dtype=F32) / half)
    ang = jnp.arange(seq).astype(F32)[:, None] * inv[None, :]
    cos = jnp.tile(jnp.cos(ang), (1, 2 * RET_HEADS))
    sin = jnp.tile(jnp.sin(ang), (1, 2 * RET_HEADS))
    log_gamma = jnp.log1p(-jnp.exp2(-5.0 - jnp.arange(RET_HEADS, dtype=F32)))
    return jnp.repeat(log_gamma, RET_DK), cos, sin


def _merge_kernel(u_ref, oa_ref, ob_ref, oc_ref, od_ref, wm_ref, wb_ref, bm_ref, m_ref):
    u = u_ref[...]
    acc = None
    for i, o_ref in enumerate((oa_ref, ob_ref, oc_ref, od_ref)):
        z = jnp.dot(u, wm_ref[i].astype(BF16), preferred_element_type=F32) + bm_ref[i]
        pr = jnp.dot(o_ref[...], wb_ref[i].astype(BF16), preferred_element_type=F32)
        t = jax.nn.sigmoid(z) * pr
        acc = t if acc is None else acc + t
    m_ref[...] = acc.astype(m_ref.dtype)


def merge_branches(u, branches, wm, wb, bm, *, layer):
    T, D = u.shape
    tm, tn = min(1024, T), min(256, D)
    ospec = pl.BlockSpec((tm, BRANCH_WIDTH), lambda i, j: (i, 0))
    return pl.pallas_call(
        _merge_kernel,
        grid=(T // tm, D // tn),
        in_specs=[
            pl.BlockSpec((tm, D), lambda i, j: (i, 0)),
            ospec, ospec, ospec, ospec,
            pl.BlockSpec((None, N_BRANCH, D, tn), lambda i, j: (layer, 0, 0, j)),
            pl.BlockSpec((None, N_BRANCH, BRANCH_WIDTH, tn), lambda i, j: (layer, 0, 0, j)),
            pl.BlockSpec((None, N_BRANCH, 1, tn), lambda i, j: (layer, 0, 0, j)),
        ],
        out_specs=pl.BlockSpec((tm, tn), lambda i, j: (i, j)),
        out_shape=jax.ShapeDtypeStruct((T, D), BF16),
        compiler_params=_cparams("parallel", "arbitrary"),
        name="merge_branches",
    )(u, *branches, wm, wb, bm.reshape(bm.shape[0], N_BRANCH, 1, D))


def _out_proj_kernel(h_ref, m_ref, w_ref, mod_ref, o_ref, *, mrow):
    m = m_ref[...]
    for c0 in range(0, o_ref.shape[1], PROJ_COL_CHUNK):
        cols = slice(c0, c0 + PROJ_COL_CHUNK)
        y = jnp.dot(m, w_ref[:, cols], preferred_element_type=F32)
        o_ref[:, cols] = h_ref[:, cols] + mod_ref[mrow:mrow + 1, cols] * y


def out_proj(h, merged, w, mod, *, layer, seq, mrow):
    T, D = h.shape
    tm = min(512, seq)
    return pl.pallas_call(
        functools.partial(_out_proj_kernel, mrow=mrow),
        grid=(T // tm,),
        in_specs=[
            pl.BlockSpec((tm, D), lambda i: (i, 0)),
            pl.BlockSpec((tm, D), lambda i: (i, 0)),
            pl.BlockSpec((None, D, D), lambda i: (layer, 0, 0), pipeline_mode=pl.Buffered(1)),
            pl.BlockSpec((None, N_MOD, D), lambda i: ((i * tm) // seq, 0, 0)),
        ],
        out_specs=pl.BlockSpec((tm, D), lambda i: (i, 0)),
        out_shape=jax.ShapeDtypeStruct((T, D), F32),
        compiler_params=_cparams("parallel"),
        name="out_proj",
    )(h, merged, w, mod)


def _w_in_prep_kernel(w_ref, ob_ref, of_ref):
    for order, o_ref in ((_ORDER_B, ob_ref), (_ORDER_F, of_ref)):
        off = 0
        for name in order:
            start, width = _REF_COLS[name]
            o_ref[:, off:off + width] = w_ref[:, start:start + width].astype(o_ref.dtype)
            off += width
        if off < o_ref.shape[1]:
            o_ref[:, off:] = jnp.zeros((o_ref.shape[0], o_ref.shape[1] - off), o_ref.dtype)


def permuted_w_in(w_in):
    L, D, C = w_in.shape
    tr = min(256, D)
    return pl.pallas_call(
        _w_in_prep_kernel,
        grid=(L, D // tr),
        in_specs=[pl.BlockSpec((None, tr, C), lambda l, r: (l, r, 0))],
        out_specs=[pl.BlockSpec((None, tr, PROJ_B_COLS), lambda l, r: (l, r, 0)),
                   pl.BlockSpec((None, tr, PROJ_F_COLS), lambda l, r: (l, r, 0))],
        out_shape=[jax.ShapeDtypeStruct((L, D, PROJ_B_COLS), BF16), jax.ShapeDtypeStruct((L, D, PROJ_F_COLS), BF16)],
        compiler_params=_cparams("parallel", "parallel"),
        name="w_in_prep",
    )(w_in)


def kernel(x, c, w_ada, b_ada, norm_g, ffn_w_gate, ffn_w_up, ffn_w_down, w_in, fox_b_forget, attn_sinks,
           gla_w_gate, gla_b_gate, gla_norm_g, ret_gn_w, ret_gn_b, w_branch, w_merge, b_merge, w_out,
           final_norm_g):
    B, S, D = x.shape
    L = w_ada.shape[0]
    T = B * S
    mod_all = ada_modulation(c, w_ada, b_ada)
    log_gamma, cos, sin = _retention_tables(S)
    w_out_bf = w_out.astype(BF16)
    w_in_b, w_in_f = permuted_w_in(w_in)
    h = x.reshape(T, D)
    for l in range(L):
        mod = mod_all[l]
        h = ffn_block(h, mod, norm_g[l, 0], ffn_w_gate, ffn_w_up, ffn_w_down, layer=l, which=0, seq=S,
                      mrow=0, res_scale=0.5)
        proj, projf, u = norm_proj(h, mod, norm_g[l, 1], w_in_b, w_in_f, layer=l, seq=S, mrow=3)
        o_a = swa_attention(proj, attn_sinks[l], batch=B, seq=S)
        fc = fox_forget_cumsum(projf, fox_b_forget[l], batch=B, seq=S)
        o_b = fox_attention(proj, fc, batch=B, seq=S)
        o_c = linear_attention(proj, projf, (gla_w_gate[l], gla_b_gate[l], gla_norm_g[l]), mode="gla",
                               batch=B, seq=S)
        o_d = linear_attention(proj, projf, (log_gamma, cos, sin, ret_gn_w[l], ret_gn_b[l]), mode="ret",
                               batch=B, seq=S)
        merged = merge_branches(u, (o_a, o_b, o_c, o_d), w_merge, w_branch, b_merge, layer=l)
        h = out_proj(h, merged, w_out_bf, mod, layer=l, seq=S, mrow=5)
        h = ffn_block(h, mod, norm_g[l, 2], ffn_w_gate, ffn_w_up, ffn_w_down, layer=l, which=1, seq=S,
                      mrow=6, res_scale=0.5, final_g=final_norm_g if l == L - 1 else None)
    return h.reshape(B, S, D)
```

```python
import functools

import jax
import jax.numpy as jnp
from jax import lax
from jax.experimental import pallas as pl
from jax.experimental.pallas import tpu as pltpu

HEAD_DIM = 64
SWA_HEADS = 8
SWA_KV_HEADS = 2
WINDOW = 128
FOX_HEADS = 8
GLA_HEADS = 4
GLA_DK = 64
GLA_DV = 128
GLA_GATE_RANK = 16
GLA_TAU = 16.0
RET_HEADS = 4
RET_DK = 64
RET_DV = 128
RET_THETA_BASE = 10000.0
CHUNK = 64
BRANCH_WIDTH = 512
N_BRANCH = 4
N_MOD = 9
EPS = 1e-6

BF16 = jnp.bfloat16
F32 = jnp.float32

VMEM_LIMIT_BYTES = 60 * 1024 * 1024

_REF_COLS = {}
_off = 0
for _name, _size in (
        ("a_q", 512), ("a_k", 128), ("a_v", 128),
        ("f_q", 512), ("f_k", 512), ("f_v", 512), ("f_f", 8),
        ("g_q", 256), ("g_k", 256), ("g_v", 512), ("g_lr", 16), ("g_r", 512),
        ("r_q", 256), ("r_k", 256), ("r_v", 512), ("r_g", 512)):
    _REF_COLS[_name] = (_off, _size)
    _off += _size
IN_COLS = _off
for _name in ("a_k", "a_v"):
    for _i in range(SWA_KV_HEADS):
        _REF_COLS[_name + str(_i)] = (_REF_COLS[_name][0] + _i * HEAD_DIM, HEAD_DIM)


def _layout(order):
    col, off = {}, 0
    for name in order:
        col.setdefault(name, off)
        off += _REF_COLS[name][1]
    return col


_ORDER_B = ("a_q", "f_q", "f_k", "f_v", "g_v", "r_v", "g_q", "g_k", "r_q", "r_k",
            "a_k0", "a_k0", "a_k1", "a_k1", "a_v0", "a_v0", "a_v1", "a_v1")
_ORDER_F = ("g_r", "r_g", "f_f", "g_lr")
COLB = _layout(_ORDER_B)
COLF = _layout(_ORDER_F)
PROJ_B_COLS = 4608
PROJ_F_COLS = 1152
SMALL_COL = COLF["f_f"]


def _cparams(*sem):
    return pltpu.CompilerParams(dimension_semantics=sem, vmem_limit_bytes=VMEM_LIMIT_BYTES)


def _tile(n, pref):
    t = (min(pref, n) // 128) * 128
    while t >= 128:
        if n % t == 0:
            return t
        t -= 128
    return n


NORM_ROWS = 64


def _adanorm_into(h_ref, g_ref, mod_ref, mrow, u_ref, zero_ref=None):
    gs = g_ref[...] * (1.0 + mod_ref[mrow + 1:mrow + 2, :])
    shift = mod_ref[mrow:mrow + 1, :]
    rows = min(NORM_ROWS, h_ref.shape[0])

    def body(r, carry):
        r0 = pl.multiple_of(r * rows, rows)
        x = h_ref[pl.ds(r0, rows), :]
        ms = jnp.mean(x * x, axis=-1, keepdims=True)
        u_ref[pl.ds(r0, rows), :] = (x * lax.rsqrt(ms + EPS) * gs + shift).astype(u_ref.dtype)
        if zero_ref is not None:
            zero_ref[pl.ds(r0, rows), :] = jnp.zeros((rows, zero_ref.shape[1]), zero_ref.dtype)
        return carry

    lax.fori_loop(0, h_ref.shape[0] // rows, body, 0)


def _log_sigmoid(x):
    return jnp.minimum(x, 0.0) - jnp.log1p(jnp.exp(-jnp.abs(x)))


def _silu(x):
    return x * jax.nn.sigmoid(x)


ADA_ROWS = 16


def _ada_kernel(c_ref, w_ref, b_ref, o_ref):
    cond = _silu(c_ref[...])
    r = jnp.dot(jnp.concatenate(_split3(cond), axis=0), w_ref[...].astype(BF16), preferred_element_type=F32)
    bp = ADA_ROWS
    o_ref[...] = (r[:bp] + r[bp:2 * bp]) + r[2 * bp:] + b_ref[...]


def ada_modulation(c, w_ada, b_ada):
    L, D, N = w_ada.shape
    B = c.shape[0]
    BP = ADA_ROWS
    cp = jnp.zeros((BP, D), F32).at[:B].set(c)
    tn = _tile(N, 1024)
    out = pl.pallas_call(
        _ada_kernel,
        grid=(L, N // tn),
        in_specs=[
            pl.BlockSpec((BP, D), lambda l, j: (0, 0)),
            pl.BlockSpec((None, D, tn), lambda l, j: (l, 0, j)),
            pl.BlockSpec((None, 1, tn), lambda l, j: (l, 0, j)),
        ],
        out_specs=pl.BlockSpec((None, BP, tn), lambda l, j: (l, 0, j)),
        out_shape=jax.ShapeDtypeStruct((L, BP, N), F32),
        compiler_params=_cparams("parallel", "parallel"),
        name="ada_modulation",
    )(cp, w_ada, b_ada.reshape(L, 1, N))
    return out[:, :B].reshape(L, B, N_MOD, D)


def _ffn_kernel(h_ref, mod_ref, g_ref, wg_ref, wu_ref, wd_ref, *rest, mrow, res_scale, final_norm):
    if final_norm:
        fg_ref, o_ref, u_scr = rest
    else:
        o_ref, u_scr = rest
    f = pl.program_id(1)

    @pl.when(f == 0)
    def _():
        _adanorm_into(h_ref, g_ref, mod_ref, mrow, u_scr, zero_ref=o_ref)

    u = u_scr[...]
    a = jnp.dot(u, wg_ref[...].astype(BF16), preferred_element_type=F32)
    b = jnp.dot(u, wu_ref[...].astype(BF16), preferred_element_type=F32)
    mid = (_silu(a) * b).astype(BF16)
    o_ref[...] += jnp.dot(mid, wd_ref[...].astype(BF16), preferred_element_type=F32)

    @pl.when(f == pl.num_programs(1) - 1)
    def _():
        gate = res_scale * mod_ref[mrow + 2:mrow + 3, :]
        rows = min(NORM_ROWS, h_ref.shape[0])

        def body(r, carry):
            r0 = pl.multiple_of(r * rows, rows)
            y = h_ref[pl.ds(r0, rows), :] + gate * o_ref[pl.ds(r0, rows), :]
            if final_norm:
                y = y * lax.rsqrt(jnp.mean(y * y, axis=-1, keepdims=True) + EPS) * fg_ref[...]
            o_ref[pl.ds(r0, rows), :] = y
            return carry

        lax.fori_loop(0, h_ref.shape[0] // rows, body, 0)


def ffn_block(h, mod, g, wg, wu, wd, *, layer, which, seq, mrow, res_scale, final_g=None):
    T, D = h.shape
    F = wg.shape[-1]
    tm = min(1024, seq)
    tf = _tile(F, 256)
    kern = functools.partial(_ffn_kernel, mrow=mrow, res_scale=res_scale, final_norm=final_g is not None)
    in_specs = [
        pl.BlockSpec((tm, D), lambda i, f: (i, 0)),
        pl.BlockSpec((None, N_MOD, D), lambda i, f: ((i * tm) // seq, 0, 0)),
        pl.BlockSpec((1, D), lambda i, f: (0, 0)),
        pl.BlockSpec((None, None, D, tf), lambda i, f: (layer, which, 0, f)),
        pl.BlockSpec((None, None, D, tf), lambda i, f: (layer, which, 0, f)),
        pl.BlockSpec((None, None, tf, D), lambda i, f: (layer, which, f, 0)),
    ]
    args = [h, mod, g.reshape(1, D), wg, wu, wd]
    if final_g is not None:
        in_specs.append(pl.BlockSpec((1, D), lambda i, f: (0, 0)))
        args.append(final_g.reshape(1, D))
    return pl.pallas_call(
        kern,
        grid=(T // tm, F // tf),
        in_specs=in_specs,
        out_specs=pl.BlockSpec((tm, D), lambda i, f: (i, 0)),
        out_shape=jax.ShapeDtypeStruct((T, D), F32),
        scratch_shapes=[pltpu.VMEM((tm, D), BF16)],
        compiler_params=_cparams("parallel", "arbitrary"),
        name="ffn_block",
    )(*args)


PROJ_COL_CHUNK = 512


def _norm_proj_kernel(h_ref, mod_ref, g_ref, wb_ref, wf_ref, pb_ref, pf_ref, u_ref, *, mrow):
    _adanorm_into(h_ref, g_ref, mod_ref, mrow, u_ref)
    u = u_ref[...]
    nt = (((1,), (1,)), ((), ()))
    for c0 in range(0, pb_ref.shape[1], PROJ_COL_CHUNK):
        cols = slice(c0, c0 + PROJ_COL_CHUNK)
        pb_ref[:, cols] = lax.dot_general(u, wb_ref[cols, :], nt, preferred_element_type=F32).astype(pb_ref.dtype)
    pf_ref[...] = lax.dot_general(u, wf_ref[...], nt, preferred_element_type=F32)


def norm_proj(h, mod, g, wb, wf, *, layer, seq, mrow):
    T, D = h.shape
    NB, NF = wb.shape[1], wf.shape[1]
    tm = min(512, seq)
    kern = functools.partial(_norm_proj_kernel, mrow=mrow)
    resident = pl.Buffered(1)
    return pl.pallas_call(
        kern,
        grid=(T // tm,),
        in_specs=[
            pl.BlockSpec((tm, D), lambda i: (i, 0)),
            pl.BlockSpec((None, N_MOD, D), lambda i: ((i * tm) // seq, 0, 0)),
            pl.BlockSpec((1, D), lambda i: (0, 0)),
            pl.BlockSpec((None, NB, D), lambda i: (layer, 0, 0), pipeline_mode=resident),
            pl.BlockSpec((None, NF, D), lambda i: (layer, 0, 0), pipeline_mode=resident),
        ],
        out_specs=[
            pl.BlockSpec((tm, NB), lambda i: (i, 0)),
            pl.BlockSpec((tm, NF), lambda i: (i, 0)),
            pl.BlockSpec((tm, D), lambda i: (i, 0)),
        ],
        out_shape=[jax.ShapeDtypeStruct((T, NB), BF16), jax.ShapeDtypeStruct((T, NF), F32),
                   jax.ShapeDtypeStruct((T, D), BF16)],
        compiler_params=_cparams("parallel"),
        name="norm_proj",
    )(h, mod, g.reshape(1, D), wb, wf)


def _swa_kernel(sink_ref, q_ref, kv_ref, kvp_ref, o_ref):
    n = pl.program_id(1)
    W = WINDOW
    d = HEAD_DIM
    rowi = lax.broadcasted_iota(jnp.int32, (2 * W, 2 * d), 0)
    lanei = lax.broadcasted_iota(jnp.int32, (2 * W, 2 * d), 1)
    own = (rowi >= W) == (lanei >= d)
    first_half = lax.broadcasted_iota(jnp.int32, (W, 2 * d), 1) < d
    qpos = lax.broadcasted_iota(jnp.int32, (2 * W, 2 * W), 0) % W + W
    kpos = lax.broadcasted_iota(jnp.int32, (2 * W, 2 * W), 1)
    rel = qpos - kpos
    band = (rel >= 0) & (rel < W)
    ones = jnp.ones((2 * W, 2 * d), BF16)
    group = SWA_HEADS // SWA_KV_HEADS
    for blk in range(SWA_BLOCKS_PER_STEP):
        rows = slice(blk * W, (blk + 1) * W)
        q = q_ref[rows, :] * (d ** -0.5)
        prev = kvp_ref[...] if blk == 0 else kv_ref[(blk - 1) * W:blk * W, :]
        kv = jnp.concatenate([prev, kv_ref[rows, :]], axis=0)
        mask = band & ((kpos >= W) | (n > 0)) if blk == 0 else band
        for g in range(SWA_HEADS // 2):
            kvh = (2 * g) // group
            k2 = kv[:, kvh * 2 * d:(kvh + 1) * 2 * d]
            v2 = kv[:, (SWA_KV_HEADS + kvh) * 2 * d:(SWA_KV_HEADS + kvh + 1) * 2 * d]
            v_ext = jnp.concatenate([v2, ones], axis=1)
            qp = q[:, g * 2 * d:(g + 1) * 2 * d]
            qstack = jnp.where(own, jnp.concatenate([qp, qp], axis=0), jnp.zeros((2 * W, 2 * d), BF16))
            logits = lax.dot_general(qstack, k2, (((1,), (1,)), ((), ())), preferred_element_type=F32)
            logits = jnp.where(mask, logits, -jnp.inf)
            sink = jnp.where(rowi >= W, sink_ref[2 * g + 1], sink_ref[2 * g])
            m = jnp.maximum(jnp.max(logits, axis=-1, keepdims=True), sink)
            p = jnp.exp(logits - jnp.concatenate([m, m], axis=1)).astype(BF16)
            r = jnp.dot(p, v_ext, preferred_element_type=F32)
            o2 = r[:, :2 * d] / (r[:, 2 * d:] + jnp.exp(sink - m))
            out = jnp.where(first_half, o2[:W], o2[W:])
            o_ref[rows, g * 2 * d:(g + 1) * 2 * d] = out.astype(o_ref.dtype)


SWA_BLOCKS_PER_STEP = 2


def swa_attention(proj, sinks, *, batch, seq):
    T = proj.shape[0]
    nb = seq // WINDOW
    per = SWA_BLOCKS_PER_STEP
    ns = nb // per
    qblk = COLB["a_q"] // 512
    kvblk = COLB["a_k0"] // 512
    return pl.pallas_call(
        _swa_kernel,
        grid=(batch, ns),
        in_specs=[
            pl.BlockSpec(memory_space=pltpu.SMEM),
            pl.BlockSpec((per * WINDOW, 512), lambda b, n: (b * ns + n, qblk)),
            pl.BlockSpec((per * WINDOW, 512), lambda b, n: (b * ns + n, kvblk)),
            pl.BlockSpec((WINDOW, 512), lambda b, n: (b * nb + jnp.maximum(per * n - 1, 0), kvblk)),
        ],
        out_specs=pl.BlockSpec((per * WINDOW, 512), lambda b, n: (b * ns + n, 0)),
        out_shape=jax.ShapeDtypeStruct((T, BRANCH_WIDTH), BF16),
        compiler_params=_cparams("parallel", "arbitrary"),
        name="swa_attention",
    )(sinks, proj, proj, proj)


def _fox_cum_kernel(x_ref, b_ref, o_ref, *, seq):
    R = 128
    ri = lax.broadcasted_iota(jnp.int32, (R, 3 * R), 0)
    ci = lax.broadcasted_iota(jnp.int32, (R, 3 * R), 1)
    tri3 = jnp.where((ci % R) <= ri, 1.0, 0.0).astype(BF16)

    def body(i, carry):
        r0 = pl.multiple_of(i * R, R)
        ls = _log_sigmoid(x_ref[pl.ds(r0, R), :] + b_ref[...])
        cum = jnp.dot(tri3, jnp.concatenate(_split3(ls), axis=0), preferred_element_type=F32) + carry
        o_ref[pl.ds(r0, R), :] = cum
        return cum[R - 1:R, :]

    lax.fori_loop(0, seq // R, body, jnp.zeros((1, R), F32))


def fox_forget_cumsum(proj, fox_b, *, batch, seq):
    T = proj.shape[0]
    blk = SMALL_COL // 128
    bias = jnp.zeros((1, 128), F32).at[0, :FOX_HEADS].set(fox_b)
    return pl.pallas_call(
        functools.partial(_fox_cum_kernel, seq=seq),
        grid=(batch,),
        in_specs=[
            pl.BlockSpec((seq, 128), lambda b: (b, blk)),
            pl.BlockSpec((1, 128), lambda b: (0, 0)),
        ],
        out_specs=pl.BlockSpec((seq, 128), lambda b: (b, 0)),
        out_shape=jax.ShapeDtypeStruct((T, 128), F32),
        compiler_params=_cparams("parallel"),
        name="fox_forget_cumsum",
    )(proj, bias)


FOX_KEY_ROWS = 512


def _fox_kernel(q_ref, k_ref, v_ref, fc_ref, fcs_ref, o_ref, m_scr, acc_scr, s_scr, kx_scr, *, tq, hp_heads):
    hp = pl.program_id(1)
    qi = pl.program_id(2)
    tk = tq
    d = HEAD_DIM
    w = hp_heads * d
    seq = k_ref.shape[0]
    e_row = lax.broadcasted_iota(jnp.int32, (w, w), 0)
    e_lane = lax.broadcasted_iota(jnp.int32, (w, w), 1)

    @pl.when(qi == 0)
    def _():
        kr = min(FOX_KEY_ROWS, seq)
        place = [sum(jnp.where((e_row == hp * hp_heads + hh) & (e_lane == 3 * hh + j), 1.0, 0.0)
                     for hh in range(hp_heads)).astype(BF16) for j in range(3)]
        for r0 in range(0, seq, kr):
            rows = slice(r0, r0 + kr)
            ext = sum(jnp.dot(piece, place[j], preferred_element_type=F32)
                      for j, piece in enumerate(_split3(-fcs_ref[rows, :])))
            kx_scr[rows, :w] = k_ref[rows, :]
            kx_scr[rows, w:] = ext.astype(BF16)

    lane = lax.broadcasted_iota(jnp.int32, (tq, w), 1)
    q = q_ref[...] * (d ** -0.5)
    zero = jnp.zeros_like(q)
    qs = [jnp.concatenate([jnp.where((lane >= hh * d) & (lane < (hh + 1) * d), q, zero),
                           jnp.where((lane >= 3 * hh) & (lane < 3 * hh + 3), 1.0, 0.0).astype(BF16)], axis=1)
          for hh in range(hp_heads)]
    m_scr[...] = jnp.full(m_scr.shape, -jnp.inf, F32)
    acc_scr[...] = jnp.zeros(acc_scr.shape, F32)
    fc = fc_ref[...]
    fqs = [jnp.broadcast_to(jnp.sum(jnp.where(lane == hp * hp_heads + hh, fc, 0.0), axis=-1, keepdims=True),
                            (tq, w)) for hh in range(hp_heads)]
    ones = jnp.ones((tk, w), BF16)

    def scores(kb, slot):
        k0 = pl.multiple_of(kb * tk, tk)
        kx = kx_scr[pl.ds(k0, tk), :]
        for hh in range(hp_heads):
            s_scr[slot, hh] = lax.dot_general(qs[hh], kx, (((1,), (1,)), ((), ())), preferred_element_type=F32)

    def softmax_pv(kb, slot, masked):
        k0 = pl.multiple_of(kb * tk, tk)
        v = jnp.concatenate([v_ref[pl.ds(k0, tk), :], ones], axis=1)
        for hh in range(hp_heads):
            t = s_scr[slot, hh]
            if masked:
                row = lax.broadcasted_iota(jnp.int32, (tq, tk), 0)
                col = lax.broadcasted_iota(jnp.int32, (tq, tk), 1)
                t = jnp.where(col <= row, t, -jnp.inf)
            m_old = m_scr[hh]
            m_new = jnp.maximum(m_old, fqs[hh] + jnp.max(t, axis=-1, keepdims=True))
            alpha = jnp.exp(m_old - m_new)
            c = fqs[hh] - m_new
            p = jnp.exp(t + jnp.concatenate([c] * (tk // w), axis=1))
            acc_scr[hh] = (jnp.concatenate([alpha, alpha], axis=1) * acc_scr[hh]
                           + jnp.dot(p.astype(BF16), v, preferred_element_type=F32))
            m_scr[hh] = m_new

    scores(0, 0)

    def body(j, carry):
        kb = 2 * j
        softmax_pv(kb, 0, False)
        scores(kb + 1, 1)
        softmax_pv(kb + 1, 1, False)
        scores(kb + 2, 0)
        return carry

    lax.fori_loop(0, qi // 2, body, 0)

    @pl.when(qi % 2 == 1)
    def _():
        softmax_pv(qi - 1, 0, False)
        scores(qi, 1)
        softmax_pv(qi, 1, True)

    @pl.when(qi % 2 == 0)
    def _():
        softmax_pv(qi, 0, True)
    out = None
    for hh in range(hp_heads):
        o = acc_scr[hh, :, :w] / acc_scr[hh, :, w:]
        out = o if out is None else jnp.where(lane >= hh * d, o, out)
    o_ref[...] = out.astype(o_ref.dtype)


def fox_attention(proj, fc, *, batch, seq):
    T = proj.shape[0]
    tq = min(512, seq)
    nq = seq // tq
    hp_heads = 2
    n_hp = FOX_HEADS // hp_heads
    qblk, kblk, vblk = COLB["f_q"] // 128, COLB["f_k"] // 128, COLB["f_v"] // 128
    kern = functools.partial(_fox_kernel, tq=tq, hp_heads=hp_heads)
    return pl.pallas_call(
        kern,
        grid=(batch, n_hp, nq),
        scratch_shapes=[pltpu.VMEM((hp_heads, tq, hp_heads * HEAD_DIM), F32),
                        pltpu.VMEM((hp_heads, tq, 2 * hp_heads * HEAD_DIM), F32),
                        pltpu.VMEM((2, hp_heads, tq, tq), F32),
                        pltpu.VMEM((seq, 2 * hp_heads * HEAD_DIM), BF16)],
        in_specs=[
            pl.BlockSpec((tq, 128), lambda b, hp, qi: (b * nq + qi, qblk + hp)),
            pl.BlockSpec((seq, 128), lambda b, hp, qi: (b, kblk + hp)),
            pl.BlockSpec((seq, 128), lambda b, hp, qi: (b, vblk + hp)),
            pl.BlockSpec((tq, 128), lambda b, hp, qi: (b * nq + qi, 0)),
            pl.BlockSpec((seq, 128), lambda b, hp, qi: (b, 0)),
        ],
        out_specs=pl.BlockSpec((tq, 128), lambda b, hp, qi: (b * nq + qi, hp)),
        out_shape=jax.ShapeDtypeStruct((T, BRANCH_WIDTH), BF16),
        compiler_params=_cparams("parallel", "parallel", "arbitrary"),
        name="fox_attention",
    )(proj, proj, proj, fc, fc)


def _split2(x):
    hi = x.astype(BF16)
    return hi, (x - hi.astype(F32)).astype(BF16)


def _split3(x):
    hi = x.astype(BF16)
    r = x - hi.astype(F32)
    mid = r.astype(BF16)
    return hi, mid, (r - mid.astype(F32)).astype(BF16)


def _rotate_half(x, neg_first_half):
    n = x.shape[-1]
    half = HEAD_DIM // 2
    fwd = pltpu.roll(x, half, 1)
    bwd = pltpu.roll(x, n - half, 1)
    return jnp.where(neg_first_half, -bwd, fwd)


def _linear_attn_kernel(*refs, mode, cb):
    if mode == "gla":
        (q_ref, k_ref, v_ref, gate_ref, small_ref, wg_ref, bg_ref, ng_ref, o_ref, st_ref) = refs
    else:
        (q_ref, k_ref, v_ref, gate_ref, lg_ref, cos_ref, sin_ref, gw_ref, gb_ref, o_ref, st_ref) = refs
    H, DK, DV, C = GLA_HEADS, GLA_DK, GLA_DV, CHUNK
    nch = cb // C
    W2 = 2 * DK
    HW = H * DK
    dn_lanes = (((1,), (1,)), ((), ()))
    dn_rows = (((0,), (0,)), ((), ()))

    @pl.when(pl.program_id(1) == 0)
    def _():
        st_ref[...] = jnp.zeros_like(st_ref)

    q = q_ref[...].astype(F32)
    k = k_ref[...].astype(F32)
    if mode == "gla":
        glr = small_ref[...][:, FOX_HEADS:FOX_HEADS + GLA_GATE_RANK]
        a_hi, a_lo = _split2(glr)
        w_hi, w_lo = _split2(wg_ref[...])
        z = (jnp.dot(a_hi, w_hi, preferred_element_type=F32) + jnp.dot(a_hi, w_lo, preferred_element_type=F32)
             + jnp.dot(a_lo, w_hi, preferred_element_type=F32)) + bg_ref[...]
        ld = _log_sigmoid(z) / GLA_TAU
        ld_w = jnp.concatenate([ld[c * C:(c + 1) * C] for c in range(nch)], axis=1)
        ri = lax.broadcasted_iota(jnp.int32, (C, 3 * C), 0)
        ci = lax.broadcasted_iota(jnp.int32, (C, 3 * C), 1)
        tri3 = jnp.where((ci % C) <= ri, 1.0, 0.0).astype(BF16)
        cum_w = jnp.dot(tri3, jnp.concatenate(_split3(ld_w), axis=0), preferred_element_type=F32)
        cum = jnp.concatenate([cum_w[:, c * HW:(c + 1) * HW] for c in range(nch)], axis=0)
        lasts = [cum_w[C - 1:C, c * HW:(c + 1) * HW] for c in range(nch)]
        last_b = jnp.concatenate([jnp.broadcast_to(l, (C, HW)) for l in lasts], axis=0)
    else:
        lane = lax.broadcasted_iota(jnp.int32, (cb, HW), 1)
        first_half = (lane % HEAD_DIM) < (HEAD_DIM // 2)
        cos = cos_ref[...]
        sin = sin_ref[...]
        q = q * cos + _rotate_half(q, first_half) * sin
        k = k * cos + _rotate_half(k, first_half) * sin
        steps = (lax.broadcasted_iota(jnp.int32, (cb, HW), 0) % C + 1).astype(F32)
        cum = steps * lg_ref[...]
        last_b = float(C) * lg_ref[...]
        lasts = [last_b] * nch
    q_in = (q * (DK ** -0.5) * jnp.exp(cum)).astype(BF16)
    k_in = (k * jnp.exp(-cum)).astype(BF16)
    k_st = (k * jnp.exp(last_b - cum)).astype(BF16)
    decs = [jnp.exp(l) for l in lasts]

    rowi = lax.broadcasted_iota(jnp.int32, (2 * C, W2), 0)
    lanei = lax.broadcasted_iota(jnp.int32, (2 * C, W2), 1)
    own = (rowi >= C) == (lanei >= DK)
    bd_causal = own & ((lanei % C) <= (rowi % C))
    zero_bf = jnp.zeros((2 * C, W2), BF16)

    def pair_ops(c, p):
        rows = slice(c * C, (c + 1) * C)
        ls = slice(p * W2, (p + 1) * W2)
        qp, kp, ksp = q_in[rows, ls], k_in[rows, ls], k_st[rows, ls]
        qstack = jnp.where(own, jnp.concatenate([qp, qp], axis=0), zero_bf)
        ksstack = jnp.where(own, jnp.concatenate([ksp, ksp], axis=0), zero_bf)
        k2 = jnp.concatenate([kp, kp], axis=0)
        vstack = jnp.concatenate([v_ref[rows, (2 * p) * DV:(2 * p + 1) * DV],
                                  v_ref[rows, (2 * p + 1) * DV:(2 * p + 2) * DV]], axis=0)
        sw = lax.dot_general(qstack, k2, dn_lanes, preferred_element_type=F32)
        sc = jnp.where(bd_causal, sw, 0.0).astype(BF16)
        o_intra = jnp.dot(sc, vstack, preferred_element_type=F32)
        upd = lax.dot_general(vstack, ksstack, dn_rows, preferred_element_type=F32)
        return qstack, o_intra, upd

    pre = [[pair_ops(c, p) for p in range(H // 2)] for c in range(nch)]
    state = [st_ref[p] for p in range(H // 2)]
    for c in range(nch):
        rows = slice(c * C, (c + 1) * C)
        for p in range(H // 2):
            qstack, o_intra, upd = pre[c][p]
            o = o_intra + lax.dot_general(qstack, state[p].astype(BF16), dn_lanes, preferred_element_type=F32)
            state[p] = state[p] * decs[c][:, p * W2:(p + 1) * W2] + upd
            for hh in range(2):
                hd = 2 * p + hh
                vs = slice(hd * DV, (hd + 1) * DV)
                oh = o[hh * C:(hh + 1) * C]
                gate = _silu(gate_ref[rows, vs])
                if mode == "gla":
                    y = oh * lax.rsqrt(jnp.mean(oh * oh, axis=-1, keepdims=True) + EPS) * ng_ref[...]
                else:
                    mu = jnp.mean(oh, axis=-1, keepdims=True)
                    var = jnp.mean(jnp.square(oh - mu), axis=-1, keepdims=True)
                    y = (oh - mu) * lax.rsqrt(var + EPS) * gw_ref[:, vs] + gb_ref[:, vs]
                o_ref[rows, vs] = (y * gate).astype(o_ref.dtype)
    for p in range(H // 2):
        st_ref[p] = state[p]


def linear_attention(proj, projf, params, *, mode, batch, seq):
    T = proj.shape[0]
    cb = min(512, seq)
    nc = seq // cb
    pre = "g" if mode == "gla" else "r"
    qblk, kblk = COLB[pre + "_q"] // 256, COLB[pre + "_k"] // 256
    vblk = COLB[pre + "_v"] // 512
    gblk = COLF["g_r" if mode == "gla" else "r_g"] // 512
    row = lambda b, c: b * nc + c
    in_specs = [
        pl.BlockSpec((cb, 256), lambda b, c: (row(b, c), qblk)),
        pl.BlockSpec((cb, 256), lambda b, c: (row(b, c), kblk)),
        pl.BlockSpec((cb, 512), lambda b, c: (row(b, c), vblk)),
        pl.BlockSpec((cb, 512), lambda b, c: (row(b, c), gblk)),
    ]
    args = [proj, proj, proj, projf]
    if mode == "gla":
        wg, bg, ng = params
        in_specs += [
            pl.BlockSpec((cb, 128), lambda b, c: (row(b, c), SMALL_COL // 128)),
            pl.BlockSpec(wg.shape, lambda b, c: (0, 0)),
            pl.BlockSpec((1, bg.shape[-1]), lambda b, c: (0, 0)),
            pl.BlockSpec((1, ng.shape[-1]), lambda b, c: (0, 0)),
        ]
        args += [projf, wg, bg.reshape(1, -1), ng.reshape(1, -1)]
    else:
        lg, cos, sin, gw, gb = params
        in_specs += [
            pl.BlockSpec((1, lg.shape[-1]), lambda b, c: (0, 0)),
            pl.BlockSpec((cb, 256), lambda b, c: (c, 0)),
            pl.BlockSpec((cb, 256), lambda b, c: (c, 0)),
            pl.BlockSpec((1, gw.shape[-1]), lambda b, c: (0, 0)),
            pl.BlockSpec((1, gb.shape[-1]), lambda b, c: (0, 0)),
        ]
        args += [lg.reshape(1, -1), cos, sin, gw.reshape(1, -1), gb.reshape(1, -1)]
    return pl.pallas_call(
        functools.partial(_linear_attn_kernel, mode=mode, cb=cb),
        grid=(batch, nc),
        in_specs=in_specs,
        out_specs=pl.BlockSpec((cb, 512), lambda b, c: (row(b, c), 0)),
        out_shape=jax.ShapeDtypeStruct((T, BRANCH_WIDTH), BF16),
        scratch_shapes=[pltpu.VMEM((GLA_HEADS // 2, GLA_DV, 2 * GLA_DK), F32)],
        compiler_params=_cparams("parallel", "arbitrary"),
        name="linear_attention_" + mode,
    )(*args)


def _retention_tables(seq):
    half = RET_DK // 2
    inv = RET_THETA_BASE ** (-jnp.arange(half, dtype=F32) / half)
    ang = jnp.arange(seq).astype(F32)[:, None] * inv[None, :]
    cos = jnp.tile(jnp.cos(ang), (1, 2 * RET_HEADS))
    sin = jnp.tile(jnp.sin(ang), (1, 2 * RET_HEADS))
    log_gamma = jnp.log1p(-jnp.exp2(-5.0 - jnp.arange(RET_HEADS, dtype=F32)))
    return jnp.repeat(log_gamma, RET_DK), cos, sin


def _merge_kernel(u_ref, oa_ref, ob_ref, oc_ref, od_ref, wm_ref, wb_ref, bm_ref, m_ref):
    u = u_ref[...]
    acc = None
    for i, o_ref in enumerate((oa_ref, ob_ref, oc_ref, od_ref)):
        z = jnp.dot(u, wm_ref[i].astype(BF16), preferred_element_type=F32) + bm_ref[i]
        pr = jnp.dot(o_ref[...], wb_ref[i].astype(BF16), preferred_element_type=F32)
        t = jax.nn.sigmoid(z) * pr
        acc = t if acc is None else acc + t
    m_ref[...] = acc.astype(m_ref.dtype)


def merge_branches(u, branches, wm, wb, bm, *, layer):
    T, D = u.shape
    tm, tn = min(1024, T), min(256, D)
    ospec = pl.BlockSpec((tm, BRANCH_WIDTH), lambda i, j: (i, 0))
    return pl.pallas_call(
        _merge_kernel,
        grid=(T // tm, D // tn),
        in_specs=[
            pl.BlockSpec((tm, D), lambda i, j: (i, 0)),
            ospec, ospec, ospec, ospec,
            pl.BlockSpec((None, N_BRANCH, D, tn), lambda i, j: (layer, 0, 0, j)),
            pl.BlockSpec((None, N_BRANCH, BRANCH_WIDTH, tn), lambda i, j: (layer, 0, 0, j)),
            pl.BlockSpec((None, N_BRANCH, 1, tn), lambda i, j: (layer, 0, 0, j)),
        ],
        out_specs=pl.BlockSpec((tm, tn), lambda i, j: (i, j)),
        out_shape=jax.ShapeDtypeStruct((T, D), BF16),
        compiler_params=_cparams("parallel", "arbitrary"),
        name="merge_branches",
    )(u, *branches, wm, wb, bm.reshape(bm.shape[0], N_BRANCH, 1, D))


def _out_proj_kernel(h_ref, m_ref, w_ref, mod_ref, o_ref, *, mrow):
    m = m_ref[...]
    for c0 in range(0, o_ref.shape[1], PROJ_COL_CHUNK):
        cols = slice(c0, c0 + PROJ_COL_CHUNK)
        y = jnp.dot(m, w_ref[:, cols], preferred_element_type=F32)
        o_ref[:, cols] = h_ref[:, cols] + mod_ref[mrow:mrow + 1, cols] * y


def out_proj(h, merged, w, mod, *, layer, seq, mrow):
    T, D = h.shape
    tm = min(512, seq)
    return pl.pallas_call(
        functools.partial(_out_proj_kernel, mrow=mrow),
        grid=(T // tm,),
        in_specs=[
            pl.BlockSpec((tm, D), lambda i: (i, 0)),
            pl.BlockSpec((tm, D), lambda i: (i, 0)),
            pl.BlockSpec((None, D, D), lambda i: (layer, 0, 0), pipeline_mode=pl.Buffered(1)),
            pl.BlockSpec((None, N_MOD, D), lambda i: ((i * tm) // seq, 0, 0)),
        ],
        out_specs=pl.BlockSpec((tm, D), lambda i: (i, 0)),
        out_shape=jax.ShapeDtypeStruct((T, D), F32),
        compiler_params=_cparams("parallel"),
        name="out_proj",
    )(h, merged, w, mod)


def _w_in_prep_kernel(w_ref, ob_ref, of_ref):
    for order, o_ref in ((_ORDER_B, ob_ref), (_ORDER_F, of_ref)):
        off = 0
        for name in order:
            start, width = _REF_COLS[name]
            o_ref[off:off + width, :] = w_ref[start:start + width, :].astype(o_ref.dtype)
            off += width
        if off < o_ref.shape[0]:
            o_ref[off:, :] = jnp.zeros((o_ref.shape[0] - off, o_ref.shape[1]), o_ref.dtype)


def permuted_w_in(w_in):
    L, D, C = w_in.shape
    td = min(256, D)
    return pl.pallas_call(
        _w_in_prep_kernel,
        grid=(L, D // td),
        in_specs=[pl.BlockSpec((None, C, td), lambda l, r: (l, 0, r))],
        out_specs=[pl.BlockSpec((None, PROJ_B_COLS, td), lambda l, r: (l, 0, r)),
                   pl.BlockSpec((None, PROJ_F_COLS, td), lambda l, r: (l, 0, r))],
        out_shape=[jax.ShapeDtypeStruct((L, PROJ_B_COLS, D), BF16), jax.ShapeDtypeStruct((L, PROJ_F_COLS, D), BF16)],
        compiler_params=_cparams("parallel", "parallel"),
        name="w_in_prep",
    )(jnp.swapaxes(w_in, 1, 2))


def kernel(x, c, w_ada, b_ada, norm_g, ffn_w_gate, ffn_w_up, ffn_w_down, w_in, fox_b_forget, attn_sinks,
           gla_w_gate, gla_b_gate, gla_norm_g, ret_gn_w, ret_gn_b, w_branch, w_merge, b_merge, w_out,
           final_norm_g):
    B, S, D = x.shape
    L = w_ada.shape[0]
    T = B * S
    mod_all = ada_modulation(c, w_ada, b_ada)
    log_gamma, cos, sin = _retention_tables(S)
    w_out_bf = w_out.astype(BF16)
    w_in_b, w_in_f = permuted_w_in(w_in)
    h = x.reshape(T, D)
    for l in range(L):
        mod = mod_all[l]
        h = ffn_block(h, mod, norm_g[l, 0], ffn_w_gate, ffn_w_up, ffn_w_down, layer=l, which=0, seq=S,
                      mrow=0, res_scale=0.5)
        proj, projf, u = norm_proj(h, mod, norm_g[l, 1], w_in_b, w_in_f, layer=l, seq=S, mrow=3)
        o_a = swa_attention(proj, attn_sinks[l], batch=B, seq=S)
        fc = fox_forget_cumsum(projf, fox_b_forget[l], batch=B, seq=S)
        o_b = fox_attention(proj, fc, batch=B, seq=S)
        o_c = linear_attention(proj, projf, (gla_w_gate[l], gla_b_gate[l], gla_norm_g[l]), mode="gla",
                               batch=B, seq=S)
        o_d = linear_attention(proj, projf, (log_gamma, cos, sin, ret_gn_w[l], ret_gn_b[l]), mode="ret",
                               batch=B, seq=S)
        merged = merge_branches(u, (o_a, o_b, o_c, o_d), w_merge, w_branch, b_merge, layer=l)
        h = out_proj(h, merged, w_out_bf, mod, layer=l, seq=S, mrow=5)
        h = ffn_block(h, mod, norm_g[l, 2], ffn_w_gate, ffn_w_up, ffn_w_down, layer=l, which=1, seq=S,
                      mrow=6, res_scale=0.5, final_g=final_norm_g if l == L - 1 else None)
    return h.reshape(B, S, D)
```

```python
import functools

import jax
import jax.numpy as jnp
from jax import lax
from jax.experimental import pallas as pl
from jax.experimental.pallas import tpu as pltpu

HEAD_DIM = 64
SWA_HEADS = 8
SWA_KV_HEADS = 2
WINDOW = 128
FOX_HEADS = 8
GLA_HEADS = 4
GLA_DK = 64
GLA_DV = 128
GLA_GATE_RANK = 16
GLA_TAU = 16.0
RET_HEADS = 4
RET_DK = 64
RET_DV = 128
RET_THETA_BASE = 10000.0
CHUNK = 64
BRANCH_WIDTH = 512
N_BRANCH = 4
N_MOD = 9
EPS = 1e-6

BF16 = jnp.bfloat16
F32 = jnp.float32

VMEM_LIMIT_BYTES = 60 * 1024 * 1024

_REF_COLS = {}
_off = 0
for _name, _size in (
        ("a_q", 512), ("a_k", 128), ("a_v", 128),
        ("f_q", 512), ("f_k", 512), ("f_v", 512), ("f_f", 8),
        ("g_q", 256), ("g_k", 256), ("g_v", 512), ("g_lr", 16), ("g_r", 512),
        ("r_q", 256), ("r_k", 256), ("r_v", 512), ("r_g", 512)):
    _REF_COLS[_name] = (_off, _size)
    _off += _size
IN_COLS = _off
for _name in ("a_k", "a_v"):
    for _i in range(SWA_KV_HEADS):
        _REF_COLS[_name + str(_i)] = (_REF_COLS[_name][0] + _i * HEAD_DIM, HEAD_DIM)


def _layout(order):
    col, off = {}, 0
    for name in order:
        col.setdefault(name, off)
        off += _REF_COLS[name][1]
    return col


_ORDER_B = ("a_q", "f_q", "f_k", "f_v", "g_v", "r_v", "g_q", "g_k", "r_q", "r_k",
            "a_k0", "a_k0", "a_k1", "a_k1", "a_v0", "a_v0", "a_v1", "a_v1")
_ORDER_F = ("g_r", "r_g", "f_f", "g_lr")
COLB = _layout(_ORDER_B)
COLF = _layout(_ORDER_F)
PROJ_B_COLS = 4608
PROJ_F_COLS = 1152
SMALL_COL = COLF["f_f"]


def _cparams(*sem):
    return pltpu.CompilerParams(dimension_semantics=sem, vmem_limit_bytes=VMEM_LIMIT_BYTES)


def _tile(n, pref):
    t = (min(pref, n) // 128) * 128
    while t >= 128:
        if n % t == 0:
            return t
        t -= 128
    return n


NORM_ROWS = 64


def _adanorm_into(h_ref, g_ref, mod_ref, mrow, u_ref, zero_ref=None):
    gs = g_ref[...] * (1.0 + mod_ref[mrow + 1:mrow + 2, :])
    shift = mod_ref[mrow:mrow + 1, :]
    rows = min(NORM_ROWS, h_ref.shape[0])

    def body(r, carry):
        r0 = pl.multiple_of(r * rows, rows)
        x = h_ref[pl.ds(r0, rows), :]
        ms = jnp.mean(x * x, axis=-1, keepdims=True)
        u_ref[pl.ds(r0, rows), :] = (x * lax.rsqrt(ms + EPS) * gs + shift).astype(u_ref.dtype)
        if zero_ref is not None:
            zero_ref[pl.ds(r0, rows), :] = jnp.zeros((rows, zero_ref.shape[1]), zero_ref.dtype)
        return carry

    lax.fori_loop(0, h_ref.shape[0] // rows, body, 0)


def _log_sigmoid(x):
    return jnp.minimum(x, 0.0) - jnp.log1p(jnp.exp(-jnp.abs(x)))


def _silu(x):
    return x * jax.nn.sigmoid(x)


ADA_ROWS = 16


def _ada_kernel(c_ref, w_ref, b_ref, o_ref):
    cond = _silu(c_ref[...])
    r = jnp.dot(jnp.concatenate(_split3(cond), axis=0), w_ref[...].astype(BF16), preferred_element_type=F32)
    bp = ADA_ROWS
    o_ref[...] = (r[:bp] + r[bp:2 * bp]) + r[2 * bp:] + b_ref[...]


def ada_modulation(c, w_ada, b_ada):
    L, D, N = w_ada.shape
    B = c.shape[0]
    BP = ADA_ROWS
    cp = jnp.zeros((BP, D), F32).at[:B].set(c)
    tn = _tile(N, 1024)
    out = pl.pallas_call(
        _ada_kernel,
        grid=(L, N // tn),
        in_specs=[
            pl.BlockSpec((BP, D), lambda l, j: (0, 0)),
            pl.BlockSpec((None, D, tn), lambda l, j: (l, 0, j)),
            pl.BlockSpec((None, 1, tn), lambda l, j: (l, 0, j)),
        ],
        out_specs=pl.BlockSpec((None, BP, tn), lambda l, j: (l, 0, j)),
        out_shape=jax.ShapeDtypeStruct((L, BP, N), F32),
        compiler_params=_cparams("parallel", "parallel"),
        name="ada_modulation",
    )(cp, w_ada, b_ada.reshape(L, 1, N))
    return out[:, :B].reshape(L, B, N_MOD, D)


def _ffn_kernel(h_ref, mod_ref, modn_ref, g_ref, wg_ref, wu_ref, wd_ref, *rest, mrow, res_scale, final_norm,
                norm_rows):
    if final_norm:
        fg_ref, o_ref, u_scr = rest
    else:
        o_ref, u_scr = rest
    i = pl.program_id(0)
    f = pl.program_id(1)
    slot = lax.rem(i, 2)
    tm = h_ref.shape[0]
    copy_rows = min(NORM_ROWS, tm)

    @pl.when((i == 0) & (f == 0))
    def _():
        _adanorm_into(h_ref, g_ref, mod_ref, mrow, u_scr.at[0])

    @pl.when(f == 0)
    def _():
        def body(r, carry):
            r0 = pl.multiple_of(r * copy_rows, copy_rows)
            o_ref[pl.ds(r0, copy_rows), :] = h_ref[pl.ds(r0, copy_rows), :]
            return carry

        lax.fori_loop(0, tm // copy_rows, body, 0)

    u = u_scr[slot]
    a = jnp.dot(u, wg_ref[...].astype(BF16), preferred_element_type=F32)
    b = jnp.dot(u, wu_ref[...].astype(BF16), preferred_element_type=F32)
    mid = (_silu(a) * b).astype(BF16)
    gate = res_scale * mod_ref[mrow + 2:mrow + 3, :]
    o_ref[...] += jnp.dot(mid, (wd_ref[...] * gate).astype(BF16), preferred_element_type=F32)

    c = jnp.minimum(jnp.maximum(f - 1, 0), tm // norm_rows - 1)
    r0 = pl.multiple_of(c * norm_rows, norm_rows)
    x = h_ref[pl.ds(r0, norm_rows), :]
    gs = g_ref[...] * (1.0 + modn_ref[mrow + 1:mrow + 2, :])
    un = x * lax.rsqrt(jnp.mean(x * x, axis=-1, keepdims=True) + EPS) * gs + modn_ref[mrow:mrow + 1, :]
    u_scr[1 - slot, pl.ds(r0, norm_rows), :] = un.astype(BF16)

    if final_norm:
        @pl.when(f == pl.num_programs(1) - 1)
        def _():
            def body(r, carry):
                r0 = pl.multiple_of(r * copy_rows, copy_rows)
                y = o_ref[pl.ds(r0, copy_rows), :]
                y = y * lax.rsqrt(jnp.mean(y * y, axis=-1, keepdims=True) + EPS) * fg_ref[...]
                o_ref[pl.ds(r0, copy_rows), :] = y
                return carry

            lax.fori_loop(0, tm // copy_rows, body, 0)


def _rows_per_step(tm, steps):
    for r in range(8, tm + 1, 8):
        if tm % r == 0 and r * steps >= tm:
            return r
    return tm


def ffn_block(h, mod, g, wg, wu, wd, *, layer, which, seq, mrow, res_scale, final_g=None):
    T, D = h.shape
    F = wg.shape[-1]
    tm = min(1024, seq)
    tf = _tile(F, 256)
    nt, nf = T // tm, F // tf
    assert nf >= 2, "the next tile's norm is spread over the hidden-dimension steps after the first"
    norm_rows = max(_rows_per_step(tm, nf - 1), min(NORM_ROWS, tm))
    kern = functools.partial(_ffn_kernel, mrow=mrow, res_scale=res_scale, final_norm=final_g is not None,
                             norm_rows=norm_rows)

    def nxt(i):
        return jnp.minimum(i + 1, nt - 1)

    in_specs = [
        pl.BlockSpec((tm, D), lambda i, f: (jnp.where(f == 0, i, nxt(i)), 0)),
        pl.BlockSpec((None, N_MOD, D), lambda i, f: ((i * tm) // seq, 0, 0)),
        pl.BlockSpec((None, N_MOD, D), lambda i, f: ((nxt(i) * tm) // seq, 0, 0)),
        pl.BlockSpec((1, D), lambda i, f: (0, 0)),
        pl.BlockSpec((None, None, D, tf), lambda i, f: (layer, which, 0, f)),
        pl.BlockSpec((None, None, D, tf), lambda i, f: (layer, which, 0, f)),
        pl.BlockSpec((None, None, tf, D), lambda i, f: (layer, which, f, 0)),
    ]
    args = [h, mod, mod, g.reshape(1, D), wg, wu, wd]
    if final_g is not None:
        in_specs.append(pl.BlockSpec((1, D), lambda i, f: (0, 0)))
        args.append(final_g.reshape(1, D))
    return pl.pallas_call(
        kern,
        grid=(nt, nf),
        in_specs=in_specs,
        out_specs=pl.BlockSpec((tm, D), lambda i, f: (i, 0)),
        out_shape=jax.ShapeDtypeStruct((T, D), F32),
        scratch_shapes=[pltpu.VMEM((2, tm, D), BF16)],
        compiler_params=_cparams("arbitrary", "arbitrary"),
        name="ffn_block",
    )(*args)


PROJ_COL_CHUNK = 512


def _norm_proj_kernel(h_ref, mod_ref, g_ref, wb_ref, wf_ref, pb_ref, pf_ref, u_ref, *, mrow):
    _adanorm_into(h_ref, g_ref, mod_ref, mrow, u_ref)
    u = u_ref[...]
    nt = (((1,), (1,)), ((), ()))
    for c0 in range(0, pb_ref.shape[1], PROJ_COL_CHUNK):
        cols = slice(c0, c0 + PROJ_COL_CHUNK)
        pb_ref[:, cols] = lax.dot_general(u, wb_ref[cols, :], nt, preferred_element_type=F32).astype(pb_ref.dtype)
    pf_ref[...] = lax.dot_general(u, wf_ref[...], nt, preferred_element_type=F32)


def norm_proj(h, mod, g, wb, wf, *, layer, seq, mrow):
    T, D = h.shape
    NB, NF = wb.shape[1], wf.shape[1]
    tm = min(512, seq)
    kern = functools.partial(_norm_proj_kernel, mrow=mrow)
    resident = pl.Buffered(1)
    return pl.pallas_call(
        kern,
        grid=(T // tm,),
        in_specs=[
            pl.BlockSpec((tm, D), lambda i: (i, 0)),
            pl.BlockSpec((None, N_MOD, D), lambda i: ((i * tm) // seq, 0, 0)),
            pl.BlockSpec((1, D), lambda i: (0, 0)),
            pl.BlockSpec((None, NB, D), lambda i: (layer, 0, 0), pipeline_mode=resident),
            pl.BlockSpec((None, NF, D), lambda i: (layer, 0, 0), pipeline_mode=resident),
        ],
        out_specs=[
            pl.BlockSpec((tm, NB), lambda i: (i, 0)),
            pl.BlockSpec((tm, NF), lambda i: (i, 0)),
            pl.BlockSpec((tm, D), lambda i: (i, 0)),
        ],
        out_shape=[jax.ShapeDtypeStruct((T, NB), BF16), jax.ShapeDtypeStruct((T, NF), F32),
                   jax.ShapeDtypeStruct((T, D), BF16)],
        compiler_params=_cparams("parallel"),
        name="norm_proj",
    )(h, mod, g.reshape(1, D), wb, wf)


def _swa_kernel(sink_ref, q_ref, kv_ref, kvp_ref, o_ref):
    n = pl.program_id(1)
    W = WINDOW
    d = HEAD_DIM
    rowi = lax.broadcasted_iota(jnp.int32, (2 * W, 2 * d), 0)
    lanei = lax.broadcasted_iota(jnp.int32, (2 * W, 2 * d), 1)
    own = (rowi >= W) == (lanei >= d)
    first_half = lax.broadcasted_iota(jnp.int32, (W, 2 * d), 1) < d
    qpos = lax.broadcasted_iota(jnp.int32, (2 * W, 2 * W), 0) % W + W
    kpos = lax.broadcasted_iota(jnp.int32, (2 * W, 2 * W), 1)
    rel = qpos - kpos
    band = (rel >= 0) & (rel < W)
    ones = jnp.ones((2 * W, 2 * d), BF16)
    group = SWA_HEADS // SWA_KV_HEADS
    for blk in range(SWA_BLOCKS_PER_STEP):
        rows = slice(blk * W, (blk + 1) * W)
        q = q_ref[rows, :] * (d ** -0.5)
        prev = kvp_ref[...] if blk == 0 else kv_ref[(blk - 1) * W:blk * W, :]
        kv = jnp.concatenate([prev, kv_ref[rows, :]], axis=0)
        mask = band & ((kpos >= W) | (n > 0)) if blk == 0 else band
        for g in range(SWA_HEADS // 2):
            kvh = (2 * g) // group
            k2 = kv[:, kvh * 2 * d:(kvh + 1) * 2 * d]
            v2 = kv[:, (SWA_KV_HEADS + kvh) * 2 * d:(SWA_KV_HEADS + kvh + 1) * 2 * d]
            v_ext = jnp.concatenate([v2, ones], axis=1)
            qp = q[:, g * 2 * d:(g + 1) * 2 * d]
            qstack = jnp.where(own, jnp.concatenate([qp, qp], axis=0), jnp.zeros((2 * W, 2 * d), BF16))
            logits = lax.dot_general(qstack, k2, (((1,), (1,)), ((), ())), preferred_element_type=F32)
            logits = jnp.where(mask, logits, -jnp.inf)
            sink = jnp.where(rowi >= W, sink_ref[2 * g + 1], sink_ref[2 * g])
            m = jnp.maximum(jnp.max(logits, axis=-1, keepdims=True), sink)
            p = jnp.exp(logits - jnp.concatenate([m, m], axis=1)).astype(BF16)
            r = jnp.dot(p, v_ext, preferred_element_type=F32)
            o2 = r[:, :2 * d] / (r[:, 2 * d:] + jnp.exp(sink - m))
            out = jnp.where(first_half, o2[:W], o2[W:])
            o_ref[rows, g * 2 * d:(g + 1) * 2 * d] = out.astype(o_ref.dtype)


SWA_BLOCKS_PER_STEP = 2


def swa_attention(proj, sinks, *, batch, seq):
    T = proj.shape[0]
    nb = seq // WINDOW
    per = SWA_BLOCKS_PER_STEP
    ns = nb // per
    qblk = COLB["a_q"] // 512
    kvblk = COLB["a_k0"] // 512
    return pl.pallas_call(
        _swa_kernel,
        grid=(batch, ns),
        in_specs=[
            pl.BlockSpec(memory_space=pltpu.SMEM),
            pl.BlockSpec((per * WINDOW, 512), lambda b, n: (b * ns + n, qblk)),
            pl.BlockSpec((per * WINDOW, 512), lambda b, n: (b * ns + n, kvblk)),
            pl.BlockSpec((WINDOW, 512), lambda b, n: (b * nb + jnp.maximum(per * n - 1, 0), kvblk)),
        ],
        out_specs=pl.BlockSpec((per * WINDOW, 512), lambda b, n: (b * ns + n, 0)),
        out_shape=jax.ShapeDtypeStruct((T, BRANCH_WIDTH), BF16),
        compiler_params=_cparams("parallel", "arbitrary"),
        name="swa_attention",
    )(sinks, proj, proj, proj)


def _fox_cum_kernel(x_ref, b_ref, o_ref, *, seq):
    R = 128
    ri = lax.broadcasted_iota(jnp.int32, (R, 3 * R), 0)
    ci = lax.broadcasted_iota(jnp.int32, (R, 3 * R), 1)
    tri3 = jnp.where((ci % R) <= ri, 1.0, 0.0).astype(BF16)

    def body(i, carry):
        r0 = pl.multiple_of(i * R, R)
        ls = _log_sigmoid(x_ref[pl.ds(r0, R), :] + b_ref[...])
        cum = jnp.dot(tri3, jnp.concatenate(_split3(ls), axis=0), preferred_element_type=F32) + carry
        o_ref[pl.ds(r0, R), :] = cum
        return cum[R - 1:R, :]

    lax.fori_loop(0, seq // R, body, jnp.zeros((1, R), F32))


def fox_forget_cumsum(proj, fox_b, *, batch, seq):
    T = proj.shape[0]
    blk = SMALL_COL // 128
    bias = jnp.zeros((1, 128), F32).at[0, :FOX_HEADS].set(fox_b)
    return pl.pallas_call(
        functools.partial(_fox_cum_kernel, seq=seq),
        grid=(batch,),
        in_specs=[
            pl.BlockSpec((seq, 128), lambda b: (b, blk)),
            pl.BlockSpec((1, 128), lambda b: (0, 0)),
        ],
        out_specs=pl.BlockSpec((seq, 128), lambda b: (b, 0)),
        out_shape=jax.ShapeDtypeStruct((T, 128), F32),
        compiler_params=_cparams("parallel"),
        name="fox_forget_cumsum",
    )(proj, bias)


FOX_KEY_ROWS = 512


def _fox_kernel(q_ref, k_ref, v_ref, fc_ref, fcs_ref, o_ref, m_scr, acc_scr, s_scr, kx_scr, *, tq, hp_heads):
    hp = pl.program_id(1)
    qi = pl.program_id(2)
    tk = tq
    d = HEAD_DIM
    w = hp_heads * d
    seq = k_ref.shape[0]
    e_row = lax.broadcasted_iota(jnp.int32, (w, w), 0)
    e_lane = lax.broadcasted_iota(jnp.int32, (w, w), 1)

    @pl.when(qi == 0)
    def _():
        kr = min(FOX_KEY_ROWS, seq)
        place = [sum(jnp.where((e_row == hp * hp_heads + hh) & (e_lane == 3 * hh + j), 1.0, 0.0)
                     for hh in range(hp_heads)).astype(BF16) for j in range(3)]
        for r0 in range(0, seq, kr):
            rows = slice(r0, r0 + kr)
            ext = sum(jnp.dot(piece, place[j], preferred_element_type=F32)
                      for j, piece in enumerate(_split3(-fcs_ref[rows, :])))
            kx_scr[rows, :w] = k_ref[rows, :]
            kx_scr[rows, w:] = ext.astype(BF16)

    lane = lax.broadcasted_iota(jnp.int32, (tq, w), 1)
    q = q_ref[...] * (d ** -0.5)
    zero = jnp.zeros_like(q)
    qs = [jnp.concatenate([jnp.where((lane >= hh * d) & (lane < (hh + 1) * d), q, zero),
                           jnp.where((lane >= 3 * hh) & (lane < 3 * hh + 3), 1.0, 0.0).astype(BF16)], axis=1)
          for hh in range(hp_heads)]
    m_scr[...] = jnp.full(m_scr.shape, -jnp.inf, F32)
    acc_scr[...] = jnp.zeros(acc_scr.shape, F32)
    fc = fc_ref[...]
    fqs = [jnp.broadcast_to(jnp.sum(jnp.where(lane == hp * hp_heads + hh, fc, 0.0), axis=-1, keepdims=True),
                            (tq, w)) for hh in range(hp_heads)]
    ones = jnp.ones((tk, w), BF16)

    def scores(kb, slot):
        k0 = pl.multiple_of(kb * tk, tk)
        kx = kx_scr[pl.ds(k0, tk), :]
        for hh in range(hp_heads):
            s_scr[slot, hh] = lax.dot_general(qs[hh], kx, (((1,), (1,)), ((), ())), preferred_element_type=F32)

    def softmax_pv(kb, slot, masked):
        k0 = pl.multiple_of(kb * tk, tk)
        v = jnp.concatenate([v_ref[pl.ds(k0, tk), :], ones], axis=1)
        for hh in range(hp_heads):
            t = s_scr[slot, hh]
            if masked:
                row = lax.broadcasted_iota(jnp.int32, (tq, tk), 0)
                col = lax.broadcasted_iota(jnp.int32, (tq, tk), 1)
                t = jnp.where(col <= row, t, -jnp.inf)
            m_old = m_scr[hh]
            m_new = jnp.maximum(m_old, fqs[hh] + jnp.max(t, axis=-1, keepdims=True))
            alpha = jnp.exp(m_old - m_new)
            c = fqs[hh] - m_new
            p = jnp.exp(t + jnp.concatenate([c] * (tk // w), axis=1))
            acc_scr[hh] = (jnp.concatenate([alpha, alpha], axis=1) * acc_scr[hh]
                           + jnp.dot(p.astype(BF16), v, preferred_element_type=F32))
            m_scr[hh] = m_new

    scores(0, 0)

    def body(j, carry):
        kb = 2 * j
        softmax_pv(kb, 0, False)
        scores(kb + 1, 1)
        softmax_pv(kb + 1, 1, False)
        scores(kb + 2, 0)
        return carry

    lax.fori_loop(0, qi // 2, body, 0)

    @pl.when(qi % 2 == 1)
    def _():
        softmax_pv(qi - 1, 0, False)
        scores(qi, 1)
        softmax_pv(qi, 1, True)

    @pl.when(qi % 2 == 0)
    def _():
        softmax_pv(qi, 0, True)
    out = None
    for hh in range(hp_heads):
        o = acc_scr[hh, :, :w] / acc_scr[hh, :, w:]
        out = o if out is None else jnp.where(lane >= hh * d, o, out)
    o_ref[...] = out.astype(o_ref.dtype)


def fox_attention(proj, fc, *, batch, seq):
    T = proj.shape[0]
    tq = min(512, seq)
    nq = seq // tq
    hp_heads = 2
    n_hp = FOX_HEADS // hp_heads
    qblk, kblk, vblk = COLB["f_q"] // 128, COLB["f_k"] // 128, COLB["f_v"] // 128
    kern = functools.partial(_fox_kernel, tq=tq, hp_heads=hp_heads)
    return pl.pallas_call(
        kern,
        grid=(batch, n_hp, nq),
        scratch_shapes=[pltpu.VMEM((hp_heads, tq, hp_heads * HEAD_DIM), F32),
                        pltpu.VMEM((hp_heads, tq, 2 * hp_heads * HEAD_DIM), F32),
                        pltpu.VMEM((2, hp_heads, tq, tq), F32),
                        pltpu.VMEM((seq, 2 * hp_heads * HEAD_DIM), BF16)],
        in_specs=[
            pl.BlockSpec((tq, 128), lambda b, hp, qi: (b * nq + qi, qblk + hp)),
            pl.BlockSpec((seq, 128), lambda b, hp, qi: (b, kblk + hp)),
            pl.BlockSpec((seq, 128), lambda b, hp, qi: (b, vblk + hp)),
            pl.BlockSpec((tq, 128), lambda b, hp, qi: (b * nq + qi, 0)),
            pl.BlockSpec((seq, 128), lambda b, hp, qi: (b, 0)),
        ],
        out_specs=pl.BlockSpec((tq, 128), lambda b, hp, qi: (b * nq + qi, hp)),
        out_shape=jax.ShapeDtypeStruct((T, BRANCH_WIDTH), BF16),
        compiler_params=_cparams("parallel", "parallel", "arbitrary"),
        name="fox_attention",
    )(proj, proj, proj, fc, fc)


def _split2(x):
    hi = x.astype(BF16)
    return hi, (x - hi.astype(F32)).astype(BF16)


def _split3(x):
    hi = x.astype(BF16)
    r = x - hi.astype(F32)
    mid = r.astype(BF16)
    return hi, mid, (r - mid.astype(F32)).astype(BF16)


def _rotate_half(x, neg_first_half):
    n = x.shape[-1]
    half = HEAD_DIM // 2
    fwd = pltpu.roll(x, half, 1)
    bwd = pltpu.roll(x, n - half, 1)
    return jnp.where(neg_first_half, -bwd, fwd)


def _linear_attn_kernel(*refs, mode, cb):
    if mode == "gla":
        (q_ref, k_ref, v_ref, gate_ref, small_ref, wg_ref, bg_ref, ng_ref, o_ref, st_ref) = refs
    else:
        (q_ref, k_ref, v_ref, gate_ref, lg_ref, cos_ref, sin_ref, gw_ref, gb_ref, o_ref, st_ref) = refs
    H, DK, DV, C = GLA_HEADS, GLA_DK, GLA_DV, CHUNK
    nch = cb // C
    W2 = 2 * DK
    HW = H * DK
    dn_lanes = (((1,), (1,)), ((), ()))
    dn_rows = (((0,), (0,)), ((), ()))

    @pl.when(pl.program_id(1) == 0)
    def _():
        st_ref[...] = jnp.zeros_like(st_ref)

    q = q_ref[...].astype(F32)
    k = k_ref[...].astype(F32)
    if mode == "gla":
        glr = small_ref[...][:, FOX_HEADS:FOX_HEADS + GLA_GATE_RANK]
        a_hi, a_lo = _split2(glr)
        w_hi, w_lo = _split2(wg_ref[...])
        z = (jnp.dot(a_hi, w_hi, preferred_element_type=F32) + jnp.dot(a_hi, w_lo, preferred_element_type=F32)
             + jnp.dot(a_lo, w_hi, preferred_element_type=F32)) + bg_ref[...]
        ld = _log_sigmoid(z) / GLA_TAU
        ld_w = jnp.concatenate([ld[c * C:(c + 1) * C] for c in range(nch)], axis=1)
        ri = lax.broadcasted_iota(jnp.int32, (C, 3 * C), 0)
        ci = lax.broadcasted_iota(jnp.int32, (C, 3 * C), 1)
        tri3 = jnp.where((ci % C) <= ri, 1.0, 0.0).astype(BF16)
        cum_w = jnp.dot(tri3, jnp.concatenate(_split3(ld_w), axis=0), preferred_element_type=F32)
        cum = jnp.concatenate([cum_w[:, c * HW:(c + 1) * HW] for c in range(nch)], axis=0)
        lasts = [cum_w[C - 1:C, c * HW:(c + 1) * HW] for c in range(nch)]
        last_b = jnp.concatenate([jnp.broadcast_to(l, (C, HW)) for l in lasts], axis=0)
    else:
        lane = lax.broadcasted_iota(jnp.int32, (cb, HW), 1)
        first_half = (lane % HEAD_DIM) < (HEAD_DIM // 2)
        cos = cos_ref[...]
        sin = sin_ref[...]
        q = q * cos + _rotate_half(q, first_half) * sin
        k = k * cos + _rotate_half(k, first_half) * sin
        steps = (lax.broadcasted_iota(jnp.int32, (cb, HW), 0) % C + 1).astype(F32)
        cum = steps * lg_ref[...]
        last_b = float(C) * lg_ref[...]
        lasts = [last_b] * nch
    q_in = (q * (DK ** -0.5) * jnp.exp(cum)).astype(BF16)
    k_in = (k * jnp.exp(-cum)).astype(BF16)
    k_st = (k * jnp.exp(last_b - cum)).astype(BF16)
    decs = [jnp.exp(l) for l in lasts]

    rowi = lax.broadcasted_iota(jnp.int32, (2 * C, W2), 0)
    lanei = lax.broadcasted_iota(jnp.int32, (2 * C, W2), 1)
    own = (rowi >= C) == (lanei >= DK)
    bd_causal = own & ((lanei % C) <= (rowi % C))
    zero_bf = jnp.zeros((2 * C, W2), BF16)

    def pair_ops(c, p):
        rows = slice(c * C, (c + 1) * C)
        ls = slice(p * W2, (p + 1) * W2)
        qp, kp, ksp = q_in[rows, ls], k_in[rows, ls], k_st[rows, ls]
        qstack = jnp.where(own, jnp.concatenate([qp, qp], axis=0), zero_bf)
        ksstack = jnp.where(own, jnp.concatenate([ksp, ksp], axis=0), zero_bf)
        k2 = jnp.concatenate([kp, kp], axis=0)
        vstack = jnp.concatenate([v_ref[rows, (2 * p) * DV:(2 * p + 1) * DV],
                                  v_ref[rows, (2 * p + 1) * DV:(2 * p + 2) * DV]], axis=0)
        sw = lax.dot_general(qstack, k2, dn_lanes, preferred_element_type=F32)
        sc = jnp.where(bd_causal, sw, 0.0).astype(BF16)
        o_intra = jnp.dot(sc, vstack, preferred_element_type=F32)
        upd = lax.dot_general(vstack, ksstack, dn_rows, preferred_element_type=F32)
        return qstack, o_intra, upd

    pre = [[pair_ops(c, p) for p in range(H // 2)] for c in range(nch)]
    state = [st_ref[p] for p in range(H // 2)]
    for c in range(nch):
        rows = slice(c * C, (c + 1) * C)
        for p in range(H // 2):
            qstack, o_intra, upd = pre[c][p]
            o = o_intra + lax.dot_general(qstack, state[p].astype(BF16), dn_lanes, preferred_element_type=F32)
            state[p] = state[p] * decs[c][:, p * W2:(p + 1) * W2] + upd
            for hh in range(2):
                hd = 2 * p + hh
                vs = slice(hd * DV, (hd + 1) * DV)
                oh = o[hh * C:(hh + 1) * C]
                gate = _silu(gate_ref[rows, vs])
                if mode == "gla":
                    y = oh * lax.rsqrt(jnp.mean(oh * oh, axis=-1, keepdims=True) + EPS) * ng_ref[...]
                else:
                    mu = jnp.mean(oh, axis=-1, keepdims=True)
                    var = jnp.mean(jnp.square(oh - mu), axis=-1, keepdims=True)
                    y = (oh - mu) * lax.rsqrt(var + EPS) * gw_ref[:, vs] + gb_ref[:, vs]
                o_ref[rows, vs] = (y * gate).astype(o_ref.dtype)
    for p in range(H // 2):
        st_ref[p] = state[p]


def linear_attention(proj, projf, params, *, mode, batch, seq):
    T = proj.shape[0]
    cb = min(512, seq)
    nc = seq // cb
    pre = "g" if mode == "gla" else "r"
    qblk, kblk = COLB[pre + "_q"] // 256, COLB[pre + "_k"] // 256
    vblk = COLB[pre + "_v"] // 512
    gblk = COLF["g_r" if mode == "gla" else "r_g"] // 512
    row = lambda b, c: b * nc + c
    in_specs = [
        pl.BlockSpec((cb, 256), lambda b, c: (row(b, c), qblk)),
        pl.BlockSpec((cb, 256), lambda b, c: (row(b, c), kblk)),
        pl.BlockSpec((cb, 512), lambda b, c: (row(b, c), vblk)),
        pl.BlockSpec((cb, 512), lambda b, c: (row(b, c), gblk)),
    ]
    args = [proj, proj, proj, projf]
    if mode == "gla":
        wg, bg, ng = params
        in_specs += [
            pl.BlockSpec((cb, 128), lambda b, c: (row(b, c), SMALL_COL // 128)),
            pl.BlockSpec(wg.shape, lambda b, c: (0, 0)),
            pl.BlockSpec((1, bg.shape[-1]), lambda b, c: (0, 0)),
            pl.BlockSpec((1, ng.shape[-1]), lambda b, c: (0, 0)),
        ]
        args += [projf, wg, bg.reshape(1, -1), ng.reshape(1, -1)]
    else:
        lg, cos, sin, gw, gb = params
        in_specs += [
            pl.BlockSpec((1, lg.shape[-1]), lambda b, c: (0, 0)),
            pl.BlockSpec((cb, 256), lambda b, c: (c, 0)),
            pl.BlockSpec((cb, 256), lambda b, c: (c, 0)),
            pl.BlockSpec((1, gw.shape[-1]), lambda b, c: (0, 0)),
            pl.BlockSpec((1, gb.shape[-1]), lambda b, c: (0, 0)),
        ]
        args += [lg.reshape(1, -1), cos, sin, gw.reshape(1, -1), gb.reshape(1, -1)]
    return pl.pallas_call(
        functools.partial(_linear_attn_kernel, mode=mode, cb=cb),
        grid=(batch, nc),
        in_specs=in_specs,
        out_specs=pl.BlockSpec((cb, 512), lambda b, c: (row(b, c), 0)),
        out_shape=jax.ShapeDtypeStruct((T, BRANCH_WIDTH), BF16),
        scratch_shapes=[pltpu.VMEM((GLA_HEADS // 2, GLA_DV, 2 * GLA_DK), F32)],
        compiler_params=_cparams("parallel", "arbitrary"),
        name="linear_attention_" + mode,
    )(*args)


def _retention_tables(seq):
    half = RET_DK // 2
    inv = RET_THETA_BASE ** (-jnp.arange(half, dtype=F32) / half)
    ang = jnp.arange(seq).astype(F32)[:, None] * inv[None, :]
    cos = jnp.tile(jnp.cos(ang), (1, 2 * RET_HEADS))
    sin = jnp.tile(jnp.sin(ang), (1, 2 * RET_HEADS))
    log_gamma = jnp.log1p(-jnp.exp2(-5.0 - jnp.arange(RET_HEADS, dtype=F32)))
    return jnp.repeat(log_gamma, RET_DK), cos, sin


def _merge_kernel(u_ref, oa_ref, ob_ref, oc_ref, od_ref, wm_ref, wb_ref, bm_ref, m_ref):
    u = u_ref[...]
    acc = None
    for i, o_ref in enumerate((oa_ref, ob_ref, oc_ref, od_ref)):
        z = jnp.dot(u, wm_ref[i].astype(BF16), preferred_element_type=F32) + bm_ref[i]
        pr = jnp.dot(o_ref[...], wb_ref[i].astype(BF16), preferred_element_type=F32)
        t = jax.nn.sigmoid(z) * pr
        acc = t if acc is None else acc + t
    m_ref[...] = acc.astype(m_ref.dtype)


def merge_branches(u, branches, wm, wb, bm, *, layer):
    T, D = u.shape
    tm, tn = min(1024, T), min(256, D)
    ospec = pl.BlockSpec((tm, BRANCH_WIDTH), lambda i, j: (i, 0))
    return pl.pallas_call(
        _merge_kernel,
        grid=(T // tm, D // tn),
        in_specs=[
            pl.BlockSpec((tm, D), lambda i, j: (i, 0)),
            ospec, ospec, ospec, ospec,
            pl.BlockSpec((None, N_BRANCH, D, tn), lambda i, j: (layer, 0, 0, j)),
            pl.BlockSpec((None, N_BRANCH, BRANCH_WIDTH, tn), lambda i, j: (layer, 0, 0, j)),
            pl.BlockSpec((None, N_BRANCH, 1, tn), lambda i, j: (layer, 0, 0, j)),
        ],
        out_specs=pl.BlockSpec((tm, tn), lambda i, j: (i, j)),
        out_shape=jax.ShapeDtypeStruct((T, D), BF16),
        compiler_params=_cparams("parallel", "arbitrary"),
        name="merge_branches",
    )(u, *branches, wm, wb, bm.reshape(bm.shape[0], N_BRANCH, 1, D))


def _out_proj_kernel(h_ref, m_ref, w_ref, mod_ref, o_ref, *, mrow):
    m = m_ref[...]
    for c0 in range(0, o_ref.shape[1], PROJ_COL_CHUNK):
        cols = slice(c0, c0 + PROJ_COL_CHUNK)
        y = jnp.dot(m, w_ref[:, cols], preferred_element_type=F32)
        o_ref[:, cols] = h_ref[:, cols] + mod_ref[mrow:mrow + 1, cols] * y


def out_proj(h, merged, w, mod, *, layer, seq, mrow):
    T, D = h.shape
    tm = min(512, seq)
    return pl.pallas_call(
        functools.partial(_out_proj_kernel, mrow=mrow),
        grid=(T // tm,),
        in_specs=[
            pl.BlockSpec((tm, D), lambda i: (i, 0)),
            pl.BlockSpec((tm, D), lambda i: (i, 0)),
            pl.BlockSpec((None, D, D), lambda i: (layer, 0, 0), pipeline_mode=pl.Buffered(1)),
            pl.BlockSpec((None, N_MOD, D), lambda i: ((i * tm) // seq, 0, 0)),
        ],
        out_specs=pl.BlockSpec((tm, D), lambda i: (i, 0)),
        out_shape=jax.ShapeDtypeStruct((T, D), F32),
        compiler_params=_cparams("parallel"),
        name="out_proj",
    )(h, merged, w, mod)


def _w_in_prep_kernel(w_ref, ob_ref, of_ref):
    for order, o_ref in ((_ORDER_B, ob_ref), (_ORDER_F, of_ref)):
        off = 0
        for name in order:
            start, width = _REF_COLS[name]
            o_ref[off:off + width, :] = w_ref[start:start + width, :].astype(o_ref.dtype)
            off += width
        if off < o_ref.shape[0]:
            o_ref[off:, :] = jnp.zeros((o_ref.shape[0] - off, o_ref.shape[1]), o_ref.dtype)


def permuted_w_in(w_in):
    L, D, C = w_in.shape
    td = min(256, D)
    return pl.pallas_call(
        _w_in_prep_kernel,
        grid=(L, D // td),
        in_specs=[pl.BlockSpec((None, C, td), lambda l, r: (l, 0, r))],
        out_specs=[pl.BlockSpec((None, PROJ_B_COLS, td), lambda l, r: (l, 0, r)),
                   pl.BlockSpec((None, PROJ_F_COLS, td), lambda l, r: (l, 0, r))],
        out_shape=[jax.ShapeDtypeStruct((L, PROJ_B_COLS, D), BF16), jax.ShapeDtypeStruct((L, PROJ_F_COLS, D), BF16)],
        compiler_params=_cparams("parallel", "parallel"),
        name="w_in_prep",
    )(jnp.swapaxes(w_in, 1, 2))


def kernel(x, c, w_ada, b_ada, norm_g, ffn_w_gate, ffn_w_up, ffn_w_down, w_in, fox_b_forget, attn_sinks,
           gla_w_gate, gla_b_gate, gla_norm_g, ret_gn_w, ret_gn_b, w_branch, w_merge, b_merge, w_out,
           final_norm_g):
    B, S, D = x.shape
    L = w_ada.shape[0]
    T = B * S
    mod_all = ada_modulation(c, w_ada, b_ada)
    log_gamma, cos, sin = _retention_tables(S)
    w_out_bf = w_out.astype(BF16)
    w_in_b, w_in_f = permuted_w_in(w_in)
    h = x.reshape(T, D)
    for l in range(L):
        mod = mod_all[l]
        h = ffn_block(h, mod, norm_g[l, 0], ffn_w_gate, ffn_w_up, ffn_w_down, layer=l, which=0, seq=S,
                      mrow=0, res_scale=0.5)
        proj, projf, u = norm_proj(h, mod, norm_g[l, 1], w_in_b, w_in_f, layer=l, seq=S, mrow=3)
        o_a = swa_attention(proj, attn_sinks[l], batch=B, seq=S)
        fc = fox_forget_cumsum(projf, fox_b_forget[l], batch=B, seq=S)
        o_b = fox_attention(proj, fc, batch=B, seq=S)
        o_c = linear_attention(proj, projf, (gla_w_gate[l], gla_b_gate[l], gla_norm_g[l]), mode="gla",
                               batch=B, seq=S)
        o_d = linear_attention(proj, projf, (log_gamma, cos, sin, ret_gn_w[l], ret_gn_b[l]), mode="ret",
                               batch=B, seq=S)
        merged = merge_branches(u, (o_a, o_b, o_c, o_d), w_merge, w_branch, b_merge, layer=l)
        h = out_proj(h, merged, w_out_bf, mod, layer=l, seq=S, mrow=5)
        h = ffn_block(h, mod, norm_g[l, 2], ffn_w_gate, ffn_w_up, ffn_w_down, layer=l, which=1, seq=S,
                      mrow=6, res_scale=0.5, final_g=final_norm_g if l == L - 1 else None)
    return h.reshape(B, S, D)
```

```python
import functools

import jax
import jax.numpy as jnp
from jax import lax
from jax.experimental import pallas as pl
from jax.experimental.pallas import tpu as pltpu

HEAD_DIM = 64
SWA_HEADS = 8
SWA_KV_HEADS = 2
WINDOW = 128
FOX_HEADS = 8
GLA_HEADS = 4
GLA_DK = 64
GLA_DV = 128
GLA_GATE_RANK = 16
GLA_TAU = 16.0
RET_HEADS = 4
RET_DK = 64
RET_DV = 128
RET_THETA_BASE = 10000.0
CHUNK = 64
BRANCH_WIDTH = 512
N_BRANCH = 4
N_MOD = 9
EPS = 1e-6

BF16 = jnp.bfloat16
F32 = jnp.float32

VMEM_LIMIT_BYTES = 60 * 1024 * 1024

_REF_COLS = {}
_off = 0
for _name, _size in (
        ("a_q", 512), ("a_k", 128), ("a_v", 128),
        ("f_q", 512), ("f_k", 512), ("f_v", 512), ("f_f", 8),
        ("g_q", 256), ("g_k", 256), ("g_v", 512), ("g_lr", 16), ("g_r", 512),
        ("r_q", 256), ("r_k", 256), ("r_v", 512), ("r_g", 512)):
    _REF_COLS[_name] = (_off, _size)
    _off += _size
IN_COLS = _off
for _name in ("a_k", "a_v"):
    for _i in range(SWA_KV_HEADS):
        _REF_COLS[_name + str(_i)] = (_REF_COLS[_name][0] + _i * HEAD_DIM, HEAD_DIM)


def _layout(order):
    col, off = {}, 0
    for name in order:
        col.setdefault(name, off)
        off += _REF_COLS[name][1]
    return col


_ORDER_B = ("a_q", "f_q", "f_k", "f_v", "g_v", "r_v", "g_q", "g_k", "r_q", "r_k",
            "a_k0", "a_k0", "a_k1", "a_k1", "a_v0", "a_v0", "a_v1", "a_v1")
_ORDER_F = ("g_r", "r_g", "f_f", "g_lr")
COLB = _layout(_ORDER_B)
COLF = _layout(_ORDER_F)
PROJ_B_COLS = 4608
PROJ_F_COLS = 1152
SMALL_COL = COLF["f_f"]


def _cparams(*sem):
    return pltpu.CompilerParams(dimension_semantics=sem, vmem_limit_bytes=VMEM_LIMIT_BYTES)


def _tile(n, pref):
    t = (min(pref, n) // 128) * 128
    while t >= 128:
        if n % t == 0:
            return t
        t -= 128
    return n


NORM_ROWS = 64


def _adanorm_into(h_ref, g_ref, mod_ref, mrow, u_ref, zero_ref=None):
    gs = g_ref[...] * (1.0 + mod_ref[mrow + 1:mrow + 2, :])
    shift = mod_ref[mrow:mrow + 1, :]
    rows = min(NORM_ROWS, h_ref.shape[0])

    def body(r, carry):
        r0 = pl.multiple_of(r * rows, rows)
        x = h_ref[pl.ds(r0, rows), :]
        ms = jnp.mean(x * x, axis=-1, keepdims=True)
        u_ref[pl.ds(r0, rows), :] = (x * lax.rsqrt(ms + EPS) * gs + shift).astype(u_ref.dtype)
        if zero_ref is not None:
            zero_ref[pl.ds(r0, rows), :] = jnp.zeros((rows, zero_ref.shape[1]), zero_ref.dtype)
        return carry

    lax.fori_loop(0, h_ref.shape[0] // rows, body, 0)


def _log_sigmoid(x):
    return jnp.minimum(x, 0.0) - jnp.log1p(jnp.exp(-jnp.abs(x)))


def _silu(x):
    return x * jax.nn.sigmoid(x)


ADA_ROWS = 16


def _ada_kernel(c_ref, w_ref, b_ref, o_ref):
    cond = _silu(c_ref[...])
    r = jnp.dot(jnp.concatenate(_split3(cond), axis=0), w_ref[...].astype(BF16), preferred_element_type=F32)
    bp = ADA_ROWS
    o_ref[...] = (r[:bp] + r[bp:2 * bp]) + r[2 * bp:] + b_ref[...]


def ada_modulation(c, w_ada, b_ada):
    L, D, N = w_ada.shape
    B = c.shape[0]
    BP = ADA_ROWS
    cp = jnp.zeros((BP, D), F32).at[:B].set(c)
    tn = _tile(N, 1024)
    out = pl.pallas_call(
        _ada_kernel,
        grid=(L, N // tn),
        in_specs=[
            pl.BlockSpec((BP, D), lambda l, j: (0, 0)),
            pl.BlockSpec((None, D, tn), lambda l, j: (l, 0, j)),
            pl.BlockSpec((None, 1, tn), lambda l, j: (l, 0, j)),
        ],
        out_specs=pl.BlockSpec((None, BP, tn), lambda l, j: (l, 0, j)),
        out_shape=jax.ShapeDtypeStruct((L, BP, N), F32),
        compiler_params=_cparams("parallel", "parallel"),
        name="ada_modulation",
    )(cp, w_ada, b_ada.reshape(L, 1, N))
    return out[:, :B].reshape(L, B, N_MOD, D)


def _ffn_kernel(h_ref, mod_ref, g_ref, wg_ref, wu_ref, wd_ref, *rest, mrow, res_scale, final_norm):
    if final_norm:
        fg_ref, o_ref, u_scr = rest
    else:
        o_ref, u_scr = rest
    f = pl.program_id(1)

    @pl.when(f == 0)
    def _():
        _adanorm_into(h_ref, g_ref, mod_ref, mrow, u_scr, zero_ref=o_ref)

    u = u_scr[...]
    a = jnp.dot(u, wg_ref[...].astype(BF16), preferred_element_type=F32)
    b = jnp.dot(u, wu_ref[...].astype(BF16), preferred_element_type=F32)
    mid = (_silu(a) * b).astype(BF16)
    o_ref[...] += jnp.dot(mid, wd_ref[...].astype(BF16), preferred_element_type=F32)

    @pl.when(f == pl.num_programs(1) - 1)
    def _():
        gate = res_scale * mod_ref[mrow + 2:mrow + 3, :]
        rows = min(NORM_ROWS, h_ref.shape[0])

        def body(r, carry):
            r0 = pl.multiple_of(r * rows, rows)
            y = h_ref[pl.ds(r0, rows), :] + gate * o_ref[pl.ds(r0, rows), :]
            if final_norm:
                y = y * lax.rsqrt(jnp.mean(y * y, axis=-1, keepdims=True) + EPS) * fg_ref[...]
            o_ref[pl.ds(r0, rows), :] = y
            return carry

        lax.fori_loop(0, h_ref.shape[0] // rows, body, 0)


def ffn_block(h, mod, g, wg, wu, wd, *, layer, which, seq, mrow, res_scale, final_g=None):
    T, D = h.shape
    F = wg.shape[-1]
    tm = min(1024, seq)
    tf = _tile(F, 256)
    kern = functools.partial(_ffn_kernel, mrow=mrow, res_scale=res_scale, final_norm=final_g is not None)
    in_specs = [
        pl.BlockSpec((tm, D), lambda i, f: (i, 0)),
        pl.BlockSpec((None, N_MOD, D), lambda i, f: ((i * tm) // seq, 0, 0)),
        pl.BlockSpec((1, D), lambda i, f: (0, 0)),
        pl.BlockSpec((None, None, D, tf), lambda i, f: (layer, which, 0, f)),
        pl.BlockSpec((None, None, D, tf), lambda i, f: (layer, which, 0, f)),
        pl.BlockSpec((None, None, tf, D), lambda i, f: (layer, which, f, 0)),
    ]
    args = [h, mod, g.reshape(1, D), wg, wu, wd]
    if final_g is not None:
        in_specs.append(pl.BlockSpec((1, D), lambda i, f: (0, 0)))
        args.append(final_g.reshape(1, D))
    return pl.pallas_call(
        kern,
        grid=(T // tm, F // tf),
        in_specs=in_specs,
        out_specs=pl.BlockSpec((tm, D), lambda i, f: (i, 0)),
        out_shape=jax.ShapeDtypeStruct((T, D), F32),
        scratch_shapes=[pltpu.VMEM((tm, D), BF16)],
        compiler_params=_cparams("parallel", "arbitrary"),
        name="ffn_block",
    )(*args)


PROJ_COL_CHUNK = 512


def _norm_proj_kernel(h_ref, mod_ref, g_ref, wb_ref, wf_ref, pb_ref, pf_ref, u_ref, *, mrow):
    _adanorm_into(h_ref, g_ref, mod_ref, mrow, u_ref)
    u = u_ref[...]
    nt = (((1,), (1,)), ((), ()))
    for c0 in range(0, pb_ref.shape[1], PROJ_COL_CHUNK):
        cols = slice(c0, c0 + PROJ_COL_CHUNK)
        pb_ref[:, cols] = lax.dot_general(u, wb_ref[cols, :], nt, preferred_element_type=F32).astype(pb_ref.dtype)
    pf_ref[...] = lax.dot_general(u, wf_ref[...], nt, preferred_element_type=F32)


def norm_proj(h, mod, g, wb, wf, *, layer, seq, mrow):
    T, D = h.shape
    NB, NF = wb.shape[1], wf.shape[1]
    tm = min(512, seq)
    kern = functools.partial(_norm_proj_kernel, mrow=mrow)
    resident = pl.Buffered(1)
    return pl.pallas_call(
        kern,
        grid=(T // tm,),
        in_specs=[
            pl.BlockSpec((tm, D), lambda i: (i, 0)),
            pl.BlockSpec((None, N_MOD, D), lambda i: ((i * tm) // seq, 0, 0)),
            pl.BlockSpec((1, D), lambda i: (0, 0)),
            pl.BlockSpec((None, NB, D), lambda i: (layer, 0, 0), pipeline_mode=resident),
            pl.BlockSpec((None, NF, D), lambda i: (layer, 0, 0), pipeline_mode=resident),
        ],
        out_specs=[
            pl.BlockSpec((tm, NB), lambda i: (i, 0)),
            pl.BlockSpec((tm, NF), lambda i: (i, 0)),
            pl.BlockSpec((tm, D), lambda i: (i, 0)),
        ],
        out_shape=[jax.ShapeDtypeStruct((T, NB), BF16), jax.ShapeDtypeStruct((T, NF), F32),
                   jax.ShapeDtypeStruct((T, D), BF16)],
        compiler_params=_cparams("parallel"),
        name="norm_proj",
    )(h, mod, g.reshape(1, D), wb, wf)


SWA_BLOCKS_PER_STEP = 2


def _swa_kernel(sink_ref, q_ref, kv_ref, kvp_ref, o_ref):
    n = pl.program_id(1)
    W = WINDOW
    d = HEAD_DIM
    rowi = lax.broadcasted_iota(jnp.int32, (2 * W, 2 * d), 0)
    lanei = lax.broadcasted_iota(jnp.int32, (2 * W, 2 * d), 1)
    own = (rowi >= W) == (lanei >= d)
    first_half = lax.broadcasted_iota(jnp.int32, (W, 2 * d), 1) < d
    qpos = lax.broadcasted_iota(jnp.int32, (2 * W, 2 * W), 0) % W + W
    kpos = lax.broadcasted_iota(jnp.int32, (2 * W, 2 * W), 1)
    rel = qpos - kpos
    band = (rel >= 0) & (rel < W)
    ones = jnp.ones((2 * W, 2 * d), BF16)
    group = SWA_HEADS // SWA_KV_HEADS
    for blk in range(SWA_BLOCKS_PER_STEP):
        rows = slice(blk * W, (blk + 1) * W)
        q = q_ref[rows, :] * (d ** -0.5)
        prev = kvp_ref[...] if blk == 0 else kv_ref[(blk - 1) * W:blk * W, :]
        kv = jnp.concatenate([prev, kv_ref[rows, :]], axis=0)
        mask = band & ((kpos >= W) | (n > 0)) if blk == 0 else band
        for g in range(SWA_HEADS // 2):
            kvh = (2 * g) // group
            k2 = kv[:, kvh * 2 * d:(kvh + 1) * 2 * d]
            v2 = kv[:, (SWA_KV_HEADS + kvh) * 2 * d:(SWA_KV_HEADS + kvh + 1) * 2 * d]
            v_ext = jnp.concatenate([v2, ones], axis=1)
            qp = q[:, g * 2 * d:(g + 1) * 2 * d]
            qstack = jnp.where(own, jnp.concatenate([qp, qp], axis=0), jnp.zeros((2 * W, 2 * d), BF16))
            logits = lax.dot_general(qstack, k2, (((1,), (1,)), ((), ())), preferred_element_type=F32)
            logits = jnp.where(mask, logits, -jnp.inf)
            sink = jnp.where(rowi >= W, sink_ref[2 * g + 1], sink_ref[2 * g])
            m = jnp.maximum(jnp.max(logits, axis=-1, keepdims=True), sink)
            p = jnp.exp(logits - jnp.concatenate([m, m], axis=1)).astype(BF16)
            r = jnp.dot(p, v_ext, preferred_element_type=F32)
            o2 = r[:, :2 * d] / (r[:, 2 * d:] + jnp.exp(sink - m))
            out = jnp.where(first_half, o2[:W], o2[W:])
            o_ref[rows, g * 2 * d:(g + 1) * 2 * d] = out.astype(o_ref.dtype)


def swa_attention(proj, sinks, *, batch, seq):
    T = proj.shape[0]
    nb = seq // WINDOW
    per = SWA_BLOCKS_PER_STEP
    ns = nb // per
    qblk = COLB["a_q"] // 512
    kvblk = COLB["a_k0"] // 512
    return pl.pallas_call(
        _swa_kernel,
        grid=(batch, ns),
        in_specs=[
            pl.BlockSpec(memory_space=pltpu.SMEM),
            pl.BlockSpec((per * WINDOW, 512), lambda b, n: (b * ns + n, qblk)),
            pl.BlockSpec((per * WINDOW, 512), lambda b, n: (b * ns + n, kvblk)),
            pl.BlockSpec((WINDOW, 512), lambda b, n: (b * nb + jnp.maximum(per * n - 1, 0), kvblk)),
        ],
        out_specs=pl.BlockSpec((per * WINDOW, 512), lambda b, n: (b * ns + n, 0)),
        out_shape=jax.ShapeDtypeStruct((T, BRANCH_WIDTH), BF16),
        compiler_params=_cparams("parallel", "arbitrary"),
        name="swa_attention",
    )(sinks, proj, proj, proj)


FOX_CUM_ROWS = 256


def _fox_cum_kernel(x_ref, b_ref, o_ref, *, seq):
    R = min(FOX_CUM_ROWS, seq)
    ri = lax.broadcasted_iota(jnp.int32, (R, 3 * R), 0)
    ci = lax.broadcasted_iota(jnp.int32, (R, 3 * R), 1)
    tri3 = jnp.where((ci % R) <= ri, 1.0, 0.0).astype(BF16)

    def body(i, carry):
        r0 = pl.multiple_of(i * R, R)
        ls = _log_sigmoid(x_ref[pl.ds(r0, R), :] + b_ref[...])
        cum = jnp.dot(tri3, jnp.concatenate(_split3(ls), axis=0), preferred_element_type=F32) + carry
        o_ref[pl.ds(r0, R), :] = cum
        return cum[R - 1:R, :]

    lax.fori_loop(0, seq // R, body, jnp.zeros((1, x_ref.shape[1]), F32))


def fox_forget_cumsum(proj, fox_b, *, batch, seq):
    T = proj.shape[0]
    blk = SMALL_COL // 128
    bias = jnp.zeros((1, 128), F32).at[0, :FOX_HEADS].set(fox_b)
    return pl.pallas_call(
        functools.partial(_fox_cum_kernel, seq=seq),
        grid=(batch,),
        in_specs=[
            pl.BlockSpec((seq, 128), lambda b: (b, blk)),
            pl.BlockSpec((1, 128), lambda b: (0, 0)),
        ],
        out_specs=pl.BlockSpec((seq, 128), lambda b: (b, 0)),
        out_shape=jax.ShapeDtypeStruct((T, 128), F32),
        compiler_params=_cparams("parallel"),
        name="fox_forget_cumsum",
    )(proj, bias)


FOX_KEY_ROWS = 512


def _fox_kernel(q_ref, k_ref, v_ref, fc_ref, fcs_ref, o_ref, m_scr, acc_scr, s_scr, kx_scr, *, tq, hp_heads):
    hp = pl.program_id(1)
    qi = pl.program_id(2)
    tk = tq
    d = HEAD_DIM
    w = hp_heads * d
    seq = k_ref.shape[0]
    e_row = lax.broadcasted_iota(jnp.int32, (w, w), 0)
    e_lane = lax.broadcasted_iota(jnp.int32, (w, w), 1)

    @pl.when(qi == 0)
    def _():
        kr = min(FOX_KEY_ROWS, seq)
        place = [sum(jnp.where((e_row == hp * hp_heads + hh) & (e_lane == 3 * hh + j), 1.0, 0.0)
                     for hh in range(hp_heads)).astype(BF16) for j in range(3)]
        for r0 in range(0, seq, kr):
            rows = slice(r0, r0 + kr)
            ext = sum(jnp.dot(piece, place[j], preferred_element_type=F32)
                      for j, piece in enumerate(_split3(-fcs_ref[rows, :])))
            kx_scr[rows, :w] = k_ref[rows, :]
            kx_scr[rows, w:] = ext.astype(BF16)

    lane = lax.broadcasted_iota(jnp.int32, (tq, w), 1)
    q = q_ref[...] * (d ** -0.5)
    zero = jnp.zeros_like(q)
    qs = [jnp.concatenate([jnp.where((lane >= hh * d) & (lane < (hh + 1) * d), q, zero),
                           jnp.where((lane >= 3 * hh) & (lane < 3 * hh + 3), 1.0, 0.0).astype(BF16)], axis=1)
          for hh in range(hp_heads)]
    m_scr[...] = jnp.full(m_scr.shape, -jnp.inf, F32)
    acc_scr[...] = jnp.zeros(acc_scr.shape, F32)
    fc = fc_ref[...]
    fqs = [jnp.broadcast_to(jnp.sum(jnp.where(lane == hp * hp_heads + hh, fc, 0.0), axis=-1, keepdims=True),
                            (tq, w)) for hh in range(hp_heads)]
    ones = jnp.ones((tk, w), BF16)

    def scores(kb, slot):
        k0 = pl.multiple_of(kb * tk, tk)
        kx = kx_scr[pl.ds(k0, tk), :]
        for hh in range(hp_heads):
            s_scr[slot, hh] = lax.dot_general(qs[hh], kx, (((1,), (1,)), ((), ())), preferred_element_type=F32)

    def softmax_pv(kb, slot, masked):
        k0 = pl.multiple_of(kb * tk, tk)
        v = jnp.concatenate([v_ref[pl.ds(k0, tk), :], ones], axis=1)
        for hh in range(hp_heads):
            t = s_scr[slot, hh]
            if masked:
                row = lax.broadcasted_iota(jnp.int32, (tq, tk), 0)
                col = lax.broadcasted_iota(jnp.int32, (tq, tk), 1)
                t = jnp.where(col <= row, t, -jnp.inf)
            m_old = m_scr[hh]
            m_new = jnp.maximum(m_old, fqs[hh] + jnp.max(t, axis=-1, keepdims=True))
            alpha = jnp.exp(m_old - m_new)
            c = fqs[hh] - m_new
            p = jnp.exp(t + jnp.concatenate([c] * (tk // w), axis=1))
            acc_scr[hh] = (jnp.concatenate([alpha, alpha], axis=1) * acc_scr[hh]
                           + jnp.dot(p.astype(BF16), v, preferred_element_type=F32))
            m_scr[hh] = m_new

    scores(0, 0)

    def body(j, carry):
        kb = 2 * j
        softmax_pv(kb, 0, False)
        scores(kb + 1, 1)
        softmax_pv(kb + 1, 1, False)
        scores(kb + 2, 0)
        return carry

    lax.fori_loop(0, qi // 2, body, 0)

    @pl.when(qi % 2 == 1)
    def _():
        softmax_pv(qi - 1, 0, False)
        scores(qi, 1)
        softmax_pv(qi, 1, True)

    @pl.when(qi % 2 == 0)
    def _():
        softmax_pv(qi, 0, True)
    out = None
    for hh in range(hp_heads):
        o = acc_scr[hh, :, :w] / acc_scr[hh, :, w:]
        out = o if out is None else jnp.where(lane >= hh * d, o, out)
    o_ref[...] = out.astype(o_ref.dtype)


def fox_attention(proj, fc, *, batch, seq):
    T = proj.shape[0]
    tq = min(512, seq)
    nq = seq // tq
    hp_heads = 2
    n_hp = FOX_HEADS // hp_heads
    qblk, kblk, vblk = COLB["f_q"] // 128, COLB["f_k"] // 128, COLB["f_v"] // 128
    kern = functools.partial(_fox_kernel, tq=tq, hp_heads=hp_heads)
    return pl.pallas_call(
        kern,
        grid=(batch, n_hp, nq),
        scratch_shapes=[pltpu.VMEM((hp_heads, tq, hp_heads * HEAD_DIM), F32),
                        pltpu.VMEM((hp_heads, tq, 2 * hp_heads * HEAD_DIM), F32),
                        pltpu.VMEM((2, hp_heads, tq, tq), F32),
                        pltpu.VMEM((seq, 2 * hp_heads * HEAD_DIM), BF16)],
        in_specs=[
            pl.BlockSpec((tq, 128), lambda b, hp, qi: (b * nq + qi, qblk + hp)),
            pl.BlockSpec((seq, 128), lambda b, hp, qi: (b, kblk + hp)),
            pl.BlockSpec((seq, 128), lambda b, hp, qi: (b, vblk + hp)),
            pl.BlockSpec((tq, 128), lambda b, hp, qi: (b * nq + qi, 0)),
            pl.BlockSpec((seq, 128), lambda b, hp, qi: (b, 0)),
        ],
        out_specs=pl.BlockSpec((tq, 128), lambda b, hp, qi: (b * nq + qi, hp)),
        out_shape=jax.ShapeDtypeStruct((T, BRANCH_WIDTH), BF16),
        compiler_params=_cparams("parallel", "parallel", "arbitrary"),
        name="fox_attention",
    )(proj, proj, proj, fc, fc)


def _split2(x):
    hi = x.astype(BF16)
    return hi, (x - hi.astype(F32)).astype(BF16)


def _split3(x):
    hi = x.astype(BF16)
    r = x - hi.astype(F32)
    mid = r.astype(BF16)
    return hi, mid, (r - mid.astype(F32)).astype(BF16)


def _rotate_half(x, neg_first_half):
    n = x.shape[-1]
    half = HEAD_DIM // 2
    fwd = pltpu.roll(x, half, 1)
    bwd = pltpu.roll(x, n - half, 1)
    return jnp.where(neg_first_half, -bwd, fwd)


def _linear_attn_kernel(*refs, mode, cb):
    if mode == "gla":
        (q_ref, k_ref, v_ref, gate_ref, small_ref, wg_ref, bg_ref, ng_ref, o_ref, st_ref) = refs
    else:
        (q_ref, k_ref, v_ref, gate_ref, lg_ref, cos_ref, sin_ref, gw_ref, gb_ref, o_ref, st_ref) = refs
    H, DK, DV, C = GLA_HEADS, GLA_DK, GLA_DV, CHUNK
    nch = cb // C
    W2 = 2 * DK
    HW = H * DK
    dn_lanes = (((1,), (1,)), ((), ()))
    dn_rows = (((0,), (0,)), ((), ()))

    @pl.when(pl.program_id(1) == 0)
    def _():
        st_ref[...] = jnp.zeros_like(st_ref)

    q = q_ref[...].astype(F32)
    k = k_ref[...].astype(F32)
    if mode == "gla":
        glr = small_ref[...][:, FOX_HEADS:FOX_HEADS + GLA_GATE_RANK]
        a_hi, a_lo = _split2(glr)
        w_hi, w_lo = _split2(wg_ref[...])
        z = (jnp.dot(a_hi, w_hi, preferred_element_type=F32) + jnp.dot(a_hi, w_lo, preferred_element_type=F32)
             + jnp.dot(a_lo, w_hi, preferred_element_type=F32)) + bg_ref[...]
        ld = _log_sigmoid(z) / GLA_TAU
        ld_w = jnp.concatenate([ld[c * C:(c + 1) * C] for c in range(nch)], axis=1)
        ri = lax.broadcasted_iota(jnp.int32, (C, 3 * C), 0)
        ci = lax.broadcasted_iota(jnp.int32, (C, 3 * C), 1)
        tri3 = jnp.where((ci % C) <= ri, 1.0, 0.0).astype(BF16)
        cum_w = jnp.dot(tri3, jnp.concatenate(_split3(ld_w), axis=0), preferred_element_type=F32)
        cum = jnp.concatenate([cum_w[:, c * HW:(c + 1) * HW] for c in range(nch)], axis=0)
        lasts = [cum_w[C - 1:C, c * HW:(c + 1) * HW] for c in range(nch)]
        last_b = jnp.concatenate([jnp.broadcast_to(l, (C, HW)) for l in lasts], axis=0)
    else:
        lane = lax.broadcasted_iota(jnp.int32, (cb, HW), 1)
        first_half = (lane % HEAD_DIM) < (HEAD_DIM // 2)
        cos = cos_ref[...]
        sin = sin_ref[...]
        q = q * cos + _rotate_half(q, first_half) * sin
        k = k * cos + _rotate_half(k, first_half) * sin
        steps = (lax.broadcasted_iota(jnp.int32, (cb, HW), 0) % C + 1).astype(F32)
        cum = steps * lg_ref[...]
        last_b = float(C) * lg_ref[...]
        lasts = [last_b] * nch
    q_in = (q * (DK ** -0.5) * jnp.exp(cum)).astype(BF16)
    k_in = (k * jnp.exp(-cum)).astype(BF16)
    k_st = (k * jnp.exp(last_b - cum)).astype(BF16)
    decs = [jnp.exp(l) for l in lasts]

    rowi = lax.broadcasted_iota(jnp.int32, (2 * C, W2), 0)
    lanei = lax.broadcasted_iota(jnp.int32, (2 * C, W2), 1)
    own = (rowi >= C) == (lanei >= DK)
    bd_causal = own & ((lanei % C) <= (rowi % C))
    zero_bf = jnp.zeros((2 * C, W2), BF16)

    def pair_ops(c, p):
        rows = slice(c * C, (c + 1) * C)
        ls = slice(p * W2, (p + 1) * W2)
        qp, kp, ksp = q_in[rows, ls], k_in[rows, ls], k_st[rows, ls]
        qstack = jnp.where(own, jnp.concatenate([qp, qp], axis=0), zero_bf)
        ksstack = jnp.where(own, jnp.concatenate([ksp, ksp], axis=0), zero_bf)
        k2 = jnp.concatenate([kp, kp], axis=0)
        vstack = jnp.concatenate([v_ref[rows, (2 * p) * DV:(2 * p + 1) * DV],
                                  v_ref[rows, (2 * p + 1) * DV:(2 * p + 2) * DV]], axis=0)
        sw = lax.dot_general(qstack, k2, dn_lanes, preferred_element_type=F32)
        sc = jnp.where(bd_causal, sw, 0.0).astype(BF16)
        o_intra = jnp.dot(sc, vstack, preferred_element_type=F32)
        upd = lax.dot_general(vstack, ksstack, dn_rows, preferred_element_type=F32)
        return qstack, o_intra, upd

    pre = [[pair_ops(c, p) for p in range(H // 2)] for c in range(nch)]
    state = [st_ref[p] for p in range(H // 2)]
    for c in range(nch):
        rows = slice(c * C, (c + 1) * C)
        for p in range(H // 2):
            qstack, o_intra, upd = pre[c][p]
            o = o_intra + lax.dot_general(qstack, state[p].astype(BF16), dn_lanes, preferred_element_type=F32)
            state[p] = state[p] * decs[c][:, p * W2:(p + 1) * W2] + upd
            for hh in range(2):
                hd = 2 * p + hh
                vs = slice(hd * DV, (hd + 1) * DV)
                oh = o[hh * C:(hh + 1) * C]
                gate = _silu(gate_ref[rows, vs])
                if mode == "gla":
                    y = oh * lax.rsqrt(jnp.mean(oh * oh, axis=-1, keepdims=True) + EPS) * ng_ref[...]
                else:
                    mu = jnp.mean(oh, axis=-1, keepdims=True)
                    var = jnp.mean(jnp.square(oh - mu), axis=-1, keepdims=True)
                    y = (oh - mu) * lax.rsqrt(var + EPS) * gw_ref[:, vs] + gb_ref[:, vs]
                o_ref[rows, vs] = (y * gate).astype(o_ref.dtype)
    for p in range(H // 2):
        st_ref[p] = state[p]


def linear_attention(proj, projf, params, *, mode, batch, seq):
    T = proj.shape[0]
    cb = min(512, seq)
    nc = seq // cb
    pre = "g" if mode == "gla" else "r"
    qblk, kblk = COLB[pre + "_q"] // 256, COLB[pre + "_k"] // 256
    vblk = COLB[pre + "_v"] // 512
    gblk = COLF["g_r" if mode == "gla" else "r_g"] // 512
    row = lambda b, c: b * nc + c
    in_specs = [
        pl.BlockSpec((cb, 256), lambda b, c: (row(b, c), qblk)),
        pl.BlockSpec((cb, 256), lambda b, c: (row(b, c), kblk)),
        pl.BlockSpec((cb, 512), lambda b, c: (row(b, c), vblk)),
        pl.BlockSpec((cb, 512), lambda b, c: (row(b, c), gblk)),
    ]
    args = [proj, proj, proj, projf]
    if mode == "gla":
        wg, bg, ng = params
        in_specs += [
            pl.BlockSpec((cb, 128), lambda b, c: (row(b, c), SMALL_COL // 128)),
            pl.BlockSpec(wg.shape, lambda b, c: (0, 0)),
            pl.BlockSpec((1, bg.shape[-1]), lambda b, c: (0, 0)),
            pl.BlockSpec((1, ng.shape[-1]), lambda b, c: (0, 0)),
        ]
        args += [projf, wg, bg.reshape(1, -1), ng.reshape(1, -1)]
    else:
        lg, cos, sin, gw, gb = params
        in_specs += [
            pl.BlockSpec((1, lg.shape[-1]), lambda b, c: (0, 0)),
            pl.BlockSpec((cb, 256), lambda b, c: (c, 0)),
            pl.BlockSpec((cb, 256), lambda b, c: (c, 0)),
            pl.BlockSpec((1, gw.shape[-1]), lambda b, c: (0, 0)),
            pl.BlockSpec((1, gb.shape[-1]), lambda b, c: (0, 0)),
        ]
        args += [lg.reshape(1, -1), cos, sin, gw.reshape(1, -1), gb.reshape(1, -1)]
    return pl.pallas_call(
        functools.partial(_linear_attn_kernel, mode=mode, cb=cb),
        grid=(batch, nc),
        in_specs=in_specs,
        out_specs=pl.BlockSpec((cb, 512), lambda b, c: (row(b, c), 0)),
        out_shape=jax.ShapeDtypeStruct((T, BRANCH_WIDTH), BF16),
        scratch_shapes=[pltpu.VMEM((GLA_HEADS // 2, GLA_DV, 2 * GLA_DK), F32)],
        compiler_params=_cparams("parallel", "arbitrary"),
        name="linear_attention_" + mode,
    )(*args)


def _retention_tables(seq):
    half = RET_DK // 2
    inv = RET_THETA_BASE ** (-jnp.arange(half, dtype=F32) / half)
    ang = jnp.arange(seq).astype(F32)[:, None] * inv[None, :]
    cos = jnp.tile(jnp.cos(ang), (1, 2 * RET_HEADS))
    sin = jnp.tile(jnp.sin(ang), (1, 2 * RET_HEADS))
    log_gamma = jnp.log1p(-jnp.exp2(-5.0 - jnp.arange(RET_HEADS, dtype=F32)))
    return jnp.repeat(log_gamma, RET_DK), cos, sin


def _merge_kernel(u_ref, oa_ref, ob_ref, oc_ref, od_ref, wm_ref, wb_ref, bm_ref, m_ref):
    u = u_ref[...]
    acc = None
    for i, o_ref in enumerate((oa_ref, ob_ref, oc_ref, od_ref)):
        z = jnp.dot(u, wm_ref[i].astype(BF16), preferred_element_type=F32) + bm_ref[i]
        pr = jnp.dot(o_ref[...], wb_ref[i].astype(BF16), preferred_element_type=F32)
        t = jax.nn.sigmoid(z) * pr
        acc = t if acc is None else acc + t
    m_ref[...] = acc.astype(m_ref.dtype)


def merge_branches(u, branches, wm, wb, bm, *, layer):
    T, D = u.shape
    tm, tn = min(1024, T), min(256, D)
    ospec = pl.BlockSpec((tm, BRANCH_WIDTH), lambda i, j: (i, 0))
    return pl.pallas_call(
        _merge_kernel,
        grid=(T // tm, D // tn),
        in_specs=[
            pl.BlockSpec((tm, D), lambda i, j: (i, 0)),
            ospec, ospec, ospec, ospec,
            pl.BlockSpec((None, N_BRANCH, D, tn), lambda i, j: (layer, 0, 0, j)),
            pl.BlockSpec((None, N_BRANCH, BRANCH_WIDTH, tn), lambda i, j: (layer, 0, 0, j)),
            pl.BlockSpec((None, N_BRANCH, 1, tn), lambda i, j: (layer, 0, 0, j)),
        ],
        out_specs=pl.BlockSpec((tm, tn), lambda i, j: (i, j)),
        out_shape=jax.ShapeDtypeStruct((T, D), BF16),
        compiler_params=_cparams("parallel", "arbitrary"),
        name="merge_branches",
    )(u, *branches, wm, wb, bm.reshape(bm.shape[0], N_BRANCH, 1, D))


def _out_proj_kernel(h_ref, m_ref, w_ref, mod_ref, o_ref, *, mrow):
    m = m_ref[...]
    for c0 in range(0, o_ref.shape[1], PROJ_COL_CHUNK):
        cols = slice(c0, c0 + PROJ_COL_CHUNK)
        y = jnp.dot(m, w_ref[:, cols], preferred_element_type=F32)
        o_ref[:, cols] = h_ref[:, cols] + mod_ref[mrow:mrow + 1, cols] * y


def out_proj(h, merged, w, mod, *, layer, seq, mrow):
    T, D = h.shape
    tm = min(512, seq)
    return pl.pallas_call(
        functools.partial(_out_proj_kernel, mrow=mrow),
        grid=(T // tm,),
        in_specs=[
            pl.BlockSpec((tm, D), lambda i: (i, 0)),
            pl.BlockSpec((tm, D), lambda i: (i, 0)),
            pl.BlockSpec((None, D, D), lambda i: (layer, 0, 0), pipeline_mode=pl.Buffered(1)),
            pl.BlockSpec((None, N_MOD, D), lambda i: ((i * tm) // seq, 0, 0)),
        ],
        out_specs=pl.BlockSpec((tm, D), lambda i: (i, 0)),
        out_shape=jax.ShapeDtypeStruct((T, D), F32),
        compiler_params=_cparams("parallel"),
        name="out_proj",
    )(h, merged, w, mod)


def _w_in_prep_kernel(w_ref, ob_ref, of_ref):
    for order, o_ref in ((_ORDER_B, ob_ref), (_ORDER_F, of_ref)):
        off = 0
        for name in order:
            start, width = _REF_COLS[name]
            o_ref[off:off + width, :] = w_ref[start:start + width, :].astype(o_ref.dtype)
            off += width
        if off < o_ref.shape[0]:
            o_ref[off:, :] = jnp.zeros((o_ref.shape[0] - off, o_ref.shape[1]), o_ref.dtype)


def permuted_w_in(w_in):
    L, D, C = w_in.shape
    td = min(256, D)
    return pl.pallas_call(
        _w_in_prep_kernel,
        grid=(L, D // td),
        in_specs=[pl.BlockSpec((None, C, td), lambda l, r: (l, 0, r))],
        out_specs=[pl.BlockSpec((None, PROJ_B_COLS, td), lambda l, r: (l, 0, r)),
                   pl.BlockSpec((None, PROJ_F_COLS, td), lambda l, r: (l, 0, r))],
        out_shape=[jax.ShapeDtypeStruct((L, PROJ_B_COLS, D), BF16), jax.ShapeDtypeStruct((L, PROJ_F_COLS, D), BF16)],
        compiler_params=_cparams("parallel", "parallel"),
        name="w_in_prep",
    )(jnp.swapaxes(w_in, 1, 2))


def kernel(x, c, w_ada, b_ada, norm_g, ffn_w_gate, ffn_w_up, ffn_w_down, w_in, fox_b_forget, attn_sinks,
           gla_w_gate, gla_b_gate, gla_norm_g, ret_gn_w, ret_gn_b, w_branch, w_merge, b_merge, w_out,
           final_norm_g):
    B, S, D = x.shape
    L = w_ada.shape[0]
    T = B * S
    mod_all = ada_modulation(c, w_ada, b_ada)
    log_gamma, cos, sin = _retention_tables(S)
    w_out_bf = w_out.astype(BF16)
    w_in_b, w_in_f = permuted_w_in(w_in)
    h = x.reshape(T, D)
    for l in range(L):
        mod = mod_all[l]
        h = ffn_block(h, mod, norm_g[l, 0], ffn_w_gate, ffn_w_up, ffn_w_down, layer=l, which=0, seq=S,
                      mrow=0, res_scale=0.5)
        proj, projf, u = norm_proj(h, mod, norm_g[l, 1], w_in_b, w_in_f, layer=l, seq=S, mrow=3)
        o_a = swa_attention(proj, attn_sinks[l], batch=B, seq=S)
        fc = fox_forget_cumsum(projf, fox_b_forget[l], batch=B, seq=S)
        o_b = fox_attention(proj, fc, batch=B, seq=S)
        o_c = linear_attention(proj, projf, (gla_w_gate[l], gla_b_gate[l], gla_norm_g[l]), mode="gla",
                               batch=B, seq=S)
        o_d = linear_attention(proj, projf, (log_gamma, cos, sin, ret_gn_w[l], ret_gn_b[l]), mode="ret",
                               batch=B, seq=S)
        merged = merge_branches(u, (o_a, o_b, o_c, o_d), w_merge, w_branch, b_merge, layer=l)
        h = out_proj(h, merged, w_out_bf, mod, layer=l, seq=S, mrow=5)
        h = ffn_block(h, mod, norm_g[l, 2], ffn_w_gate, ffn_w_up, ffn_w_down, layer=l, which=1, seq=S,
                      mrow=6, res_scale=0.5, final_g=final_norm_g if l == L - 1 else None)
    return h.reshape(B, S, D)
```

```python
import functools

import jax
import jax.numpy as jnp
from jax import lax
from jax.experimental import pallas as pl
from jax.experimental.pallas import tpu as pltpu

HEAD_DIM = 64
SWA_HEADS = 8
SWA_KV_HEADS = 2
WINDOW = 128
FOX_HEADS = 8
GLA_HEADS = 4
GLA_DK = 64
GLA_DV = 128
GLA_GATE_RANK = 16
GLA_TAU = 16.0
RET_HEADS = 4
RET_DK = 64
RET_DV = 128
RET_THETA_BASE = 10000.0
CHUNK = 64
BRANCH_WIDTH = 512
N_BRANCH = 4
N_MOD = 9
EPS = 1e-6

BF16 = jnp.bfloat16
F32 = jnp.float32

VMEM_LIMIT_BYTES = 60 * 1024 * 1024

_REF_COLS = {}
_off = 0
for _name, _size in (
        ("a_q", 512), ("a_k", 128), ("a_v", 128),
        ("f_q", 512), ("f_k", 512), ("f_v", 512), ("f_f", 8),
        ("g_q", 256), ("g_k", 256), ("g_v", 512), ("g_lr", 16), ("g_r", 512),
        ("r_q", 256), ("r_k", 256), ("r_v", 512), ("r_g", 512)):
    _REF_COLS[_name] = (_off, _size)
    _off += _size
IN_COLS = _off
for _name in ("a_k", "a_v"):
    for _i in range(SWA_KV_HEADS):
        _REF_COLS[_name + str(_i)] = (_REF_COLS[_name][0] + _i * HEAD_DIM, HEAD_DIM)


def _layout(order):
    col, off = {}, 0
    for name in order:
        col.setdefault(name, off)
        off += _REF_COLS[name][1]
    return col


_ORDER_B = ("a_q", "f_q", "f_k", "f_v", "g_v", "r_v", "g_q", "g_k", "r_q", "r_k",
            "a_k0", "a_k0", "a_k1", "a_k1", "a_v0", "a_v0", "a_v1", "a_v1")
_ORDER_F = ("g_r", "r_g", "f_f", "g_lr")
COLB = _layout(_ORDER_B)
COLF = _layout(_ORDER_F)
PROJ_B_COLS = 4608
PROJ_F_COLS = 1152
SMALL_COL = COLF["f_f"]


def _cparams(*sem):
    return pltpu.CompilerParams(dimension_semantics=sem, vmem_limit_bytes=VMEM_LIMIT_BYTES)


def _tile(n, pref):
    t = (min(pref, n) // 128) * 128
    while t >= 128:
        if n % t == 0:
            return t
        t -= 128
    return n


NORM_ROWS = 64


def _adanorm_into(h_ref, g_ref, mod_ref, mrow, u_ref, zero_ref=None):
    gs = g_ref[...] * (1.0 + mod_ref[mrow + 1:mrow + 2, :])
    shift = mod_ref[mrow:mrow + 1, :]
    rows = min(NORM_ROWS, h_ref.shape[0])

    def body(r, carry):
        r0 = pl.multiple_of(r * rows, rows)
        x = h_ref[pl.ds(r0, rows), :]
        ms = jnp.mean(x * x, axis=-1, keepdims=True)
        u_ref[pl.ds(r0, rows), :] = (x * lax.rsqrt(ms + EPS) * gs + shift).astype(u_ref.dtype)
        if zero_ref is not None:
            zero_ref[pl.ds(r0, rows), :] = jnp.zeros((rows, zero_ref.shape[1]), zero_ref.dtype)
        return carry

    lax.fori_loop(0, h_ref.shape[0] // rows, body, 0)


def _log_sigmoid(x):
    return jnp.minimum(x, 0.0) - jnp.log1p(jnp.exp(-jnp.abs(x)))


def _silu(x):
    return x * jax.nn.sigmoid(x)


ADA_ROWS = 16


def _ada_kernel(c_ref, w_ref, b_ref, o_ref):
    cond = _silu(c_ref[...])
    r = jnp.dot(jnp.concatenate(_split3(cond), axis=0), w_ref[...].astype(BF16), preferred_element_type=F32)
    bp = ADA_ROWS
    o_ref[...] = (r[:bp] + r[bp:2 * bp]) + r[2 * bp:] + b_ref[...]


def ada_modulation(c, w_ada, b_ada):
    L, D, N = w_ada.shape
    B = c.shape[0]
    BP = ADA_ROWS
    cp = jnp.zeros((BP, D), F32).at[:B].set(c)
    tn = _tile(N, 1024)
    out = pl.pallas_call(
        _ada_kernel,
        grid=(L, N // tn),
        in_specs=[
            pl.BlockSpec((BP, D), lambda l, j: (0, 0)),
            pl.BlockSpec((None, D, tn), lambda l, j: (l, 0, j)),
            pl.BlockSpec((None, 1, tn), lambda l, j: (l, 0, j)),
        ],
        out_specs=pl.BlockSpec((None, BP, tn), lambda l, j: (l, 0, j)),
        out_shape=jax.ShapeDtypeStruct((L, BP, N), F32),
        compiler_params=_cparams("parallel", "parallel"),
        name="ada_modulation",
    )(cp, w_ada, b_ada.reshape(L, 1, N))
    return out[:, :B].reshape(L, B, N_MOD, D)


def _ffn_kernel(h_ref, mod_ref, g_ref, wg_ref, wu_ref, wd_ref, *rest, mrow, res_scale, final_norm):
    if final_norm:
        fg_ref, o_ref, u_scr = rest
    else:
        o_ref, u_scr = rest
    f = pl.program_id(1)

    @pl.when(f == 0)
    def _():
        _adanorm_into(h_ref, g_ref, mod_ref, mrow, u_scr, zero_ref=o_ref)

    u = u_scr[...]
    a = jnp.dot(u, wg_ref[...].astype(BF16), preferred_element_type=F32)
    b = jnp.dot(u, wu_ref[...].astype(BF16), preferred_element_type=F32)
    mid = (_silu(a) * b).astype(BF16)
    o_ref[...] += jnp.dot(mid, wd_ref[...].astype(BF16), preferred_element_type=F32)

    @pl.when(f == pl.num_programs(1) - 1)
    def _():
        gate = res_scale * mod_ref[mrow + 2:mrow + 3, :]
        rows = min(NORM_ROWS, h_ref.shape[0])

        def body(r, carry):
            r0 = pl.multiple_of(r * rows, rows)
            y = h_ref[pl.ds(r0, rows), :] + gate * o_ref[pl.ds(r0, rows), :]
            if final_norm:
                y = y * lax.rsqrt(jnp.mean(y * y, axis=-1, keepdims=True) + EPS) * fg_ref[...]
            o_ref[pl.ds(r0, rows), :] = y
            return carry

        lax.fori_loop(0, h_ref.shape[0] // rows, body, 0)


def ffn_block(h, mod, g, wg, wu, wd, *, layer, which, seq, mrow, res_scale, final_g=None):
    T, D = h.shape
    F = wg.shape[-1]
    tm = min(1024, seq)
    tf = _tile(F, 256)
    kern = functools.partial(_ffn_kernel, mrow=mrow, res_scale=res_scale, final_norm=final_g is not None)
    in_specs = [
        pl.BlockSpec((tm, D), lambda i, f: (i, 0)),
        pl.BlockSpec((None, N_MOD, D), lambda i, f: ((i * tm) // seq, 0, 0)),
        pl.BlockSpec((1, D), lambda i, f: (0, 0)),
        pl.BlockSpec((None, None, D, tf), lambda i, f: (layer, which, 0, f)),
        pl.BlockSpec((None, None, D, tf), lambda i, f: (layer, which, 0, f)),
        pl.BlockSpec((None, None, tf, D), lambda i, f: (layer, which, f, 0)),
    ]
    args = [h, mod, g.reshape(1, D), wg, wu, wd]
    if final_g is not None:
        in_specs.append(pl.BlockSpec((1, D), lambda i, f: (0, 0)))
        args.append(final_g.reshape(1, D))
    return pl.pallas_call(
        kern,
        grid=(T // tm, F // tf),
        in_specs=in_specs,
        out_specs=pl.BlockSpec((tm, D), lambda i, f: (i, 0)),
        out_shape=jax.ShapeDtypeStruct((T, D), F32),
        scratch_shapes=[pltpu.VMEM((tm, D), BF16)],
        compiler_params=_cparams("parallel", "arbitrary"),
        name="ffn_block",
    )(*args)


PROJ_COL_CHUNK = 512


def _norm_proj_kernel(h_ref, mod_ref, g_ref, wb_ref, wf_ref, pb_ref, pf_ref, u_ref, *, mrow):
    _adanorm_into(h_ref, g_ref, mod_ref, mrow, u_ref)
    u = u_ref[...]
    nt = (((1,), (1,)), ((), ()))
    for c0 in range(0, pb_ref.shape[1], PROJ_COL_CHUNK):
        cols = slice(c0, c0 + PROJ_COL_CHUNK)
        pb_ref[:, cols] = lax.dot_general(u, wb_ref[cols, :], nt, preferred_element_type=F32).astype(pb_ref.dtype)
    pf_ref[...] = lax.dot_general(u, wf_ref[...], nt, preferred_element_type=F32)


def norm_proj(h, mod, g, wb, wf, *, layer, seq, mrow):
    T, D = h.shape
    NB, NF = wb.shape[1], wf.shape[1]
    tm = min(512, seq)
    kern = functools.partial(_norm_proj_kernel, mrow=mrow)
    resident = pl.Buffered(1)
    return pl.pallas_call(
        kern,
        grid=(T // tm,),
        in_specs=[
            pl.BlockSpec((tm, D), lambda i: (i, 0)),
            pl.BlockSpec((None, N_MOD, D), lambda i: ((i * tm) // seq, 0, 0)),
            pl.BlockSpec((1, D), lambda i: (0, 0)),
            pl.BlockSpec((None, NB, D), lambda i: (layer, 0, 0), pipeline_mode=resident),
            pl.BlockSpec((None, NF, D), lambda i: (layer, 0, 0), pipeline_mode=resident),
        ],
        out_specs=[
            pl.BlockSpec((tm, NB), lambda i: (i, 0)),
            pl.BlockSpec((tm, NF), lambda i: (i, 0)),
            pl.BlockSpec((tm, D), lambda i: (i, 0)),
        ],
        out_shape=[jax.ShapeDtypeStruct((T, NB), BF16), jax.ShapeDtypeStruct((T, NF), F32),
                   jax.ShapeDtypeStruct((T, D), BF16)],
        compiler_params=_cparams("parallel"),
        name="norm_proj",
    )(h, mod, g.reshape(1, D), wb, wf)


SWA_BLOCKS_PER_STEP = 4


def _swa_kernel(sink_ref, q_ref, kv_ref, kvp_ref, o_ref):
    n = pl.program_id(1)
    W = WINDOW
    d = HEAD_DIM
    rowi = lax.broadcasted_iota(jnp.int32, (2 * W, 2 * d), 0)
    lanei = lax.broadcasted_iota(jnp.int32, (2 * W, 2 * d), 1)
    own = (rowi >= W) == (lanei >= d)
    first_half = lax.broadcasted_iota(jnp.int32, (W, 2 * d), 1) < d
    qpos = lax.broadcasted_iota(jnp.int32, (2 * W, 2 * W), 0) % W + W
    kpos = lax.broadcasted_iota(jnp.int32, (2 * W, 2 * W), 1)
    rel = qpos - kpos
    band = (rel >= 0) & (rel < W)
    ones = jnp.ones((2 * W, 2 * d), BF16)
    group = SWA_HEADS // SWA_KV_HEADS
    for blk in range(SWA_BLOCKS_PER_STEP):
        rows = slice(blk * W, (blk + 1) * W)
        q = q_ref[rows, :] * (d ** -0.5)
        prev = kvp_ref[...] if blk == 0 else kv_ref[(blk - 1) * W:blk * W, :]
        kv = jnp.concatenate([prev, kv_ref[rows, :]], axis=0)
        mask = band & ((kpos >= W) | (n > 0)) if blk == 0 else band
        for g in range(SWA_HEADS // 2):
            kvh = (2 * g) // group
            k2 = kv[:, kvh * 2 * d:(kvh + 1) * 2 * d]
            v2 = kv[:, (SWA_KV_HEADS + kvh) * 2 * d:(SWA_KV_HEADS + kvh + 1) * 2 * d]
            v_ext = jnp.concatenate([v2, ones], axis=1)
            qp = q[:, g * 2 * d:(g + 1) * 2 * d]
            qstack = jnp.where(own, jnp.concatenate([qp, qp], axis=0), jnp.zeros((2 * W, 2 * d), BF16))
            logits = lax.dot_general(qstack, k2, (((1,), (1,)), ((), ())), preferred_element_type=F32)
            logits = jnp.where(mask, logits, -jnp.inf)
            sink = jnp.where(rowi >= W, sink_ref[2 * g + 1], sink_ref[2 * g])
            m = jnp.maximum(jnp.max(logits, axis=-1, keepdims=True), sink)
            p = jnp.exp(logits - jnp.concatenate([m, m], axis=1)).astype(BF16)
            r = jnp.dot(p, v_ext, preferred_element_type=F32)
            o2 = r[:, :2 * d] / (r[:, 2 * d:] + jnp.exp(sink - m))
            out = jnp.where(first_half, o2[:W], o2[W:])
            o_ref[rows, g * 2 * d:(g + 1) * 2 * d] = out.astype(o_ref.dtype)


def swa_attention(proj, sinks, *, batch, seq):
    T = proj.shape[0]
    nb = seq // WINDOW
    per = SWA_BLOCKS_PER_STEP
    ns = nb // per
    qblk = COLB["a_q"] // 512
    kvblk = COLB["a_k0"] // 512
    return pl.pallas_call(
        _swa_kernel,
        grid=(batch, ns),
        in_specs=[
            pl.BlockSpec(memory_space=pltpu.SMEM),
            pl.BlockSpec((per * WINDOW, 512), lambda b, n: (b * ns + n, qblk)),
            pl.BlockSpec((per * WINDOW, 512), lambda b, n: (b * ns + n, kvblk)),
            pl.BlockSpec((WINDOW, 512), lambda b, n: (b * nb + jnp.maximum(per * n - 1, 0), kvblk)),
        ],
        out_specs=pl.BlockSpec((per * WINDOW, 512), lambda b, n: (b * ns + n, 0)),
        out_shape=jax.ShapeDtypeStruct((T, BRANCH_WIDTH), BF16),
        compiler_params=_cparams("parallel", "arbitrary"),
        name="swa_attention",
    )(sinks, proj, proj, proj)


FOX_CUM_ROWS = 256


def _fox_cum_kernel(x_ref, b_ref, o_ref, *, seq):
    R = min(FOX_CUM_ROWS, seq)
    ri = lax.broadcasted_iota(jnp.int32, (R, 3 * R), 0)
    ci = lax.broadcasted_iota(jnp.int32, (R, 3 * R), 1)
    tri3 = jnp.where((ci % R) <= ri, 1.0, 0.0).astype(BF16)

    def body(i, carry):
        r0 = pl.multiple_of(i * R, R)
        ls = _log_sigmoid(x_ref[pl.ds(r0, R), :] + b_ref[...])
        cum = jnp.dot(tri3, jnp.concatenate(_split3(ls), axis=0), preferred_element_type=F32) + carry
        o_ref[pl.ds(r0, R), :] = cum
        return cum[R - 1:R, :]

    lax.fori_loop(0, seq // R, body, jnp.zeros((1, x_ref.shape[1]), F32))


def fox_forget_cumsum(proj, fox_b, *, batch, seq):
    T = proj.shape[0]
    blk = SMALL_COL // 128
    bias = jnp.zeros((1, 128), F32).at[0, :FOX_HEADS].set(fox_b)
    return pl.pallas_call(
        functools.partial(_fox_cum_kernel, seq=seq),
        grid=(batch,),
        in_specs=[
            pl.BlockSpec((seq, 128), lambda b: (b, blk)),
            pl.BlockSpec((1, 128), lambda b: (0, 0)),
        ],
        out_specs=pl.BlockSpec((seq, 128), lambda b: (b, 0)),
        out_shape=jax.ShapeDtypeStruct((T, 128), F32),
        compiler_params=_cparams("parallel"),
        name="fox_forget_cumsum",
    )(proj, bias)


FOX_KEY_ROWS = 512


def _fox_kernel(q_ref, k_ref, v_ref, fc_ref, fcs_ref, o_ref, m_scr, acc_scr, s_scr, kx_scr, *, tq, hp_heads):
    hp = pl.program_id(1)
    qi = pl.program_id(2)
    tk = tq
    d = HEAD_DIM
    w = hp_heads * d
    seq = k_ref.shape[0]
    e_row = lax.broadcasted_iota(jnp.int32, (w, w), 0)
    e_lane = lax.broadcasted_iota(jnp.int32, (w, w), 1)

    @pl.when(qi == 0)
    def _():
        kr = min(FOX_KEY_ROWS, seq)
        place = [sum(jnp.where((e_row == hp * hp_heads + hh) & (e_lane == 3 * hh + j), 1.0, 0.0)
                     for hh in range(hp_heads)).astype(BF16) for j in range(3)]
        for r0 in range(0, seq, kr):
            rows = slice(r0, r0 + kr)
            ext = sum(jnp.dot(piece, place[j], preferred_element_type=F32)
                      for j, piece in enumerate(_split3(-fcs_ref[rows, :])))
            kx_scr[rows, :w] = k_ref[rows, :]
            kx_scr[rows, w:] = ext.astype(BF16)

    lane = lax.broadcasted_iota(jnp.int32, (tq, w), 1)
    q = q_ref[...] * (d ** -0.5)
    zero = jnp.zeros_like(q)
    qs = [jnp.concatenate([jnp.where((lane >= hh * d) & (lane < (hh + 1) * d), q, zero),
                           jnp.where((lane >= 3 * hh) & (lane < 3 * hh + 3), 1.0, 0.0).astype(BF16)], axis=1)
          for hh in range(hp_heads)]
    m_scr[...] = jnp.full(m_scr.shape, -jnp.inf, F32)
    acc_scr[...] = jnp.zeros(acc_scr.shape, F32)
    fc = fc_ref[...]
    fqs = [jnp.broadcast_to(jnp.sum(jnp.where(lane == hp * hp_heads + hh, fc, 0.0), axis=-1, keepdims=True),
                            (tq, w)) for hh in range(hp_heads)]
    ones = jnp.ones((tk, w), BF16)

    def scores(kb, slot):
        k0 = pl.multiple_of(kb * tk, tk)
        kx = kx_scr[pl.ds(k0, tk), :]
        for hh in range(hp_heads):
            s_scr[slot, hh] = lax.dot_general(qs[hh], kx, (((1,), (1,)), ((), ())), preferred_element_type=F32)

    def softmax_pv(kb, slot, masked):
        k0 = pl.multiple_of(kb * tk, tk)
        v = jnp.concatenate([v_ref[pl.ds(k0, tk), :], ones], axis=1)
        for hh in range(hp_heads):
            t = s_scr[slot, hh]
            if masked:
                row = lax.broadcasted_iota(jnp.int32, (tq, tk), 0)
                col = lax.broadcasted_iota(jnp.int32, (tq, tk), 1)
                t = jnp.where(col <= row, t, -jnp.inf)
            m_old = m_scr[hh]
            m_new = jnp.maximum(m_old, fqs[hh] + jnp.max(t, axis=-1, keepdims=True))
            alpha = jnp.exp(m_old - m_new)
            c = fqs[hh] - m_new
            p = jnp.exp(t + jnp.concatenate([c] * (tk // w), axis=1))
            acc_scr[hh] = (jnp.concatenate([alpha, alpha], axis=1) * acc_scr[hh]
                           + jnp.dot(p.astype(BF16), v, preferred_element_type=F32))
            m_scr[hh] = m_new

    scores(0, 0)

    def body(j, carry):
        kb = 2 * j
        softmax_pv(kb, 0, False)
        scores(kb + 1, 1)
        softmax_pv(kb + 1, 1, False)
        scores(kb + 2, 0)
        return carry

    lax.fori_loop(0, qi // 2, body, 0)

    @pl.when(qi % 2 == 1)
    def _():
        softmax_pv(qi - 1, 0, False)
        scores(qi, 1)
        softmax_pv(qi, 1, True)

    @pl.when(qi % 2 == 0)
    def _():
        softmax_pv(qi, 0, True)
    out = None
    for hh in range(hp_heads):
        o = acc_scr[hh, :, :w] / acc_scr[hh, :, w:]
        out = o if out is None else jnp.where(lane >= hh * d, o, out)
    o_ref[...] = out.astype(o_ref.dtype)


def fox_attention(proj, fc, *, batch, seq):
    T = proj.shape[0]
    tq = min(512, seq)
    nq = seq // tq
    hp_heads = 2
    n_hp = FOX_HEADS // hp_heads
    qblk, kblk, vblk = COLB["f_q"] // 128, COLB["f_k"] // 128, COLB["f_v"] // 128
    kern = functools.partial(_fox_kernel, tq=tq, hp_heads=hp_heads)
    return pl.pallas_call(
        kern,
        grid=(batch, n_hp, nq),
        scratch_shapes=[pltpu.VMEM((hp_heads, tq, hp_heads * HEAD_DIM), F32),
                        pltpu.VMEM((hp_heads, tq, 2 * hp_heads * HEAD_DIM), F32),
                        pltpu.VMEM((2, hp_heads, tq, tq), F32),
                        pltpu.VMEM((seq, 2 * hp_heads * HEAD_DIM), BF16)],
        in_specs=[
            pl.BlockSpec((tq, 128), lambda b, hp, qi: (b * nq + qi, qblk + hp)),
            pl.BlockSpec((seq, 128), lambda b, hp, qi: (b, kblk + hp)),
            pl.BlockSpec((seq, 128), lambda b, hp, qi: (b, vblk + hp)),
            pl.BlockSpec((tq, 128), lambda b, hp, qi: (b * nq + qi, 0)),
            pl.BlockSpec((seq, 128), lambda b, hp, qi: (b, 0)),
        ],
        out_specs=pl.BlockSpec((tq, 128), lambda b, hp, qi: (b * nq + qi, hp)),
        out_shape=jax.ShapeDtypeStruct((T, BRANCH_WIDTH), BF16),
        compiler_params=_cparams("parallel", "parallel", "arbitrary"),
        name="fox_attention",
    )(proj, proj, proj, fc, fc)


def _split2(x):
    hi = x.astype(BF16)
    return hi, (x - hi.astype(F32)).astype(BF16)


def _split3(x):
    hi = x.astype(BF16)
    r = x - hi.astype(F32)
    mid = r.astype(BF16)
    return hi, mid, (r - mid.astype(F32)).astype(BF16)


def _rotate_half(x, neg_first_half):
    n = x.shape[-1]
    half = HEAD_DIM // 2
    fwd = pltpu.roll(x, half, 1)
    bwd = pltpu.roll(x, n - half, 1)
    return jnp.where(neg_first_half, -bwd, fwd)


def _linear_attn_kernel(*refs, mode, cb):
    if mode == "gla":
        (q_ref, k_ref, v_ref, gate_ref, small_ref, wg_ref, bg_ref, ng_ref, o_ref, st_ref) = refs
    else:
        (q_ref, k_ref, v_ref, gate_ref, lg_ref, cos_ref, sin_ref, gw_ref, gb_ref, o_ref, st_ref) = refs
    H, DK, DV, C = GLA_HEADS, GLA_DK, GLA_DV, CHUNK
    nch = cb // C
    W2 = 2 * DK
    HW = H * DK
    dn_lanes = (((1,), (1,)), ((), ()))
    dn_rows = (((0,), (0,)), ((), ()))

    @pl.when(pl.program_id(1) == 0)
    def _():
        st_ref[...] = jnp.zeros_like(st_ref)

    q = q_ref[...].astype(F32)
    k = k_ref[...].astype(F32)
    if mode == "gla":
        glr = small_ref[...][:, FOX_HEADS:FOX_HEADS + GLA_GATE_RANK]
        a_hi, a_lo = _split2(glr)
        w_hi, w_lo = _split2(wg_ref[...])
        z = (jnp.dot(a_hi, w_hi, preferred_element_type=F32) + jnp.dot(a_hi, w_lo, preferred_element_type=F32)
             + jnp.dot(a_lo, w_hi, preferred_element_type=F32)) + bg_ref[...]
        ld = _log_sigmoid(z) / GLA_TAU
        ld_w = jnp.concatenate([ld[c * C:(c + 1) * C] for c in range(nch)], axis=1)
        ri = lax.broadcasted_iota(jnp.int32, (C, 3 * C), 0)
        ci = lax.broadcasted_iota(jnp.int32, (C, 3 * C), 1)
        tri3 = jnp.where((ci % C) <= ri, 1.0, 0.0).astype(BF16)
        cum_w = jnp.dot(tri3, jnp.concatenate(_split3(ld_w), axis=0), preferred_element_type=F32)
        cum = jnp.concatenate([cum_w[:, c * HW:(c + 1) * HW] for c in range(nch)], axis=0)
        lasts = [cum_w[C - 1:C, c * HW:(c + 1) * HW] for c in range(nch)]
        last_b = jnp.concatenate([jnp.broadcast_to(l, (C, HW)) for l in lasts], axis=0)
    else:
        lane = lax.broadcasted_iota(jnp.int32, (cb, HW), 1)
        first_half = (lane % HEAD_DIM) < (HEAD_DIM // 2)
        cos = cos_ref[...]
        sin = sin_ref[...]
        q = q * cos + _rotate_half(q, first_half) * sin
        k = k * cos + _rotate_half(k, first_half) * sin
        steps = (lax.broadcasted_iota(jnp.int32, (cb, HW), 0) % C + 1).astype(F32)
        cum = steps * lg_ref[...]
        last_b = float(C) * lg_ref[...]
        lasts = [last_b] * nch
    q_in = (q * (DK ** -0.5) * jnp.exp(cum)).astype(BF16)
    k_in = (k * jnp.exp(-cum)).astype(BF16)
    k_st = (k * jnp.exp(last_b - cum)).astype(BF16)
    decs = [jnp.exp(l) for l in lasts]

    rowi = lax.broadcasted_iota(jnp.int32, (2 * C, W2), 0)
    lanei = lax.broadcasted_iota(jnp.int32, (2 * C, W2), 1)
    own = (rowi >= C) == (lanei >= DK)
    bd_causal = own & ((lanei % C) <= (rowi % C))
    zero_bf = jnp.zeros((2 * C, W2), BF16)

    def pair_ops(c, p):
        rows = slice(c * C, (c + 1) * C)
        ls = slice(p * W2, (p + 1) * W2)
        qp, kp, ksp = q_in[rows, ls], k_in[rows, ls], k_st[rows, ls]
        qstack = jnp.where(own, jnp.concatenate([qp, qp], axis=0), zero_bf)
        ksstack = jnp.where(own, jnp.concatenate([ksp, ksp], axis=0), zero_bf)
        k2 = jnp.concatenate([kp, kp], axis=0)
        vstack = jnp.concatenate([v_ref[rows, (2 * p) * DV:(2 * p + 1) * DV],
                                  v_ref[rows, (2 * p + 1) * DV:(2 * p + 2) * DV]], axis=0)
        sw = lax.dot_general(qstack, k2, dn_lanes, preferred_element_type=F32)
        sc = jnp.where(bd_causal, sw, 0.0).astype(BF16)
        o_intra = jnp.dot(sc, vstack, preferred_element_type=F32)
        upd = lax.dot_general(vstack, ksstack, dn_rows, preferred_element_type=F32)
        return qstack, o_intra, upd

    pre = [[pair_ops(c, p) for p in range(H // 2)] for c in range(nch)]
    state = [st_ref[p] for p in range(H // 2)]
    for c in range(nch):
        rows = slice(c * C, (c + 1) * C)
        for p in range(H // 2):
            qstack, o_intra, upd = pre[c][p]
            o = o_intra + lax.dot_general(qstack, state[p].astype(BF16), dn_lanes, preferred_element_type=F32)
            state[p] = state[p] * decs[c][:, p * W2:(p + 1) * W2] + upd
            for hh in range(2):
                hd = 2 * p + hh
                vs = slice(hd * DV, (hd + 1) * DV)
                oh = o[hh * C:(hh + 1) * C]
                gate = _silu(gate_ref[rows, vs])
                if mode == "gla":
                    y = oh * lax.rsqrt(jnp.mean(oh * oh, axis=-1, keepdims=True) + EPS) * ng_ref[...]
                else:
                    mu = jnp.mean(oh, axis=-1, keepdims=True)
                    var = jnp.mean(jnp.square(oh - mu), axis=-1, keepdims=True)
                    y = (oh - mu) * lax.rsqrt(var + EPS) * gw_ref[:, vs] + gb_ref[:, vs]
                o_ref[rows, vs] = (y * gate).astype(o_ref.dtype)
    for p in range(H // 2):
        st_ref[p] = state[p]


LINEAR_ATTN_BLOCK = {"gla": 1024, "ret": 512}


def linear_attention(proj, projf, params, *, mode, batch, seq):
    T = proj.shape[0]
    cb = min(LINEAR_ATTN_BLOCK[mode], seq)
    nc = seq // cb
    pre = "g" if mode == "gla" else "r"
    qblk, kblk = COLB[pre + "_q"] // 256, COLB[pre + "_k"] // 256
    vblk = COLB[pre + "_v"] // 512
    gblk = COLF["g_r" if mode == "gla" else "r_g"] // 512
    row = lambda b, c: b * nc + c
    in_specs = [
        pl.BlockSpec((cb, 256), lambda b, c: (row(b, c), qblk)),
        pl.BlockSpec((cb, 256), lambda b, c: (row(b, c), kblk)),
        pl.BlockSpec((cb, 512), lambda b, c: (row(b, c), vblk)),
        pl.BlockSpec((cb, 512), lambda b, c: (row(b, c), gblk)),
    ]
    args = [proj, proj, proj, projf]
    if mode == "gla":
        wg, bg, ng = params
        in_specs += [
            pl.BlockSpec((cb, 128), lambda b, c: (row(b, c), SMALL_COL // 128)),
            pl.BlockSpec(wg.shape, lambda b, c: (0, 0)),
            pl.BlockSpec((1, bg.shape[-1]), lambda b, c: (0, 0)),
            pl.BlockSpec((1, ng.shape[-1]), lambda b, c: (0, 0)),
        ]
        args += [projf, wg, bg.reshape(1, -1), ng.reshape(1, -1)]
    else:
        lg, cos, sin, gw, gb = params
        in_specs += [
            pl.BlockSpec((1, lg.shape[-1]), lambda b, c: (0, 0)),
            pl.BlockSpec((cb, 256), lambda b, c: (c, 0)),
            pl.BlockSpec((cb, 256), lambda b, c: (c, 0)),
            pl.BlockSpec((1, gw.shape[-1]), lambda b, c: (0, 0)),
            pl.BlockSpec((1, gb.shape[-1]), lambda b, c: (0, 0)),
        ]
        args += [lg.reshape(1, -1), cos, sin, gw.reshape(1, -1), gb.reshape(1, -1)]
    return pl.pallas_call(
        functools.partial(_linear_attn_kernel, mode=mode, cb=cb),
        grid=(batch, nc),
        in_specs=in_specs,
        out_specs=pl.BlockSpec((cb, 512), lambda b, c: (row(b, c), 0)),
        out_shape=jax.ShapeDtypeStruct((T, BRANCH_WIDTH), BF16),
        scratch_shapes=[pltpu.VMEM((GLA_HEADS // 2, GLA_DV, 2 * GLA_DK), F32)],
        compiler_params=_cparams("parallel", "arbitrary"),
        name="linear_attention_" + mode,
    )(*args)


def _retention_tables(seq):
    half = RET_DK // 2
    inv = RET_THETA_BASE ** (-jnp.arange(half, dtype=F32) / half)
    ang = jnp.arange(seq).astype(F32)[:, None] * inv[None, :]
    cos = jnp.tile(jnp.cos(ang), (1, 2 * RET_HEADS))
    sin = jnp.tile(jnp.sin(ang), (1, 2 * RET_HEADS))
    log_gamma = jnp.log1p(-jnp.exp2(-5.0 - jnp.arange(RET_HEADS, dtype=F32)))
    return jnp.repeat(log_gamma, RET_DK), cos, sin


def _merge_kernel(u_ref, oa_ref, ob_ref, oc_ref, od_ref, wm_ref, wb_ref, bm_ref, m_ref):
    u = u_ref[...]
    acc = None
    for i, o_ref in enumerate((oa_ref, ob_ref, oc_ref, od_ref)):
        z = jnp.dot(u, wm_ref[i].astype(BF16), preferred_element_type=F32) + bm_ref[i]
        pr = jnp.dot(o_ref[...], wb_ref[i].astype(BF16), preferred_element_type=F32)
        t = jax.nn.sigmoid(z) * pr
        acc = t if acc is None else acc + t
    m_ref[...] = acc.astype(m_ref.dtype)


def merge_branches(u, branches, wm, wb, bm, *, layer):
    T, D = u.shape
    tm, tn = min(1024, T), min(256, D)
    ospec = pl.BlockSpec((tm, BRANCH_WIDTH), lambda i, j: (i, 0))
    return pl.pallas_call(
        _merge_kernel,
        grid=(T // tm, D // tn),
        in_specs=[
            pl.BlockSpec((tm, D), lambda i, j: (i, 0)),
            ospec, ospec, ospec, ospec,
            pl.BlockSpec((None, N_BRANCH, D, tn), lambda i, j: (layer, 0, 0, j)),
            pl.BlockSpec((None, N_BRANCH, BRANCH_WIDTH, tn), lambda i, j: (layer, 0, 0, j)),
            pl.BlockSpec((None, N_BRANCH, 1, tn), lambda i, j: (layer, 0, 0, j)),
        ],
        out_specs=pl.BlockSpec((tm, tn), lambda i, j: (i, j)),
        out_shape=jax.ShapeDtypeStruct((T, D), BF16),
        compiler_params=_cparams("parallel", "arbitrary"),
        name="merge_branches",
    )(u, *branches, wm, wb, bm.reshape(bm.shape[0], N_BRANCH, 1, D))


def _out_proj_kernel(h_ref, m_ref, w_ref, mod_ref, o_ref, *, mrow):
    m = m_ref[...]
    for c0 in range(0, o_ref.shape[1], PROJ_COL_CHUNK):
        cols = slice(c0, c0 + PROJ_COL_CHUNK)
        y = jnp.dot(m, w_ref[:, cols], preferred_element_type=F32)
        o_ref[:, cols] = h_ref[:, cols] + mod_ref[mrow:mrow + 1, cols] * y


def out_proj(h, merged, w, mod, *, layer, seq, mrow):
    T, D = h.shape
    tm = min(512, seq)
    return pl.pallas_call(
        functools.partial(_out_proj_kernel, mrow=mrow),
        grid=(T // tm,),
        in_specs=[
            pl.BlockSpec((tm, D), lambda i: (i, 0)),
            pl.BlockSpec((tm, D), lambda i: (i, 0)),
            pl.BlockSpec((None, D, D), lambda i: (layer, 0, 0), pipeline_mode=pl.Buffered(1)),
            pl.BlockSpec((None, N_MOD, D), lambda i: ((i * tm) // seq, 0, 0)),
        ],
        out_specs=pl.BlockSpec((tm, D), lambda i: (i, 0)),
        out_shape=jax.ShapeDtypeStruct((T, D), F32),
        compiler_params=_cparams("parallel"),
        name="out_proj",
    )(h, merged, w, mod)


def _w_in_prep_kernel(w_ref, ob_ref, of_ref):
    for order, o_ref in ((_ORDER_B, ob_ref), (_ORDER_F, of_ref)):
        off = 0
        for name in order:
            start, width = _REF_COLS[name]
            o_ref[off:off + width, :] = w_ref[start:start + width, :].astype(o_ref.dtype)
            off += width
        if off < o_ref.shape[0]:
            o_ref[off:, :] = jnp.zeros((o_ref.shape[0] - off, o_ref.shape[1]), o_ref.dtype)


def permuted_w_in(w_in):
    L, D, C = w_in.shape
    td = min(256, D)
    return pl.pallas_call(
        _w_in_prep_kernel,
        grid=(L, D // td),
        in_specs=[pl.BlockSpec((None, C, td), lambda l, r: (l, 0, r))],
        out_specs=[pl.BlockSpec((None, PROJ_B_COLS, td), lambda l, r: (l, 0, r)),
                   pl.BlockSpec((None, PROJ_F_COLS, td), lambda l, r: (l, 0, r))],
        out_shape=[jax.ShapeDtypeStruct((L, PROJ_B_COLS, D), BF16), jax.ShapeDtypeStruct((L, PROJ_F_COLS, D), BF16)],
        compiler_params=_cparams("parallel", "parallel"),
        name="w_in_prep",
    )(jnp.swapaxes(w_in, 1, 2))


def kernel(x, c, w_ada, b_ada, norm_g, ffn_w_gate, ffn_w_up, ffn_w_down, w_in, fox_b_forget, attn_sinks,
           gla_w_gate, gla_b_gate, gla_norm_g, ret_gn_w, ret_gn_b, w_branch, w_merge, b_merge, w_out,
           final_norm_g):
    B, S, D = x.shape
    L = w_ada.shape[0]
    T = B * S
    mod_all = ada_modulation(c, w_ada, b_ada)
    log_gamma, cos, sin = _retention_tables(S)
    w_out_bf = w_out.astype(BF16)
    w_in_b, w_in_f = permuted_w_in(w_in)
    h = x.reshape(T, D)
    for l in range(L):
        mod = mod_all[l]
        h = ffn_block(h, mod, norm_g[l, 0], ffn_w_gate, ffn_w_up, ffn_w_down, layer=l, which=0, seq=S,
                      mrow=0, res_scale=0.5)
        proj, projf, u = norm_proj(h, mod, norm_g[l, 1], w_in_b, w_in_f, layer=l, seq=S, mrow=3)
        o_a = swa_attention(proj, attn_sinks[l], batch=B, seq=S)
        fc = fox_forget_cumsum(projf, fox_b_forget[l], batch=B, seq=S)
        o_b = fox_attention(proj, fc, batch=B, seq=S)
        o_c = linear_attention(proj, projf, (gla_w_gate[l], gla_b_gate[l], gla_norm_g[l]), mode="gla",
                               batch=B, seq=S)
        o_d = linear_attention(proj, projf, (log_gamma, cos, sin, ret_gn_w[l], ret_gn_b[l]), mode="ret",
                               batch=B, seq=S)
        merged = merge_branches(u, (o_a, o_b, o_c, o_d), w_merge, w_branch, b_merge, layer=l)
        h = out_proj(h, merged, w_out_bf, mod, layer=l, seq=S, mrow=5)
        h = ffn_block(h, mod, norm_g[l, 2], ffn_w_gate, ffn_w_up, ffn_w_down, layer=l, which=1, seq=S,
                      mrow=6, res_scale=0.5, final_g=final_norm_g if l == L - 1 else None)
    return h.reshape(B, S, D)
```

```python
import functools

import jax
import jax.numpy as jnp
from jax import lax
from jax.experimental import pallas as pl
from jax.experimental.pallas import tpu as pltpu

HEAD_DIM = 64
SWA_HEADS = 8
SWA_KV_HEADS = 2
WINDOW = 128
FOX_HEADS = 8
GLA_HEADS = 4
GLA_DK = 64
GLA_DV = 128
GLA_GATE_RANK = 16
GLA_TAU = 16.0
RET_HEADS = 4
RET_DK = 64
RET_DV = 128
RET_THETA_BASE = 10000.0
CHUNK = 64
BRANCH_WIDTH = 512
N_BRANCH = 4
N_MOD = 9
EPS = 1e-6

BF16 = jnp.bfloat16
F32 = jnp.float32

VMEM_LIMIT_BYTES = 60 * 1024 * 1024

_REF_COLS = {}
_off = 0
for _name, _size in (
        ("a_q", 512), ("a_k", 128), ("a_v", 128),
        ("f_q", 512), ("f_k", 512), ("f_v", 512), ("f_f", 8),
        ("g_q", 256), ("g_k", 256), ("g_v", 512), ("g_lr", 16), ("g_r", 512),
        ("r_q", 256), ("r_k", 256), ("r_v", 512), ("r_g", 512)):
    _REF_COLS[_name] = (_off, _size)
    _off += _size
IN_COLS = _off
for _name in ("a_k", "a_v"):
    for _i in range(SWA_KV_HEADS):
        _REF_COLS[_name + str(_i)] = (_REF_COLS[_name][0] + _i * HEAD_DIM, HEAD_DIM)


def _layout(order):
    col, off = {}, 0
    for name in order:
        col.setdefault(name, off)
        off += _REF_COLS[name][1]
    return col


_ORDER_B = ("a_q", "f_q", "f_k", "f_v", "g_v", "r_v", "g_q", "g_k", "r_q", "r_k",
            "a_k0", "a_k0", "a_k1", "a_k1", "a_v0", "a_v0", "a_v1", "a_v1")
_ORDER_F = ("g_r", "r_g", "f_f", "g_lr")
COLB = _layout(_ORDER_B)
COLF = _layout(_ORDER_F)
PROJ_B_COLS = 4608
PROJ_F_COLS = 1152
SMALL_COL = COLF["f_f"]


def _cparams(*sem):
    return pltpu.CompilerParams(dimension_semantics=sem, vmem_limit_bytes=VMEM_LIMIT_BYTES)


def _tile(n, pref):
    t = (min(pref, n) // 128) * 128
    while t >= 128:
        if n % t == 0:
            return t
        t -= 128
    return n


NORM_ROWS = 64


def _adanorm_into(h_ref, g_ref, mod_ref, mrow, u_ref, copy_ref=None):
    gs = g_ref[...] * (1.0 + mod_ref[mrow + 1:mrow + 2, :])
    shift = mod_ref[mrow:mrow + 1, :]
    rows = min(NORM_ROWS, h_ref.shape[0])

    def body(r, carry):
        r0 = pl.multiple_of(r * rows, rows)
        x = h_ref[pl.ds(r0, rows), :]
        ms = jnp.mean(x * x, axis=-1, keepdims=True)
        u_ref[pl.ds(r0, rows), :] = (x * lax.rsqrt(ms + EPS) * gs + shift).astype(u_ref.dtype)
        if copy_ref is not None:
            copy_ref[pl.ds(r0, rows), :] = x
        return carry

    lax.fori_loop(0, h_ref.shape[0] // rows, body, 0)


def _log_sigmoid(x):
    return jnp.minimum(x, 0.0) - jnp.log1p(jnp.exp(-jnp.abs(x)))


def _silu(x):
    return x * jax.nn.sigmoid(x)


ADA_ROWS = 16


def _ada_kernel(c_ref, w_ref, b_ref, o_ref):
    cond = _silu(c_ref[...])
    r = jnp.dot(jnp.concatenate(_split3(cond), axis=0), w_ref[...].astype(BF16), preferred_element_type=F32)
    bp = ADA_ROWS
    o_ref[...] = (r[:bp] + r[bp:2 * bp]) + r[2 * bp:] + b_ref[...]


def ada_modulation(c, w_ada, b_ada):
    L, D, N = w_ada.shape
    B = c.shape[0]
    BP = ADA_ROWS
    cp = jnp.zeros((BP, D), F32).at[:B].set(c)
    tn = _tile(N, 1024)
    out = pl.pallas_call(
        _ada_kernel,
        grid=(L, N // tn),
        in_specs=[
            pl.BlockSpec((BP, D), lambda l, j: (0, 0)),
            pl.BlockSpec((None, D, tn), lambda l, j: (l, 0, j)),
            pl.BlockSpec((None, 1, tn), lambda l, j: (l, 0, j)),
        ],
        out_specs=pl.BlockSpec((None, BP, tn), lambda l, j: (l, 0, j)),
        out_shape=jax.ShapeDtypeStruct((L, BP, N), F32),
        compiler_params=_cparams("parallel", "parallel"),
        name="ada_modulation",
    )(cp, w_ada, b_ada.reshape(L, 1, N))
    return out[:, :B].reshape(L, B, N_MOD, D)


def _ffn_kernel(h_ref, mod_ref, g_ref, wg_ref, wu_ref, wd_ref, *rest, mrow, res_scale, final_norm):
    if final_norm:
        fg_ref, o_ref, u_scr = rest
    else:
        o_ref, u_scr = rest
    f = pl.program_id(1)

    @pl.when(f == 0)
    def _():
        _adanorm_into(h_ref, g_ref, mod_ref, mrow, u_scr, copy_ref=o_ref)

    u = u_scr[...]
    a = jnp.dot(u, wg_ref[...].astype(BF16), preferred_element_type=F32)
    b = jnp.dot(u, wu_ref[...].astype(BF16), preferred_element_type=F32)
    mid = (_silu(a) * b).astype(BF16)
    gate = res_scale * mod_ref[mrow + 2:mrow + 3, :]
    o_ref[...] += jnp.dot(mid, (wd_ref[...] * gate).astype(BF16), preferred_element_type=F32)

    if final_norm:
        @pl.when(f == pl.num_programs(1) - 1)
        def _():
            rows = min(NORM_ROWS, h_ref.shape[0])

            def body(r, carry):
                r0 = pl.multiple_of(r * rows, rows)
                y = o_ref[pl.ds(r0, rows), :]
                y = y * lax.rsqrt(jnp.mean(y * y, axis=-1, keepdims=True) + EPS) * fg_ref[...]
                o_ref[pl.ds(r0, rows), :] = y
                return carry

            lax.fori_loop(0, h_ref.shape[0] // rows, body, 0)


def ffn_block(h, mod, g, wg, wu, wd, *, layer, which, seq, mrow, res_scale, final_g=None):
    T, D = h.shape
    F = wg.shape[-1]
    tm = min(1024, seq)
    tf = _tile(F, 256)
    kern = functools.partial(_ffn_kernel, mrow=mrow, res_scale=res_scale, final_norm=final_g is not None)
    in_specs = [
        pl.BlockSpec((tm, D), lambda i, f: (i, 0)),
        pl.BlockSpec((None, N_MOD, D), lambda i, f: ((i * tm) // seq, 0, 0)),
        pl.BlockSpec((1, D), lambda i, f: (0, 0)),
        pl.BlockSpec((None, None, D, tf), lambda i, f: (layer, which, 0, f)),
        pl.BlockSpec((None, None, D, tf), lambda i, f: (layer, which, 0, f)),
        pl.BlockSpec((None, None, tf, D), lambda i, f: (layer, which, f, 0)),
    ]
    args = [h, mod, g.reshape(1, D), wg, wu, wd]
    if final_g is not None:
        in_specs.append(pl.BlockSpec((1, D), lambda i, f: (0, 0)))
        args.append(final_g.reshape(1, D))
    return pl.pallas_call(
        kern,
        grid=(T // tm, F // tf),
        in_specs=in_specs,
        out_specs=pl.BlockSpec((tm, D), lambda i, f: (i, 0)),
        out_shape=jax.ShapeDtypeStruct((T, D), F32),
        scratch_shapes=[pltpu.VMEM((tm, D), BF16)],
        compiler_params=_cparams("parallel", "arbitrary"),
        name="ffn_block",
    )(*args)


PROJ_COL_CHUNK = 512


def _norm_proj_kernel(h_ref, mod_ref, g_ref, wb_ref, wf_ref, pb_ref, pf_ref, u_ref, *, mrow):
    _adanorm_into(h_ref, g_ref, mod_ref, mrow, u_ref)
    u = u_ref[...]
    nt = (((1,), (1,)), ((), ()))
    for c0 in range(0, pb_ref.shape[1], PROJ_COL_CHUNK):
        cols = slice(c0, c0 + PROJ_COL_CHUNK)
        pb_ref[:, cols] = lax.dot_general(u, wb_ref[cols, :], nt, preferred_element_type=F32).astype(pb_ref.dtype)
    pf_ref[...] = lax.dot_general(u, wf_ref[...], nt, preferred_element_type=F32)


def norm_proj(h, mod, g, wb, wf, *, layer, seq, mrow):
    T, D = h.shape
    NB, NF = wb.shape[1], wf.shape[1]
    tm = min(512, seq)
    kern = functools.partial(_norm_proj_kernel, mrow=mrow)
    resident = pl.Buffered(1)
    return pl.pallas_call(
        kern,
        grid=(T // tm,),
        in_specs=[
            pl.BlockSpec((tm, D), lambda i: (i, 0)),
            pl.BlockSpec((None, N_MOD, D), lambda i: ((i * tm) // seq, 0, 0)),
            pl.BlockSpec((1, D), lambda i: (0, 0)),
            pl.BlockSpec((None, NB, D), lambda i: (layer, 0, 0), pipeline_mode=resident),
            pl.BlockSpec((None, NF, D), lambda i: (layer, 0, 0), pipeline_mode=resident),
        ],
        out_specs=[
            pl.BlockSpec((tm, NB), lambda i: (i, 0)),
            pl.BlockSpec((tm, NF), lambda i: (i, 0)),
            pl.BlockSpec((tm, D), lambda i: (i, 0)),
        ],
        out_shape=[jax.ShapeDtypeStruct((T, NB), BF16), jax.ShapeDtypeStruct((T, NF), F32),
                   jax.ShapeDtypeStruct((T, D), BF16)],
        compiler_params=_cparams("parallel"),
        name="norm_proj",
    )(h, mod, g.reshape(1, D), wb, wf)


SWA_BLOCKS_PER_STEP = 4


def _swa_kernel(sink_ref, q_ref, kv_ref, kvp_ref, o_ref):
    n = pl.program_id(1)
    W = WINDOW
    d = HEAD_DIM
    rowi = lax.broadcasted_iota(jnp.int32, (2 * W, 2 * d), 0)
    lanei = lax.broadcasted_iota(jnp.int32, (2 * W, 2 * d), 1)
    own = (rowi >= W) == (lanei >= d)
    first_half = lax.broadcasted_iota(jnp.int32, (W, 2 * d), 1) < d
    qpos = lax.broadcasted_iota(jnp.int32, (2 * W, 2 * W), 0) % W + W
    kpos = lax.broadcasted_iota(jnp.int32, (2 * W, 2 * W), 1)
    rel = qpos - kpos
    band = (rel >= 0) & (rel < W)
    ones = jnp.ones((2 * W, 2 * d), BF16)
    group = SWA_HEADS // SWA_KV_HEADS
    for blk in range(SWA_BLOCKS_PER_STEP):
        rows = slice(blk * W, (blk + 1) * W)
        q = q_ref[rows, :] * (d ** -0.5)
        prev = kvp_ref[...] if blk == 0 else kv_ref[(blk - 1) * W:blk * W, :]
        kv = jnp.concatenate([prev, kv_ref[rows, :]], axis=0)
        mask = band & ((kpos >= W) | (n > 0)) if blk == 0 else band
        for g in range(SWA_HEADS // 2):
            kvh = (2 * g) // group
            k2 = kv[:, kvh * 2 * d:(kvh + 1) * 2 * d]
            v2 = kv[:, (SWA_KV_HEADS + kvh) * 2 * d:(SWA_KV_HEADS + kvh + 1) * 2 * d]
            v_ext = jnp.concatenate([v2, ones], axis=1)
            qp = q[:, g * 2 * d:(g + 1) * 2 * d]
            qstack = jnp.where(own, jnp.concatenate([qp, qp], axis=0), jnp.zeros((2 * W, 2 * d), BF16))
            logits = lax.dot_general(qstack, k2, (((1,), (1,)), ((), ())), preferred_element_type=F32)
            logits = jnp.where(mask, logits, -jnp.inf)
            sink = jnp.where(rowi >= W, sink_ref[2 * g + 1], sink_ref[2 * g])
            m = jnp.maximum(jnp.max(logits, axis=-1, keepdims=True), sink)
            p = jnp.exp(logits - jnp.concatenate([m, m], axis=1)).astype(BF16)
            r = jnp.dot(p, v_ext, preferred_element_type=F32)
            o2 = r[:, :2 * d] / (r[:, 2 * d:] + jnp.exp(sink - m))
            out = jnp.where(first_half, o2[:W], o2[W:])
            o_ref[rows, g * 2 * d:(g + 1) * 2 * d] = out.astype(o_ref.dtype)


def swa_attention(proj, sinks, *, batch, seq):
    T = proj.shape[0]
    nb = seq // WINDOW
    per = SWA_BLOCKS_PER_STEP
    ns = nb // per
    qblk = COLB["a_q"] // 512
    kvblk = COLB["a_k0"] // 512
    return pl.pallas_call(
        _swa_kernel,
        grid=(batch, ns),
        in_specs=[
            pl.BlockSpec(memory_space=pltpu.SMEM),
            pl.BlockSpec((per * WINDOW, 512), lambda b, n: (b * ns + n, qblk)),
            pl.BlockSpec((per * WINDOW, 512), lambda b, n: (b * ns + n, kvblk)),
            pl.BlockSpec((WINDOW, 512), lambda b, n: (b * nb + jnp.maximum(per * n - 1, 0), kvblk)),
        ],
        out_specs=pl.BlockSpec((per * WINDOW, 512), lambda b, n: (b * ns + n, 0)),
        out_shape=jax.ShapeDtypeStruct((T, BRANCH_WIDTH), BF16),
        compiler_params=_cparams("parallel", "arbitrary"),
        name="swa_attention",
    )(sinks, proj, proj, proj)


FOX_CUM_ROWS = 256


def _fox_cum_kernel(x_ref, b_ref, o_ref, *, seq):
    R = min(FOX_CUM_ROWS, seq)
    ri = lax.broadcasted_iota(jnp.int32, (R, 3 * R), 0)
    ci = lax.broadcasted_iota(jnp.int32, (R, 3 * R), 1)
    tri3 = jnp.where((ci % R) <= ri, 1.0, 0.0).astype(BF16)

    def body(i, carry):
        r0 = pl.multiple_of(i * R, R)
        ls = _log_sigmoid(x_ref[pl.ds(r0, R), :] + b_ref[...])
        cum = jnp.dot(tri3, jnp.concatenate(_split3(ls), axis=0), preferred_element_type=F32) + carry
        o_ref[pl.ds(r0, R), :] = cum
        return cum[R - 1:R, :]

    lax.fori_loop(0, seq // R, body, jnp.zeros((1, x_ref.shape[1]), F32))


def fox_forget_cumsum(proj, fox_b, *, batch, seq):
    T = proj.shape[0]
    blk = SMALL_COL // 128
    bias = jnp.zeros((1, 128), F32).at[0, :FOX_HEADS].set(fox_b)
    return pl.pallas_call(
        functools.partial(_fox_cum_kernel, seq=seq),
        grid=(batch,),
        in_specs=[
            pl.BlockSpec((seq, 128), lambda b: (b, blk)),
            pl.BlockSpec((1, 128), lambda b: (0, 0)),
        ],
        out_specs=pl.BlockSpec((seq, 128), lambda b: (b, 0)),
        out_shape=jax.ShapeDtypeStruct((T, 128), F32),
        compiler_params=_cparams("parallel"),
        name="fox_forget_cumsum",
    )(proj, bias)


FOX_KEY_ROWS = 512


def _fox_kernel(q_ref, k_ref, v_ref, fc_ref, fcs_ref, o_ref, m_scr, acc_scr, s_scr, kx_scr, *, tq, hp_heads):
    hp = pl.program_id(1)
    qi = pl.program_id(2)
    tk = tq
    d = HEAD_DIM
    w = hp_heads * d
    seq = k_ref.shape[0]
    e_row = lax.broadcasted_iota(jnp.int32, (w, w), 0)
    e_lane = lax.broadcasted_iota(jnp.int32, (w, w), 1)

    @pl.when(qi == 0)
    def _():
        kr = min(FOX_KEY_ROWS, seq)
        place = [sum(jnp.where((e_row == hp * hp_heads + hh) & (e_lane == 3 * hh + j), 1.0, 0.0)
                     for hh in range(hp_heads)).astype(BF16) for j in range(3)]
        for r0 in range(0, seq, kr):
            rows = slice(r0, r0 + kr)
            ext = sum(jnp.dot(piece, place[j], preferred_element_type=F32)
                      for j, piece in enumerate(_split3(-fcs_ref[rows, :])))
            kx_scr[rows, :w] = k_ref[rows, :]
            kx_scr[rows, w:] = ext.astype(BF16)

    lane = lax.broadcasted_iota(jnp.int32, (tq, w), 1)
    q = q_ref[...] * (d ** -0.5)
    zero = jnp.zeros_like(q)
    qs = [jnp.concatenate([jnp.where((lane >= hh * d) & (lane < (hh + 1) * d), q, zero),
                           jnp.where((lane >= 3 * hh) & (lane < 3 * hh + 3), 1.0, 0.0).astype(BF16)], axis=1)
          for hh in range(hp_heads)]
    m_scr[...] = jnp.full(m_scr.shape, -jnp.inf, F32)
    acc_scr[...] = jnp.zeros(acc_scr.shape, F32)
    fc = fc_ref[...]
    fqs = [jnp.broadcast_to(jnp.sum(jnp.where(lane == hp * hp_heads + hh, fc, 0.0), axis=-1, keepdims=True),
                            (tq, w)) for hh in range(hp_heads)]
    ones = jnp.ones((tk, w), BF16)

    def scores(kb, slot):
        k0 = pl.multiple_of(kb * tk, tk)
        kx = kx_scr[pl.ds(k0, tk), :]
        for hh in range(hp_heads):
            s_scr[slot, hh] = lax.dot_general(qs[hh], kx, (((1,), (1,)), ((), ())), preferred_element_type=F32)

    def softmax_pv(kb, slot, masked):
        k0 = pl.multiple_of(kb * tk, tk)
        v = jnp.concatenate([v_ref[pl.ds(k0, tk), :], ones], axis=1)
        for hh in range(hp_heads):
            t = s_scr[slot, hh]
            if masked:
                row = lax.broadcasted_iota(jnp.int32, (tq, tk), 0)
                col = lax.broadcasted_iota(jnp.int32, (tq, tk), 1)
                t = jnp.where(col <= row, t, -jnp.inf)
            m_old = m_scr[hh]
            m_new = jnp.maximum(m_old, fqs[hh] + jnp.max(t, axis=-1, keepdims=True))
            alpha = jnp.exp(m_old - m_new)
            c = fqs[hh] - m_new
            p = jnp.exp(t + jnp.concatenate([c] * (tk // w), axis=1))
            acc_scr[hh] = (jnp.concatenate([alpha, alpha], axis=1) * acc_scr[hh]
                           + jnp.dot(p.astype(BF16), v, preferred_element_type=F32))
            m_scr[hh] = m_new

    scores(0, 0)

    def body(j, carry):
        kb = 2 * j
        softmax_pv(kb, 0, False)
        scores(kb + 1, 1)
        softmax_pv(kb + 1, 1, False)
        scores(kb + 2, 0)
        return carry

    lax.fori_loop(0, qi // 2, body, 0)

    @pl.when(qi % 2 == 1)
    def _():
        softmax_pv(qi - 1, 0, False)
        scores(qi, 1)
        softmax_pv(qi, 1, True)

    @pl.when(qi % 2 == 0)
    def _():
        softmax_pv(qi, 0, True)
    out = None
    for hh in range(hp_heads):
        o = acc_scr[hh, :, :w] / acc_scr[hh, :, w:]
        out = o if out is None else jnp.where(lane >= hh * d, o, out)
    o_ref[...] = out.astype(o_ref.dtype)


def fox_attention(proj, fc, *, batch, seq):
    T = proj.shape[0]
    tq = min(512, seq)
    nq = seq // tq
    hp_heads = 2
    n_hp = FOX_HEADS // hp_heads
    qblk, kblk, vblk = COLB["f_q"] // 128, COLB["f_k"] // 128, COLB["f_v"] // 128
    kern = functools.partial(_fox_kernel, tq=tq, hp_heads=hp_heads)
    return pl.pallas_call(
        kern,
        grid=(batch, n_hp, nq),
        scratch_shapes=[pltpu.VMEM((hp_heads, tq, hp_heads * HEAD_DIM), F32),
                        pltpu.VMEM((hp_heads, tq, 2 * hp_heads * HEAD_DIM), F32),
                        pltpu.VMEM((2, hp_heads, tq, tq), F32),
                        pltpu.VMEM((seq, 2 * hp_heads * HEAD_DIM), BF16)],
        in_specs=[
            pl.BlockSpec((tq, 128), lambda b, hp, qi: (b * nq + qi, qblk + hp)),
            pl.BlockSpec((seq, 128), lambda b, hp, qi: (b, kblk + hp)),
            pl.BlockSpec((seq, 128), lambda b, hp, qi: (b, vblk + hp)),
            pl.BlockSpec((tq, 128), lambda b, hp, qi: (b * nq + qi, 0)),
            pl.BlockSpec((seq, 128), lambda b, hp, qi: (b, 0)),
        ],
        out_specs=pl.BlockSpec((tq, 128), lambda b, hp, qi: (b * nq + qi, hp)),
        out_shape=jax.ShapeDtypeStruct((T, BRANCH_WIDTH), BF16),
        compiler_params=_cparams("parallel", "parallel", "arbitrary"),
        name="fox_attention",
    )(proj, proj, proj, fc, fc)


def _split2(x):
    hi = x.astype(BF16)
    return hi, (x - hi.astype(F32)).astype(BF16)


def _split3(x):
    hi = x.astype(BF16)
    r = x - hi.astype(F32)
    mid = r.astype(BF16)
    return hi, mid, (r - mid.astype(F32)).astype(BF16)


def _rotate_half(x, neg_first_half):
    n = x.shape[-1]
    half = HEAD_DIM // 2
    fwd = pltpu.roll(x, half, 1)
    bwd = pltpu.roll(x, n - half, 1)
    return jnp.where(neg_first_half, -bwd, fwd)


def _linear_attn_kernel(*refs, mode, cb):
    if mode == "gla":
        (q_ref, k_ref, v_ref, gate_ref, small_ref, wg_ref, bg_ref, ng_ref, o_ref, st_ref) = refs
    else:
        (q_ref, k_ref, v_ref, gate_ref, lg_ref, cos_ref, sin_ref, gw_ref, gb_ref, o_ref, st_ref) = refs
    H, DK, DV, C = GLA_HEADS, GLA_DK, GLA_DV, CHUNK
    nch = cb // C
    W2 = 2 * DK
    HW = H * DK
    dn_lanes = (((1,), (1,)), ((), ()))
    dn_rows = (((0,), (0,)), ((), ()))

    @pl.when(pl.program_id(1) == 0)
    def _():
        st_ref[...] = jnp.zeros_like(st_ref)

    q = q_ref[...].astype(F32)
    k = k_ref[...].astype(F32)
    if mode == "gla":
        glr = small_ref[...][:, FOX_HEADS:FOX_HEADS + GLA_GATE_RANK]
        a_hi, a_lo = _split2(glr)
        w_hi, w_lo = _split2(wg_ref[...])
        z = (jnp.dot(a_hi, w_hi, preferred_element_type=F32) + jnp.dot(a_hi, w_lo, preferred_element_type=F32)
             + jnp.dot(a_lo, w_hi, preferred_element_type=F32)) + bg_ref[...]
        ld = _log_sigmoid(z) / GLA_TAU
        ld_w = jnp.concatenate([ld[c * C:(c + 1) * C] for c in range(nch)], axis=1)
        ri = lax.broadcasted_iota(jnp.int32, (C, 3 * C), 0)
        ci = lax.broadcasted_iota(jnp.int32, (C, 3 * C), 1)
        tri3 = jnp.where((ci % C) <= ri, 1.0, 0.0).astype(BF16)
        cum_w = jnp.dot(tri3, jnp.concatenate(_split3(ld_w), axis=0), preferred_element_type=F32)
        cum = jnp.concatenate([cum_w[:, c * HW:(c + 1) * HW] for c in range(nch)], axis=0)
        lasts = [cum_w[C - 1:C, c * HW:(c + 1) * HW] for c in range(nch)]
        last_b = jnp.concatenate([jnp.broadcast_to(l, (C, HW)) for l in lasts], axis=0)
    else:
        lane = lax.broadcasted_iota(jnp.int32, (cb, HW), 1)
        first_half = (lane % HEAD_DIM) < (HEAD_DIM // 2)
        cos = cos_ref[...]
        sin = sin_ref[...]
        q = q * cos + _rotate_half(q, first_half) * sin
        k = k * cos + _rotate_half(k, first_half) * sin
        steps = (lax.broadcasted_iota(jnp.int32, (cb, HW), 0) % C + 1).astype(F32)
        cum = steps * lg_ref[...]
        last_b = float(C) * lg_ref[...]
        lasts = [last_b] * nch
    q_in = (q * (DK ** -0.5) * jnp.exp(cum)).astype(BF16)
    k_in = (k * jnp.exp(-cum)).astype(BF16)
    k_st = (k * jnp.exp(last_b - cum)).astype(BF16)
    decs = [jnp.exp(l) for l in lasts]

    rowi = lax.broadcasted_iota(jnp.int32, (2 * C, W2), 0)
    lanei = lax.broadcasted_iota(jnp.int32, (2 * C, W2), 1)
    own = (rowi >= C) == (lanei >= DK)
    bd_causal = own & ((lanei % C) <= (rowi % C))
    zero_bf = jnp.zeros((2 * C, W2), BF16)

    def pair_ops(c, p):
        rows = slice(c * C, (c + 1) * C)
        ls = slice(p * W2, (p + 1) * W2)
        qp, kp, ksp = q_in[rows, ls], k_in[rows, ls], k_st[rows, ls]
        qstack = jnp.where(own, jnp.concatenate([qp, qp], axis=0), zero_bf)
        ksstack = jnp.where(own, jnp.concatenate([ksp, ksp], axis=0), zero_bf)
        k2 = jnp.concatenate([kp, kp], axis=0)
        vstack = jnp.concatenate([v_ref[rows, (2 * p) * DV:(2 * p + 1) * DV],
                                  v_ref[rows, (2 * p + 1) * DV:(2 * p + 2) * DV]], axis=0)
        sw = lax.dot_general(qstack, k2, dn_lanes, preferred_element_type=F32)
        sc = jnp.where(bd_causal, sw, 0.0).astype(BF16)
        o_intra = jnp.dot(sc, vstack, preferred_element_type=F32)
        upd = lax.dot_general(vstack, ksstack, dn_rows, preferred_element_type=F32)
        return qstack, o_intra, upd

    pre = [[pair_ops(c, p) for p in range(H // 2)] for c in range(nch)]
    state = [st_ref[p] for p in range(H // 2)]
    for c in range(nch):
        rows = slice(c * C, (c + 1) * C)
        for p in range(H // 2):
            qstack, o_intra, upd = pre[c][p]
            o = o_intra + lax.dot_general(qstack, state[p].astype(BF16), dn_lanes, preferred_element_type=F32)
            state[p] = state[p] * decs[c][:, p * W2:(p + 1) * W2] + upd
            for hh in range(2):
                hd = 2 * p + hh
                vs = slice(hd * DV, (hd + 1) * DV)
                oh = o[hh * C:(hh + 1) * C]
                gate = _silu(gate_ref[rows, vs])
                if mode == "gla":
                    y = oh * lax.rsqrt(jnp.mean(oh * oh, axis=-1, keepdims=True) + EPS) * ng_ref[...]
                else:
                    mu = jnp.mean(oh, axis=-1, keepdims=True)
                    var = jnp.mean(jnp.square(oh - mu), axis=-1, keepdims=True)
                    y = (oh - mu) * lax.rsqrt(var + EPS) * gw_ref[:, vs] + gb_ref[:, vs]
                o_ref[rows, vs] = (y * gate).astype(o_ref.dtype)
    for p in range(H // 2):
        st_ref[p] = state[p]


LINEAR_ATTN_BLOCK = {"gla": 1024, "ret": 512}


def linear_attention(proj, projf, params, *, mode, batch, seq):
    T = proj.shape[0]
    cb = min(LINEAR_ATTN_BLOCK[mode], seq)
    nc = seq // cb
    pre = "g" if mode == "gla" else "r"
    qblk, kblk = COLB[pre + "_q"] // 256, COLB[pre + "_k"] // 256
    vblk = COLB[pre + "_v"] // 512
    gblk = COLF["g_r" if mode == "gla" else "r_g"] // 512
    row = lambda b, c: b * nc + c
    in_specs = [
        pl.BlockSpec((cb, 256), lambda b, c: (row(b, c), qblk)),
        pl.BlockSpec((cb, 256), lambda b, c: (row(b, c), kblk)),
        pl.BlockSpec((cb, 512), lambda b, c: (row(b, c), vblk)),
        pl.BlockSpec((cb, 512), lambda b, c: (row(b, c), gblk)),
    ]
    args = [proj, proj, proj, projf]
    if mode == "gla":
        wg, bg, ng = params
        in_specs += [
            pl.BlockSpec((cb, 128), lambda b, c: (row(b, c), SMALL_COL // 128)),
            pl.BlockSpec(wg.shape, lambda b, c: (0, 0)),
            pl.BlockSpec((1, bg.shape[-1]), lambda b, c: (0, 0)),
            pl.BlockSpec((1, ng.shape[-1]), lambda b, c: (0, 0)),
        ]
        args += [projf, wg, bg.reshape(1, -1), ng.reshape(1, -1)]
    else:
        lg, cos, sin, gw, gb = params
        in_specs += [
            pl.BlockSpec((1, lg.shape[-1]), lambda b, c: (0, 0)),
            pl.BlockSpec((cb, 256), lambda b, c: (c, 0)),
            pl.BlockSpec((cb, 256), lambda b, c: (c, 0)),
            pl.BlockSpec((1, gw.shape[-1]), lambda b, c: (0, 0)),
            pl.BlockSpec((1, gb.shape[-1]), lambda b, c: (0, 0)),
        ]
        args += [lg.reshape(1, -1), cos, sin, gw.reshape(1, -1), gb.reshape(1, -1)]
    return pl.pallas_call(
        functools.partial(_linear_attn_kernel, mode=mode, cb=cb),
        grid=(batch, nc),
        in_specs=in_specs,
        out_specs=pl.BlockSpec((cb, 512), lambda b, c: (row(b, c), 0)),
        out_shape=jax.ShapeDtypeStruct((T, BRANCH_WIDTH), BF16),
        scratch_shapes=[pltpu.VMEM((GLA_HEADS // 2, GLA_DV, 2 * GLA_DK), F32)],
        compiler_params=_cparams("parallel", "arbitrary"),
        name="linear_attention_" + mode,
    )(*args)


def _retention_tables(seq):
    half = RET_DK // 2
    inv = RET_THETA_BASE ** (-jnp.arange(half, dtype=F32) / half)
    ang = jnp.arange(seq).astype(F32)[:, None] * inv[None, :]
    cos = jnp.tile(jnp.cos(ang), (1, 2 * RET_HEADS))
    sin = jnp.tile(jnp.sin(ang), (1, 2 * RET_HEADS))
    log_gamma = jnp.log1p(-jnp.exp2(-5.0 - jnp.arange(RET_HEADS, dtype=F32)))
    return jnp.repeat(log_gamma, RET_DK), cos, sin


def _merge_kernel(u_ref, oa_ref, ob_ref, oc_ref, od_ref, wm_ref, wb_ref, bm_ref, m_ref):
    u = u_ref[...]
    acc = None
    for i, o_ref in enumerate((oa_ref, ob_ref, oc_ref, od_ref)):
        z = jnp.dot(u, wm_ref[i].astype(BF16), preferred_element_type=F32) + bm_ref[i]
        pr = jnp.dot(o_ref[...], wb_ref[i].astype(BF16), preferred_element_type=F32)
        t = jax.nn.sigmoid(z) * pr
        acc = t if acc is None else acc + t
    m_ref[...] = acc.astype(m_ref.dtype)


def merge_branches(u, branches, wm, wb, bm, *, layer):
    T, D = u.shape
    tm, tn = min(1024, T), min(256, D)
    ospec = pl.BlockSpec((tm, BRANCH_WIDTH), lambda i, j: (i, 0))
    return pl.pallas_call(
        _merge_kernel,
        grid=(T // tm, D // tn),
        in_specs=[
            pl.BlockSpec((tm, D), lambda i, j: (i, 0)),
            ospec, ospec, ospec, ospec,
            pl.BlockSpec((None, N_BRANCH, D, tn), lambda i, j: (layer, 0, 0, j)),
            pl.BlockSpec((None, N_BRANCH, BRANCH_WIDTH, tn), lambda i, j: (layer, 0, 0, j)),
            pl.BlockSpec((None, N_BRANCH, 1, tn), lambda i, j: (layer, 0, 0, j)),
        ],
        out_specs=pl.BlockSpec((tm, tn), lambda i, j: (i, j)),
        out_shape=jax.ShapeDtypeStruct((T, D), BF16),
        compiler_params=_cparams("parallel", "arbitrary"),
        name="merge_branches",
    )(u, *branches, wm, wb, bm.reshape(bm.shape[0], N_BRANCH, 1, D))


def _out_proj_kernel(h_ref, m_ref, w_ref, mod_ref, o_ref, *, mrow):
    m = m_ref[...]
    for c0 in range(0, o_ref.shape[1], PROJ_COL_CHUNK):
        cols = slice(c0, c0 + PROJ_COL_CHUNK)
        y = jnp.dot(m, w_ref[:, cols], preferred_element_type=F32)
        o_ref[:, cols] = h_ref[:, cols] + mod_ref[mrow:mrow + 1, cols] * y


def out_proj(h, merged, w, mod, *, layer, seq, mrow):
    T, D = h.shape
    tm = min(512, seq)
    return pl.pallas_call(
        functools.partial(_out_proj_kernel, mrow=mrow),
        grid=(T // tm,),
        in_specs=[
            pl.BlockSpec((tm, D), lambda i: (i, 0)),
            pl.BlockSpec((tm, D), lambda i: (i, 0)),
            pl.BlockSpec((None, D, D), lambda i: (layer, 0, 0), pipeline_mode=pl.Buffered(1)),
            pl.BlockSpec((None, N_MOD, D), lambda i: ((i * tm) // seq, 0, 0)),
        ],
        out_specs=pl.BlockSpec((tm, D), lambda i: (i, 0)),
        out_shape=jax.ShapeDtypeStruct((T, D), F32),
        compiler_params=_cparams("parallel"),
        name="out_proj",
    )(h, merged, w, mod)


def _w_in_prep_kernel(w_ref, ob_ref, of_ref):
    for order, o_ref in ((_ORDER_B, ob_ref), (_ORDER_F, of_ref)):
        off = 0
        for name in order:
            start, width = _REF_COLS[name]
            o_ref[off:off + width, :] = w_ref[start:start + width, :].astype(o_ref.dtype)
            off += width
        if off < o_ref.shape[0]:
            o_ref[off:, :] = jnp.zeros((o_ref.shape[0] - off, o_ref.shape[1]), o_ref.dtype)


def permuted_w_in(w_in):
    L, D, C = w_in.shape
    td = min(256, D)
    return pl.pallas_call(
        _w_in_prep_kernel,
        grid=(L, D // td),
        in_specs=[pl.BlockSpec((None, C, td), lambda l, r: (l, 0, r))],
        out_specs=[pl.BlockSpec((None, PROJ_B_COLS, td), lambda l, r: (l, 0, r)),
                   pl.BlockSpec((None, PROJ_F_COLS, td), lambda l, r: (l, 0, r))],
        out_shape=[jax.ShapeDtypeStruct((L, PROJ_B_COLS, D), BF16), jax.ShapeDtypeStruct((L, PROJ_F_COLS, D), BF16)],
        compiler_params=_cparams("parallel", "parallel"),
        name="w_in_prep",
    )(jnp.swapaxes(w_in, 1, 2))


def kernel(x, c, w_ada, b_ada, norm_g, ffn_w_gate, ffn_w_up, ffn_w_down, w_in, fox_b_forget, attn_sinks,
           gla_w_gate, gla_b_gate, gla_norm_g, ret_gn_w, ret_gn_b, w_branch, w_merge, b_merge, w_out,
           final_norm_g):
    B, S, D = x.shape
    L = w_ada.shape[0]
    T = B * S
    mod_all = ada_modulation(c, w_ada, b_ada)
    log_gamma, cos, sin = _retention_tables(S)
    w_out_bf = w_out.astype(BF16)
    w_in_b, w_in_f = permuted_w_in(w_in)
    h = x.reshape(T, D)
    for l in range(L):
        mod = mod_all[l]
        h = ffn_block(h, mod, norm_g[l, 0], ffn_w_gate, ffn_w_up, ffn_w_down, layer=l, which=0, seq=S,
                      mrow=0, res_scale=0.5)
        proj, projf, u = norm_proj(h, mod, norm_g[l, 1], w_in_b, w_in_f, layer=l, seq=S, mrow=3)
        o_a = swa_attention(proj, attn_sinks[l], batch=B, seq=S)
        fc = fox_forget_cumsum(projf, fox_b_forget[l], batch=B, seq=S)
        o_b = fox_attention(proj, fc, batch=B, seq=S)
        o_c = linear_attention(proj, projf, (gla_w_gate[l], gla_b_gate[l], gla_norm_g[l]), mode="gla",
                               batch=B, seq=S)
        o_d = linear_attention(proj, projf, (log_gamma, cos, sin, ret_gn_w[l], ret_gn_b[l]), mode="ret",
                               batch=B, seq=S)
        merged = merge_branches(u, (o_a, o_b, o_c, o_d), w_merge, w_branch, b_merge, layer=l)
        h = out_proj(h, merged, w_out_bf, mod, layer=l, seq=S, mrow=5)
        h = ffn_block(h, mod, norm_g[l, 2], ffn_w_gate, ffn_w_up, ffn_w_down, layer=l, which=1, seq=S,
                      mrow=6, res_scale=0.5, final_g=final_norm_g if l == L - 1 else None)
    return h.reshape(B, S, D)
```

```python
import functools

import jax
import jax.numpy as jnp
from jax import lax
from jax.experimental import pallas as pl
from jax.experimental.pallas import tpu as pltpu

HEAD_DIM = 64
SWA_HEADS = 8
SWA_KV_HEADS = 2
WINDOW = 128
FOX_HEADS = 8
GLA_HEADS = 4
GLA_DK = 64
GLA_DV = 128
GLA_GATE_RANK = 16
GLA_TAU = 16.0
RET_HEADS = 4
RET_DK = 64
RET_DV = 128
RET_THETA_BASE = 10000.0
CHUNK = 64
BRANCH_WIDTH = 512
N_BRANCH = 4
N_MOD = 9
EPS = 1e-6

BF16 = jnp.bfloat16
F32 = jnp.float32

VMEM_LIMIT_BYTES = 60 * 1024 * 1024

_REF_COLS = {}
_off = 0
for _name, _size in (
        ("a_q", 512), ("a_k", 128), ("a_v", 128),
        ("f_q", 512), ("f_k", 512), ("f_v", 512), ("f_f", 8),
        ("g_q", 256), ("g_k", 256), ("g_v", 512), ("g_lr", 16), ("g_r", 512),
        ("r_q", 256), ("r_k", 256), ("r_v", 512), ("r_g", 512)):
    _REF_COLS[_name] = (_off, _size)
    _off += _size
IN_COLS = _off
for _name in ("a_k", "a_v"):
    for _i in range(SWA_KV_HEADS):
        _REF_COLS[_name + str(_i)] = (_REF_COLS[_name][0] + _i * HEAD_DIM, HEAD_DIM)


def _layout(order):
    col, off = {}, 0
    for name in order:
        col.setdefault(name, off)
        off += _REF_COLS[name][1]
    return col


_ORDER_B = ("a_q", "f_q", "f_k", "f_v", "g_v", "r_v", "g_q", "g_k", "r_q", "r_k",
            "a_k0", "a_k0", "a_k1", "a_k1", "a_v0", "a_v0", "a_v1", "a_v1")
_ORDER_F = ("g_r", "r_g", "f_f", "g_lr")
COLB = _layout(_ORDER_B)
COLF = _layout(_ORDER_F)
PROJ_B_COLS = 4608
PROJ_F_COLS = 1152
SMALL_COL = COLF["f_f"]


def _cparams(*sem):
    return pltpu.CompilerParams(dimension_semantics=sem, vmem_limit_bytes=VMEM_LIMIT_BYTES)


def _tile(n, pref):
    t = (min(pref, n) // 128) * 128
    while t >= 128:
        if n % t == 0:
            return t
        t -= 128
    return n


NORM_ROWS = 64


def _adanorm_into(h_ref, g_ref, mod_ref, mrow, u_ref, zero_ref=None):
    gs = g_ref[...] * (1.0 + mod_ref[mrow + 1:mrow + 2, :])
    shift = mod_ref[mrow:mrow + 1, :]
    rows = min(NORM_ROWS, h_ref.shape[0])

    def body(r, carry):
        r0 = pl.multiple_of(r * rows, rows)
        x = h_ref[pl.ds(r0, rows), :]
        ms = jnp.mean(x * x, axis=-1, keepdims=True)
        u_ref[pl.ds(r0, rows), :] = (x * lax.rsqrt(ms + EPS) * gs + shift).astype(u_ref.dtype)
        if zero_ref is not None:
            zero_ref[pl.ds(r0, rows), :] = jnp.zeros((rows, zero_ref.shape[1]), zero_ref.dtype)
        return carry

    lax.fori_loop(0, h_ref.shape[0] // rows, body, 0)


def _log_sigmoid(x):
    return jnp.minimum(x, 0.0) - jnp.log1p(jnp.exp(-jnp.abs(x)))


def _silu(x):
    return x * jax.nn.sigmoid(x)


ADA_ROWS = 16


def _ada_kernel(c_ref, w_ref, b_ref, o_ref):
    cond = _silu(c_ref[...])
    r = jnp.dot(jnp.concatenate(_split3(cond), axis=0), w_ref[...].astype(BF16), preferred_element_type=F32)
    bp = ADA_ROWS
    o_ref[...] = (r[:bp] + r[bp:2 * bp]) + r[2 * bp:] + b_ref[...]


def ada_modulation(c, w_ada, b_ada):
    L, D, N = w_ada.shape
    B = c.shape[0]
    BP = ADA_ROWS
    cp = jnp.zeros((BP, D), F32).at[:B].set(c)
    tn = _tile(N, 1024)
    out = pl.pallas_call(
        _ada_kernel,
        grid=(L, N // tn),
        in_specs=[
            pl.BlockSpec((BP, D), lambda l, j: (0, 0)),
            pl.BlockSpec((None, D, tn), lambda l, j: (l, 0, j)),
            pl.BlockSpec((None, 1, tn), lambda l, j: (l, 0, j)),
        ],
        out_specs=pl.BlockSpec((None, BP, tn), lambda l, j: (l, 0, j)),
        out_shape=jax.ShapeDtypeStruct((L, BP, N), F32),
        compiler_params=_cparams("parallel", "parallel"),
        name="ada_modulation",
    )(cp, w_ada, b_ada.reshape(L, 1, N))
    return out[:, :B].reshape(L, B, N_MOD, D)


def _ffn_kernel(h_ref, mod_ref, g_ref, wg_ref, wu_ref, wd_ref, *rest, mrow, res_scale, final_norm):
    if final_norm:
        fg_ref, o_ref, u_scr = rest
    else:
        o_ref, u_scr = rest
    f = pl.program_id(1)

    @pl.when(f == 0)
    def _():
        _adanorm_into(h_ref, g_ref, mod_ref, mrow, u_scr, zero_ref=o_ref)

    u = u_scr[...]
    a = jnp.dot(u, wg_ref[...].astype(BF16), preferred_element_type=F32)
    b = jnp.dot(u, wu_ref[...].astype(BF16), preferred_element_type=F32)
    mid = (_silu(a) * b).astype(BF16)
    o_ref[...] += jnp.dot(mid, wd_ref[...].astype(BF16), preferred_element_type=F32)

    @pl.when(f == pl.num_programs(1) - 1)
    def _():
        gate = res_scale * mod_ref[mrow + 2:mrow + 3, :]
        rows = min(NORM_ROWS, h_ref.shape[0])

        def body(r, carry):
            r0 = pl.multiple_of(r * rows, rows)
            y = h_ref[pl.ds(r0, rows), :] + gate * o_ref[pl.ds(r0, rows), :]
            if final_norm:
                y = y * lax.rsqrt(jnp.mean(y * y, axis=-1, keepdims=True) + EPS) * fg_ref[...]
            o_ref[pl.ds(r0, rows), :] = y
            return carry

        lax.fori_loop(0, h_ref.shape[0] // rows, body, 0)


def ffn_block(h, mod, g, wg, wu, wd, *, layer, which, seq, mrow, res_scale, final_g=None):
    T, D = h.shape
    F = wg.shape[-1]
    tm = min(1024, seq)
    tf = _tile(F, 256)
    kern = functools.partial(_ffn_kernel, mrow=mrow, res_scale=res_scale, final_norm=final_g is not None)
    in_specs = [
        pl.BlockSpec((tm, D), lambda i, f: (i, 0)),
        pl.BlockSpec((None, N_MOD, D), lambda i, f: ((i * tm) // seq, 0, 0)),
        pl.BlockSpec((1, D), lambda i, f: (0, 0)),
        pl.BlockSpec((None, None, D, tf), lambda i, f: (layer, which, 0, f)),
        pl.BlockSpec((None, None, D, tf), lambda i, f: (layer, which, 0, f)),
        pl.BlockSpec((None, None, tf, D), lambda i, f: (layer, which, f, 0)),
    ]
    args = [h, mod, g.reshape(1, D), wg, wu, wd]
    if final_g is not None:
        in_specs.append(pl.BlockSpec((1, D), lambda i, f: (0, 0)))
        args.append(final_g.reshape(1, D))
    return pl.pallas_call(
        kern,
        grid=(T // tm, F // tf),
        in_specs=in_specs,
        out_specs=pl.BlockSpec((tm, D), lambda i, f: (i, 0)),
        out_shape=jax.ShapeDtypeStruct((T, D), F32),
        scratch_shapes=[pltpu.VMEM((tm, D), BF16)],
        compiler_params=_cparams("parallel", "arbitrary"),
        name="ffn_block",
    )(*args)


def _ffn_up_kernel(h_ref, mod_ref, g_ref, wg_ref, wu_ref, mid_ref, u_scr, *, mrow):
    @pl.when(pl.program_id(1) == 0)
    def _():
        _adanorm_into(h_ref, g_ref, mod_ref, mrow, u_scr)

    u = u_scr[...]
    a = jnp.dot(u, wg_ref[...].astype(BF16), preferred_element_type=F32)
    b = jnp.dot(u, wu_ref[...].astype(BF16), preferred_element_type=F32)
    mid_ref[...] = (_silu(a) * b).astype(mid_ref.dtype)


def _ffn_down_kernel(h_ref, mid_ref, wd_ref, mod_ref, o_ref, *, mrow, res_scale, tn):
    c0 = pl.multiple_of(pl.program_id(1) * tn, tn)
    gate = res_scale * mod_ref[mrow + 2:mrow + 3, pl.ds(c0, tn)]
    y = jnp.dot(mid_ref[...], wd_ref[...].astype(BF16), preferred_element_type=F32)
    o_ref[...] = h_ref[...] + gate * y


def ffn_split(h, mod, g, wg, wu, wd, *, layer, which, seq, mrow, res_scale):
    T, D = h.shape
    F = wg.shape[-1]
    tm = min(1024, seq)
    tf = _tile(F, 512)
    tn = _tile(D, 256)
    mid = pl.pallas_call(
        functools.partial(_ffn_up_kernel, mrow=mrow),
        grid=(T // tm, F // tf),
        in_specs=[
            pl.BlockSpec((tm, D), lambda i, f: (i, 0)),
            pl.BlockSpec((None, N_MOD, D), lambda i, f: ((i * tm) // seq, 0, 0)),
            pl.BlockSpec((1, D), lambda i, f: (0, 0)),
            pl.BlockSpec((None, None, D, tf), lambda i, f: (layer, which, 0, f)),
            pl.BlockSpec((None, None, D, tf), lambda i, f: (layer, which, 0, f)),
        ],
        out_specs=pl.BlockSpec((tm, tf), lambda i, f: (i, f)),
        out_shape=jax.ShapeDtypeStruct((T, F), BF16),
        scratch_shapes=[pltpu.VMEM((tm, D), BF16)],
        compiler_params=_cparams("parallel", "arbitrary"),
        name="ffn_up",
    )(h, mod, g.reshape(1, D), wg, wu)
    return pl.pallas_call(
        functools.partial(_ffn_down_kernel, mrow=mrow, res_scale=res_scale, tn=tn),
        grid=(T // tm, D // tn),
        in_specs=[
            pl.BlockSpec((tm, tn), lambda i, j: (i, j)),
            pl.BlockSpec((tm, F), lambda i, j: (i, 0)),
            pl.BlockSpec((None, None, F, tn), lambda i, j: (layer, which, 0, j)),
            pl.BlockSpec((None, N_MOD, D), lambda i, j: ((i * tm) // seq, 0, 0)),
        ],
        out_specs=pl.BlockSpec((tm, tn), lambda i, j: (i, j)),
        out_shape=jax.ShapeDtypeStruct((T, D), F32),
        compiler_params=_cparams("parallel", "arbitrary"),
        name="ffn_down",
    )(h, mid, wd, mod)


PROJ_COL_CHUNK = 512


def _norm_proj_kernel(h_ref, mod_ref, g_ref, wb_ref, wf_ref, pb_ref, pf_ref, u_ref, *, mrow):
    _adanorm_into(h_ref, g_ref, mod_ref, mrow, u_ref)
    u = u_ref[...]
    nt = (((1,), (1,)), ((), ()))
    for c0 in range(0, pb_ref.shape[1], PROJ_COL_CHUNK):
        cols = slice(c0, c0 + PROJ_COL_CHUNK)
        pb_ref[:, cols] = lax.dot_general(u, wb_ref[cols, :], nt, preferred_element_type=F32).astype(pb_ref.dtype)
    pf_ref[...] = lax.dot_general(u, wf_ref[...], nt, preferred_element_type=F32)


def norm_proj(h, mod, g, wb, wf, *, layer, seq, mrow):
    T, D = h.shape
    NB, NF = wb.shape[1], wf.shape[1]
    tm = min(512, seq)
    kern = functools.partial(_norm_proj_kernel, mrow=mrow)
    resident = pl.Buffered(1)
    return pl.pallas_call(
        kern,
        grid=(T // tm,),
        in_specs=[
            pl.BlockSpec((tm, D), lambda i: (i, 0)),
            pl.BlockSpec((None, N_MOD, D), lambda i: ((i * tm) // seq, 0, 0)),
            pl.BlockSpec((1, D), lambda i: (0, 0)),
            pl.BlockSpec((None, NB, D), lambda i: (layer, 0, 0), pipeline_mode=resident),
            pl.BlockSpec((None, NF, D), lambda i: (layer, 0, 0), pipeline_mode=resident),
        ],
        out_specs=[
            pl.BlockSpec((tm, NB), lambda i: (i, 0)),
            pl.BlockSpec((tm, NF), lambda i: (i, 0)),
            pl.BlockSpec((tm, D), lambda i: (i, 0)),
        ],
        out_shape=[jax.ShapeDtypeStruct((T, NB), BF16), jax.ShapeDtypeStruct((T, NF), F32),
                   jax.ShapeDtypeStruct((T, D), BF16)],
        compiler_params=_cparams("parallel"),
        name="norm_proj",
    )(h, mod, g.reshape(1, D), wb, wf)


SWA_BLOCKS_PER_STEP = 4


def _swa_kernel(sink_ref, q_ref, kv_ref, kvp_ref, o_ref):
    n = pl.program_id(1)
    W = WINDOW
    d = HEAD_DIM
    rowi = lax.broadcasted_iota(jnp.int32, (2 * W, 2 * d), 0)
    lanei = lax.broadcasted_iota(jnp.int32, (2 * W, 2 * d), 1)
    own = (rowi >= W) == (lanei >= d)
    first_half = lax.broadcasted_iota(jnp.int32, (W, 2 * d), 1) < d
    qpos = lax.broadcasted_iota(jnp.int32, (2 * W, 2 * W), 0) % W + W
    kpos = lax.broadcasted_iota(jnp.int32, (2 * W, 2 * W), 1)
    rel = qpos - kpos
    band = (rel >= 0) & (rel < W)
    ones = jnp.ones((2 * W, 2 * d), BF16)
    group = SWA_HEADS // SWA_KV_HEADS
    for blk in range(SWA_BLOCKS_PER_STEP):
        rows = slice(blk * W, (blk + 1) * W)
        q = q_ref[rows, :] * (d ** -0.5)
        prev = kvp_ref[...] if blk == 0 else kv_ref[(blk - 1) * W:blk * W, :]
        kv = jnp.concatenate([prev, kv_ref[rows, :]], axis=0)
        mask = band & ((kpos >= W) | (n > 0)) if blk == 0 else band
        for g in range(SWA_HEADS // 2):
            kvh = (2 * g) // group
            k2 = kv[:, kvh * 2 * d:(kvh + 1) * 2 * d]
            v2 = kv[:, (SWA_KV_HEADS + kvh) * 2 * d:(SWA_KV_HEADS + kvh + 1) * 2 * d]
            v_ext = jnp.concatenate([v2, ones], axis=1)
            qp = q[:, g * 2 * d:(g + 1) * 2 * d]
            qstack = jnp.where(own, jnp.concatenate([qp, qp], axis=0), jnp.zeros((2 * W, 2 * d), BF16))
            logits = lax.dot_general(qstack, k2, (((1,), (1,)), ((), ())), preferred_element_type=F32)
            logits = jnp.where(mask, logits, -jnp.inf)
            sink = jnp.where(rowi >= W, sink_ref[2 * g + 1], sink_ref[2 * g])
            m = jnp.maximum(jnp.max(logits, axis=-1, keepdims=True), sink)
            p = jnp.exp(logits - jnp.concatenate([m, m], axis=1)).astype(BF16)
            r = jnp.dot(p, v_ext, preferred_element_type=F32)
            o2 = r[:, :2 * d] / (r[:, 2 * d:] + jnp.exp(sink - m))
            out = jnp.where(first_half, o2[:W], o2[W:])
            o_ref[rows, g * 2 * d:(g + 1) * 2 * d] = out.astype(o_ref.dtype)


def swa_attention(proj, sinks, *, batch, seq):
    T = proj.shape[0]
    nb = seq // WINDOW
    per = SWA_BLOCKS_PER_STEP
    ns = nb // per
    qblk = COLB["a_q"] // 512
    kvblk = COLB["a_k0"] // 512
    return pl.pallas_call(
        _swa_kernel,
        grid=(batch, ns),
        in_specs=[
            pl.BlockSpec(memory_space=pltpu.SMEM),
            pl.BlockSpec((per * WINDOW, 512), lambda b, n: (b * ns + n, qblk)),
            pl.BlockSpec((per * WINDOW, 512), lambda b, n: (b * ns + n, kvblk)),
            pl.BlockSpec((WINDOW, 512), lambda b, n: (b * nb + jnp.maximum(per * n - 1, 0), kvblk)),
        ],
        out_specs=pl.BlockSpec((per * WINDOW, 512), lambda b, n: (b * ns + n, 0)),
        out_shape=jax.ShapeDtypeStruct((T, BRANCH_WIDTH), BF16),
        compiler_params=_cparams("parallel", "arbitrary"),
        name="swa_attention",
    )(sinks, proj, proj, proj)


FOX_CUM_ROWS = 256


def _fox_cum_kernel(x_ref, b_ref, o_ref, *, seq):
    R = min(FOX_CUM_ROWS, seq)
    ri = lax.broadcasted_iota(jnp.int32, (R, 3 * R), 0)
    ci = lax.broadcasted_iota(jnp.int32, (R, 3 * R), 1)
    tri3 = jnp.where((ci % R) <= ri, 1.0, 0.0).astype(BF16)

    def body(i, carry):
        r0 = pl.multiple_of(i * R, R)
        ls = _log_sigmoid(x_ref[pl.ds(r0, R), :] + b_ref[...])
        cum = jnp.dot(tri3, jnp.concatenate(_split3(ls), axis=0), preferred_element_type=F32) + carry
        o_ref[pl.ds(r0, R), :] = cum
        return cum[R - 1:R, :]

    lax.fori_loop(0, seq // R, body, jnp.zeros((1, x_ref.shape[1]), F32))


def fox_forget_cumsum(proj, fox_b, *, batch, seq):
    T = proj.shape[0]
    blk = SMALL_COL // 128
    bias = jnp.zeros((1, 128), F32).at[0, :FOX_HEADS].set(fox_b)
    return pl.pallas_call(
        functools.partial(_fox_cum_kernel, seq=seq),
        grid=(batch,),
        in_specs=[
            pl.BlockSpec((seq, 128), lambda b: (b, blk)),
            pl.BlockSpec((1, 128), lambda b: (0, 0)),
        ],
        out_specs=pl.BlockSpec((seq, 128), lambda b: (b, 0)),
        out_shape=jax.ShapeDtypeStruct((T, 128), F32),
        compiler_params=_cparams("parallel"),
        name="fox_forget_cumsum",
    )(proj, bias)


FOX_KEY_ROWS = 512


def _fox_kernel(q_ref, k_ref, v_ref, fc_ref, fcs_ref, o_ref, m_scr, acc_scr, s_scr, kx_scr, *, tq, hp_heads):
    hp = pl.program_id(1)
    qi = pl.program_id(2)
    tk = tq
    d = HEAD_DIM
    w = hp_heads * d
    seq = k_ref.shape[0]
    e_row = lax.broadcasted_iota(jnp.int32, (w, w), 0)
    e_lane = lax.broadcasted_iota(jnp.int32, (w, w), 1)

    @pl.when(qi == 0)
    def _():
        kr = min(FOX_KEY_ROWS, seq)
        place = [sum(jnp.where((e_row == hp * hp_heads + hh) & (e_lane == 3 * hh + j), 1.0, 0.0)
                     for hh in range(hp_heads)).astype(BF16) for j in range(3)]
        for r0 in range(0, seq, kr):
            rows = slice(r0, r0 + kr)
            ext = sum(jnp.dot(piece, place[j], preferred_element_type=F32)
                      for j, piece in enumerate(_split3(-fcs_ref[rows, :])))
            kx_scr[rows, :w] = k_ref[rows, :]
            kx_scr[rows, w:] = ext.astype(BF16)

    lane = lax.broadcasted_iota(jnp.int32, (tq, w), 1)
    q = q_ref[...] * (d ** -0.5)
    zero = jnp.zeros_like(q)
    qs = [jnp.concatenate([jnp.where((lane >= hh * d) & (lane < (hh + 1) * d), q, zero),
                           jnp.where((lane >= 3 * hh) & (lane < 3 * hh + 3), 1.0, 0.0).astype(BF16)], axis=1)
          for hh in range(hp_heads)]
    m_scr[...] = jnp.full(m_scr.shape, -jnp.inf, F32)
    acc_scr[...] = jnp.zeros(acc_scr.shape, F32)
    fc = fc_ref[...]
    fqs = [jnp.broadcast_to(jnp.sum(jnp.where(lane == hp * hp_heads + hh, fc, 0.0), axis=-1, keepdims=True),
                            (tq, w)) for hh in range(hp_heads)]
    ones = jnp.ones((tk, w), BF16)

    def scores(kb, slot):
        k0 = pl.multiple_of(kb * tk, tk)
        kx = kx_scr[pl.ds(k0, tk), :]
        for hh in range(hp_heads):
            s_scr[slot, hh] = lax.dot_general(qs[hh], kx, (((1,), (1,)), ((), ())), preferred_element_type=F32)

    def softmax_pv(kb, slot, masked):
        k0 = pl.multiple_of(kb * tk, tk)
        v = jnp.concatenate([v_ref[pl.ds(k0, tk), :], ones], axis=1)
        for hh in range(hp_heads):
            t = s_scr[slot, hh]
            if masked:
                row = lax.broadcasted_iota(jnp.int32, (tq, tk), 0)
                col = lax.broadcasted_iota(jnp.int32, (tq, tk), 1)
                t = jnp.where(col <= row, t, -jnp.inf)
            m_old = m_scr[hh]
            m_new = jnp.maximum(m_old, fqs[hh] + jnp.max(t, axis=-1, keepdims=True))
            alpha = jnp.exp(m_old - m_new)
            c = fqs[hh] - m_new
            p = jnp.exp(t + jnp.concatenate([c] * (tk // w), axis=1))
            acc_scr[hh] = (jnp.concatenate([alpha, alpha], axis=1) * acc_scr[hh]
                           + jnp.dot(p.astype(BF16), v, preferred_element_type=F32))
            m_scr[hh] = m_new

    scores(0, 0)

    def body(j, carry):
        kb = 2 * j
        softmax_pv(kb, 0, False)
        scores(kb + 1, 1)
        softmax_pv(kb + 1, 1, False)
        scores(kb + 2, 0)
        return carry

    lax.fori_loop(0, qi // 2, body, 0)

    @pl.when(qi % 2 == 1)
    def _():
        softmax_pv(qi - 1, 0, False)
        scores(qi, 1)
        softmax_pv(qi, 1, True)

    @pl.when(qi % 2 == 0)
    def _():
        softmax_pv(qi, 0, True)
    out = None
    for hh in range(hp_heads):
        o = acc_scr[hh, :, :w] / acc_scr[hh, :, w:]
        out = o if out is None else jnp.where(lane >= hh * d, o, out)
    o_ref[...] = out.astype(o_ref.dtype)


def fox_attention(proj, fc, *, batch, seq):
    T = proj.shape[0]
    tq = min(512, seq)
    nq = seq // tq
    hp_heads = 2
    n_hp = FOX_HEADS // hp_heads
    qblk, kblk, vblk = COLB["f_q"] // 128, COLB["f_k"] // 128, COLB["f_v"] // 128
    kern = functools.partial(_fox_kernel, tq=tq, hp_heads=hp_heads)
    return pl.pallas_call(
        kern,
        grid=(batch, n_hp, nq),
        scratch_shapes=[pltpu.VMEM((hp_heads, tq, hp_heads * HEAD_DIM), F32),
                        pltpu.VMEM((hp_heads, tq, 2 * hp_heads * HEAD_DIM), F32),
                        pltpu.VMEM((2, hp_heads, tq, tq), F32),
                        pltpu.VMEM((seq, 2 * hp_heads * HEAD_DIM), BF16)],
        in_specs=[
            pl.BlockSpec((tq, 128), lambda b, hp, qi: (b * nq + qi, qblk + hp)),
            pl.BlockSpec((seq, 128), lambda b, hp, qi: (b, kblk + hp)),
            pl.BlockSpec((seq, 128), lambda b, hp, qi: (b, vblk + hp)),
            pl.BlockSpec((tq, 128), lambda b, hp, qi: (b * nq + qi, 0)),
            pl.BlockSpec((seq, 128), lambda b, hp, qi: (b, 0)),
        ],
        out_specs=pl.BlockSpec((tq, 128), lambda b, hp, qi: (b * nq + qi, hp)),
        out_shape=jax.ShapeDtypeStruct((T, BRANCH_WIDTH), BF16),
        compiler_params=_cparams("parallel", "parallel", "arbitrary"),
        name="fox_attention",
    )(proj, proj, proj, fc, fc)


def _split2(x):
    hi = x.astype(BF16)
    return hi, (x - hi.astype(F32)).astype(BF16)


def _split3(x):
    hi = x.astype(BF16)
    r = x - hi.astype(F32)
    mid = r.astype(BF16)
    return hi, mid, (r - mid.astype(F32)).astype(BF16)


def _rotate_half(x, neg_first_half):
    n = x.shape[-1]
    half = HEAD_DIM // 2
    fwd = pltpu.roll(x, half, 1)
    bwd = pltpu.roll(x, n - half, 1)
    return jnp.where(neg_first_half, -bwd, fwd)


def _linear_attn_kernel(*refs, mode, cb):
    if mode == "gla":
        (q_ref, k_ref, v_ref, gate_ref, small_ref, wg_ref, bg_ref, ng_ref, o_ref, st_ref) = refs
    else:
        (q_ref, k_ref, v_ref, gate_ref, lg_ref, cos_ref, sin_ref, gw_ref, gb_ref, o_ref, st_ref) = refs
    H, DK, DV, C = GLA_HEADS, GLA_DK, GLA_DV, CHUNK
    nch = cb // C
    W2 = 2 * DK
    HW = H * DK
    dn_lanes = (((1,), (1,)), ((), ()))
    dn_rows = (((0,), (0,)), ((), ()))

    @pl.when(pl.program_id(1) == 0)
    def _():
        st_ref[...] = jnp.zeros_like(st_ref)

    q = q_ref[...].astype(F32)
    k = k_ref[...].astype(F32)
    if mode == "gla":
        glr = small_ref[...][:, FOX_HEADS:FOX_HEADS + GLA_GATE_RANK]
        a_hi, a_lo = _split2(glr)
        w_hi, w_lo = _split2(wg_ref[...])
        z = (jnp.dot(a_hi, w_hi, preferred_element_type=F32) + jnp.dot(a_hi, w_lo, preferred_element_type=F32)
             + jnp.dot(a_lo, w_hi, preferred_element_type=F32)) + bg_ref[...]
        ld = _log_sigmoid(z) / GLA_TAU
        ld_w = jnp.concatenate([ld[c * C:(c + 1) * C] for c in range(nch)], axis=1)
        ri = lax.broadcasted_iota(jnp.int32, (C, 3 * C), 0)
        ci = lax.broadcasted_iota(jnp.int32, (C, 3 * C), 1)
        tri3 = jnp.where((ci % C) <= ri, 1.0, 0.0).astype(BF16)
        cum_w = jnp.dot(tri3, jnp.concatenate(_split3(ld_w), axis=0), preferred_element_type=F32)
        cum = jnp.concatenate([cum_w[:, c * HW:(c + 1) * HW] for c in range(nch)], axis=0)
        lasts = [cum_w[C - 1:C, c * HW:(c + 1) * HW] for c in range(nch)]
        last_b = jnp.concatenate([jnp.broadcast_to(l, (C, HW)) for l in lasts], axis=0)
    else:
        lane = lax.broadcasted_iota(jnp.int32, (cb, HW), 1)
        first_half = (lane % HEAD_DIM) < (HEAD_DIM // 2)
        cos = cos_ref[...]
        sin = sin_ref[...]
        q = q * cos + _rotate_half(q, first_half) * sin
        k = k * cos + _rotate_half(k, first_half) * sin
        steps = (lax.broadcasted_iota(jnp.int32, (cb, HW), 0) % C + 1).astype(F32)
        cum = steps * lg_ref[...]
        last_b = float(C) * lg_ref[...]
        lasts = [last_b] * nch
    q_in = (q * (DK ** -0.5) * jnp.exp(cum)).astype(BF16)
    k_in = (k * jnp.exp(-cum)).astype(BF16)
    k_st = (k * jnp.exp(last_b - cum)).astype(BF16)
    decs = [jnp.exp(l) for l in lasts]

    rowi = lax.broadcasted_iota(jnp.int32, (2 * C, W2), 0)
    lanei = lax.broadcasted_iota(jnp.int32, (2 * C, W2), 1)
    own = (rowi >= C) == (lanei >= DK)
    bd_causal = own & ((lanei % C) <= (rowi % C))
    zero_bf = jnp.zeros((2 * C, W2), BF16)

    def pair_ops(c, p):
        rows = slice(c * C, (c + 1) * C)
        ls = slice(p * W2, (p + 1) * W2)
        qp, kp, ksp = q_in[rows, ls], k_in[rows, ls], k_st[rows, ls]
        qstack = jnp.where(own, jnp.concatenate([qp, qp], axis=0), zero_bf)
        ksstack = jnp.where(own, jnp.concatenate([ksp, ksp], axis=0), zero_bf)
        k2 = jnp.concatenate([kp, kp], axis=0)
        vstack = jnp.concatenate([v_ref[rows, (2 * p) * DV:(2 * p + 1) * DV],
                                  v_ref[rows, (2 * p + 1) * DV:(2 * p + 2) * DV]], axis=0)
        sw = lax.dot_general(qstack, k2, dn_lanes, preferred_element_type=F32)
        sc = jnp.where(bd_causal, sw, 0.0).astype(BF16)
        o_intra = jnp.dot(sc, vstack, preferred_element_type=F32)
        upd = lax.dot_general(vstack, ksstack, dn_rows, preferred_element_type=F32)
        return qstack, o_intra, upd

    pre = [[pair_ops(c, p) for p in range(H // 2)] for c in range(nch)]
    state = [st_ref[p] for p in range(H // 2)]
    for c in range(nch):
        rows = slice(c * C, (c + 1) * C)
        for p in range(H // 2):
            qstack, o_intra, upd = pre[c][p]
            o = o_intra + lax.dot_general(qstack, state[p].astype(BF16), dn_lanes, preferred_element_type=F32)
            state[p] = state[p] * decs[c][:, p * W2:(p + 1) * W2] + upd
            for hh in range(2):
                hd = 2 * p + hh
                vs = slice(hd * DV, (hd + 1) * DV)
                oh = o[hh * C:(hh + 1) * C]
                gate = _silu(gate_ref[rows, vs])
                if mode == "gla":
                    y = oh * lax.rsqrt(jnp.mean(oh * oh, axis=-1, keepdims=True) + EPS) * ng_ref[...]
                else:
                    mu = jnp.mean(oh, axis=-1, keepdims=True)
                    var = jnp.mean(jnp.square(oh - mu), axis=-1, keepdims=True)
                    y = (oh - mu) * lax.rsqrt(var + EPS) * gw_ref[:, vs] + gb_ref[:, vs]
                o_ref[rows, vs] = (y * gate).astype(o_ref.dtype)
    for p in range(H // 2):
        st_ref[p] = state[p]


LINEAR_ATTN_BLOCK = {"gla": 1024, "ret": 512}


def linear_attention(proj, projf, params, *, mode, batch, seq):
    T = proj.shape[0]
    cb = min(LINEAR_ATTN_BLOCK[mode], seq)
    nc = seq // cb
    pre = "g" if mode == "gla" else "r"
    qblk, kblk = COLB[pre + "_q"] // 256, COLB[pre + "_k"] // 256
    vblk = COLB[pre + "_v"] // 512
    gblk = COLF["g_r" if mode == "gla" else "r_g"] // 512
    row = lambda b, c: b * nc + c
    in_specs = [
        pl.BlockSpec((cb, 256), lambda b, c: (row(b, c), qblk)),
        pl.BlockSpec((cb, 256), lambda b, c: (row(b, c), kblk)),
        pl.BlockSpec((cb, 512), lambda b, c: (row(b, c), vblk)),
        pl.BlockSpec((cb, 512), lambda b, c: (row(b, c), gblk)),
    ]
    args = [proj, proj, proj, projf]
    if mode == "gla":
        wg, bg, ng = params
        in_specs += [
            pl.BlockSpec((cb, 128), lambda b, c: (row(b, c), SMALL_COL // 128)),
            pl.BlockSpec(wg.shape, lambda b, c: (0, 0)),
            pl.BlockSpec((1, bg.shape[-1]), lambda b, c: (0, 0)),
            pl.BlockSpec((1, ng.shape[-1]), lambda b, c: (0, 0)),
        ]
        args += [projf, wg, bg.reshape(1, -1), ng.reshape(1, -1)]
    else:
        lg, cos, sin, gw, gb = params
        in_specs += [
            pl.BlockSpec((1, lg.shape[-1]), lambda b, c: (0, 0)),
            pl.BlockSpec((cb, 256), lambda b, c: (c, 0)),
            pl.BlockSpec((cb, 256), lambda b, c: (c, 0)),
            pl.BlockSpec((1, gw.shape[-1]), lambda b, c: (0, 0)),
            pl.BlockSpec((1, gb.shape[-1]), lambda b, c: (0, 0)),
        ]
        args += [lg.reshape(1, -1), cos, sin, gw.reshape(1, -1), gb.reshape(1, -1)]
    return pl.pallas_call(
        functools.partial(_linear_attn_kernel, mode=mode, cb=cb),
        grid=(batch, nc),
        in_specs=in_specs,
        out_specs=pl.BlockSpec((cb, 512), lambda b, c: (row(b, c), 0)),
        out_shape=jax.ShapeDtypeStruct((T, BRANCH_WIDTH), BF16),
        scratch_shapes=[pltpu.VMEM((GLA_HEADS // 2, GLA_DV, 2 * GLA_DK), F32)],
        compiler_params=_cparams("parallel", "arbitrary"),
        name="linear_attention_" + mode,
    )(*args)


def _retention_tables(seq):
    half = RET_DK // 2
    inv = RET_THETA_BASE ** (-jnp.arange(half, dtype=F32) / half)
    ang = jnp.arange(seq).astype(F32)[:, None] * inv[None, :]
    cos = jnp.tile(jnp.cos(ang), (1, 2 * RET_HEADS))
    sin = jnp.tile(jnp.sin(ang), (1, 2 * RET_HEADS))
    log_gamma = jnp.log1p(-jnp.exp2(-5.0 - jnp.arange(RET_HEADS, dtype=F32)))
    return jnp.repeat(log_gamma, RET_DK), cos, sin


def _merge_kernel(u_ref, oa_ref, ob_ref, oc_ref, od_ref, wm_ref, wb_ref, bm_ref, m_ref):
    u = u_ref[...]
    acc = None
    for i, o_ref in enumerate((oa_ref, ob_ref, oc_ref, od_ref)):
        z = jnp.dot(u, wm_ref[i].astype(BF16), preferred_element_type=F32) + bm_ref[i]
        pr = jnp.dot(o_ref[...], wb_ref[i].astype(BF16), preferred_element_type=F32)
        t = jax.nn.sigmoid(z) * pr
        acc = t if acc is None else acc + t
    m_ref[...] = acc.astype(m_ref.dtype)


def merge_branches(u, branches, wm, wb, bm, *, layer):
    T, D = u.shape
    tm, tn = min(1024, T), min(256, D)
    ospec = pl.BlockSpec((tm, BRANCH_WIDTH), lambda i, j: (i, 0))
    return pl.pallas_call(
        _merge_kernel,
        grid=(T // tm, D // tn),
        in_specs=[
            pl.BlockSpec((tm, D), lambda i, j: (i, 0)),
            ospec, ospec, ospec, ospec,
            pl.BlockSpec((None, N_BRANCH, D, tn), lambda i, j: (layer, 0, 0, j)),
            pl.BlockSpec((None, N_BRANCH, BRANCH_WIDTH, tn), lambda i, j: (layer, 0, 0, j)),
            pl.BlockSpec((None, N_BRANCH, 1, tn), lambda i, j: (layer, 0, 0, j)),
        ],
        out_specs=pl.BlockSpec((tm, tn), lambda i, j: (i, j)),
        out_shape=jax.ShapeDtypeStruct((T, D), BF16),
        compiler_params=_cparams("parallel", "arbitrary"),
        name="merge_branches",
    )(u, *branches, wm, wb, bm.reshape(bm.shape[0], N_BRANCH, 1, D))


def _out_proj_kernel(h_ref, m_ref, w_ref, mod_ref, o_ref, *, mrow):
    m = m_ref[...]
    for c0 in range(0, o_ref.shape[1], PROJ_COL_CHUNK):
        cols = slice(c0, c0 + PROJ_COL_CHUNK)
        y = jnp.dot(m, w_ref[:, cols], preferred_element_type=F32)
        o_ref[:, cols] = h_ref[:, cols] + mod_ref[mrow:mrow + 1, cols] * y


def out_proj(h, merged, w, mod, *, layer, seq, mrow):
    T, D = h.shape
    tm = min(512, seq)
    return pl.pallas_call(
        functools.partial(_out_proj_kernel, mrow=mrow),
        grid=(T // tm,),
        in_specs=[
            pl.BlockSpec((tm, D), lambda i: (i, 0)),
            pl.BlockSpec((tm, D), lambda i: (i, 0)),
            pl.BlockSpec((None, D, D), lambda i: (layer, 0, 0), pipeline_mode=pl.Buffered(1)),
            pl.BlockSpec((None, N_MOD, D), lambda i: ((i * tm) // seq, 0, 0)),
        ],
        out_specs=pl.BlockSpec((tm, D), lambda i: (i, 0)),
        out_shape=jax.ShapeDtypeStruct((T, D), F32),
        compiler_params=_cparams("parallel"),
        name="out_proj",
    )(h, merged, w, mod)


def _w_in_prep_kernel(w_ref, ob_ref, of_ref):
    for order, o_ref in ((_ORDER_B, ob_ref), (_ORDER_F, of_ref)):
        off = 0
        for name in order:
            start, width = _REF_COLS[name]
            o_ref[off:off + width, :] = w_ref[start:start + width, :].astype(o_ref.dtype)
            off += width
        if off < o_ref.shape[0]:
            o_ref[off:, :] = jnp.zeros((o_ref.shape[0] - off, o_ref.shape[1]), o_ref.dtype)


def permuted_w_in(w_in):
    L, D, C = w_in.shape
    td = min(256, D)
    return pl.pallas_call(
        _w_in_prep_kernel,
        grid=(L, D // td),
        in_specs=[pl.BlockSpec((None, C, td), lambda l, r: (l, 0, r))],
        out_specs=[pl.BlockSpec((None, PROJ_B_COLS, td), lambda l, r: (l, 0, r)),
                   pl.BlockSpec((None, PROJ_F_COLS, td), lambda l, r: (l, 0, r))],
        out_shape=[jax.ShapeDtypeStruct((L, PROJ_B_COLS, D), BF16), jax.ShapeDtypeStruct((L, PROJ_F_COLS, D), BF16)],
        compiler_params=_cparams("parallel", "parallel"),
        name="w_in_prep",
    )(jnp.swapaxes(w_in, 1, 2))


def kernel(x, c, w_ada, b_ada, norm_g, ffn_w_gate, ffn_w_up, ffn_w_down, w_in, fox_b_forget, attn_sinks,
           gla_w_gate, gla_b_gate, gla_norm_g, ret_gn_w, ret_gn_b, w_branch, w_merge, b_merge, w_out,
           final_norm_g):
    B, S, D = x.shape
    L = w_ada.shape[0]
    T = B * S
    mod_all = ada_modulation(c, w_ada, b_ada)
    log_gamma, cos, sin = _retention_tables(S)
    w_out_bf = w_out.astype(BF16)
    w_in_b, w_in_f = permuted_w_in(w_in)
    h = x.reshape(T, D)
    for l in range(L):
        mod = mod_all[l]
        h = ffn_split(h, mod, norm_g[l, 0], ffn_w_gate, ffn_w_up, ffn_w_down, layer=l, which=0, seq=S,
                      mrow=0, res_scale=0.5)
        proj, projf, u = norm_proj(h, mod, norm_g[l, 1], w_in_b, w_in_f, layer=l, seq=S, mrow=3)
        o_a = swa_attention(proj, attn_sinks[l], batch=B, seq=S)
        fc = fox_forget_cumsum(projf, fox_b_forget[l], batch=B, seq=S)
        o_b = fox_attention(proj, fc, batch=B, seq=S)
        o_c = linear_attention(proj, projf, (gla_w_gate[l], gla_b_gate[l], gla_norm_g[l]), mode="gla",
                               batch=B, seq=S)
        o_d = linear_attention(proj, projf, (log_gamma, cos, sin, ret_gn_w[l], ret_gn_b[l]), mode="ret",
                               batch=B, seq=S)
        merged = merge_branches(u, (o_a, o_b, o_c, o_d), w_merge, w_branch, b_merge, layer=l)
        h = out_proj(h, merged, w_out_bf, mod, layer=l, seq=S, mrow=5)
        if l == L - 1:
            h = ffn_block(h, mod, norm_g[l, 2], ffn_w_gate, ffn_w_up, ffn_w_down, layer=l, which=1, seq=S,
                          mrow=6, res_scale=0.5, final_g=final_norm_g)
        else:
            h = ffn_split(h, mod, norm_g[l, 2], ffn_w_gate, ffn_w_up, ffn_w_down, layer=l, which=1, seq=S,
                          mrow=6, res_scale=0.5)
    return h.reshape(B, S, D)
```

```python
import functools

import jax
import jax.numpy as jnp
from jax import lax
from jax.experimental import pallas as pl
from jax.experimental.pallas import tpu as pltpu

HEAD_DIM = 64
SWA_HEADS = 8
SWA_KV_HEADS = 2
WINDOW = 128
FOX_HEADS = 8
GLA_HEADS = 4
GLA_DK = 64
GLA_DV = 128
GLA_GATE_RANK = 16
GLA_TAU = 16.0
RET_HEADS = 4
RET_DK = 64
RET_DV = 128
RET_THETA_BASE = 10000.0
CHUNK = 64
BRANCH_WIDTH = 512
N_BRANCH = 4
N_MOD = 9
EPS = 1e-6

BF16 = jnp.bfloat16
F32 = jnp.float32

VMEM_LIMIT_BYTES = 60 * 1024 * 1024

_REF_COLS = {}
_off = 0
for _name, _size in (
        ("a_q", 512), ("a_k", 128), ("a_v", 128),
        ("f_q", 512), ("f_k", 512), ("f_v", 512), ("f_f", 8),
        ("g_q", 256), ("g_k", 256), ("g_v", 512), ("g_lr", 16), ("g_r", 512),
        ("r_q", 256), ("r_k", 256), ("r_v", 512), ("r_g", 512)):
    _REF_COLS[_name] = (_off, _size)
    _off += _size
IN_COLS = _off
for _name in ("a_k", "a_v"):
    for _i in range(SWA_KV_HEADS):
        _REF_COLS[_name + str(_i)] = (_REF_COLS[_name][0] + _i * HEAD_DIM, HEAD_DIM)


def _layout(order):
    col, off = {}, 0
    for name in order:
        col.setdefault(name, off)
        off += _REF_COLS[name][1]
    return col


_ORDER_B = ("a_q", "f_q", "f_k", "f_v", "g_v", "r_v", "g_q", "g_k", "r_q", "r_k",
            "a_k0", "a_k0", "a_k1", "a_k1", "a_v0", "a_v0", "a_v1", "a_v1")
_ORDER_F = ("g_r", "r_g", "f_f", "g_lr")
COLB = _layout(_ORDER_B)
COLF = _layout(_ORDER_F)
PROJ_B_COLS = 4608
PROJ_F_COLS = 1152
SMALL_COL = COLF["f_f"]


def _cparams(*sem):
    return pltpu.CompilerParams(dimension_semantics=sem, vmem_limit_bytes=VMEM_LIMIT_BYTES)


def _tile(n, pref):
    t = (min(pref, n) // 128) * 128
    while t >= 128:
        if n % t == 0:
            return t
        t -= 128
    return n


NORM_ROWS = 64


def _adanorm_into(h_ref, g_ref, mod_ref, mrow, u_ref, zero_ref=None):
    gs = g_ref[...] * (1.0 + mod_ref[mrow + 1:mrow + 2, :])
    shift = mod_ref[mrow:mrow + 1, :]
    rows = min(NORM_ROWS, h_ref.shape[0])

    def body(r, carry):
        r0 = pl.multiple_of(r * rows, rows)
        x = h_ref[pl.ds(r0, rows), :]
        ms = jnp.mean(x * x, axis=-1, keepdims=True)
        u_ref[pl.ds(r0, rows), :] = (x * lax.rsqrt(ms + EPS) * gs + shift).astype(u_ref.dtype)
        if zero_ref is not None:
            zero_ref[pl.ds(r0, rows), :] = jnp.zeros((rows, zero_ref.shape[1]), zero_ref.dtype)
        return carry

    lax.fori_loop(0, h_ref.shape[0] // rows, body, 0)


def _log_sigmoid(x):
    return jnp.minimum(x, 0.0) - jnp.log1p(jnp.exp(-jnp.abs(x)))


def _silu(x):
    return x * jax.nn.sigmoid(x)


ADA_ROWS = 16


def _ada_kernel(c_ref, w_ref, b_ref, o_ref):
    cond = _silu(c_ref[...])
    r = jnp.dot(jnp.concatenate(_split3(cond), axis=0), w_ref[...].astype(BF16), preferred_element_type=F32)
    bp = ADA_ROWS
    o_ref[...] = (r[:bp] + r[bp:2 * bp]) + r[2 * bp:] + b_ref[...]


def ada_modulation(c, w_ada, b_ada):
    L, D, N = w_ada.shape
    B = c.shape[0]
    BP = ADA_ROWS
    cp = jnp.zeros((BP, D), F32).at[:B].set(c)
    tn = _tile(N, 1024)
    out = pl.pallas_call(
        _ada_kernel,
        grid=(L, N // tn),
        in_specs=[
            pl.BlockSpec((BP, D), lambda l, j: (0, 0)),
            pl.BlockSpec((None, D, tn), lambda l, j: (l, 0, j)),
            pl.BlockSpec((None, 1, tn), lambda l, j: (l, 0, j)),
        ],
        out_specs=pl.BlockSpec((None, BP, tn), lambda l, j: (l, 0, j)),
        out_shape=jax.ShapeDtypeStruct((L, BP, N), F32),
        compiler_params=_cparams("parallel", "parallel"),
        name="ada_modulation",
    )(cp, w_ada, b_ada.reshape(L, 1, N))
    return out[:, :B].reshape(L, B, N_MOD, D)


def _ffn_kernel(h_ref, mod_ref, g_ref, wg_ref, wu_ref, wd_ref, *rest, mrow, res_scale, final_norm):
    if final_norm:
        fg_ref, o_ref, u_scr = rest
    else:
        o_ref, u_scr = rest
    f = pl.program_id(1)

    @pl.when(f == 0)
    def _():
        _adanorm_into(h_ref, g_ref, mod_ref, mrow, u_scr, zero_ref=o_ref)

    u = u_scr[...]
    a = jnp.dot(u, wg_ref[...].astype(BF16), preferred_element_type=F32)
    b = jnp.dot(u, wu_ref[...].astype(BF16), preferred_element_type=F32)
    mid = (_silu(a) * b).astype(BF16)
    o_ref[...] += jnp.dot(mid, wd_ref[...].astype(BF16), preferred_element_type=F32)

    @pl.when(f == pl.num_programs(1) - 1)
    def _():
        gate = res_scale * mod_ref[mrow + 2:mrow + 3, :]
        rows = min(NORM_ROWS, h_ref.shape[0])

        def body(r, carry):
            r0 = pl.multiple_of(r * rows, rows)
            y = h_ref[pl.ds(r0, rows), :] + gate * o_ref[pl.ds(r0, rows), :]
            if final_norm:
                y = y * lax.rsqrt(jnp.mean(y * y, axis=-1, keepdims=True) + EPS) * fg_ref[...]
            o_ref[pl.ds(r0, rows), :] = y
            return carry

        lax.fori_loop(0, h_ref.shape[0] // rows, body, 0)


def ffn_block(h, mod, g, wg, wu, wd, *, layer, which, seq, mrow, res_scale, final_g=None):
    T, D = h.shape
    F = wg.shape[-1]
    tm = min(1024, seq)
    tf = _tile(F, 256)
    kern = functools.partial(_ffn_kernel, mrow=mrow, res_scale=res_scale, final_norm=final_g is not None)
    in_specs = [
        pl.BlockSpec((tm, D), lambda i, f: (i, 0)),
        pl.BlockSpec((None, N_MOD, D), lambda i, f: ((i * tm) // seq, 0, 0)),
        pl.BlockSpec((1, D), lambda i, f: (0, 0)),
        pl.BlockSpec((None, None, D, tf), lambda i, f: (layer, which, 0, f)),
        pl.BlockSpec((None, None, D, tf), lambda i, f: (layer, which, 0, f)),
        pl.BlockSpec((None, None, tf, D), lambda i, f: (layer, which, f, 0)),
    ]
    args = [h, mod, g.reshape(1, D), wg, wu, wd]
    if final_g is not None:
        in_specs.append(pl.BlockSpec((1, D), lambda i, f: (0, 0)))
        args.append(final_g.reshape(1, D))
    return pl.pallas_call(
        kern,
        grid=(T // tm, F // tf),
        in_specs=in_specs,
        out_specs=pl.BlockSpec((tm, D), lambda i, f: (i, 0)),
        out_shape=jax.ShapeDtypeStruct((T, D), F32),
        scratch_shapes=[pltpu.VMEM((tm, D), BF16)],
        compiler_params=_cparams("parallel", "arbitrary"),
        name="ffn_block",
    )(*args)


PROJ_COL_CHUNK = 512


def _norm_proj_kernel(h_ref, mod_ref, g_ref, wb_ref, wf_ref, pb_ref, pf_ref, u_ref, *, mrow):
    _adanorm_into(h_ref, g_ref, mod_ref, mrow, u_ref)
    u = u_ref[...]
    nt = (((1,), (1,)), ((), ()))
    for c0 in range(0, pb_ref.shape[1], PROJ_COL_CHUNK):
        cols = slice(c0, c0 + PROJ_COL_CHUNK)
        pb_ref[:, cols] = lax.dot_general(u, wb_ref[cols, :], nt, preferred_element_type=F32).astype(pb_ref.dtype)
    pf_ref[...] = lax.dot_general(u, wf_ref[...], nt, preferred_element_type=F32)


def norm_proj(h, mod, g, wb, wf, *, layer, seq, mrow):
    T, D = h.shape
    NB, NF = wb.shape[1], wf.shape[1]
    tm = min(512, seq)
    kern = functools.partial(_norm_proj_kernel, mrow=mrow)
    resident = pl.Buffered(1)
    return pl.pallas_call(
        kern,
        grid=(T // tm,),
        in_specs=[
            pl.BlockSpec((tm, D), lambda i: (i, 0)),
            pl.BlockSpec((None, N_MOD, D), lambda i: ((i * tm) // seq, 0, 0)),
            pl.BlockSpec((1, D), lambda i: (0, 0)),
            pl.BlockSpec((None, NB, D), lambda i: (layer, 0, 0), pipeline_mode=resident),
            pl.BlockSpec((None, NF, D), lambda i: (layer, 0, 0), pipeline_mode=resident),
        ],
        out_specs=[
            pl.BlockSpec((tm, NB), lambda i: (i, 0)),
            pl.BlockSpec((tm, NF), lambda i: (i, 0)),
            pl.BlockSpec((tm, D), lambda i: (i, 0)),
        ],
        out_shape=[jax.ShapeDtypeStruct((T, NB), BF16), jax.ShapeDtypeStruct((T, NF), F32),
                   jax.ShapeDtypeStruct((T, D), BF16)],
        compiler_params=_cparams("parallel"),
        name="norm_proj",
    )(h, mod, g.reshape(1, D), wb, wf)


SWA_BLOCKS_PER_STEP = 4


def _swa_kernel(sink_ref, q_ref, kv_ref, kvp_ref, o_ref):
    n = pl.program_id(1)
    W = WINDOW
    d = HEAD_DIM
    rowi = lax.broadcasted_iota(jnp.int32, (2 * W, 2 * d), 0)
    lanei = lax.broadcasted_iota(jnp.int32, (2 * W, 2 * d), 1)
    own = (rowi >= W) == (lanei >= d)
    first_half = lax.broadcasted_iota(jnp.int32, (W, 2 * d), 1) < d
    qpos = lax.broadcasted_iota(jnp.int32, (2 * W, 2 * W), 0) % W + W
    kpos = lax.broadcasted_iota(jnp.int32, (2 * W, 2 * W), 1)
    rel = qpos - kpos
    band = (rel >= 0) & (rel < W)
    ones = jnp.ones((2 * W, 2 * d), BF16)
    group = SWA_HEADS // SWA_KV_HEADS
    for blk in range(SWA_BLOCKS_PER_STEP):
        rows = slice(blk * W, (blk + 1) * W)
        q = q_ref[rows, :] * (d ** -0.5)
        prev = kvp_ref[...] if blk == 0 else kv_ref[(blk - 1) * W:blk * W, :]
        kv = jnp.concatenate([prev, kv_ref[rows, :]], axis=0)
        mask = band & ((kpos >= W) | (n > 0)) if blk == 0 else band
        for g in range(SWA_HEADS // 2):
            kvh = (2 * g) // group
            k2 = kv[:, kvh * 2 * d:(kvh + 1) * 2 * d]
            v2 = kv[:, (SWA_KV_HEADS + kvh) * 2 * d:(SWA_KV_HEADS + kvh + 1) * 2 * d]
            v_ext = jnp.concatenate([v2, ones], axis=1)
            qp = q[:, g * 2 * d:(g + 1) * 2 * d]
            qstack = jnp.where(own, jnp.concatenate([qp, qp], axis=0), jnp.zeros((2 * W, 2 * d), BF16))
            logits = lax.dot_general(qstack, k2, (((1,), (1,)), ((), ())), preferred_element_type=F32)
            logits = jnp.where(mask, logits, -jnp.inf)
            sink = jnp.where(rowi >= W, sink_ref[2 * g + 1], sink_ref[2 * g])
            m = jnp.maximum(jnp.max(logits, axis=-1, keepdims=True), sink)
            p = jnp.exp(logits - jnp.concatenate([m, m], axis=1)).astype(BF16)
            r = jnp.dot(p, v_ext, preferred_element_type=F32)
            o2 = r[:, :2 * d] / (r[:, 2 * d:] + jnp.exp(sink - m))
            out = jnp.where(first_half, o2[:W], o2[W:])
            o_ref[rows, g * 2 * d:(g + 1) * 2 * d] = out.astype(o_ref.dtype)


def swa_attention(proj, sinks, *, batch, seq):
    T = proj.shape[0]
    nb = seq // WINDOW
    per = SWA_BLOCKS_PER_STEP
    ns = nb // per
    qblk = COLB["a_q"] // 512
    kvblk = COLB["a_k0"] // 512
    return pl.pallas_call(
        _swa_kernel,
        grid=(batch, ns),
        in_specs=[
            pl.BlockSpec(memory_space=pltpu.SMEM),
            pl.BlockSpec((per * WINDOW, 512), lambda b, n: (b * ns + n, qblk)),
            pl.BlockSpec((per * WINDOW, 512), lambda b, n: (b * ns + n, kvblk)),
            pl.BlockSpec((WINDOW, 512), lambda b, n: (b * nb + jnp.maximum(per * n - 1, 0), kvblk)),
        ],
        out_specs=pl.BlockSpec((per * WINDOW, 512), lambda b, n: (b * ns + n, 0)),
        out_shape=jax.ShapeDtypeStruct((T, BRANCH_WIDTH), BF16),
        compiler_params=_cparams("parallel", "arbitrary"),
        name="swa_attention",
    )(sinks, proj, proj, proj)


FOX_CUM_ROWS = 256


def _fox_cum_kernel(x_ref, b_ref, o_ref, *, seq):
    R = min(FOX_CUM_ROWS, seq)
    ri = lax.broadcasted_iota(jnp.int32, (R, 3 * R), 0)
    ci = lax.broadcasted_iota(jnp.int32, (R, 3 * R), 1)
    tri3 = jnp.where((ci % R) <= ri, 1.0, 0.0).astype(BF16)

    def body(i, carry):
        r0 = pl.multiple_of(i * R, R)
        ls = _log_sigmoid(x_ref[pl.ds(r0, R), :] + b_ref[...])
        cum = jnp.dot(tri3, jnp.concatenate(_split3(ls), axis=0), preferred_element_type=F32) + carry
        o_ref[pl.ds(r0, R), :] = cum
        return cum[R - 1:R, :]

    lax.fori_loop(0, seq // R, body, jnp.zeros((1, x_ref.shape[1]), F32))


def fox_forget_cumsum(proj, fox_b, *, batch, seq):
    T = proj.shape[0]
    blk = SMALL_COL // 128
    bias = jnp.zeros((1, 128), F32).at[0, :FOX_HEADS].set(fox_b)
    return pl.pallas_call(
        functools.partial(_fox_cum_kernel, seq=seq),
        grid=(batch,),
        in_specs=[
            pl.BlockSpec((seq, 128), lambda b: (b, blk)),
            pl.BlockSpec((1, 128), lambda b: (0, 0)),
        ],
        out_specs=pl.BlockSpec((seq, 128), lambda b: (b, 0)),
        out_shape=jax.ShapeDtypeStruct((T, 128), F32),
        compiler_params=_cparams("parallel"),
        name="fox_forget_cumsum",
    )(proj, bias)


FOX_KEY_ROWS = 512
FOX_ZERO_LOGIT = -110.0


def _fox_kernel(q_ref, k_ref, v_ref, fc_ref, fcs_ref, o_ref, m_scr, acc_scr, s_scr, kx_scr, k2_scr, *, tq,
                hp_heads):
    hp = pl.program_id(1)
    qi = pl.program_id(2)
    tk = tq
    d = HEAD_DIM
    w = hp_heads * d
    seq = k_ref.shape[0]
    e_row = lax.broadcasted_iota(jnp.int32, (w, w), 0)
    e_lane = lax.broadcasted_iota(jnp.int32, (w, w), 1)

    @pl.when(qi == 0)
    def _():
        kr = min(FOX_KEY_ROWS, seq)
        place = [sum(jnp.where((e_row == hp * hp_heads + hh) & (e_lane == 3 * hh + j), 1.0, 0.0)
                     for hh in range(hp_heads)).astype(BF16) for j in range(3)]
        k2 = jnp.zeros((1, 1), F32)
        for r0 in range(0, seq, kr):
            rows = slice(r0, r0 + kr)
            ext = sum(jnp.dot(piece, place[j], preferred_element_type=F32)
                      for j, piece in enumerate(_split3(-fcs_ref[rows, :])))
            kx_scr[rows, :w] = k_ref[rows, :]
            kx_scr[rows, w:] = ext.astype(BF16)
            kf = k_ref[rows, :].astype(F32)
            k2 = jnp.maximum(k2, jnp.max(jnp.sum(kf * kf, axis=-1, keepdims=True), axis=0, keepdims=True))
        k2_scr[...] = jnp.broadcast_to(k2, k2_scr.shape)

    lane = lax.broadcasted_iota(jnp.int32, (tq, w), 1)
    q = q_ref[...] * (d ** -0.5)

    qf = q.astype(F32)
    q2 = jnp.max(jnp.sum(qf * qf, axis=-1, keepdims=True), axis=0, keepdims=True)
    qk_bound = 2.0 * jnp.sqrt(q2 * k2_scr[0:1, 0:1])
    nblk = seq // tk
    f_end = jnp.concatenate([fcs_ref[(kb + 1) * tk - 1:(kb + 1) * tk, :] for kb in range(nblk)], axis=0)
    head_lane = lax.broadcasted_iota(jnp.int32, (nblk, w), 1)
    in_pair = (head_lane >= hp * hp_heads) & (head_lane < (hp + 1) * hp_heads)
    bound = qk_bound + fc_ref[0:1, :] - f_end
    zero_w = jnp.where(in_pair, jnp.where(bound < FOX_ZERO_LOGIT, 1.0, 0.0), 1.0)
    n_skip = jnp.sum(jnp.min(zero_w, axis=-1, keepdims=True)).astype(jnp.int32)
    first_trip = n_skip // 2
    zero = jnp.zeros_like(q)
    qs = [jnp.concatenate([jnp.where((lane >= hh * d) & (lane < (hh + 1) * d), q, zero),
                           jnp.where((lane >= 3 * hh) & (lane < 3 * hh + 3), 1.0, 0.0).astype(BF16)], axis=1)
          for hh in range(hp_heads)]
    m_scr[...] = jnp.full(m_scr.shape, -jnp.inf, F32)
    acc_scr[...] = jnp.zeros(acc_scr.shape, F32)
    fc = fc_ref[...]
    fqs = [jnp.broadcast_to(jnp.sum(jnp.where(lane == hp * hp_heads + hh, fc, 0.0), axis=-1, keepdims=True),
                            (tq, w)) for hh in range(hp_heads)]
    ones = jnp.ones((tk, w), BF16)

    def scores(kb, slot):
        k0 = pl.multiple_of(kb * tk, tk)
        kx = kx_scr[pl.ds(k0, tk), :]
        for hh in range(hp_heads):
            s_scr[slot, hh] = lax.dot_general(qs[hh], kx, (((1,), (1,)), ((), ())), preferred_element_type=F32)

    def softmax_pv(kb, slot, masked):
        k0 = pl.multiple_of(kb * tk, tk)
        v = jnp.concatenate([v_ref[pl.ds(k0, tk), :], ones], axis=1)
        for hh in range(hp_heads):
            t = s_scr[slot, hh]
            if masked:
                row = lax.broadcasted_iota(jnp.int32, (tq, tk), 0)
                col = lax.broadcasted_iota(jnp.int32, (tq, tk), 1)
                t = jnp.where(col <= row, t, -jnp.inf)
            m_old = m_scr[hh]
            m_new = jnp.maximum(m_old, fqs[hh] + jnp.max(t, axis=-1, keepdims=True))
            alpha = jnp.exp(m_old - m_new)
            c = fqs[hh] - m_new
            p = jnp.exp(t + jnp.concatenate([c] * (tk // w), axis=1))
            acc_scr[hh] = (jnp.concatenate([alpha, alpha], axis=1) * acc_scr[hh]
                           + jnp.dot(p.astype(BF16), v, preferred_element_type=F32))
            m_scr[hh] = m_new

    scores(2 * first_trip, 0)

    def body(j, carry):
        kb = 2 * j
        softmax_pv(kb, 0, False)
        scores(kb + 1, 1)
        softmax_pv(kb + 1, 1, False)
        scores(kb + 2, 0)
        return carry

    lax.fori_loop(first_trip, qi // 2, body, 0)

    @pl.when(qi % 2 == 1)
    def _():
        softmax_pv(qi - 1, 0, False)
        scores(qi, 1)
        softmax_pv(qi, 1, True)

    @pl.when(qi % 2 == 0)
    def _():
        softmax_pv(qi, 0, True)
    out = None
    for hh in range(hp_heads):
        o = acc_scr[hh, :, :w] / acc_scr[hh, :, w:]
        out = o if out is None else jnp.where(lane >= hh * d, o, out)
    o_ref[...] = out.astype(o_ref.dtype)


def fox_attention(proj, fc, *, batch, seq):
    T = proj.shape[0]
    tq = min(512, seq)
    nq = seq // tq
    hp_heads = 2
    n_hp = FOX_HEADS // hp_heads
    qblk, kblk, vblk = COLB["f_q"] // 128, COLB["f_k"] // 128, COLB["f_v"] // 128
    kern = functools.partial(_fox_kernel, tq=tq, hp_heads=hp_heads)
    return pl.pallas_call(
        kern,
        grid=(batch, n_hp, nq),
        scratch_shapes=[pltpu.VMEM((hp_heads, tq, hp_heads * HEAD_DIM), F32),
                        pltpu.VMEM((hp_heads, tq, 2 * hp_heads * HEAD_DIM), F32),
                        pltpu.VMEM((2, hp_heads, tq, tq), F32),
                        pltpu.VMEM((seq, 2 * hp_heads * HEAD_DIM), BF16),
                        pltpu.VMEM((8, 128), F32)],
        in_specs=[
            pl.BlockSpec((tq, 128), lambda b, hp, qi: (b * nq + qi, qblk + hp)),
            pl.BlockSpec((seq, 128), lambda b, hp, qi: (b, kblk + hp)),
            pl.BlockSpec((seq, 128), lambda b, hp, qi: (b, vblk + hp)),
            pl.BlockSpec((tq, 128), lambda b, hp, qi: (b * nq + qi, 0)),
            pl.BlockSpec((seq, 128), lambda b, hp, qi: (b, 0)),
        ],
        out_specs=pl.BlockSpec((tq, 128), lambda b, hp, qi: (b * nq + qi, hp)),
        out_shape=jax.ShapeDtypeStruct((T, BRANCH_WIDTH), BF16),
        compiler_params=_cparams("parallel", "parallel", "arbitrary"),
        name="fox_attention",
    )(proj, proj, proj, fc, fc)


def _split2(x):
    hi = x.astype(BF16)
    return hi, (x - hi.astype(F32)).astype(BF16)


def _split3(x):
    hi = x.astype(BF16)
    r = x - hi.astype(F32)
    mid = r.astype(BF16)
    return hi, mid, (r - mid.astype(F32)).astype(BF16)


def _rotate_half(x, neg_first_half):
    n = x.shape[-1]
    half = HEAD_DIM // 2
    fwd = pltpu.roll(x, half, 1)
    bwd = pltpu.roll(x, n - half, 1)
    return jnp.where(neg_first_half, -bwd, fwd)


def _linear_attn_kernel(*refs, mode, cb):
    if mode == "gla":
        (q_ref, k_ref, v_ref, gate_ref, small_ref, wg_ref, bg_ref, ng_ref, o_ref, st_ref) = refs
    else:
        (q_ref, k_ref, v_ref, gate_ref, lg_ref, cos_ref, sin_ref, gw_ref, gb_ref, o_ref, st_ref) = refs
    H, DK, DV, C = GLA_HEADS, GLA_DK, GLA_DV, CHUNK
    nch = cb // C
    W2 = 2 * DK
    HW = H * DK
    dn_lanes = (((1,), (1,)), ((), ()))
    dn_rows = (((0,), (0,)), ((), ()))

    @pl.when(pl.program_id(1) == 0)
    def _():
        st_ref[...] = jnp.zeros_like(st_ref)

    q = q_ref[...].astype(F32)
    k = k_ref[...].astype(F32)
    if mode == "gla":
        glr = small_ref[...][:, FOX_HEADS:FOX_HEADS + GLA_GATE_RANK]
        a_hi, a_lo = _split2(glr)
        w_hi, w_lo = _split2(wg_ref[...])
        z = (jnp.dot(a_hi, w_hi, preferred_element_type=F32) + jnp.dot(a_hi, w_lo, preferred_element_type=F32)
             + jnp.dot(a_lo, w_hi, preferred_element_type=F32)) + bg_ref[...]
        ld = _log_sigmoid(z) / GLA_TAU
        ld_w = jnp.concatenate([ld[c * C:(c + 1) * C] for c in range(nch)], axis=1)
        ri = lax.broadcasted_iota(jnp.int32, (C, 3 * C), 0)
        ci = lax.broadcasted_iota(jnp.int32, (C, 3 * C), 1)
        tri3 = jnp.where((ci % C) <= ri, 1.0, 0.0).astype(BF16)
        cum_w = jnp.dot(tri3, jnp.concatenate(_split3(ld_w), axis=0), preferred_element_type=F32)
        cum = jnp.concatenate([cum_w[:, c * HW:(c + 1) * HW] for c in range(nch)], axis=0)
        lasts = [cum_w[C - 1:C, c * HW:(c + 1) * HW] for c in range(nch)]
        last_b = jnp.concatenate([jnp.broadcast_to(l, (C, HW)) for l in lasts], axis=0)
    else:
        lane = lax.broadcasted_iota(jnp.int32, (cb, HW), 1)
        first_half = (lane % HEAD_DIM) < (HEAD_DIM // 2)
        cos = cos_ref[...]
        sin = sin_ref[...]
        q = q * cos + _rotate_half(q, first_half) * sin
        k = k * cos + _rotate_half(k, first_half) * sin
        steps = (lax.broadcasted_iota(jnp.int32, (cb, HW), 0) % C + 1).astype(F32)
        cum = steps * lg_ref[...]
        last_b = float(C) * lg_ref[...]
        lasts = [last_b] * nch
    q_in = (q * (DK ** -0.5) * jnp.exp(cum)).astype(BF16)
    k_in = (k * jnp.exp(-cum)).astype(BF16)
    k_st = (k * jnp.exp(last_b - cum)).astype(BF16)
    decs = [jnp.exp(l) for l in lasts]

    rowi = lax.broadcasted_iota(jnp.int32, (2 * C, W2), 0)
    lanei = lax.broadcasted_iota(jnp.int32, (2 * C, W2), 1)
    own = (rowi >= C) == (lanei >= DK)
    bd_causal = own & ((lanei % C) <= (rowi % C))
    zero_bf = jnp.zeros((2 * C, W2), BF16)

    def pair_ops(c, p):
        rows = slice(c * C, (c + 1) * C)
        ls = slice(p * W2, (p + 1) * W2)
        qp, kp, ksp = q_in[rows, ls], k_in[rows, ls], k_st[rows, ls]
        qstack = jnp.where(own, jnp.concatenate([qp, qp], axis=0), zero_bf)
        ksstack = jnp.where(own, jnp.concatenate([ksp, ksp], axis=0), zero_bf)
        k2 = jnp.concatenate([kp, kp], axis=0)
        vstack = jnp.concatenate([v_ref[rows, (2 * p) * DV:(2 * p + 1) * DV],
                                  v_ref[rows, (2 * p + 1) * DV:(2 * p + 2) * DV]], axis=0)
        sw = lax.dot_general(qstack, k2, dn_lanes, preferred_element_type=F32)
        sc = jnp.where(bd_causal, sw, 0.0).astype(BF16)
        o_intra = jnp.dot(sc, vstack, preferred_element_type=F32)
        upd = lax.dot_general(vstack, ksstack, dn_rows, preferred_element_type=F32)
        return qstack, o_intra, upd

    pre = [[pair_ops(c, p) for p in range(H // 2)] for c in range(nch)]
    state = [st_ref[p] for p in range(H // 2)]
    for c in range(nch):
        rows = slice(c * C, (c + 1) * C)
        for p in range(H // 2):
            qstack, o_intra, upd = pre[c][p]
            o = o_intra + lax.dot_general(qstack, state[p].astype(BF16), dn_lanes, preferred_element_type=F32)
            state[p] = state[p] * decs[c][:, p * W2:(p + 1) * W2] + upd
            for hh in range(2):
                hd = 2 * p + hh
                vs = slice(hd * DV, (hd + 1) * DV)
                oh = o[hh * C:(hh + 1) * C]
                gate = _silu(gate_ref[rows, vs])
                if mode == "gla":
                    y = oh * lax.rsqrt(jnp.mean(oh * oh, axis=-1, keepdims=True) + EPS) * ng_ref[...]
                else:
                    mu = jnp.mean(oh, axis=-1, keepdims=True)
                    var = jnp.mean(jnp.square(oh - mu), axis=-1, keepdims=True)
                    y = (oh - mu) * lax.rsqrt(var + EPS) * gw_ref[:, vs] + gb_ref[:, vs]
                o_ref[rows, vs] = (y * gate).astype(o_ref.dtype)
    for p in range(H // 2):
        st_ref[p] = state[p]


LINEAR_ATTN_BLOCK = {"gla": 1024, "ret": 512}


def linear_attention(proj, projf, params, *, mode, batch, seq):
    T = proj.shape[0]
    cb = min(LINEAR_ATTN_BLOCK[mode], seq)
    nc = seq // cb
    pre = "g" if mode == "gla" else "r"
    qblk, kblk = COLB[pre + "_q"] // 256, COLB[pre + "_k"] // 256
    vblk = COLB[pre + "_v"] // 512
    gblk = COLF["g_r" if mode == "gla" else "r_g"] // 512
    row = lambda b, c: b * nc + c
    in_specs = [
        pl.BlockSpec((cb, 256), lambda b, c: (row(b, c), qblk)),
        pl.BlockSpec((cb, 256), lambda b, c: (row(b, c), kblk)),
        pl.BlockSpec((cb, 512), lambda b, c: (row(b, c), vblk)),
        pl.BlockSpec((cb, 512), lambda b, c: (row(b, c), gblk)),
    ]
    args = [proj, proj, proj, projf]
    if mode == "gla":
        wg, bg, ng = params
        in_specs += [
            pl.BlockSpec((cb, 128), lambda b, c: (row(b, c), SMALL_COL // 128)),
            pl.BlockSpec(wg.shape, lambda b, c: (0, 0)),
            pl.BlockSpec((1, bg.shape[-1]), lambda b, c: (0, 0)),
            pl.BlockSpec((1, ng.shape[-1]), lambda b, c: (0, 0)),
        ]
        args += [projf, wg, bg.reshape(1, -1), ng.reshape(1, -1)]
    else:
        lg, cos, sin, gw, gb = params
        in_specs += [
            pl.BlockSpec((1, lg.shape[-1]), lambda b, c: (0, 0)),
            pl.BlockSpec((cb, 256), lambda b, c: (c, 0)),
            pl.BlockSpec((cb, 256), lambda b, c: (c, 0)),
            pl.BlockSpec((1, gw.shape[-1]), lambda b, c: (0, 0)),
            pl.BlockSpec((1, gb.shape[-1]), lambda b, c: (0, 0)),
        ]
        args += [lg.reshape(1, -1), cos, sin, gw.reshape(1, -1), gb.reshape(1, -1)]
    return pl.pallas_call(
        functools.partial(_linear_attn_kernel, mode=mode, cb=cb),
        grid=(batch, nc),
        in_specs=in_specs,
        out_specs=pl.BlockSpec((cb, 512), lambda b, c: (row(b, c), 0)),
        out_shape=jax.ShapeDtypeStruct((T, BRANCH_WIDTH), BF16),
        scratch_shapes=[pltpu.VMEM((GLA_HEADS // 2, GLA_DV, 2 * GLA_DK), F32)],
        compiler_params=_cparams("parallel", "arbitrary"),
        name="linear_attention_" + mode,
    )(*args)


def _retention_tables(seq):
    half = RET_DK // 2
    inv = RET_THETA_BASE ** (-jnp.arange(half, dtype=F32) / half)
    ang = jnp.arange(seq).astype(F32)[:, None] * inv[None, :]
    cos = jnp.tile(jnp.cos(ang), (1, 2 * RET_HEADS))
    sin = jnp.tile(jnp.sin(ang), (1, 2 * RET_HEADS))
    log_gamma = jnp.log1p(-jnp.exp2(-5.0 - jnp.arange(RET_HEADS, dtype=F32)))
    return jnp.repeat(log_gamma, RET_DK), cos, sin


def _merge_kernel(u_ref, oa_ref, ob_ref, oc_ref, od_ref, wm_ref, wb_ref, bm_ref, m_ref):
    u = u_ref[...]
    acc = None
    for i, o_ref in enumerate((oa_ref, ob_ref, oc_ref, od_ref)):
        z = jnp.dot(u, wm_ref[i].astype(BF16), preferred_element_type=F32) + bm_ref[i]
        pr = jnp.dot(o_ref[...], wb_ref[i].astype(BF16), preferred_element_type=F32)
        t = jax.nn.sigmoid(z) * pr
        acc = t if acc is None else acc + t
    m_ref[...] = acc.astype(m_ref.dtype)


def merge_branches(u, branches, wm, wb, bm, *, layer):
    T, D = u.shape
    tm, tn = min(1024, T), min(256, D)
    ospec = pl.BlockSpec((tm, BRANCH_WIDTH), lambda i, j: (i, 0))
    return pl.pallas_call(
        _merge_kernel,
        grid=(T // tm, D // tn),
        in_specs=[
            pl.BlockSpec((tm, D), lambda i, j: (i, 0)),
            ospec, ospec, ospec, ospec,
            pl.BlockSpec((None, N_BRANCH, D, tn), lambda i, j: (layer, 0, 0, j)),
            pl.BlockSpec((None, N_BRANCH, BRANCH_WIDTH, tn), lambda i, j: (layer, 0, 0, j)),
            pl.BlockSpec((None, N_BRANCH, 1, tn), lambda i, j: (layer, 0, 0, j)),
        ],
        out_specs=pl.BlockSpec((tm, tn), lambda i, j: (i, j)),
        out_shape=jax.ShapeDtypeStruct((T, D), BF16),
        compiler_params=_cparams("parallel", "arbitrary"),
        name="merge_branches",
    )(u, *branches, wm, wb, bm.reshape(bm.shape[0], N_BRANCH, 1, D))


def _out_proj_kernel(h_ref, m_ref, w_ref, mod_ref, o_ref, *, mrow):
    m = m_ref[...]
    for c0 in range(0, o_ref.shape[1], PROJ_COL_CHUNK):
        cols = slice(c0, c0 + PROJ_COL_CHUNK)
        y = jnp.dot(m, w_ref[:, cols], preferred_element_type=F32)
        o_ref[:, cols] = h_ref[:, cols] + mod_ref[mrow:mrow + 1, cols] * y


def out_proj(h, merged, w, mod, *, layer, seq, mrow):
    T, D = h.shape
    tm = min(512, seq)
    return pl.pallas_call(
        functools.partial(_out_proj_kernel, mrow=mrow),
        grid=(T // tm,),
        in_specs=[
            pl.BlockSpec((tm, D), lambda i: (i, 0)),
            pl.BlockSpec((tm, D), lambda i: (i, 0)),
            pl.BlockSpec((None, D, D), lambda i: (layer, 0, 0), pipeline_mode=pl.Buffered(1)),
            pl.BlockSpec((None, N_MOD, D), lambda i: ((i * tm) // seq, 0, 0)),
        ],
        out_specs=pl.BlockSpec((tm, D), lambda i: (i, 0)),
        out_shape=jax.ShapeDtypeStruct((T, D), F32),
        compiler_params=_cparams("parallel"),
        name="out_proj",
    )(h, merged, w, mod)


def _w_in_prep_kernel(w_ref, ob_ref, of_ref):
    for order, o_ref in ((_ORDER_B, ob_ref), (_ORDER_F, of_ref)):
        off = 0
        for name in order:
            start, width = _REF_COLS[name]
            o_ref[off:off + width, :] = w_ref[start:start + width, :].astype(o_ref.dtype)
            off += width
        if off < o_ref.shape[0]:
            o_ref[off:, :] = jnp.zeros((o_ref.shape[0] - off, o_ref.shape[1]), o_ref.dtype)


def permuted_w_in(w_in):
    L, D, C = w_in.shape
    td = min(256, D)
    return pl.pallas_call(
        _w_in_prep_kernel,
        grid=(L, D // td),
        in_specs=[pl.BlockSpec((None, C, td), lambda l, r: (l, 0, r))],
        out_specs=[pl.BlockSpec((None, PROJ_B_COLS, td), lambda l, r: (l, 0, r)),
                   pl.BlockSpec((None, PROJ_F_COLS, td), lambda l, r: (l, 0, r))],
        out_shape=[jax.ShapeDtypeStruct((L, PROJ_B_COLS, D), BF16), jax.ShapeDtypeStruct((L, PROJ_F_COLS, D), BF16)],
        compiler_params=_cparams("parallel", "parallel"),
        name="w_in_prep",
    )(jnp.swapaxes(w_in, 1, 2))


def kernel(x, c, w_ada, b_ada, norm_g, ffn_w_gate, ffn_w_up, ffn_w_down, w_in, fox_b_forget, attn_sinks,
           gla_w_gate, gla_b_gate, gla_norm_g, ret_gn_w, ret_gn_b, w_branch, w_merge, b_merge, w_out,
           final_norm_g):
    B, S, D = x.shape
    L = w_ada.shape[0]
    T = B * S
    mod_all = ada_modulation(c, w_ada, b_ada)
    log_gamma, cos, sin = _retention_tables(S)
    w_out_bf = w_out.astype(BF16)
    w_in_b, w_in_f = permuted_w_in(w_in)
    h = x.reshape(T, D)
    for l in range(L):
        mod = mod_all[l]
        h = ffn_block(h, mod, norm_g[l, 0], ffn_w_gate, ffn_w_up, ffn_w_down, layer=l, which=0, seq=S,
                      mrow=0, res_scale=0.5)
        proj, projf, u = norm_proj(h, mod, norm_g[l, 1], w_in_b, w_in_f, layer=l, seq=S, mrow=3)
        o_a = swa_attention(proj, attn_sinks[l], batch=B, seq=S)
        fc = fox_forget_cumsum(projf, fox_b_forget[l], batch=B, seq=S)
        o_b = fox_attention(proj, fc, batch=B, seq=S)
        o_c = linear_attention(proj, projf, (gla_w_gate[l], gla_b_gate[l], gla_norm_g[l]), mode="gla",
                               batch=B, seq=S)
        o_d = linear_attention(proj, projf, (log_gamma, cos, sin, ret_gn_w[l], ret_gn_b[l]), mode="ret",
                               batch=B, seq=S)
        merged = merge_branches(u, (o_a, o_b, o_c, o_d), w_merge, w_branch, b_merge, layer=l)
        h = out_proj(h, merged, w_out_bf, mod, layer=l, seq=S, mrow=5)
        h = ffn_block(h, mod, norm_g[l, 2], ffn_w_gate, ffn_w_up, ffn_w_down, layer=l, which=1, seq=S,
                      mrow=6, res_scale=0.5, final_g=final_norm_g if l == L - 1 else None)
    return h.reshape(B, S, D)
```

```python
import functools

import jax
import jax.numpy as jnp
from jax import lax
from jax.experimental import pallas as pl
from jax.experimental.pallas import tpu as pltpu

HEAD_DIM = 64
SWA_HEADS = 8
SWA_KV_HEADS = 2
WINDOW = 128
FOX_HEADS = 8
GLA_HEADS = 4
GLA_DK = 64
GLA_DV = 128
GLA_GATE_RANK = 16
GLA_TAU = 16.0
RET_HEADS = 4
RET_DK = 64
RET_DV = 128
RET_THETA_BASE = 10000.0
CHUNK = 64
BRANCH_WIDTH = 512
N_BRANCH = 4
N_MOD = 9
EPS = 1e-6

BF16 = jnp.bfloat16
F32 = jnp.float32

VMEM_LIMIT_BYTES = 60 * 1024 * 1024

_REF_COLS = {}
_off = 0
for _name, _size in (
        ("a_q", 512), ("a_k", 128), ("a_v", 128),
        ("f_q", 512), ("f_k", 512), ("f_v", 512), ("f_f", 8),
        ("g_q", 256), ("g_k", 256), ("g_v", 512), ("g_lr", 16), ("g_r", 512),
        ("r_q", 256), ("r_k", 256), ("r_v", 512), ("r_g", 512)):
    _REF_COLS[_name] = (_off, _size)
    _off += _size
IN_COLS = _off
for _name in ("a_k", "a_v"):
    for _i in range(SWA_KV_HEADS):
        _REF_COLS[_name + str(_i)] = (_REF_COLS[_name][0] + _i * HEAD_DIM, HEAD_DIM)


def _layout(order):
    col, off = {}, 0
    for name in order:
        col.setdefault(name, off)
        off += _REF_COLS[name][1]
    return col


_ORDER_B = ("a_q", "f_q", "f_k", "f_v", "g_v", "r_v", "g_q", "g_k", "r_q", "r_k",
            "a_k0", "a_k0", "a_k1", "a_k1", "a_v0", "a_v0", "a_v1", "a_v1")
_ORDER_F = ("g_r", "r_g", "f_f", "g_lr")
COLB = _layout(_ORDER_B)
COLF = _layout(_ORDER_F)
PROJ_B_COLS = 4608
PROJ_F_COLS = 1152
SMALL_COL = COLF["f_f"]


def _cparams(*sem):
    return pltpu.CompilerParams(dimension_semantics=sem, vmem_limit_bytes=VMEM_LIMIT_BYTES)


def _tile(n, pref):
    t = (min(pref, n) // 128) * 128
    while t >= 128:
        if n % t == 0:
            return t
        t -= 128
    return n


NORM_ROWS = 64


def _adanorm_into(h_ref, g_ref, mod_ref, mrow, u_ref, zero_ref=None):
    gs = g_ref[...] * (1.0 + mod_ref[mrow + 1:mrow + 2, :])
    shift = mod_ref[mrow:mrow + 1, :]
    rows = min(NORM_ROWS, h_ref.shape[0])

    def body(r, carry):
        r0 = pl.multiple_of(r * rows, rows)
        x = h_ref[pl.ds(r0, rows), :]
        ms = jnp.mean(x * x, axis=-1, keepdims=True)
        u_ref[pl.ds(r0, rows), :] = (x * lax.rsqrt(ms + EPS) * gs + shift).astype(u_ref.dtype)
        if zero_ref is not None:
            zero_ref[pl.ds(r0, rows), :] = jnp.zeros((rows, zero_ref.shape[1]), zero_ref.dtype)
        return carry

    lax.fori_loop(0, h_ref.shape[0] // rows, body, 0)


def _log_sigmoid(x):
    return jnp.minimum(x, 0.0) - jnp.log1p(jnp.exp(-jnp.abs(x)))


def _silu(x):
    return x * jax.nn.sigmoid(x)


ADA_ROWS = 16


def _ada_kernel(c_ref, w_ref, b_ref, o_ref):
    cond = _silu(c_ref[...])
    r = jnp.dot(jnp.concatenate(_split3(cond), axis=0), w_ref[...].astype(BF16), preferred_element_type=F32)
    bp = ADA_ROWS
    o_ref[...] = (r[:bp] + r[bp:2 * bp]) + r[2 * bp:] + b_ref[...]


def ada_modulation(c, w_ada, b_ada):
    L, D, N = w_ada.shape
    B = c.shape[0]
    BP = ADA_ROWS
    cp = jnp.zeros((BP, D), F32).at[:B].set(c)
    tn = _tile(N, 1024)
    out = pl.pallas_call(
        _ada_kernel,
        grid=(L, N // tn),
        in_specs=[
            pl.BlockSpec((BP, D), lambda l, j: (0, 0)),
            pl.BlockSpec((None, D, tn), lambda l, j: (l, 0, j)),
            pl.BlockSpec((None, 1, tn), lambda l, j: (l, 0, j)),
        ],
        out_specs=pl.BlockSpec((None, BP, tn), lambda l, j: (l, 0, j)),
        out_shape=jax.ShapeDtypeStruct((L, BP, N), F32),
        compiler_params=_cparams("parallel", "parallel"),
        name="ada_modulation",
    )(cp, w_ada, b_ada.reshape(L, 1, N))
    return out[:, :B].reshape(L, B, N_MOD, D)


def _ffn_kernel(h_ref, mod_ref, g_ref, wg_ref, wu_ref, wd_ref, *rest, mrow, res_scale, final_norm):
    if final_norm:
        fg_ref, o_ref, u_scr = rest
    else:
        o_ref, u_scr = rest
    f = pl.program_id(1)

    @pl.when(f == 0)
    def _():
        _adanorm_into(h_ref, g_ref, mod_ref, mrow, u_scr, zero_ref=o_ref)

    u = u_scr[...]
    a = jnp.dot(u, wg_ref[...].astype(BF16), preferred_element_type=F32)
    b = jnp.dot(u, wu_ref[...].astype(BF16), preferred_element_type=F32)
    mid = (_silu(a) * b).astype(BF16)
    o_ref[...] += jnp.dot(mid, wd_ref[...].astype(BF16), preferred_element_type=F32)

    @pl.when(f == pl.num_programs(1) - 1)
    def _():
        gate = res_scale * mod_ref[mrow + 2:mrow + 3, :]
        rows = min(NORM_ROWS, h_ref.shape[0])

        def body(r, carry):
            r0 = pl.multiple_of(r * rows, rows)
            y = h_ref[pl.ds(r0, rows), :] + gate * o_ref[pl.ds(r0, rows), :]
            if final_norm:
                y = y * lax.rsqrt(jnp.mean(y * y, axis=-1, keepdims=True) + EPS) * fg_ref[...]
            o_ref[pl.ds(r0, rows), :] = y
            return carry

        lax.fori_loop(0, h_ref.shape[0] // rows, body, 0)


def ffn_block(h, mod, g, wg, wu, wd, *, layer, which, seq, mrow, res_scale, final_g=None):
    T, D = h.shape
    F = wg.shape[-1]
    tm = min(1024, seq)
    tf = _tile(F, 256)
    kern = functools.partial(_ffn_kernel, mrow=mrow, res_scale=res_scale, final_norm=final_g is not None)
    in_specs = [
        pl.BlockSpec((tm, D), lambda i, f: (i, 0)),
        pl.BlockSpec((None, N_MOD, D), lambda i, f: ((i * tm) // seq, 0, 0)),
        pl.BlockSpec((1, D), lambda i, f: (0, 0)),
        pl.BlockSpec((None, None, D, tf), lambda i, f: (layer, which, 0, f)),
        pl.BlockSpec((None, None, D, tf), lambda i, f: (layer, which, 0, f)),
        pl.BlockSpec((None, None, tf, D), lambda i, f: (layer, which, f, 0)),
    ]
    args = [h, mod, g.reshape(1, D), wg, wu, wd]
    if final_g is not None:
        in_specs.append(pl.BlockSpec((1, D), lambda i, f: (0, 0)))
        args.append(final_g.reshape(1, D))
    return pl.pallas_call(
        kern,
        grid=(T // tm, F // tf),
        in_specs=in_specs,
        out_specs=pl.BlockSpec((tm, D), lambda i, f: (i, 0)),
        out_shape=jax.ShapeDtypeStruct((T, D), F32),
        scratch_shapes=[pltpu.VMEM((tm, D), BF16)],
        compiler_params=_cparams("parallel", "arbitrary"),
        name="ffn_block",
    )(*args)


PROJ_COL_CHUNK = 512


def _norm_proj_kernel(h_ref, mod_ref, g_ref, wb_ref, wf_ref, pb_ref, pf_ref, u_ref, *, mrow):
    _adanorm_into(h_ref, g_ref, mod_ref, mrow, u_ref)
    u = u_ref[...]
    nt = (((1,), (1,)), ((), ()))
    for c0 in range(0, pb_ref.shape[1], PROJ_COL_CHUNK):
        cols = slice(c0, c0 + PROJ_COL_CHUNK)
        pb_ref[:, cols] = lax.dot_general(u, wb_ref[cols, :], nt, preferred_element_type=F32).astype(pb_ref.dtype)
    pf_ref[...] = lax.dot_general(u, wf_ref[...], nt, preferred_element_type=F32)


def norm_proj(h, mod, g, wb, wf, *, layer, seq, mrow):
    T, D = h.shape
    NB, NF = wb.shape[1], wf.shape[1]
    tm = min(512, seq)
    kern = functools.partial(_norm_proj_kernel, mrow=mrow)
    resident = pl.Buffered(1)
    return pl.pallas_call(
        kern,
        grid=(T // tm,),
        in_specs=[
            pl.BlockSpec((tm, D), lambda i: (i, 0)),
            pl.BlockSpec((None, N_MOD, D), lambda i: ((i * tm) // seq, 0, 0)),
            pl.BlockSpec((1, D), lambda i: (0, 0)),
            pl.BlockSpec((None, NB, D), lambda i: (layer, 0, 0), pipeline_mode=resident),
            pl.BlockSpec((None, NF, D), lambda i: (layer, 0, 0), pipeline_mode=resident),
        ],
        out_specs=[
            pl.BlockSpec((tm, NB), lambda i: (i, 0)),
            pl.BlockSpec((tm, NF), lambda i: (i, 0)),
            pl.BlockSpec((tm, D), lambda i: (i, 0)),
        ],
        out_shape=[jax.ShapeDtypeStruct((T, NB), BF16), jax.ShapeDtypeStruct((T, NF), F32),
                   jax.ShapeDtypeStruct((T, D), BF16)],
        compiler_params=_cparams("parallel"),
        name="norm_proj",
    )(h, mod, g.reshape(1, D), wb, wf)


SWA_BLOCKS_PER_STEP = 4


def _swa_kernel(sink_ref, q_ref, kv_ref, kvp_ref, o_ref):
    n = pl.program_id(1)
    W = WINDOW
    d = HEAD_DIM
    rowi = lax.broadcasted_iota(jnp.int32, (2 * W, 2 * d), 0)
    lanei = lax.broadcasted_iota(jnp.int32, (2 * W, 2 * d), 1)
    own = (rowi >= W) == (lanei >= d)
    first_half = lax.broadcasted_iota(jnp.int32, (W, 2 * d), 1) < d
    qpos = lax.broadcasted_iota(jnp.int32, (2 * W, 2 * W), 0) % W + W
    kpos = lax.broadcasted_iota(jnp.int32, (2 * W, 2 * W), 1)
    rel = qpos - kpos
    band = (rel >= 0) & (rel < W)
    ones = jnp.ones((2 * W, 2 * d), BF16)
    group = SWA_HEADS // SWA_KV_HEADS
    for blk in range(SWA_BLOCKS_PER_STEP):
        rows = slice(blk * W, (blk + 1) * W)
        q = q_ref[rows, :] * (d ** -0.5)
        prev = kvp_ref[...] if blk == 0 else kv_ref[(blk - 1) * W:blk * W, :]
        kv = jnp.concatenate([prev, kv_ref[rows, :]], axis=0)
        mask = band & ((kpos >= W) | (n > 0)) if blk == 0 else band
        for g in range(SWA_HEADS // 2):
            kvh = (2 * g) // group
            k2 = kv[:, kvh * 2 * d:(kvh + 1) * 2 * d]
            v2 = kv[:, (SWA_KV_HEADS + kvh) * 2 * d:(SWA_KV_HEADS + kvh + 1) * 2 * d]
            v_ext = jnp.concatenate([v2, ones], axis=1)
            qp = q[:, g * 2 * d:(g + 1) * 2 * d]
            qstack = jnp.where(own, jnp.concatenate([qp, qp], axis=0), jnp.zeros((2 * W, 2 * d), BF16))
            logits = lax.dot_general(qstack, k2, (((1,), (1,)), ((), ())), preferred_element_type=F32)
            logits = jnp.where(mask, logits, -jnp.inf)
            sink = jnp.where(rowi >= W, sink_ref[2 * g + 1], sink_ref[2 * g])
            m = jnp.maximum(jnp.max(logits, axis=-1, keepdims=True), sink)
            p = jnp.exp(logits - jnp.concatenate([m, m], axis=1)).astype(BF16)
            r = jnp.dot(p, v_ext, preferred_element_type=F32)
            o2 = r[:, :2 * d] / (r[:, 2 * d:] + jnp.exp(sink - m))
            out = jnp.where(first_half, o2[:W], o2[W:])
            o_ref[rows, g * 2 * d:(g + 1) * 2 * d] = out.astype(o_ref.dtype)


def swa_attention(proj, sinks, *, batch, seq):
    T = proj.shape[0]
    nb = seq // WINDOW
    per = SWA_BLOCKS_PER_STEP
    ns = nb // per
    qblk = COLB["a_q"] // 512
    kvblk = COLB["a_k0"] // 512
    return pl.pallas_call(
        _swa_kernel,
        grid=(batch, ns),
        in_specs=[
            pl.BlockSpec(memory_space=pltpu.SMEM),
            pl.BlockSpec((per * WINDOW, 512), lambda b, n: (b * ns + n, qblk)),
            pl.BlockSpec((per * WINDOW, 512), lambda b, n: (b * ns + n, kvblk)),
            pl.BlockSpec((WINDOW, 512), lambda b, n: (b * nb + jnp.maximum(per * n - 1, 0), kvblk)),
        ],
        out_specs=pl.BlockSpec((per * WINDOW, 512), lambda b, n: (b * ns + n, 0)),
        out_shape=jax.ShapeDtypeStruct((T, BRANCH_WIDTH), BF16),
        compiler_params=_cparams("parallel", "arbitrary"),
        name="swa_attention",
    )(sinks, proj, proj, proj)


FOX_CUM_ROWS = 256


def _fox_cum_kernel(x_ref, b_ref, o_ref, *, seq):
    R = min(FOX_CUM_ROWS, seq)
    ri = lax.broadcasted_iota(jnp.int32, (R, 3 * R), 0)
    ci = lax.broadcasted_iota(jnp.int32, (R, 3 * R), 1)
    tri3 = jnp.where((ci % R) <= ri, 1.0, 0.0).astype(BF16)

    def body(i, carry):
        r0 = pl.multiple_of(i * R, R)
        ls = _log_sigmoid(x_ref[pl.ds(r0, R), :] + b_ref[...])
        cum = jnp.dot(tri3, jnp.concatenate(_split3(ls), axis=0), preferred_element_type=F32) + carry
        o_ref[pl.ds(r0, R), :] = cum
        return cum[R - 1:R, :]

    lax.fori_loop(0, seq // R, body, jnp.zeros((1, x_ref.shape[1]), F32))


def fox_forget_cumsum(proj, fox_b, *, batch, seq):
    T = proj.shape[0]
    blk = SMALL_COL // 128
    bias = jnp.zeros((1, 128), F32).at[0, :FOX_HEADS].set(fox_b)
    return pl.pallas_call(
        functools.partial(_fox_cum_kernel, seq=seq),
        grid=(batch,),
        in_specs=[
            pl.BlockSpec((seq, 128), lambda b: (b, blk)),
            pl.BlockSpec((1, 128), lambda b: (0, 0)),
        ],
        out_specs=pl.BlockSpec((seq, 128), lambda b: (b, 0)),
        out_shape=jax.ShapeDtypeStruct((T, 128), F32),
        compiler_params=_cparams("parallel"),
        name="fox_forget_cumsum",
    )(proj, bias)


FOX_KEY_ROWS = 512
FOX_ZERO_LOGIT = -110.0


def _fox_kernel(q_ref, k_ref, v_ref, fc_ref, fcs_ref, o_ref, m_scr, acc_scr, s_scr, kx_scr, k2_scr, *, tq,
                hp_heads):
    hp = pl.program_id(1)
    qi = pl.program_id(2)
    tk = tq
    d = HEAD_DIM
    w = hp_heads * d
    seq = k_ref.shape[0]
    e_row = lax.broadcasted_iota(jnp.int32, (w, w), 0)
    e_lane = lax.broadcasted_iota(jnp.int32, (w, w), 1)

    @pl.when(qi == 0)
    def _():
        kr = min(FOX_KEY_ROWS, seq)
        place = [sum(jnp.where((e_row == hp * hp_heads + hh) & (e_lane == 3 * hh + j), 1.0, 0.0)
                     for hh in range(hp_heads)).astype(BF16) for j in range(3)]
        k2 = jnp.zeros((1, 1), F32)
        for r0 in range(0, seq, kr):
            rows = slice(r0, r0 + kr)
            ext = sum(jnp.dot(piece, place[j], preferred_element_type=F32)
                      for j, piece in enumerate(_split3(-fcs_ref[rows, :])))
            kx_scr[rows, :w] = k_ref[rows, :]
            kx_scr[rows, w:] = ext.astype(BF16)
            kf = k_ref[rows, :].astype(F32)
            k2 = jnp.maximum(k2, jnp.max(jnp.sum(kf * kf, axis=-1, keepdims=True), axis=0, keepdims=True))
        k2_scr[...] = jnp.broadcast_to(k2, k2_scr.shape)

    lane = lax.broadcasted_iota(jnp.int32, (tq, w), 1)
    q = q_ref[...] * (d ** -0.5)

    qf = q.astype(F32)
    q_norm = jnp.sqrt(jnp.sum(qf * qf, axis=-1, keepdims=True))
    k_max = jnp.sqrt(k2_scr[0:1, 0:1])
    own_k = k_ref[pl.ds(pl.multiple_of(qi * tq, tq), tq), :].astype(F32)
    qk_own = qf * own_k
    nblk = seq // tk
    f_end = jnp.concatenate([fcs_ref[(kb + 1) * tk - 1:(kb + 1) * tk, :] for kb in range(nblk)], axis=0)
    head_lane = lax.broadcasted_iota(jnp.int32, (nblk, w), 1)
    in_pair = (head_lane >= hp * hp_heads) & (head_lane < (hp + 1) * hp_heads)
    qk_bound = jnp.zeros((nblk, w), F32)
    for hh in range(hp_heads):
        diag = jnp.sum(jnp.where((lane >= hh * d) & (lane < (hh + 1) * d), qk_own, 0.0), axis=-1, keepdims=True)
        worst = jnp.max(q_norm * k_max - diag, axis=0, keepdims=True)
        qk_bound = jnp.where(head_lane == hp * hp_heads + hh, worst, qk_bound)
    bound = qk_bound + fc_ref[0:1, :] - f_end
    zero_w = jnp.where(in_pair, jnp.where(bound < FOX_ZERO_LOGIT, 1.0, 0.0), 1.0)
    n_skip = jnp.sum(jnp.min(zero_w, axis=-1, keepdims=True)).astype(jnp.int32)
    first_trip = (n_skip + 1) // 2
    zero = jnp.zeros_like(q)
    qs = [jnp.concatenate([jnp.where((lane >= hh * d) & (lane < (hh + 1) * d), q, zero),
                           jnp.where((lane >= 3 * hh) & (lane < 3 * hh + 3), 1.0, 0.0).astype(BF16)], axis=1)
          for hh in range(hp_heads)]
    m_scr[...] = jnp.full(m_scr.shape, -jnp.inf, F32)
    acc_scr[...] = jnp.zeros(acc_scr.shape, F32)
    fc = fc_ref[...]
    fqs = [jnp.broadcast_to(jnp.sum(jnp.where(lane == hp * hp_heads + hh, fc, 0.0), axis=-1, keepdims=True),
                            (tq, w)) for hh in range(hp_heads)]
    ones = jnp.ones((tk, w), BF16)

    def scores(kb, slot):
        k0 = pl.multiple_of(kb * tk, tk)
        kx = kx_scr[pl.ds(k0, tk), :]
        for hh in range(hp_heads):
            s_scr[slot, hh] = lax.dot_general(qs[hh], kx, (((1,), (1,)), ((), ())), preferred_element_type=F32)

    def softmax_pv(kb, slot, masked):
        k0 = pl.multiple_of(kb * tk, tk)
        v = jnp.concatenate([v_ref[pl.ds(k0, tk), :], ones], axis=1)
        for hh in range(hp_heads):
            t = s_scr[slot, hh]
            if masked:
                row = lax.broadcasted_iota(jnp.int32, (tq, tk), 0)
                col = lax.broadcasted_iota(jnp.int32, (tq, tk), 1)
                t = jnp.where(col <= row, t, -jnp.inf)
            m_old = m_scr[hh]
            m_new = jnp.maximum(m_old, fqs[hh] + jnp.max(t, axis=-1, keepdims=True))
            alpha = jnp.exp(m_old - m_new)
            c = fqs[hh] - m_new
            p = jnp.exp(t + jnp.concatenate([c] * (tk // w), axis=1))
            acc_scr[hh] = (jnp.concatenate([alpha, alpha], axis=1) * acc_scr[hh]
                           + jnp.dot(p.astype(BF16), v, preferred_element_type=F32))
            m_scr[hh] = m_new

    @pl.when(n_skip % 2 == 0)
    def _():
        scores(n_skip, 0)

    @pl.when(n_skip % 2 == 1)
    def _():
        scores(n_skip, 1)
        softmax_pv(n_skip, 1, False)
        scores(n_skip + 1, 0)

    def body(j, carry):
        kb = 2 * j
        softmax_pv(kb, 0, False)
        scores(kb + 1, 1)
        softmax_pv(kb + 1, 1, False)
        scores(kb + 2, 0)
        return carry

    lax.fori_loop(first_trip, qi // 2, body, 0)

    @pl.when(qi % 2 == 1)
    def _():
        softmax_pv(qi - 1, 0, False)
        scores(qi, 1)
        softmax_pv(qi, 1, True)

    @pl.when(qi % 2 == 0)
    def _():
        softmax_pv(qi, 0, True)
    out = None
    for hh in range(hp_heads):
        o = acc_scr[hh, :, :w] / acc_scr[hh, :, w:]
        out = o if out is None else jnp.where(lane >= hh * d, o, out)
    o_ref[...] = out.astype(o_ref.dtype)


def fox_attention(proj, fc, *, batch, seq):
    T = proj.shape[0]
    tq = min(512, seq)
    nq = seq // tq
    hp_heads = 2
    n_hp = FOX_HEADS // hp_heads
    qblk, kblk, vblk = COLB["f_q"] // 128, COLB["f_k"] // 128, COLB["f_v"] // 128
    kern = functools.partial(_fox_kernel, tq=tq, hp_heads=hp_heads)
    return pl.pallas_call(
        kern,
        grid=(batch, n_hp, nq),
        scratch_shapes=[pltpu.VMEM((hp_heads, tq, hp_heads * HEAD_DIM), F32),
                        pltpu.VMEM((hp_heads, tq, 2 * hp_heads * HEAD_DIM), F32),
                        pltpu.VMEM((2, hp_heads, tq, tq), F32),
                        pltpu.VMEM((seq, 2 * hp_heads * HEAD_DIM), BF16),
                        pltpu.VMEM((8, 128), F32)],
        in_specs=[
            pl.BlockSpec((tq, 128), lambda b, hp, qi: (b * nq + qi, qblk + hp)),
            pl.BlockSpec((seq, 128), lambda b, hp, qi: (b, kblk + hp)),
            pl.BlockSpec((seq, 128), lambda b, hp, qi: (b, vblk + hp)),
            pl.BlockSpec((tq, 128), lambda b, hp, qi: (b * nq + qi, 0)),
            pl.BlockSpec((seq, 128), lambda b, hp, qi: (b, 0)),
        ],
        out_specs=pl.BlockSpec((tq, 128), lambda b, hp, qi: (b * nq + qi, hp)),
        out_shape=jax.ShapeDtypeStruct((T, BRANCH_WIDTH), BF16),
        compiler_params=_cparams("parallel", "parallel", "arbitrary"),
        name="fox_attention",
    )(proj, proj, proj, fc, fc)


def _split2(x):
    hi = x.astype(BF16)
    return hi, (x - hi.astype(F32)).astype(BF16)


def _split3(x):
    hi = x.astype(BF16)
    r = x - hi.astype(F32)
    mid = r.astype(BF16)
    return hi, mid, (r - mid.astype(F32)).astype(BF16)


def _rotate_half(x, neg_first_half):
    n = x.shape[-1]
    half = HEAD_DIM // 2
    fwd = pltpu.roll(x, half, 1)
    bwd = pltpu.roll(x, n - half, 1)
    return jnp.where(neg_first_half, -bwd, fwd)


def _linear_attn_kernel(*refs, mode, cb):
    if mode == "gla":
        (q_ref, k_ref, v_ref, gate_ref, small_ref, wg_ref, bg_ref, ng_ref, o_ref, st_ref) = refs
    else:
        (q_ref, k_ref, v_ref, gate_ref, lg_ref, cos_ref, sin_ref, gw_ref, gb_ref, o_ref, st_ref) = refs
    H, DK, DV, C = GLA_HEADS, GLA_DK, GLA_DV, CHUNK
    nch = cb // C
    W2 = 2 * DK
    HW = H * DK
    dn_lanes = (((1,), (1,)), ((), ()))
    dn_rows = (((0,), (0,)), ((), ()))

    @pl.when(pl.program_id(1) == 0)
    def _():
        st_ref[...] = jnp.zeros_like(st_ref)

    q = q_ref[...].astype(F32)
    k = k_ref[...].astype(F32)
    if mode == "gla":
        glr = small_ref[...][:, FOX_HEADS:FOX_HEADS + GLA_GATE_RANK]
        a_hi, a_lo = _split2(glr)
        w_hi, w_lo = _split2(wg_ref[...])
        z = (jnp.dot(a_hi, w_hi, preferred_element_type=F32) + jnp.dot(a_hi, w_lo, preferred_element_type=F32)
             + jnp.dot(a_lo, w_hi, preferred_element_type=F32)) + bg_ref[...]
        ld = _log_sigmoid(z) / GLA_TAU
        ld_w = jnp.concatenate([ld[c * C:(c + 1) * C] for c in range(nch)], axis=1)
        ri = lax.broadcasted_iota(jnp.int32, (C, 3 * C), 0)
        ci = lax.broadcasted_iota(jnp.int32, (C, 3 * C), 1)
        tri3 = jnp.where((ci % C) <= ri, 1.0, 0.0).astype(BF16)
        cum_w = jnp.dot(tri3, jnp.concatenate(_split3(ld_w), axis=0), preferred_element_type=F32)
        cum = jnp.concatenate([cum_w[:, c * HW:(c + 1) * HW] for c in range(nch)], axis=0)
        lasts = [cum_w[C - 1:C, c * HW:(c + 1) * HW] for c in range(nch)]
        last_b = jnp.concatenate([jnp.broadcast_to(l, (C, HW)) for l in lasts], axis=0)
    else:
        lane = lax.broadcasted_iota(jnp.int32, (cb, HW), 1)
        first_half = (lane % HEAD_DIM) < (HEAD_DIM // 2)
        cos = cos_ref[...]
        sin = sin_ref[...]
        q = q * cos + _rotate_half(q, first_half) * sin
        k = k * cos + _rotate_half(k, first_half) * sin
        steps = (lax.broadcasted_iota(jnp.int32, (cb, HW), 0) % C + 1).astype(F32)
        cum = steps * lg_ref[...]
        last_b = float(C) * lg_ref[...]
        lasts = [last_b] * nch
    q_in = (q * (DK ** -0.5) * jnp.exp(cum)).astype(BF16)
    k_in = (k * jnp.exp(-cum)).astype(BF16)
    k_st = (k * jnp.exp(last_b - cum)).astype(BF16)
    decs = [jnp.exp(l) for l in lasts]

    rowi = lax.broadcasted_iota(jnp.int32, (2 * C, W2), 0)
    lanei = lax.broadcasted_iota(jnp.int32, (2 * C, W2), 1)
    own = (rowi >= C) == (lanei >= DK)
    bd_causal = own & ((lanei % C) <= (rowi % C))
    zero_bf = jnp.zeros((2 * C, W2), BF16)

    def pair_ops(c, p):
        rows = slice(c * C, (c + 1) * C)
        ls = slice(p * W2, (p + 1) * W2)
        qp, kp, ksp = q_in[rows, ls], k_in[rows, ls], k_st[rows, ls]
        qstack = jnp.where(own, jnp.concatenate([qp, qp], axis=0), zero_bf)
        ksstack = jnp.where(own, jnp.concatenate([ksp, ksp], axis=0), zero_bf)
        k2 = jnp.concatenate([kp, kp], axis=0)
        vstack = jnp.concatenate([v_ref[rows, (2 * p) * DV:(2 * p + 1) * DV],
                                  v_ref[rows, (2 * p + 1) * DV:(2 * p + 2) * DV]], axis=0)
        sw = lax.dot_general(qstack, k2, dn_lanes, preferred_element_type=F32)
        sc = jnp.where(bd_causal, sw, 0.0).astype(BF16)
        o_intra = jnp.dot(sc, vstack, preferred_element_type=F32)
        upd = lax.dot_general(vstack, ksstack, dn_rows, preferred_element_type=F32)
        return qstack, o_intra, upd

    pre = [[pair_ops(c, p) for p in range(H // 2)] for c in range(nch)]
    state = [st_ref[p] for p in range(H // 2)]
    for c in range(nch):
        rows = slice(c * C, (c + 1) * C)
        for p in range(H // 2):
            qstack, o_intra, upd = pre[c][p]
            o = o_intra + lax.dot_general(qstack, state[p].astype(BF16), dn_lanes, preferred_element_type=F32)
            state[p] = state[p] * decs[c][:, p * W2:(p + 1) * W2] + upd
            for hh in range(2):
                hd = 2 * p + hh
                vs = slice(hd * DV, (hd + 1) * DV)
                oh = o[hh * C:(hh + 1) * C]
                gate = _silu(gate_ref[rows, vs])
                if mode == "gla":
                    y = oh * lax.rsqrt(jnp.mean(oh * oh, axis=-1, keepdims=True) + EPS) * ng_ref[...]
                else:
                    mu = jnp.mean(oh, axis=-1, keepdims=True)
                    var = jnp.mean(jnp.square(oh - mu), axis=-1, keepdims=True)
                    y = (oh - mu) * lax.rsqrt(var + EPS) * gw_ref[:, vs] + gb_ref[:, vs]
                o_ref[rows, vs] = (y * gate).astype(o_ref.dtype)
    for p in range(H // 2):
        st_ref[p] = state[p]


LINEAR_ATTN_BLOCK = {"gla": 1024, "ret": 512}


def linear_attention(proj, projf, params, *, mode, batch, seq):
    T = proj.shape[0]
    cb = min(LINEAR_ATTN_BLOCK[mode], seq)
    nc = seq // cb
    pre = "g" if mode == "gla" else "r"
    qblk, kblk = COLB[pre + "_q"] // 256, COLB[pre + "_k"] // 256
    vblk = COLB[pre + "_v"] // 512
    gblk = COLF["g_r" if mode == "gla" else "r_g"] // 512
    row = lambda b, c: b * nc + c
    in_specs = [
        pl.BlockSpec((cb, 256), lambda b, c: (row(b, c), qblk)),
        pl.BlockSpec((cb, 256), lambda b, c: (row(b, c), kblk)),
        pl.BlockSpec((cb, 512), lambda b, c: (row(b, c), vblk)),
        pl.BlockSpec((cb, 512), lambda b, c: (row(b, c), gblk)),
    ]
    args = [proj, proj, proj, projf]
    if mode == "gla":
        wg, bg, ng = params
        in_specs += [
            pl.BlockSpec((cb, 128), lambda b, c: (row(b, c), SMALL_COL // 128)),
            pl.BlockSpec(wg.shape, lambda b, c: (0, 0)),
            pl.BlockSpec((1, bg.shape[-1]), lambda b, c: (0, 0)),
            pl.BlockSpec((1, ng.shape[-1]), lambda b, c: (0, 0)),
        ]
        args += [projf, wg, bg.reshape(1, -1), ng.reshape(1, -1)]
    else:
        lg, cos, sin, gw, gb = params
        in_specs += [
            pl.BlockSpec((1, lg.shape[-1]), lambda b, c: (0, 0)),
            pl.BlockSpec((cb, 256), lambda b, c: (c, 0)),
            pl.BlockSpec((cb, 256), lambda b, c: (c, 0)),
            pl.BlockSpec((1, gw.shape[-1]), lambda b, c: (0, 0)),
            pl.BlockSpec((1, gb.shape[-1]), lambda b, c: (0, 0)),
        ]
        args += [lg.reshape(1, -1), cos, sin, gw.reshape(1, -1), gb.reshape(1, -1)]
    return pl.pallas_call(
        functools.partial(_linear_attn_kernel, mode=mode, cb=cb),
        grid=(batch, nc),
        in_specs=in_specs,
        out_specs=pl.BlockSpec((cb, 512), lambda b, c: (row(b, c), 0)),
        out_shape=jax.ShapeDtypeStruct((T, BRANCH_WIDTH), BF16),
        scratch_shapes=[pltpu.VMEM((GLA_HEADS // 2, GLA_DV, 2 * GLA_DK), F32)],
        compiler_params=_cparams("parallel", "arbitrary"),
        name="linear_attention_" + mode,
    )(*args)


def _retention_tables(seq):
    half = RET_DK // 2
    inv = RET_THETA_BASE ** (-jnp.arange(half, dtype=F32) / half)
    ang = jnp.arange(seq).astype(F32)[:, None] * inv[None, :]
    cos = jnp.tile(jnp.cos(ang), (1, 2 * RET_HEADS))
    sin = jnp.tile(jnp.sin(ang), (1, 2 * RET_HEADS))
    log_gamma = jnp.log1p(-jnp.exp2(-5.0 - jnp.arange(RET_HEADS, dtype=F32)))
    return jnp.repeat(log_gamma, RET_DK), cos, sin


def _merge_kernel(u_ref, oa_ref, ob_ref, oc_ref, od_ref, wm_ref, wb_ref, bm_ref, m_ref):
    u = u_ref[...]
    acc = None
    for i, o_ref in enumerate((oa_ref, ob_ref, oc_ref, od_ref)):
        z = jnp.dot(u, wm_ref[i].astype(BF16), preferred_element_type=F32) + bm_ref[i]
        pr = jnp.dot(o_ref[...], wb_ref[i].astype(BF16), preferred_element_type=F32)
        t = jax.nn.sigmoid(z) * pr
        acc = t if acc is None else acc + t
    m_ref[...] = acc.astype(m_ref.dtype)


def merge_branches(u, branches, wm, wb, bm, *, layer):
    T, D = u.shape
    tm, tn = min(1024, T), min(256, D)
    ospec = pl.BlockSpec((tm, BRANCH_WIDTH), lambda i, j: (i, 0))
    return pl.pallas_call(
        _merge_kernel,
        grid=(T // tm, D // tn),
        in_specs=[
            pl.BlockSpec((tm, D), lambda i, j: (i, 0)),
            ospec, ospec, ospec, ospec,
            pl.BlockSpec((None, N_BRANCH, D, tn), lambda i, j: (layer, 0, 0, j)),
            pl.BlockSpec((None, N_BRANCH, BRANCH_WIDTH, tn), lambda i, j: (layer, 0, 0, j)),
            pl.BlockSpec((None, N_BRANCH, 1, tn), lambda i, j: (layer, 0, 0, j)),
        ],
        out_specs=pl.BlockSpec((tm, tn), lambda i, j: (i, j)),
        out_shape=jax.ShapeDtypeStruct((T, D), BF16),
        compiler_params=_cparams("parallel", "arbitrary"),
        name="merge_branches",
    )(u, *branches, wm, wb, bm.reshape(bm.shape[0], N_BRANCH, 1, D))


def _out_proj_kernel(h_ref, m_ref, w_ref, mod_ref, o_ref, *, mrow):
    m = m_ref[...]
    for c0 in range(0, o_ref.shape[1], PROJ_COL_CHUNK):
        cols = slice(c0, c0 + PROJ_COL_CHUNK)
        y = jnp.dot(m, w_ref[:, cols], preferred_element_type=F32)
        o_ref[:, cols] = h_ref[:, cols] + mod_ref[mrow:mrow + 1, cols] * y


def out_proj(h, merged, w, mod, *, layer, seq, mrow):
    T, D = h.shape
    tm = min(512, seq)
    return pl.pallas_call(
        functools.partial(_out_proj_kernel, mrow=mrow),
        grid=(T // tm,),
        in_specs=[
            pl.BlockSpec((tm, D), lambda i: (i, 0)),
            pl.BlockSpec((tm, D), lambda i: (i, 0)),
            pl.BlockSpec((None, D, D), lambda i: (layer, 0, 0), pipeline_mode=pl.Buffered(1)),
            pl.BlockSpec((None, N_MOD, D), lambda i: ((i * tm) // seq, 0, 0)),
        ],
        out_specs=pl.BlockSpec((tm, D), lambda i: (i, 0)),
        out_shape=jax.ShapeDtypeStruct((T, D), F32),
        compiler_params=_cparams("parallel"),
        name="out_proj",
    )(h, merged, w, mod)


def _w_in_prep_kernel(w_ref, ob_ref, of_ref):
    for order, o_ref in ((_ORDER_B, ob_ref), (_ORDER_F, of_ref)):
        off = 0
        for name in order:
            start, width = _REF_COLS[name]
            o_ref[off:off + width, :] = w_ref[start:start + width, :].astype(o_ref.dtype)
            off += width
        if off < o_ref.shape[0]:
            o_ref[off:, :] = jnp.zeros((o_ref.shape[0] - off, o_ref.shape[1]), o_ref.dtype)


def permuted_w_in(w_in):
    L, D, C = w_in.shape
    td = min(256, D)
    return pl.pallas_call(
        _w_in_prep_kernel,
        grid=(L, D // td),
        in_specs=[pl.BlockSpec((None, C, td), lambda l, r: (l, 0, r))],
        out_specs=[pl.BlockSpec((None, PROJ_B_COLS, td), lambda l, r: (l, 0, r)),
                   pl.BlockSpec((None, PROJ_F_COLS, td), lambda l, r: (l, 0, r))],
        out_shape=[jax.ShapeDtypeStruct((L, PROJ_B_COLS, D), BF16), jax.ShapeDtypeStruct((L, PROJ_F_COLS, D), BF16)],
        compiler_params=_cparams("parallel", "parallel"),
        name="w_in_prep",
    )(jnp.swapaxes(w_in, 1, 2))


def kernel(x, c, w_ada, b_ada, norm_g, ffn_w_gate, ffn_w_up, ffn_w_down, w_in, fox_b_forget, attn_sinks,
           gla_w_gate, gla_b_gate, gla_norm_g, ret_gn_w, ret_gn_b, w_branch, w_merge, b_merge, w_out,
           final_norm_g):
    B, S, D = x.shape
    L = w_ada.shape[0]
    T = B * S
    mod_all = ada_modulation(c, w_ada, b_ada)
    log_gamma, cos, sin = _retention_tables(S)
    w_out_bf = w_out.astype(BF16)
    w_in_b, w_in_f = permuted_w_in(w_in)
    h = x.reshape(T, D)
    for l in range(L):
        mod = mod_all[l]
        h = ffn_block(h, mod, norm_g[l, 0], ffn_w_gate, ffn_w_up, ffn_w_down, layer=l, which=0, seq=S,
                      mrow=0, res_scale=0.5)
        proj, projf, u = norm_proj(h, mod, norm_g[l, 1], w_in_b, w_in_f, layer=l, seq=S, mrow=3)
        o_a = swa_attention(proj, attn_sinks[l], batch=B, seq=S)
        fc = fox_forget_cumsum(projf, fox_b_forget[l], batch=B, seq=S)
        o_b = fox_attention(proj, fc, batch=B, seq=S)
        o_c = linear_attention(proj, projf, (gla_w_gate[l], gla_b_gate[l], gla_norm_g[l]), mode="gla",
                               batch=B, seq=S)
        o_d = linear_attention(proj, projf, (log_gamma, cos, sin, ret_gn_w[l], ret_gn_b[l]), mode="ret",
                               batch=B, seq=S)
        merged = merge_branches(u, (o_a, o_b, o_c, o_d), w_merge, w_branch, b_merge, layer=l)
        h = out_proj(h, merged, w_out_bf, mod, layer=l, seq=S, mrow=5)
        h = ffn_block(h, mod, norm_g[l, 2], ffn_w_gate, ffn_w_up, ffn_w_down, layer=l, which=1, seq=S,
                      mrow=6, res_scale=0.5, final_g=final_norm_g if l == L - 1 else None)
    return h.reshape(B, S, D)
```
